```python
import jax, jax.numpy as jnp
from jax import lax
import numpy as np

D_MODEL = 1024
BATCH = 8
SEQ = 2048
DEPTH = 1

CHUNK = 64
N_MEM = 256
EPS = 1e-6
GDN_HEADS = 4
GDN_DK = 128
GDN_DV = 128
CONV_K = 4
DSA_HEADS = 8
DSA_HD = 64
IDX_HEADS = 16
IDX_HD = 64
TOPK_MAX = 256
XA_HEADS = 4
XA_HD = 128
D_FF = ((8 * D_MODEL + 3 * 256 - 1) // (3 * 256)) * 256

GDN_KEY_DIM = GDN_HEADS * GDN_DK
GDN_VAL_DIM = GDN_HEADS * GDN_DV
GDN_CONV_CH = 2 * GDN_KEY_DIM + GDN_VAL_DIM
DSA_DIM = DSA_HEADS * DSA_HD
IDX_Q_DIM = IDX_HEADS * IDX_HD
IN_SPLITS = (GDN_CONV_CH, GDN_VAL_DIM, GDN_HEADS, GDN_HEADS,
             DSA_DIM, DSA_DIM, DSA_DIM, IDX_Q_DIM, IDX_HD, IDX_HEADS)
D_IN = GDN_CONV_CH + GDN_VAL_DIM + 2 * GDN_HEADS + 3 * DSA_DIM + IDX_Q_DIM + IDX_HD + IDX_HEADS
D_MIX = GDN_VAL_DIM + DSA_DIM
XA_DIM = XA_HEADS * XA_HD

kernel_name = "hybrid_gdn_dsa_memxattn_block"


def rms_norm(x, g):
    xf = x.astype(jnp.float32)
    y = xf * lax.rsqrt(jnp.mean(xf * xf, axis=-1, keepdims=True) + EPS)
    return (y * g.astype(jnp.float32)).astype(x.dtype)


def l2norm(x):
    xf = x.astype(jnp.float32)
    return xf * lax.rsqrt(jnp.sum(xf * xf, axis=-1, keepdims=True) + EPS)


def split_cols(a, sizes):
    idx, s = [], 0
    for n in sizes[:-1]:
        s += n
        idx.append(s)
    return jnp.split(a, idx, axis=-1)


def alibi_slopes(n):
    return jnp.exp2(-8.0 * jnp.arange(1, n + 1, dtype=jnp.float32) / n)


def causal_conv(u, w):
    K, C = w.shape
    return lax.conv_general_dilated(u, w[:, None, :].astype(u.dtype), window_strides=(1,),
                                    padding=[(K - 1, 0)], dimension_numbers=('NWC', 'WIO', 'NWC'),
                                    feature_group_count=C)


def gated_delta_rule(q, k, v, g, beta):
    B, T, H, dk = q.shape
    dv = v.shape[-1]
    n = T // CHUNK

    def to_chunks(a):
        return jnp.moveaxis(a.reshape(B, n, CHUNK, *a.shape[2:]), 2, 3)

    qc, kc, vc = to_chunks(q), to_chunks(k), to_chunks(v)
    gc, bc = to_chunks(g), to_chunks(beta)
    decay = jnp.cumsum(gc, axis=-1)
    pos = jnp.arange(CHUNK)
    incl = pos[:, None] >= pos[None, :]
    strict = pos[:, None] > pos[None, :]
    diff = decay[..., :, None] - decay[..., None, :]
    gamma = jnp.exp(jnp.where(incl, diff, -jnp.inf))
    kb = kc * bc[..., None]
    vb = vc * bc[..., None]
    a_mat = jnp.where(strict, jnp.einsum('bnhid,bnhjd->bnhij', kb, kc) * gamma, 0.0)
    eye = jnp.eye(CHUNK, dtype=jnp.float32)
    t_mat = jax.lax.linalg.triangular_solve(eye + a_mat, jnp.broadcast_to(eye, a_mat.shape),
                                            left_side=True, lower=True)
    u = jnp.einsum('bnhij,bnhjd->bnhid', t_mat, vb)
    w = jnp.einsum('bnhij,bnhjd->bnhid', t_mat, kb * jnp.exp(decay)[..., None])
    qk = jnp.einsum('bnhid,bnhjd->bnhij', qc, kc) * gamma
    q_dec = qc * jnp.exp(decay)[..., None]
    k_dec = kc * jnp.exp(decay[..., -1:] - decay)[..., None]
    last = jnp.exp(decay[..., -1])

    def step(S, inp):
        qd, kd, u_, w_, qk_, l_ = inp
        v_new = u_ - jnp.einsum('bhcd,bhde->bhce', w_, S)
        o = jnp.einsum('bhcd,bhde->bhce', qd, S) + jnp.einsum('bhij,bhje->bhie', qk_, v_new)
        S = S * l_[..., None, None] + jnp.einsum('bhcd,bhce->bhde', kd, v_new)
        return S, o

    xs = tuple(jnp.moveaxis(a, 1, 0) for a in (q_dec, k_dec, u, w, qk, last))
    S0 = jnp.zeros((B, H, dk, dv), jnp.float32)
    _, o = lax.scan(step, S0, xs)
    o = jnp.moveaxis(jnp.moveaxis(o, 0, 1), 2, 3)
    return o.reshape(B, T, H, dv)


def gated_deltanet(qkv, z, b_logit, a_logit, conv_w, a_log, dt_bias, g_out):
    B, T, _ = qkv.shape
    qkv = jax.nn.silu(causal_conv(qkv, conv_w))
    q = l2norm(qkv[..., :GDN_KEY_DIM].reshape(B, T, GDN_HEADS, GDN_DK)) * (GDN_DK ** -0.5)
    k = l2norm(qkv[..., GDN_KEY_DIM:2 * GDN_KEY_DIM].reshape(B, T, GDN_HEADS, GDN_DK))
    v = qkv[..., 2 * GDN_KEY_DIM:].reshape(B, T, GDN_HEADS, GDN_DV).astype(jnp.float32)
    beta = jax.nn.sigmoid(b_logit.astype(jnp.float32))
    g = -jnp.exp(a_log.astype(jnp.float32)) * jax.nn.softplus(a_logit.astype(jnp.float32) + dt_bias.astype(jnp.float32))
    o = gated_delta_rule(q, k, v, g, beta)
    o = rms_norm(o, g_out) * jax.nn.silu(z.astype(jnp.float32).reshape(B, T, GDN_HEADS, GDN_DV))
    return o.reshape(B, T, GDN_VAL_DIM).astype(qkv.dtype)


def indexed_sparse_attention(q, k, v, q_idx, k_idx, w_idx, g_q, g_k):
    B, T, _ = q.shape
    topk = min(TOPK_MAX, T // 4)
    q = rms_norm(q.reshape(B, T, DSA_HEADS, DSA_HD), g_q)
    k = rms_norm(k.reshape(B, T, DSA_HEADS, DSA_HD), g_k).reshape(B, T, DSA_DIM)
    qi = q_idx.astype(jnp.float32).reshape(B, T, IDX_HEADS, IDX_HD)
    ki = k_idx.astype(jnp.float32)
    wi = w_idx.astype(jnp.float32) * (IDX_HEADS ** -0.5)
    slopes = alibi_slopes(DSA_HEADS)
    key_chunk = jnp.arange(T) // CHUNK

    def block(c):
        start = c * CHUNK
        qb = lax.dynamic_slice_in_dim(q, start, CHUNK, axis=1)
        qib = lax.dynamic_slice_in_dim(qi, start, CHUNK, axis=1)
        wib = lax.dynamic_slice_in_dim(wi, start, CHUNK, axis=1)
        dots = jnp.einsum('bqhd,bsd->bqhs', qib, ki) * (IDX_HD ** -0.5)
        score = jnp.einsum('bqh,bqhs->bqs', wib, jax.nn.relu(dots))
        admissible = key_chunk <= c
        score = jnp.where(admissible[None, None, :], score, -jnp.inf)
        _, sel = lax.top_k(score, topk)
        valid = admissible[sel]
        kg = jax.vmap(lambda kb, ib: kb[ib])(k, sel).reshape(B, CHUNK, topk, DSA_HEADS, DSA_HD)
        vg = jax.vmap(lambda vb, ib: vb[ib])(v, sel).reshape(B, CHUNK, topk, DSA_HEADS, DSA_HD)
        logits = jnp.einsum('bqhd,bqkhd->bhqk', qb, kg).astype(jnp.float32) * (DSA_HD ** -0.5)
        qpos = start + jnp.arange(CHUNK)
        dist = jnp.abs(qpos[None, :, None] - sel).astype(jnp.float32)
        logits = logits - slopes[None, :, None, None] * dist[:, None]
        logits = jnp.where(valid[:, None], logits, -jnp.inf)
        p = jax.nn.softmax(logits, axis=-1)
        out = jnp.einsum('bhqk,bqkhd->bqhd', p.astype(vg.dtype), vg)
        return out.reshape(B, CHUNK, DSA_DIM)

    outs = lax.map(block, jnp.arange(T // CHUNK))
    return jnp.moveaxis(outs, 0, 1).reshape(B, T, DSA_DIM)


def memory_cross_attention(h, m, w_q, w_kv, w_o, g_q, g_k):
    B, T, _ = h.shape
    M = m.shape[1]
    q = rms_norm((h @ w_q).reshape(B, T, XA_HEADS, XA_HD), g_q)
    k, v = jnp.split(m @ w_kv, 2, axis=-1)
    k = rms_norm(k.reshape(B, M, XA_HEADS, XA_HD), g_k)
    v = v.reshape(B, M, XA_HEADS, XA_HD)
    logits = jnp.einsum('bthd,bmhd->bhtm', q, k).astype(jnp.float32) * (XA_HD ** -0.5)
    p = jax.nn.softmax(logits, axis=-1)
    o = jnp.einsum('bhtm,bmhd->bthd', p.astype(v.dtype), v).reshape(B, T, XA_DIM)
    return o @ w_o


def setup_inputs(seed: int = 0) -> dict:
    key = jax.random.key(seed)
    ks = jax.random.split(key, 24)
    f32 = jnp.float32

    def dense(k, fan_in, fan_out):
        return jax.random.normal(k, (DEPTH, fan_in, fan_out), f32) * fan_in ** -0.5

    def gain(k, n):
        return 1.0 + 0.02 * jax.random.normal(k, (DEPTH, n), f32)

    x = jax.random.normal(ks[0], (BATCH, SEQ, D_MODEL), f32)
    mem = jax.random.normal(ks[1], (BATCH, N_MEM, D_MODEL), f32)
    dt = jnp.exp(jax.random.uniform(ks[6], (DEPTH, GDN_HEADS), f32, np.log(1e-3), np.log(1e-1)))
    return {
        "x": x,
        "mem": mem,
        "g_mix": gain(ks[2], D_MODEL),
        "w_in": dense(ks[3], D_MODEL, D_IN),
        "conv_w": jax.random.normal(ks[4], (DEPTH, CONV_K, GDN_CONV_CH), f32) * CONV_K ** -0.5,
        "a_log": jnp.log(jax.random.uniform(ks[5], (DEPTH, GDN_HEADS), f32, 1.0, 16.0)),
        "dt_bias": dt + jnp.log(-jnp.expm1(-dt)),
        "g_gdn_out": gain(ks[7], GDN_DV),
        "g_q_dsa": gain(ks[8], DSA_HD),
        "g_k_dsa": gain(ks[9], DSA_HD),
        "w_out": dense(ks[10], D_MIX, D_MODEL),
        "g_xattn": gain(ks[11], D_MODEL),
        "g_mem": gain(ks[12], D_MODEL),
        "w_xq": dense(ks[13], D_MODEL, XA_DIM),
        "w_xkv": dense(ks[14], D_MODEL, 2 * XA_DIM),
        "g_xq": gain(ks[15], XA_HD),
        "g_xk": gain(ks[16], XA_HD),
        "w_xo": dense(ks[17], XA_DIM, D_MODEL),
        "g_ffn": gain(ks[18], D_MODEL),
        "w_gu": dense(ks[19], D_MODEL, 2 * D_FF),
        "w_down": dense(ks[20], D_FF, D_MODEL),
    }


def reference(x, mem, g_mix, w_in, conv_w, a_log, dt_bias, g_gdn_out, g_q_dsa, g_k_dsa, w_out,
              g_xattn, g_mem, w_xq, w_xkv, g_xq, g_xk, w_xo, g_ffn, w_gu, w_down):
    for l in range(DEPTH):
        h = rms_norm(x, g_mix[l])
        proj = h @ w_in[l]
        (gdn_qkv, gdn_z, gdn_b, gdn_a, dsa_q, dsa_k, dsa_v,
         idx_q, idx_k, idx_w) = split_cols(proj, IN_SPLITS)
        y_a = gated_deltanet(gdn_qkv, gdn_z, gdn_b, gdn_a, conv_w[l], a_log[l], dt_bias[l], g_gdn_out[l])
        y_b = indexed_sparse_attention(dsa_q, dsa_k, dsa_v, idx_q, idx_k, idx_w, g_q_dsa[l], g_k_dsa[l])
        x = x + jnp.concatenate([y_a, y_b], axis=-1) @ w_out[l]
        h = rms_norm(x, g_xattn[l])
        m = rms_norm(mem, g_mem[l])
        x = x + memory_cross_attention(h, m, w_xq[l], w_xkv[l], w_xo[l], g_xq[l], g_xk[l])
        h = rms_norm(x, g_ffn[l])
        gate, up = jnp.split(h @ w_gu[l], 2, axis=-1)
        x = x + (jax.nn.silu(gate) * up) @ w_down[l]
    return x
```

```python
import functools

import jax
import jax.numpy as jnp
from jax import lax
from jax.experimental import pallas as pl
from jax.experimental.pallas import tpu as pltpu

F32 = jnp.float32
BF16 = jnp.bfloat16
I32 = jnp.int32

EPS = 1e-6
CHUNK = 64
GDN_HEADS, GDN_DK, GDN_DV, CONV_K = 4, 128, 128, 4
DSA_HEADS, DSA_HD = 8, 64
IDX_HEADS, IDX_HD = 16, 64
TOPK_MAX = 256
XA_HEADS, XA_HD = 4, 128

GDN_KEY_DIM = GDN_HEADS * GDN_DK
GDN_VAL_DIM = GDN_HEADS * GDN_DV
GDN_CONV_CH = 2 * GDN_KEY_DIM + GDN_VAL_DIM
DSA_DIM = DSA_HEADS * DSA_HD
IDX_Q_DIM = IDX_HEADS * IDX_HD
XA_DIM = XA_HEADS * XA_HD

LANES = 128
SUBLANES = 8
VMEM_LIMIT = 56 * 1024 * 1024

RS_IDXK = 0
RS_BETA = IDX_HD
RS_A = IDX_HD + GDN_HEADS

ROW_TILE = 512
KEY_TILE = 256
Q_BLOCK = 128
GDN_CHUNKS_PER_STEP = 2

INT_MIN = -2 ** 31
NEG_BIG = -1e30


def _dot(a, b):
    return jnp.dot(a, b, preferred_element_type=F32)


def _dot_nt(a, b):
    return lax.dot_general(a, b, (((1,), (1,)), ((), ())), preferred_element_type=F32)


def _dot_f32(a, b):
    return jnp.dot(a, b, preferred_element_type=F32, precision=lax.Precision.HIGHEST)


def _rms_rows(x, g):
    ms = jnp.mean(x * x, axis=-1, keepdims=True)
    return x * lax.rsqrt(ms + EPS) * g


def _const_spec(shape):
    nd = len(shape)
    return pl.BlockSpec(shape, lambda *_: (0,) * nd, pipeline_mode=pl.Buffered(1))


def _block_diag_ones(n, blk):
    r = lax.broadcasted_iota(I32, (n, n), 0) // blk
    c = lax.broadcasted_iota(I32, (n, n), 1) // blk
    return (r == c).astype(BF16)


def _inproj_kernel(x_ref, g_ref, wrow_ref, wcol_ref, bd_ref, gk_ref, gq_ref,
                   qkv_ref, z_ref, kn_ref, rs_ref, qT_ref, vT_ref, qiT_ref, wT_ref):
    h = _rms_rows(x_ref[...], g_ref[...]).astype(BF16)
    c0, c1, c2, c3 = GDN_CONV_CH, GDN_CONV_CH + GDN_VAL_DIM, GDN_CONV_CH + GDN_VAL_DIM + DSA_DIM, \
        GDN_CONV_CH + GDN_VAL_DIM + DSA_DIM + LANES
    qkv_ref[...] = _dot(h, wrow_ref[:, 0:c0])
    z_ref[...] = _dot(h, wrow_ref[:, c0:c1])
    k = _dot(h, wrow_ref[:, c1:c2])
    kms = _dot((k * k).astype(BF16), bd_ref[...]) * (1.0 / DSA_HD)
    kn_ref[...] = (k * lax.rsqrt(kms + EPS) * gk_ref[...]).astype(BF16)
    rs_ref[...] = _dot(h, wrow_ref[:, c1 + DSA_DIM:c3])

    r0, r1, r2, r3 = DSA_DIM, 2 * DSA_DIM, 2 * DSA_DIM + IDX_Q_DIM, 2 * DSA_DIM + IDX_Q_DIM + IDX_HEADS
    qT = _dot_nt(wcol_ref[0:r0, :], h)
    qms = _dot(bd_ref[...], (qT * qT).astype(BF16)) * (1.0 / DSA_HD)
    qT_ref[...] = (qT * lax.rsqrt(qms + EPS) * gq_ref[...] * (DSA_HD ** -0.5)).astype(BF16)
    vT = _dot_nt(wcol_ref[r0:r1, :], h).astype(BF16)
    for i in range(vT_ref.shape[0]):
        vT_ref[i] = vT[:, i * KEY_TILE:(i + 1) * KEY_TILE]
    qiT_ref[...] = (_dot_nt(wcol_ref[r1:r2, :], h) * (IDX_HD ** -0.5)).astype(BF16)
    wT_ref[...] = _dot_nt(wcol_ref[r2:r3, :], h) * (IDX_HEADS ** -0.5)


def _inproj(x2, g_mix, w_in, g_q, g_k):
    n, d = x2.shape
    tm = ROW_TILE
    sizes = (GDN_CONV_CH, GDN_VAL_DIM, GDN_HEADS, GDN_HEADS, DSA_DIM, DSA_DIM, DSA_DIM, IDX_Q_DIM, IDX_HD, IDX_HEADS)
    offs = [0]
    for s in sizes:
        offs.append(offs[-1] + s)
    (w_qkv, w_z, w_b, w_a, w_q, w_k, w_v, w_iq, w_ik, w_iw) = [w_in[:, offs[i]:offs[i + 1]] for i in range(len(sizes))]
    pad = jnp.zeros((d, LANES - IDX_HD - 2 * GDN_HEADS), w_in.dtype)
    w_row = jnp.concatenate([w_qkv, w_z, w_k, w_ik, w_b, w_a, pad], axis=1).astype(BF16)
    w_col = jnp.concatenate([w_q, w_v, w_iq, w_iw], axis=1).T.astype(BF16)
    bd = _block_diag_ones(DSA_DIM, DSA_HD)
    gk_row = jnp.tile(g_k, DSA_HEADS)[None, :]
    gq_col = jnp.tile(g_q, DSA_HEADS)[:, None]
    nrow, ncol = w_row.shape[1], w_col.shape[0]
    out_shape = (
        jax.ShapeDtypeStruct((n, GDN_CONV_CH), F32),
        jax.ShapeDtypeStruct((n, GDN_VAL_DIM), F32),
        jax.ShapeDtypeStruct((n, DSA_DIM), BF16),
        jax.ShapeDtypeStruct((n, LANES), F32),
        jax.ShapeDtypeStruct((DSA_DIM, n), BF16),
        jax.ShapeDtypeStruct((n // KEY_TILE, DSA_DIM, KEY_TILE), BF16),
        jax.ShapeDtypeStruct((IDX_Q_DIM, n), BF16),
        jax.ShapeDtypeStruct((IDX_HEADS, n), F32),
    )
    return pl.pallas_call(
        _inproj_kernel,
        grid=(n // tm,),
        in_specs=[
            pl.BlockSpec((tm, d), lambda i: (i, 0)),
            _const_spec((1, d)),
            _const_spec((d, nrow)),
            _const_spec((ncol, d)),
            _const_spec((DSA_DIM, DSA_DIM)),
            _const_spec((1, DSA_DIM)),
            _const_spec((DSA_DIM, 1)),
        ],
        out_specs=(
            pl.BlockSpec((tm, GDN_CONV_CH), lambda i: (i, 0)),
            pl.BlockSpec((tm, GDN_VAL_DIM), lambda i: (i, 0)),
            pl.BlockSpec((tm, DSA_DIM), lambda i: (i, 0)),
            pl.BlockSpec((tm, LANES), lambda i: (i, 0)),
            pl.BlockSpec((DSA_DIM, tm), lambda i: (0, i)),
            pl.BlockSpec((tm // KEY_TILE, DSA_DIM, KEY_TILE), lambda i: (i, 0, 0)),
            pl.BlockSpec((IDX_Q_DIM, tm), lambda i: (0, i)),
            pl.BlockSpec((IDX_HEADS, tm), lambda i: (0, i)),
        ),
        out_shape=out_shape,
        compiler_params=pltpu.CompilerParams(dimension_semantics=("arbitrary",), vmem_limit_bytes=VMEM_LIMIT),
        name="inproj",
    )(x2, g_mix[None, :], w_row, w_col, bd, gk_row, gq_col)


def _gdn_kernel(qkv_ref, z_ref, rs_ref, cw_ref, alog_ref, dtb_ref, gout_ref, y_ref, xc_ref, s_ref):
    rows = qkv_ref.shape[0]
    nc = rows // CHUNK

    @pl.when(pl.program_id(1) == 0)
    def _start_of_sequence():
        xc_ref[0:SUBLANES, :] = jnp.zeros((SUBLANES, GDN_CONV_CH), F32)
        s_ref[...] = jnp.zeros(s_ref.shape, F32)

    xc_ref[SUBLANES:SUBLANES + rows, :] = qkv_ref[...]
    cw = cw_ref[...]
    conv = cw[0:1, :] * xc_ref[SUBLANES - 3:SUBLANES - 3 + rows, :]
    for j in range(1, CONV_K):
        conv = conv + cw[j:j + 1, :] * xc_ref[SUBLANES - 3 + j:SUBLANES - 3 + j + rows, :]
    xc_ref[0:SUBLANES, :] = xc_ref[rows:rows + SUBLANES, :]
    act = conv * jax.nn.sigmoid(conv)

    rs = rs_ref[...]
    beta_all = jax.nn.sigmoid(rs)
    sp_in = rs + dtb_ref[...]
    softplus = jnp.maximum(sp_in, 0.0) + jnp.log(1.0 + jnp.exp(-jnp.abs(sp_in)))
    g_all = -jnp.exp(alog_ref[...]) * softplus

    ri = lax.broadcasted_iota(I32, (CHUNK, CHUNK), 0)
    ci = lax.broadcasted_iota(I32, (CHUNK, CHUNK), 1)
    incl = ri >= ci
    strict = ri > ci
    ltri = incl.astype(F32)
    eye = (ri == ci).astype(F32)
    gout = gout_ref[...]

    pre = []
    for c in range(nc):
        r = slice(c * CHUNK, (c + 1) * CHUNK)
        dcum = _dot_f32(ltri, g_all[r, :])
        dcum_t = dcum.T
        heads = []
        for h in range(GDN_HEADS):
            q = act[r, h * GDN_DK:(h + 1) * GDN_DK]
            k = act[r, GDN_KEY_DIM + h * GDN_DK:GDN_KEY_DIM + (h + 1) * GDN_DK]
            v = act[r, 2 * GDN_KEY_DIM + h * GDN_DV:2 * GDN_KEY_DIM + (h + 1) * GDN_DV]
            q = q * lax.rsqrt(jnp.sum(q * q, axis=-1, keepdims=True) + EPS) * (GDN_DK ** -0.5)
            k = k * lax.rsqrt(jnp.sum(k * k, axis=-1, keepdims=True) + EPS)
            beta = beta_all[r, RS_BETA + h:RS_BETA + h + 1]
            d_col = dcum[:, RS_A + h:RS_A + h + 1]
            d_row = dcum_t[RS_A + h:RS_A + h + 1, :]
            d_last = dcum[CHUNK - 1:CHUNK, RS_A + h:RS_A + h + 1]
            gamma = jnp.exp(jnp.where(incl, d_col - d_row, -jnp.inf))
            e_col = jnp.exp(d_col)
            kb = k * beta
            vb = v * beta
            k16 = k.astype(BF16)
            a_mat = jnp.where(strict, _dot_nt(kb.astype(BF16), k16) * gamma, 0.0)
            p = -a_mat
            t_mat = eye + p
            for _ in range(5):
                p = _dot_f32(p, p)
                t_mat = t_mat + _dot_f32(t_mat, p)
            t16 = t_mat.astype(BF16)
            u = _dot(t16, vb.astype(BF16))
            w = _dot(t16, (kb * e_col).astype(BF16))
            qk = _dot_nt(q.astype(BF16), k16) * gamma
            q_dec = q * e_col
            k_dec = k * jnp.exp(d_last - d_col)
            heads.append((q_dec.astype(BF16), k_dec.T.astype(BF16), u, w.astype(BF16), qk.astype(BF16), jnp.exp(d_last)))
        pre.append(heads)

    for h in range(GDN_HEADS):
        s = s_ref[h]
        for c in range(nc):
            r = slice(c * CHUNK, (c + 1) * CHUNK)
            q_dec, k_dec_t, u, w, qk, last = pre[c][h]
            s16 = s.astype(BF16)
            v_new = u - _dot(w, s16)
            o = _dot(q_dec, s16) + _dot(qk, v_new.astype(BF16))
            s = s * last + _dot(k_dec_t, v_new.astype(BF16))
            zc = z_ref[r, h * GDN_DV:(h + 1) * GDN_DV]
            y = _rms_rows(o, gout) * (zc * jax.nn.sigmoid(zc))
            y_ref[r, h * GDN_DV:(h + 1) * GDN_DV] = y.astype(y_ref.dtype)
        s_ref[h] = s


def _gdn(qkv, z, rs, conv_w, a_log, dt_bias, g_out, batch, seq):
    n = qkv.shape[0]
    rows = GDN_CHUNKS_PER_STEP * CHUNK
    steps = seq // rows
    lane_vec = lambda v: jnp.zeros((1, LANES), F32).at[0, RS_A:RS_A + GDN_HEADS].set(v)
    row_map = lambda b, i: (b * steps + i, 0)
    return pl.pallas_call(
        _gdn_kernel,
        grid=(batch, steps),
        in_specs=[
            pl.BlockSpec((rows, GDN_CONV_CH), row_map),
            pl.BlockSpec((rows, GDN_VAL_DIM), row_map),
            pl.BlockSpec((rows, LANES), row_map),
            _const_spec((CONV_K, GDN_CONV_CH)),
            _const_spec((1, LANES)),
            _const_spec((1, LANES)),
            _const_spec((1, GDN_DV)),
        ],
        out_specs=pl.BlockSpec((rows, GDN_VAL_DIM), row_map),
        out_shape=jax.ShapeDtypeStruct((n, GDN_VAL_DIM), BF16),
        scratch_shapes=[
            pltpu.VMEM((rows + SUBLANES, GDN_CONV_CH), F32),
            pltpu.VMEM((GDN_HEADS, GDN_DK, GDN_DV), F32),
        ],
        compiler_params=pltpu.CompilerParams(dimension_semantics=("arbitrary", "arbitrary"), vmem_limit_bytes=VMEM_LIMIT),
        name="gdn",
    )(qkv, z, rs, conv_w, lane_vec(a_log), lane_vec(dt_bias), g_out[None, :])


def _dsa_kernel(qT_ref, qiT_ref, wT_ref, kn_ref, vT_ref, rs_ref, o_ref,
                idxk_scr, sc_scr, thr_scr, m_scr, l_scr, acc_scr, *, topk, n_pos_bits):
    j = pl.program_id(1)

    @pl.when(j == 0)
    def _new_sequence():
        idxk_scr[...] = rs_ref[:, RS_IDXK:RS_IDXK + IDX_HD].astype(BF16)

    n_tiles = (j * Q_BLOCK + Q_BLOCK + KEY_TILE - 1) // KEY_TILE
    lane = lax.broadcasted_iota(I32, (1, Q_BLOCK), 1)
    qpos = j * Q_BLOCK + lane
    key_limit = ((qpos >> 6) + 1) << 6
    row = lax.broadcasted_iota(I32, (KEY_TILE, Q_BLOCK), 0)

    w_t = wT_ref[...]
    qi_pairs = [jnp.concatenate([qiT_ref[(2 * p) * IDX_HD:(2 * p + 1) * IDX_HD, :],
                                 qiT_ref[(2 * p + 1) * IDX_HD:(2 * p + 2) * IDX_HD, :]], axis=1)
                for p in range(IDX_HEADS // 2)]

    def score_tile(t, carry):
        r0 = pl.multiple_of(t * KEY_TILE, KEY_TILE)
        kt = idxk_scr[pl.ds(r0, KEY_TILE), :]
        acc = jnp.zeros((KEY_TILE, Q_BLOCK), F32)
        for p in range(IDX_HEADS // 2):
            d = _dot(kt, qi_pairs[p])
            acc = acc + jnp.maximum(d[:, :Q_BLOCK], 0.0) * w_t[2 * p:2 * p + 1, :]
            acc = acc + jnp.maximum(d[:, Q_BLOCK:], 0.0) * w_t[2 * p + 1:2 * p + 2, :]
        sc_scr[t] = jnp.where(r0 + row < key_limit, acc, -jnp.inf)
        return carry

    lax.fori_loop(0, n_tiles, score_tile, 0)

    def count_ge(thr):
        def count_tile(t, cnt):
            ge = (sc_scr[t] >= thr).astype(I32)
            part = ge[0:SUBLANES, :]
            for r in range(1, KEY_TILE // SUBLANES):
                part = part + ge[r * SUBLANES:(r + 1) * SUBLANES, :]
            return cnt + part

        cnt8 = lax.fori_loop(0, n_tiles, count_tile, jnp.zeros((SUBLANES, Q_BLOCK), I32))
        return jnp.sum(cnt8, axis=0, keepdims=True)

    def key_to_f32(key):
        return lax.bitcast_convert_type(jnp.where(key < 0, key ^ jnp.int32(0x7FFFFFFF), key), F32)

    def bit_step(bi, st):
        key, cnt_key = st
        cand = key + lax.shift_left(jnp.int32(1), 31 - bi)
        cnt = count_ge(key_to_f32(cand))
        take = cnt >= topk
        return jnp.where(take, cand, key), jnp.where(take, cnt, cnt_key)

    key, cnt = lax.fori_loop(0, 32, bit_step, (jnp.full((1, Q_BLOCK), INT_MIN, I32), jnp.zeros((1, Q_BLOCK), I32)))
    few = key_limit < topk
    thr_scr[...] = jnp.where(few, jnp.finfo(F32).min, key_to_f32(key))
    cnt = jnp.where(few, 0, cnt)

    @pl.when(jnp.max(cnt) > topk)
    def _resolve_ties():
        def min_above(thr):
            def tile(t, m):
                sc = sc_scr[t]
                return jnp.minimum(m, jnp.min(jnp.where(sc > thr, sc, jnp.inf), axis=0, keepdims=True))
            return lax.fori_loop(0, n_tiles, tile, jnp.full((1, Q_BLOCK), jnp.inf, F32))

        def count_eq_below(thr, pos_limit):
            def tile(t, c):
                r0 = t * KEY_TILE
                hit = (sc_scr[t] == thr) & (r0 + row < pos_limit)
                return c + jnp.sum(hit.astype(I32), axis=0, keepdims=True)
            return lax.fori_loop(0, n_tiles, tile, jnp.zeros((1, Q_BLOCK), I32))

        def body(st):
            thr, cnt = st
            nxt = min_above(thr)
            cnt_n = count_ge(nxt)
            active = cnt > topk
            advance = active & (cnt_n >= topk)
            tie = active & (cnt_n < topk)
            need = topk - cnt_n
            pos = jnp.zeros((1, Q_BLOCK), I32)
            for b in range(n_pos_bits - 1, -1, -1):
                cand = pos + (1 << b)
                pos = jnp.where(count_eq_below(thr, cand) < need, cand, pos)

            def drop_tile(t, carry):
                r0 = t * KEY_TILE
                sc = sc_scr[t]
                sc_scr[t] = jnp.where(tie & (sc == thr) & (r0 + row > pos), -jnp.inf, sc)
                return carry

            lax.fori_loop(0, n_tiles, drop_tile, 0)
            return jnp.where(advance, nxt, thr), jnp.where(advance, cnt_n, jnp.where(tie, topk, cnt))

        thr, _ = lax.while_loop(lambda st: jnp.max(st[1]) > topk, body, (thr_scr[...], cnt))
        thr_scr[...] = thr

    thr = thr_scr[...]

    zero = jnp.zeros((DSA_HD, Q_BLOCK), BF16)
    q_pairs = []
    for p in range(DSA_HEADS // 2):
        a = qT_ref[(2 * p) * DSA_HD:(2 * p + 1) * DSA_HD, :]
        b = qT_ref[(2 * p + 1) * DSA_HD:(2 * p + 2) * DSA_HD, :]
        q_pairs.append(jnp.concatenate([jnp.concatenate([a, zero], axis=1),
                                        jnp.concatenate([zero, b], axis=1)], axis=0))
    m_scr[...] = jnp.full(m_scr.shape, NEG_BIG, F32)
    l_scr[...] = jnp.zeros(l_scr.shape, F32)
    acc_scr[...] = jnp.zeros(acc_scr.shape, F32)
    qposf = qpos.astype(F32)

    def attend_tile(t, carry):
        r0 = pl.multiple_of(t * KEY_TILE, KEY_TILE)
        sel = sc_scr[t] >= thr
        dist = jnp.abs(qposf - (r0 + row).astype(F32))
        kt = kn_ref[pl.ds(r0, KEY_TILE), :]
        vt = vT_ref[t]
        for p in range(DSA_HEADS // 2):
            s2 = _dot(kt[:, p * 2 * DSA_HD:(p + 1) * 2 * DSA_HD], q_pairs[p])
            for e in range(2):
                h = 2 * p + e
                slope = 2.0 ** (-8.0 * (h + 1) / DSA_HEADS)
                s = jnp.where(sel, s2[:, e * Q_BLOCK:(e + 1) * Q_BLOCK] - slope * dist, NEG_BIG)
                m_old = m_scr[h:h + 1, :]
                m_new = jnp.maximum(m_old, jnp.max(s, axis=0, keepdims=True))
                alpha = jnp.exp(m_old - m_new)
                pr = jnp.exp(s - m_new)
                l_scr[h:h + 1, :] = alpha * l_scr[h:h + 1, :] + jnp.sum(pr, axis=0, keepdims=True)
                m_scr[h:h + 1, :] = m_new
                hs = slice(h * DSA_HD, (h + 1) * DSA_HD)
                acc_scr[hs, :] = alpha * acc_scr[hs, :] + _dot(vt[hs, :], pr.astype(BF16))
        return carry

    lax.fori_loop(0, n_tiles, attend_tile, 0)

    outs = [acc_scr[h * DSA_HD:(h + 1) * DSA_HD, :] / l_scr[h:h + 1, :] for h in range(DSA_HEADS)]
    o_ref[...] = jnp.concatenate(outs, axis=0).T.astype(o_ref.dtype)


def _dsa(qT, qiT, wT, kn, vT, rs, batch, seq):
    n = kn.shape[0]
    blocks = seq // Q_BLOCK
    tiles = seq // KEY_TILE
    topk = min(TOPK_MAX, seq // 4)
    col_map = lambda b, j: (0, b * blocks + j)
    return pl.pallas_call(
        functools.partial(_dsa_kernel, topk=topk, n_pos_bits=seq.bit_length()),
        grid=(batch, blocks),
        in_specs=[
            pl.BlockSpec((DSA_DIM, Q_BLOCK), col_map),
            pl.BlockSpec((IDX_Q_DIM, Q_BLOCK), col_map),
            pl.BlockSpec((IDX_HEADS, Q_BLOCK), col_map),
            pl.BlockSpec((seq, DSA_DIM), lambda b, j: (b, 0)),
            pl.BlockSpec((tiles, DSA_DIM, KEY_TILE), lambda b, j: (b, 0, 0)),
            pl.BlockSpec((seq, LANES), lambda b, j: (b, 0)),
        ],
        out_specs=pl.BlockSpec((Q_BLOCK, DSA_DIM), lambda b, j: (b * blocks + j, 0)),
        out_shape=jax.ShapeDtypeStruct((n, DSA_DIM), BF16),
        scratch_shapes=[
            pltpu.VMEM((seq, IDX_HD), BF16),
            pltpu.VMEM((tiles, KEY_TILE, Q_BLOCK), F32),
            pltpu.VMEM((1, Q_BLOCK), F32),
            pltpu.VMEM((DSA_HEADS, Q_BLOCK), F32),
            pltpu.VMEM((DSA_HEADS, Q_BLOCK), F32),
            pltpu.VMEM((DSA_DIM, Q_BLOCK), F32),
        ],
        compiler_params=pltpu.CompilerParams(dimension_semantics=("arbitrary", "arbitrary"), vmem_limit_bytes=VMEM_LIMIT),
        name="dsa",
    )(qT, qiT, wT, kn, vT, rs)


def _memkv_kernel(mem_ref, g_ref, w_ref, bd_ref, gk_ref, k_ref, v_ref):
    h = _rms_rows(mem_ref[...], g_ref[...]).astype(BF16)
    kv = _dot(h, w_ref[...])
    k = kv[:, :XA_DIM]
    kms = _dot((k * k).astype(BF16), bd_ref[...]) * (1.0 / XA_HD)
    k_ref[...] = (k * lax.rsqrt(kms + EPS) * gk_ref[...]).astype(BF16)
    v_ref[...] = kv[:, XA_DIM:].astype(BF16)


def _memkv(mem2, g_mem, w_xkv, g_xk):
    n, d = mem2.shape
    tm = min(ROW_TILE, n)
    return pl.pallas_call(
        _memkv_kernel,
        grid=(n // tm,),
        in_specs=[
            pl.BlockSpec((tm, d), lambda i: (i, 0)),
            _const_spec((1, d)),
            _const_spec((d, 2 * XA_DIM)),
            _const_spec((XA_DIM, XA_DIM)),
            _const_spec((1, XA_DIM)),
        ],
        out_specs=(pl.BlockSpec((tm, XA_DIM), lambda i: (i, 0)), pl.BlockSpec((tm, XA_DIM), lambda i: (i, 0))),
        out_shape=(jax.ShapeDtypeStruct((n, XA_DIM), BF16), jax.ShapeDtypeStruct((n, XA_DIM), BF16)),
        compiler_params=pltpu.CompilerParams(dimension_semantics=("arbitrary",), vmem_limit_bytes=VMEM_LIMIT),
        name="memkv",
    )(mem2, g_mem[None, :], w_xkv.astype(BF16), _block_diag_ones(XA_DIM, XA_HD), jnp.tile(g_xk, XA_HEADS)[None, :])


def _tail_kernel(x_ref, ya_ref, yb_ref, wout_ref, gx_ref, wxq_ref, bd_ref, gxq_ref, km_ref, vm_ref, wxo_ref,
                 gf_ref, wgu_ref, wd_ref, o_ref, *, ff_chunks):
    x1 = x_ref[...] + _dot(ya_ref[...], wout_ref[0:GDN_VAL_DIM, :]) + _dot(yb_ref[...], wout_ref[GDN_VAL_DIM:, :])

    h = _rms_rows(x1, gx_ref[...]).astype(BF16)
    q = _dot(h, wxq_ref[...])
    qms = _dot((q * q).astype(BF16), bd_ref[...]) * (1.0 / XA_HD)
    qn = (q * lax.rsqrt(qms + EPS) * gxq_ref[...]).astype(BF16)
    heads = []
    for hd in range(XA_HEADS):
        hs = slice(hd * XA_HD, (hd + 1) * XA_HD)
        s = _dot_nt(qn[:, hs], km_ref[0, :, hs]) * (XA_HD ** -0.5)
        pr = jnp.exp(s - jnp.max(s, axis=-1, keepdims=True))
        o = _dot(pr.astype(BF16), vm_ref[0, :, hs]) / jnp.sum(pr, axis=-1, keepdims=True)
        heads.append(o.astype(BF16))
    x2 = x1 + _dot(jnp.concatenate(heads, axis=1), wxo_ref[...])

    h = _rms_rows(x2, gf_ref[...]).astype(BF16)
    d_ff = wd_ref.shape[0]
    fc = d_ff // ff_chunks
    acc = x2
    for c in range(ff_chunks):
        gate = _dot(h, wgu_ref[:, c * fc:(c + 1) * fc])
        up = _dot(h, wgu_ref[:, d_ff + c * fc:d_ff + (c + 1) * fc])
        a = (gate * jax.nn.sigmoid(gate) * up).astype(BF16)
        acc = acc + _dot(a, wd_ref[c * fc:(c + 1) * fc, :])
    o_ref[...] = acc


def _tail(x2, ya, yb, w_out, g_xattn, w_xq, g_xq, km, vm, w_xo, g_ffn, w_gu, w_down, seq):
    n, d = x2.shape
    tm = ROW_TILE
    d_ff = w_down.shape[0]
    ff_chunks = 2 if (d_ff // 2) % LANES == 0 else 1
    n_mem = km.shape[1]
    per_seq = seq // tm
    return pl.pallas_call(
        functools.partial(_tail_kernel, ff_chunks=ff_chunks),
        grid=(n // tm,),
        in_specs=[
            pl.BlockSpec((tm, d), lambda i: (i, 0)),
            pl.BlockSpec((tm, GDN_VAL_DIM), lambda i: (i, 0)),
            pl.BlockSpec((tm, DSA_DIM), lambda i: (i, 0)),
            _const_spec((GDN_VAL_DIM + DSA_DIM, d)),
            _const_spec((1, d)),
            _const_spec((d, XA_DIM)),
            _const_spec((XA_DIM, XA_DIM)),
            _const_spec((1, XA_DIM)),
            pl.BlockSpec((1, n_mem, XA_DIM), lambda i: (i // per_seq, 0, 0)),
            pl.BlockSpec((1, n_mem, XA_DIM), lambda i: (i // per_seq, 0, 0)),
            _const_spec((XA_DIM, d)),
            _const_spec((1, d)),
            _const_spec((d, 2 * d_ff)),
            _const_spec((d_ff, d)),
        ],
        out_specs=pl.BlockSpec((tm, d), lambda i: (i, 0)),
        out_shape=jax.ShapeDtypeStruct((n, d), F32),
        compiler_params=pltpu.CompilerParams(dimension_semantics=("arbitrary",), vmem_limit_bytes=VMEM_LIMIT),
        name="tail",
    )(x2, ya, yb, w_out.astype(BF16), g_xattn[None, :], w_xq.astype(BF16), _block_diag_ones(XA_DIM, XA_HD),
      jnp.tile(g_xq, XA_HEADS)[None, :], km, vm, w_xo.astype(BF16), g_ffn[None, :], w_gu.astype(BF16),
      w_down.astype(BF16))


def kernel(x, mem, g_mix, w_in, conv_w, a_log, dt_bias, g_gdn_out, g_q_dsa, g_k_dsa, w_out, g_xattn, g_mem, w_xq,
           w_xkv, g_xq, g_xk, w_xo, g_ffn, w_gu, w_down):
    batch, seq, d = x.shape
    n_mem = mem.shape[1]
    assert seq % ROW_TILE == 0 and seq % KEY_TILE == 0 and ROW_TILE % KEY_TILE == 0
    for l in range(g_mix.shape[0]):
        x2 = x.reshape(batch * seq, d)
        qkv, z, kn, rs, qT, vT, qiT, wT = _inproj(x2, g_mix[l], w_in[l], g_q_dsa[l], g_k_dsa[l])
        ya = _gdn(qkv, z, rs, conv_w[l], a_log[l], dt_bias[l], g_gdn_out[l], batch, seq)
        yb = _dsa(qT, qiT, wT, kn, vT, rs, batch, seq)
        km, vm = _memkv(mem.reshape(batch * n_mem, d), g_mem[l], w_xkv[l], g_xk[l])
        km = km.reshape(batch, n_mem, XA_DIM)
        vm = vm.reshape(batch, n_mem, XA_DIM)
        x = _tail(x2, ya, yb, w_out[l], g_xattn[l], w_xq[l], g_xq[l], km, vm, w_xo[l], g_ffn[l], w_gu[l],
                  w_down[l], seq).reshape(batch, seq, d)
    return x
```

```python
import functools

import jax
import jax.numpy as jnp
from jax import lax
from jax.experimental import pallas as pl
from jax.experimental.pallas import tpu as pltpu

F32 = jnp.float32
BF16 = jnp.bfloat16
I32 = jnp.int32

EPS = 1e-6
CHUNK = 64
GDN_HEADS, GDN_DK, GDN_DV, CONV_K = 4, 128, 128, 4
DSA_HEADS, DSA_HD = 8, 64
IDX_HEADS, IDX_HD = 16, 64
TOPK_MAX = 256
XA_HEADS, XA_HD = 4, 128

GDN_KEY_DIM = GDN_HEADS * GDN_DK
GDN_VAL_DIM = GDN_HEADS * GDN_DV
GDN_CONV_CH = 2 * GDN_KEY_DIM + GDN_VAL_DIM
DSA_DIM = DSA_HEADS * DSA_HD
IDX_Q_DIM = IDX_HEADS * IDX_HD
XA_DIM = XA_HEADS * XA_HD

LANES = 128
SUBLANES = 8
VMEM_LIMIT = 56 * 1024 * 1024

RS_IDXK = 0
RS_BETA = IDX_HD
RS_A = IDX_HD + GDN_HEADS

ROW_TILE = 512
KEY_TILE = 256
Q_BLOCK = 128
GDN_CHUNKS_PER_STEP = 4

INT_MIN = -2 ** 31
NEG_BIG = -1e30


def _dot(a, b):
    return jnp.dot(a, b, preferred_element_type=F32)


def _dot_nt(a, b):
    return lax.dot_general(a, b, (((1,), (1,)), ((), ())), preferred_element_type=F32)


def _dot_f32(a, b):
    return jnp.dot(a, b, preferred_element_type=F32, precision=lax.Precision.HIGHEST)


def _rms_rows(x, g):
    ms = jnp.mean(x * x, axis=-1, keepdims=True)
    return x * lax.rsqrt(ms + EPS) * g


def _const_spec(shape):
    nd = len(shape)
    return pl.BlockSpec(shape, lambda *_: (0,) * nd, pipeline_mode=pl.Buffered(1))


def _block_diag_ones(n, blk):
    r = lax.broadcasted_iota(I32, (n, n), 0) // blk
    c = lax.broadcasted_iota(I32, (n, n), 1) // blk
    return (r == c).astype(BF16)


def _inproj_kernel(x_ref, g_ref, wrow_ref, wcol_ref, bd_ref, gk_ref, gq_ref,
                   qkv_ref, z_ref, kn_ref, rs_ref, qT_ref, vT_ref, qiT_ref, wT_ref):
    h = _rms_rows(x_ref[...], g_ref[...]).astype(BF16)
    c0, c1, c2, c3 = GDN_CONV_CH, GDN_CONV_CH + GDN_VAL_DIM, GDN_CONV_CH + GDN_VAL_DIM + DSA_DIM, \
        GDN_CONV_CH + GDN_VAL_DIM + DSA_DIM + LANES
    qkv_ref[...] = _dot(h, wrow_ref[:, 0:c0])
    z_ref[...] = _dot(h, wrow_ref[:, c0:c1])
    k = _dot(h, wrow_ref[:, c1:c2])
    kms = _dot((k * k).astype(BF16), bd_ref[...]) * (1.0 / DSA_HD)
    kn_ref[...] = (k * lax.rsqrt(kms + EPS) * gk_ref[...]).astype(BF16)
    rs_ref[...] = _dot(h, wrow_ref[:, c1 + DSA_DIM:c3])

    r0, r1, r2, r3 = DSA_DIM, 2 * DSA_DIM, 2 * DSA_DIM + IDX_Q_DIM, 2 * DSA_DIM + IDX_Q_DIM + IDX_HEADS
    qT = _dot_nt(wcol_ref[0:r0, :], h)
    qms = _dot(bd_ref[...], (qT * qT).astype(BF16)) * (1.0 / DSA_HD)
    qT_ref[...] = (qT * lax.rsqrt(qms + EPS) * gq_ref[...] * (DSA_HD ** -0.5)).astype(BF16)
    vT = _dot_nt(wcol_ref[r0:r1, :], h).astype(BF16)
    for i in range(vT_ref.shape[0]):
        vT_ref[i] = vT[:, i * KEY_TILE:(i + 1) * KEY_TILE]
    qiT_ref[...] = (_dot_nt(wcol_ref[r1:r2, :], h) * (IDX_HD ** -0.5)).astype(BF16)
    wT_ref[...] = _dot_nt(wcol_ref[r2:r3, :], h) * (IDX_HEADS ** -0.5)


def _inproj(x2, g_mix, w_in, g_q, g_k):
    n, d = x2.shape
    tm = ROW_TILE
    sizes = (GDN_CONV_CH, GDN_VAL_DIM, GDN_HEADS, GDN_HEADS, DSA_DIM, DSA_DIM, DSA_DIM, IDX_Q_DIM, IDX_HD, IDX_HEADS)
    offs = [0]
    for s in sizes:
        offs.append(offs[-1] + s)
    (w_qkv, w_z, w_b, w_a, w_q, w_k, w_v, w_iq, w_ik, w_iw) = [w_in[:, offs[i]:offs[i + 1]] for i in range(len(sizes))]
    pad = jnp.zeros((d, LANES - IDX_HD - 2 * GDN_HEADS), w_in.dtype)
    w_row = jnp.concatenate([w_qkv, w_z, w_k, w_ik, w_b, w_a, pad], axis=1).astype(BF16)
    w_col = jnp.concatenate([w_q, w_v, w_iq, w_iw], axis=1).T.astype(BF16)
    bd = _block_diag_ones(DSA_DIM, DSA_HD)
    gk_row = jnp.tile(g_k, DSA_HEADS)[None, :]
    gq_col = jnp.tile(g_q, DSA_HEADS)[:, None]
    nrow, ncol = w_row.shape[1], w_col.shape[0]
    out_shape = (
        jax.ShapeDtypeStruct((n, GDN_CONV_CH), F32),
        jax.ShapeDtypeStruct((n, GDN_VAL_DIM), F32),
        jax.ShapeDtypeStruct((n, DSA_DIM), BF16),
        jax.ShapeDtypeStruct((n, LANES), F32),
        jax.ShapeDtypeStruct((DSA_DIM, n), BF16),
        jax.ShapeDtypeStruct((n // KEY_TILE, DSA_DIM, KEY_TILE), BF16),
        jax.ShapeDtypeStruct((IDX_Q_DIM, n), BF16),
        jax.ShapeDtypeStruct((IDX_HEADS, n), F32),
    )
    return pl.pallas_call(
        _inproj_kernel,
        grid=(n // tm,),
        in_specs=[
            pl.BlockSpec((tm, d), lambda i: (i, 0)),
            _const_spec((1, d)),
            _const_spec((d, nrow)),
            _const_spec((ncol, d)),
            _const_spec((DSA_DIM, DSA_DIM)),
            _const_spec((1, DSA_DIM)),
            _const_spec((DSA_DIM, 1)),
        ],
        out_specs=(
            pl.BlockSpec((tm, GDN_CONV_CH), lambda i: (i, 0)),
            pl.BlockSpec((tm, GDN_VAL_DIM), lambda i: (i, 0)),
            pl.BlockSpec((tm, DSA_DIM), lambda i: (i, 0)),
            pl.BlockSpec((tm, LANES), lambda i: (i, 0)),
            pl.BlockSpec((DSA_DIM, tm), lambda i: (0, i)),
            pl.BlockSpec((tm // KEY_TILE, DSA_DIM, KEY_TILE), lambda i: (i, 0, 0)),
            pl.BlockSpec((IDX_Q_DIM, tm), lambda i: (0, i)),
            pl.BlockSpec((IDX_HEADS, tm), lambda i: (0, i)),
        ),
        out_shape=out_shape,
        compiler_params=pltpu.CompilerParams(dimension_semantics=("arbitrary",), vmem_limit_bytes=VMEM_LIMIT),
        name="inproj",
    )(x2, g_mix[None, :], w_row, w_col, bd, gk_row, gq_col)


def _gdn_kernel(qkv_ref, z_ref, rs_ref, cw_ref, alog_ref, dtb_ref, gout_ref, y_ref, xc_ref, s_ref):
    rows = qkv_ref.shape[0]
    nc = rows // CHUNK

    @pl.when(pl.program_id(1) == 0)
    def _start_of_sequence():
        xc_ref[0:SUBLANES, :] = jnp.zeros((SUBLANES, GDN_CONV_CH), F32)
        s_ref[...] = jnp.zeros(s_ref.shape, F32)

    xc_ref[SUBLANES:SUBLANES + rows, :] = qkv_ref[...]
    cw = cw_ref[...]
    conv = cw[0:1, :] * xc_ref[SUBLANES - 3:SUBLANES - 3 + rows, :]
    for j in range(1, CONV_K):
        conv = conv + cw[j:j + 1, :] * xc_ref[SUBLANES - 3 + j:SUBLANES - 3 + j + rows, :]
    xc_ref[0:SUBLANES, :] = xc_ref[rows:rows + SUBLANES, :]
    act = conv * jax.nn.sigmoid(conv)

    rs = rs_ref[...]
    beta_all = jax.nn.sigmoid(rs)
    sp_in = rs + dtb_ref[...]
    softplus = jnp.maximum(sp_in, 0.0) + jnp.log(1.0 + jnp.exp(-jnp.abs(sp_in)))
    g_all = -jnp.exp(alog_ref[...]) * softplus

    ri = lax.broadcasted_iota(I32, (CHUNK, CHUNK), 0)
    ci = lax.broadcasted_iota(I32, (CHUNK, CHUNK), 1)
    incl = ri >= ci
    strict = ri > ci
    ltri = incl.astype(F32)
    eye = (ri == ci).astype(F32)
    gout = gout_ref[...]

    chains = [(c, h) for c in range(nc) for h in range(GDN_HEADS)]
    dcum, dcum_t = [], []
    for c in range(nc):
        d = _dot_f32(ltri, g_all[c * CHUNK:(c + 1) * CHUNK, :])
        dcum.append(d)
        dcum_t.append(d.T)

    q16, k16, kb16, vb16, kw16, qdec16, kdect16, gamma, last = ([] for _ in range(9))
    for c, h in chains:
        r = slice(c * CHUNK, (c + 1) * CHUNK)
        q = act[r, h * GDN_DK:(h + 1) * GDN_DK]
        k = act[r, GDN_KEY_DIM + h * GDN_DK:GDN_KEY_DIM + (h + 1) * GDN_DK]
        v = act[r, 2 * GDN_KEY_DIM + h * GDN_DV:2 * GDN_KEY_DIM + (h + 1) * GDN_DV]
        q = q * lax.rsqrt(jnp.sum(q * q, axis=-1, keepdims=True) + EPS) * (GDN_DK ** -0.5)
        k = k * lax.rsqrt(jnp.sum(k * k, axis=-1, keepdims=True) + EPS)
        beta = beta_all[r, RS_BETA + h:RS_BETA + h + 1]
        d_col = dcum[c][:, RS_A + h:RS_A + h + 1]
        d_row = dcum_t[c][RS_A + h:RS_A + h + 1, :]
        d_last = dcum[c][CHUNK - 1:CHUNK, RS_A + h:RS_A + h + 1]
        e_col = jnp.exp(d_col)
        kb = k * beta
        gamma.append(jnp.exp(jnp.where(incl, d_col - d_row, -jnp.inf)))
        q16.append(q.astype(BF16))
        k16.append(k.astype(BF16))
        kb16.append(kb.astype(BF16))
        vb16.append((v * beta).astype(BF16))
        kw16.append((kb * e_col).astype(BF16))
        qdec16.append((q * e_col).astype(BF16))
        kdect16.append((k * jnp.exp(d_last - d_col)).T.astype(BF16))
        last.append(jnp.exp(d_last))

    n = len(chains)
    p16 = [(-jnp.where(strict, _dot_nt(kb16[i], k16[i]) * gamma[i], 0.0)).astype(BF16) for i in range(n)]
    qk16 = [(_dot_nt(q16[i], k16[i]) * gamma[i]).astype(BF16) for i in range(n)]
    t_mat = [eye + p16[i].astype(F32) for i in range(n)]
    p16 = [_dot(p16[i], p16[i]).astype(BF16) for i in range(n)]
    for _ in range(4):
        t_mat = [t_mat[i] + _dot(t_mat[i].astype(BF16), p16[i]) for i in range(n)]
        p16 = [_dot(p16[i], p16[i]).astype(BF16) for i in range(n)]
    t16 = [(t_mat[i] + _dot(t_mat[i].astype(BF16), p16[i])).astype(BF16) for i in range(n)]
    u = [_dot(t16[i], vb16[i]) for i in range(n)]
    w16 = [_dot(t16[i], kw16[i]).astype(BF16) for i in range(n)]

    s = [s_ref[h] for h in range(GDN_HEADS)]
    for c in range(nc):
        ids = [c * GDN_HEADS + h for h in range(GDN_HEADS)]
        s16 = [s[h].astype(BF16) for h in range(GDN_HEADS)]
        v_new16 = [(u[i] - _dot(w16[i], s16[h])).astype(BF16) for h, i in enumerate(ids)]
        o_state = [_dot(qdec16[i], s16[h]) for h, i in enumerate(ids)]
        o = [o_state[h] + _dot(qk16[i], v_new16[h]) for h, i in enumerate(ids)]
        s = [s[h] * last[i] + _dot(kdect16[i], v_new16[h]) for h, i in enumerate(ids)]
        for h in range(GDN_HEADS):
            zc = z_ref[c * CHUNK:(c + 1) * CHUNK, h * GDN_DV:(h + 1) * GDN_DV]
            y = _rms_rows(o[h], gout) * (zc * jax.nn.sigmoid(zc))
            y_ref[c * CHUNK:(c + 1) * CHUNK, h * GDN_DV:(h + 1) * GDN_DV] = y.astype(y_ref.dtype)
    for h in range(GDN_HEADS):
        s_ref[h] = s[h]


def _gdn(qkv, z, rs, conv_w, a_log, dt_bias, g_out, batch, seq):
    n = qkv.shape[0]
    rows = GDN_CHUNKS_PER_STEP * CHUNK
    steps = seq // rows
    lane_vec = lambda v: jnp.zeros((1, LANES), F32).at[0, RS_A:RS_A + GDN_HEADS].set(v)
    row_map = lambda b, i: (b * steps + i, 0)
    return pl.pallas_call(
        _gdn_kernel,
        grid=(batch, steps),
        in_specs=[
            pl.BlockSpec((rows, GDN_CONV_CH), row_map),
            pl.BlockSpec((rows, GDN_VAL_DIM), row_map),
            pl.BlockSpec((rows, LANES), row_map),
            _const_spec((CONV_K, GDN_CONV_CH)),
            _const_spec((1, LANES)),
            _const_spec((1, LANES)),
            _const_spec((1, GDN_DV)),
        ],
        out_specs=pl.BlockSpec((rows, GDN_VAL_DIM), row_map),
        out_shape=jax.ShapeDtypeStruct((n, GDN_VAL_DIM), BF16),
        scratch_shapes=[
            pltpu.VMEM((rows + SUBLANES, GDN_CONV_CH), F32),
            pltpu.VMEM((GDN_HEADS, GDN_DK, GDN_DV), F32),
        ],
        compiler_params=pltpu.CompilerParams(dimension_semantics=("arbitrary", "arbitrary"), vmem_limit_bytes=VMEM_LIMIT),
        name="gdn",
    )(qkv, z, rs, conv_w, lane_vec(a_log), lane_vec(dt_bias), g_out[None, :])


def _dsa_kernel(qT_ref, qiT_ref, wT_ref, kn_ref, vT_ref, rs_ref, o_ref,
                idxk_scr, sc_scr, thr_scr, cnt_scr, m_scr, l_scr, acc_scr, *, topk, n_pos_bits):
    j = pl.program_id(1)

    @pl.when(j == 0)
    def _new_sequence():
        idxk_scr[...] = rs_ref[:, RS_IDXK:RS_IDXK + IDX_HD].astype(BF16)

    n_tiles = (j * Q_BLOCK + Q_BLOCK + KEY_TILE - 1) // KEY_TILE
    lane = lax.broadcasted_iota(I32, (1, Q_BLOCK), 1)
    qpos = j * Q_BLOCK + lane
    key_limit = ((qpos >> 6) + 1) << 6
    row = lax.broadcasted_iota(I32, (KEY_TILE, Q_BLOCK), 0)

    w_t = wT_ref[...]
    qi_pairs = [jnp.concatenate([qiT_ref[(2 * p) * IDX_HD:(2 * p + 1) * IDX_HD, :],
                                 qiT_ref[(2 * p + 1) * IDX_HD:(2 * p + 2) * IDX_HD, :]], axis=1)
                for p in range(IDX_HEADS // 2)]

    def score_tile(t, carry):
        r0 = pl.multiple_of(t * KEY_TILE, KEY_TILE)
        kt = idxk_scr[pl.ds(r0, KEY_TILE), :]
        acc = jnp.zeros((KEY_TILE, Q_BLOCK), F32)
        for p in range(IDX_HEADS // 2):
            d = _dot(kt, qi_pairs[p])
            acc = acc + jnp.maximum(d[:, :Q_BLOCK], 0.0) * w_t[2 * p:2 * p + 1, :]
            acc = acc + jnp.maximum(d[:, Q_BLOCK:], 0.0) * w_t[2 * p + 1:2 * p + 2, :]
        sc_scr[t] = jnp.where(r0 + row < key_limit, acc, -jnp.inf)
        return carry

    lax.fori_loop(0, n_tiles, score_tile, 0)

    def count_ge(thr):
        cnt_scr[...] = jnp.zeros(cnt_scr.shape, I32)
        base = jnp.int32(0)
        arm = 1 << (sc_scr.shape[0].bit_length() - 1)
        while arm >= 1:
            has = (n_tiles & arm) != 0

            @pl.when(has)
            def _run(base=base, arm=arm):
                parts = [jnp.zeros((SUBLANES, Q_BLOCK), I32) for _ in range(4)]
                for i in range(arm):
                    ge = jnp.where(sc_scr[base + i] >= thr, 1, 0)
                    for r in range(KEY_TILE // SUBLANES):
                        parts[r % 4] = parts[r % 4] + ge[r * SUBLANES:(r + 1) * SUBLANES, :]
                cnt_scr[...] += (parts[0] + parts[1]) + (parts[2] + parts[3])

            base = base + jnp.where(has, arm, 0)
            arm //= 2
        return jnp.sum(cnt_scr[...], axis=0, keepdims=True)

    def key_to_f32(key):
        return lax.bitcast_convert_type(jnp.where(key < 0, key ^ jnp.int32(0x7FFFFFFF), key), F32)

    def bit_step(bi, st):
        key, cnt_key = st
        cand = key + lax.shift_left(jnp.int32(1), 31 - bi)
        cnt = count_ge(key_to_f32(cand))
        take = cnt >= topk
        return jnp.where(take, cand, key), jnp.where(take, cnt, cnt_key)

    key, cnt = lax.fori_loop(0, 32, bit_step, (jnp.full((1, Q_BLOCK), INT_MIN, I32), jnp.zeros((1, Q_BLOCK), I32)))
    few = key_limit < topk
    thr_scr[...] = jnp.where(few, jnp.finfo(F32).min, key_to_f32(key))
    cnt = jnp.where(few, 0, cnt)

    @pl.when(jnp.max(cnt) > topk)
    def _resolve_ties():
        def min_above(thr):
            def tile(t, m):
                sc = sc_scr[t]
                return jnp.minimum(m, jnp.min(jnp.where(sc > thr, sc, jnp.inf), axis=0, keepdims=True))
            return lax.fori_loop(0, n_tiles, tile, jnp.full((1, Q_BLOCK), jnp.inf, F32))

        def count_eq_below(thr, pos_limit):
            def tile(t, c):
                r0 = t * KEY_TILE
                hit = (sc_scr[t] == thr) & (r0 + row < pos_limit)
                return c + jnp.sum(hit.astype(I32), axis=0, keepdims=True)
            return lax.fori_loop(0, n_tiles, tile, jnp.zeros((1, Q_BLOCK), I32))

        def body(st):
            thr, cnt = st
            nxt = min_above(thr)
            cnt_n = count_ge(nxt)
            active = cnt > topk
            advance = active & (cnt_n >= topk)
            tie = active & (cnt_n < topk)
            need = topk - cnt_n
            pos = jnp.zeros((1, Q_BLOCK), I32)
            for b in range(n_pos_bits - 1, -1, -1):
                cand = pos + (1 << b)
                pos = jnp.where(count_eq_below(thr, cand) < need, cand, pos)

            def drop_tile(t, carry):
                r0 = t * KEY_TILE
                sc = sc_scr[t]
                sc_scr[t] = jnp.where(tie & (sc == thr) & (r0 + row > pos), -jnp.inf, sc)
                return carry

            lax.fori_loop(0, n_tiles, drop_tile, 0)
            return jnp.where(advance, nxt, thr), jnp.where(advance, cnt_n, jnp.where(tie, topk, cnt))

        thr, _ = lax.while_loop(lambda st: jnp.max(st[1]) > topk, body, (thr_scr[...], cnt))
        thr_scr[...] = thr

    thr = thr_scr[...]

    zero = jnp.zeros((DSA_HD, Q_BLOCK), BF16)
    q_pairs = []
    for p in range(DSA_HEADS // 2):
        a = qT_ref[(2 * p) * DSA_HD:(2 * p + 1) * DSA_HD, :]
        b = qT_ref[(2 * p + 1) * DSA_HD:(2 * p + 2) * DSA_HD, :]
        q_pairs.append(jnp.concatenate([jnp.concatenate([a, zero], axis=1),
                                        jnp.concatenate([zero, b], axis=1)], axis=0))
    m_scr[...] = jnp.full(m_scr.shape, NEG_BIG, F32)
    l_scr[...] = jnp.zeros(l_scr.shape, F32)
    acc_scr[...] = jnp.zeros(acc_scr.shape, F32)
    qposf = qpos.astype(F32)

    def attend_tile(t, carry):
        r0 = pl.multiple_of(t * KEY_TILE, KEY_TILE)
        sel = sc_scr[t] >= thr
        dist = jnp.abs(qposf - (r0 + row).astype(F32))
        kt = kn_ref[pl.ds(r0, KEY_TILE), :]
        vt = vT_ref[t]
        s2 = [_dot(kt[:, p * 2 * DSA_HD:(p + 1) * 2 * DSA_HD], q_pairs[p]) for p in range(DSA_HEADS // 2)]
        for p in range(DSA_HEADS // 2):
            for e in range(2):
                h = 2 * p + e
                slope = 2.0 ** (-8.0 * (h + 1) / DSA_HEADS)
                s = jnp.where(sel, s2[p][:, e * Q_BLOCK:(e + 1) * Q_BLOCK] - slope * dist, NEG_BIG)
                m_old = m_scr[h:h + 1, :]
                m_new = jnp.maximum(m_old, jnp.max(s, axis=0, keepdims=True))
                alpha = jnp.exp(m_old - m_new)
                pr = jnp.exp(s - m_new)
                l_scr[h:h + 1, :] = alpha * l_scr[h:h + 1, :] + jnp.sum(pr, axis=0, keepdims=True)
                m_scr[h:h + 1, :] = m_new
                hs = slice(h * DSA_HD, (h + 1) * DSA_HD)
                acc_scr[hs, :] = alpha * acc_scr[hs, :] + _dot(vt[hs, :], pr.astype(BF16))
        return carry

    lax.fori_loop(0, n_tiles, attend_tile, 0)

    outs = [acc_scr[h * DSA_HD:(h + 1) * DSA_HD, :] / l_scr[h:h + 1, :] for h in range(DSA_HEADS)]
    o_ref[...] = jnp.concatenate(outs, axis=0).T.astype(o_ref.dtype)


def _dsa(qT, qiT, wT, kn, vT, rs, batch, seq):
    n = kn.shape[0]
    blocks = seq // Q_BLOCK
    tiles = seq // KEY_TILE
    topk = min(TOPK_MAX, seq // 4)
    col_map = lambda b, j: (0, b * blocks + j)
    return pl.pallas_call(
        functools.partial(_dsa_kernel, topk=topk, n_pos_bits=seq.bit_length()),
        grid=(batch, blocks),
        in_specs=[
            pl.BlockSpec((DSA_DIM, Q_BLOCK), col_map),
            pl.BlockSpec((IDX_Q_DIM, Q_BLOCK), col_map),
            pl.BlockSpec((IDX_HEADS, Q_BLOCK), col_map),
            pl.BlockSpec((seq, DSA_DIM), lambda b, j: (b, 0)),
            pl.BlockSpec((tiles, DSA_DIM, KEY_TILE), lambda b, j: (b, 0, 0)),
            pl.BlockSpec((seq, LANES), lambda b, j: (b, 0)),
        ],
        out_specs=pl.BlockSpec((Q_BLOCK, DSA_DIM), lambda b, j: (b * blocks + j, 0)),
        out_shape=jax.ShapeDtypeStruct((n, DSA_DIM), BF16),
        scratch_shapes=[
            pltpu.VMEM((seq, IDX_HD), BF16),
            pltpu.VMEM((tiles, KEY_TILE, Q_BLOCK), F32),
            pltpu.VMEM((1, Q_BLOCK), F32),
            pltpu.VMEM((SUBLANES, Q_BLOCK), I32),
            pltpu.VMEM((DSA_HEADS, Q_BLOCK), F32),
            pltpu.VMEM((DSA_HEADS, Q_BLOCK), F32),
            pltpu.VMEM((DSA_DIM, Q_BLOCK), F32),
        ],
        compiler_params=pltpu.CompilerParams(dimension_semantics=("arbitrary", "arbitrary"), vmem_limit_bytes=VMEM_LIMIT),
        name="dsa",
    )(qT, qiT, wT, kn, vT, rs)


def _memkv_kernel(mem_ref, g_ref, w_ref, bd_ref, gk_ref, k_ref, v_ref):
    h = _rms_rows(mem_ref[...], g_ref[...]).astype(BF16)
    kv = _dot(h, w_ref[...])
    k = kv[:, :XA_DIM]
    kms = _dot((k * k).astype(BF16), bd_ref[...]) * (1.0 / XA_HD)
    k_ref[...] = (k * lax.rsqrt(kms + EPS) * gk_ref[...]).astype(BF16)
    v_ref[...] = kv[:, XA_DIM:].astype(BF16)


def _memkv(mem2, g_mem, w_xkv, g_xk):
    n, d = mem2.shape
    tm = min(ROW_TILE, n)
    return pl.pallas_call(
        _memkv_kernel,
        grid=(n // tm,),
        in_specs=[
            pl.BlockSpec((tm, d), lambda i: (i, 0)),
            _const_spec((1, d)),
            _const_spec((d, 2 * XA_DIM)),
            _const_spec((XA_DIM, XA_DIM)),
            _const_spec((1, XA_DIM)),
        ],
        out_specs=(pl.BlockSpec((tm, XA_DIM), lambda i: (i, 0)), pl.BlockSpec((tm, XA_DIM), lambda i: (i, 0))),
        out_shape=(jax.ShapeDtypeStruct((n, XA_DIM), BF16), jax.ShapeDtypeStruct((n, XA_DIM), BF16)),
        compiler_params=pltpu.CompilerParams(dimension_semantics=("arbitrary",), vmem_limit_bytes=VMEM_LIMIT),
        name="memkv",
    )(mem2, g_mem[None, :], w_xkv.astype(BF16), _block_diag_ones(XA_DIM, XA_HD), jnp.tile(g_xk, XA_HEADS)[None, :])


def _tail_kernel(x_ref, ya_ref, yb_ref, wout_ref, gx_ref, wxq_ref, bd_ref, gxq_ref, km_ref, vm_ref, wxo_ref,
                 gf_ref, wgu_ref, wd_ref, o_ref, *, ff_chunks):
    x1 = x_ref[...] + _dot(ya_ref[...], wout_ref[0:GDN_VAL_DIM, :]) + _dot(yb_ref[...], wout_ref[GDN_VAL_DIM:, :])

    h = _rms_rows(x1, gx_ref[...]).astype(BF16)
    q = _dot(h, wxq_ref[...])
    qms = _dot((q * q).astype(BF16), bd_ref[...]) * (1.0 / XA_HD)
    qn = (q * lax.rsqrt(qms + EPS) * gxq_ref[...]).astype(BF16)
    heads = []
    for hd in range(XA_HEADS):
        hs = slice(hd * XA_HD, (hd + 1) * XA_HD)
        s = _dot_nt(qn[:, hs], km_ref[0, :, hs]) * (XA_HD ** -0.5)
        pr = jnp.exp(s - jnp.max(s, axis=-1, keepdims=True))
        o = _dot(pr.astype(BF16), vm_ref[0, :, hs]) / jnp.sum(pr, axis=-1, keepdims=True)
        heads.append(o.astype(BF16))
    x2 = x1 + _dot(jnp.concatenate(heads, axis=1), wxo_ref[...])

    h = _rms_rows(x2, gf_ref[...]).astype(BF16)
    d_ff = wd_ref.shape[0]
    fc = d_ff // ff_chunks
    acc = x2
    for c in range(ff_chunks):
        gate = _dot(h, wgu_ref[:, c * fc:(c + 1) * fc])
        up = _dot(h, wgu_ref[:, d_ff + c * fc:d_ff + (c + 1) * fc])
        a = (gate * jax.nn.sigmoid(gate) * up).astype(BF16)
        acc = acc + _dot(a, wd_ref[c * fc:(c + 1) * fc, :])
    o_ref[...] = acc


def _tail(x2, ya, yb, w_out, g_xattn, w_xq, g_xq, km, vm, w_xo, g_ffn, w_gu, w_down, seq):
    n, d = x2.shape
    tm = ROW_TILE
    d_ff = w_down.shape[0]
    ff_chunks = 2 if (d_ff // 2) % LANES == 0 else 1
    n_mem = km.shape[1]
    per_seq = seq // tm
    return pl.pallas_call(
        functools.partial(_tail_kernel, ff_chunks=ff_chunks),
        grid=(n // tm,),
        in_specs=[
            pl.BlockSpec((tm, d), lambda i: (i, 0)),
            pl.BlockSpec((tm, GDN_VAL_DIM), lambda i: (i, 0)),
            pl.BlockSpec((tm, DSA_DIM), lambda i: (i, 0)),
            _const_spec((GDN_VAL_DIM + DSA_DIM, d)),
            _const_spec((1, d)),
            _const_spec((d, XA_DIM)),
            _const_spec((XA_DIM, XA_DIM)),
            _const_spec((1, XA_DIM)),
            pl.BlockSpec((1, n_mem, XA_DIM), lambda i: (i // per_seq, 0, 0)),
            pl.BlockSpec((1, n_mem, XA_DIM), lambda i: (i // per_seq, 0, 0)),
            _const_spec((XA_DIM, d)),
            _const_spec((1, d)),
            _const_spec((d, 2 * d_ff)),
            _const_spec((d_ff, d)),
        ],
        out_specs=pl.BlockSpec((tm, d), lambda i: (i, 0)),
        out_shape=jax.ShapeDtypeStruct((n, d), F32),
        compiler_params=pltpu.CompilerParams(dimension_semantics=("arbitrary",), vmem_limit_bytes=VMEM_LIMIT),
        name="tail",
    )(x2, ya, yb, w_out.astype(BF16), g_xattn[None, :], w_xq.astype(BF16), _block_diag_ones(XA_DIM, XA_HD),
      jnp.tile(g_xq, XA_HEADS)[None, :], km, vm, w_xo.astype(BF16), g_ffn[None, :], w_gu.astype(BF16),
      w_down.astype(BF16))


def kernel(x, mem, g_mix, w_in, conv_w, a_log, dt_bias, g_gdn_out, g_q_dsa, g_k_dsa, w_out, g_xattn, g_mem, w_xq,
           w_xkv, g_xq, g_xk, w_xo, g_ffn, w_gu, w_down):
    batch, seq, d = x.shape
    n_mem = mem.shape[1]
    assert seq % ROW_TILE == 0 and seq % KEY_TILE == 0 and ROW_TILE % KEY_TILE == 0
    for l in range(g_mix.shape[0]):
        x2 = x.reshape(batch * seq, d)
        qkv, z, kn, rs, qT, vT, qiT, wT = _inproj(x2, g_mix[l], w_in[l], g_q_dsa[l], g_k_dsa[l])
        ya = _gdn(qkv, z, rs, conv_w[l], a_log[l], dt_bias[l], g_gdn_out[l], batch, seq)
        yb = _dsa(qT, qiT, wT, kn, vT, rs, batch, seq)
        km, vm = _memkv(mem.reshape(batch * n_mem, d), g_mem[l], w_xkv[l], g_xk[l])
        km = km.reshape(batch, n_mem, XA_DIM)
        vm = vm.reshape(batch, n_mem, XA_DIM)
        x = _tail(x2, ya, yb, w_out[l], g_xattn[l], w_xq[l], g_xq[l], km, vm, w_xo[l], g_ffn[l], w_gu[l],
                  w_down[l], seq).reshape(batch, seq, d)
    return x
```

```python
import functools

import jax
import jax.numpy as jnp
from jax import lax
from jax.experimental import pallas as pl
from jax.experimental.pallas import tpu as pltpu

F32 = jnp.float32
BF16 = jnp.bfloat16
I32 = jnp.int32

EPS = 1e-6
CHUNK = 64
GDN_HEADS, GDN_DK, GDN_DV, CONV_K = 4, 128, 128, 4
DSA_HEADS, DSA_HD = 8, 64
IDX_HEADS, IDX_HD = 16, 64
TOPK_MAX = 256
XA_HEADS, XA_HD = 4, 128

GDN_KEY_DIM = GDN_HEADS * GDN_DK
GDN_VAL_DIM = GDN_HEADS * GDN_DV
GDN_CONV_CH = 2 * GDN_KEY_DIM + GDN_VAL_DIM
DSA_DIM = DSA_HEADS * DSA_HD
IDX_Q_DIM = IDX_HEADS * IDX_HD
XA_DIM = XA_HEADS * XA_HD

LANES = 128
SUBLANES = 8
VMEM_LIMIT = 56 * 1024 * 1024

RS_IDXK = 0
RS_BETA = IDX_HD
RS_A = IDX_HD + GDN_HEADS

ROW_TILE = 512
KEY_TILE = 256
Q_BLOCK = 128
GDN_CHUNKS_PER_STEP = 4

BISECT_ROUNDS = 18
NEG_BIG = -1e30


def _dot(a, b):
    return jnp.dot(a, b, preferred_element_type=F32)


def _dot_nt(a, b):
    return lax.dot_general(a, b, (((1,), (1,)), ((), ())), preferred_element_type=F32)


def _dot_f32(a, b):
    return jnp.dot(a, b, preferred_element_type=F32, precision=lax.Precision.HIGHEST)


def _rms_rows(x, g):
    ms = jnp.mean(x * x, axis=-1, keepdims=True)
    return x * lax.rsqrt(ms + EPS) * g


def _const_spec(shape):
    nd = len(shape)
    return pl.BlockSpec(shape, lambda *_: (0,) * nd, pipeline_mode=pl.Buffered(1))


def _block_diag_ones(n, blk):
    r = lax.broadcasted_iota(I32, (n, n), 0) // blk
    c = lax.broadcasted_iota(I32, (n, n), 1) // blk
    return (r == c).astype(BF16)


def _inproj_kernel(x_ref, g_ref, wrow_ref, wcol_ref, bd_ref, gk_ref, gq_ref,
                   qkv_ref, z_ref, kn_ref, rs_ref, qT_ref, vT_ref, qiT_ref, wT_ref):
    h = _rms_rows(x_ref[...], g_ref[...]).astype(BF16)
    c0, c1, c2, c3 = GDN_CONV_CH, GDN_CONV_CH + GDN_VAL_DIM, GDN_CONV_CH + GDN_VAL_DIM + DSA_DIM, \
        GDN_CONV_CH + GDN_VAL_DIM + DSA_DIM + LANES
    qkv_ref[...] = _dot(h, wrow_ref[:, 0:c0])
    z_ref[...] = _dot(h, wrow_ref[:, c0:c1])
    k = _dot(h, wrow_ref[:, c1:c2])
    kms = _dot((k * k).astype(BF16), bd_ref[...]) * (1.0 / DSA_HD)
    kn_ref[...] = (k * lax.rsqrt(kms + EPS) * gk_ref[...]).astype(BF16)
    rs_ref[...] = _dot(h, wrow_ref[:, c1 + DSA_DIM:c3])

    r0, r1, r2, r3 = DSA_DIM, 2 * DSA_DIM, 2 * DSA_DIM + IDX_Q_DIM, 2 * DSA_DIM + IDX_Q_DIM + IDX_HEADS
    qT = _dot_nt(wcol_ref[0:r0, :], h)
    qms = _dot(bd_ref[...], (qT * qT).astype(BF16)) * (1.0 / DSA_HD)
    qT_ref[...] = (qT * lax.rsqrt(qms + EPS) * gq_ref[...] * (DSA_HD ** -0.5)).astype(BF16)
    vT = _dot_nt(wcol_ref[r0:r1, :], h).astype(BF16)
    for i in range(vT_ref.shape[0]):
        vT_ref[i] = vT[:, i * KEY_TILE:(i + 1) * KEY_TILE]
    qiT_ref[...] = (_dot_nt(wcol_ref[r1:r2, :], h) * (IDX_HD ** -0.5)).astype(BF16)
    wT_ref[...] = _dot_nt(wcol_ref[r2:r3, :], h) * (IDX_HEADS ** -0.5)


def _inproj(x2, g_mix, w_in, g_q, g_k):
    n, d = x2.shape
    tm = ROW_TILE
    sizes = (GDN_CONV_CH, GDN_VAL_DIM, GDN_HEADS, GDN_HEADS, DSA_DIM, DSA_DIM, DSA_DIM, IDX_Q_DIM, IDX_HD, IDX_HEADS)
    offs = [0]
    for s in sizes:
        offs.append(offs[-1] + s)
    (w_qkv, w_z, w_b, w_a, w_q, w_k, w_v, w_iq, w_ik, w_iw) = [w_in[:, offs[i]:offs[i + 1]] for i in range(len(sizes))]
    pad = jnp.zeros((d, LANES - IDX_HD - 2 * GDN_HEADS), w_in.dtype)
    w_row = jnp.concatenate([w_qkv, w_z, w_k, w_ik, w_b, w_a, pad], axis=1).astype(BF16)
    w_col = jnp.concatenate([w_q, w_v, w_iq, w_iw], axis=1).T.astype(BF16)
    bd = _block_diag_ones(DSA_DIM, DSA_HD)
    gk_row = jnp.tile(g_k, DSA_HEADS)[None, :]
    gq_col = jnp.tile(g_q, DSA_HEADS)[:, None]
    nrow, ncol = w_row.shape[1], w_col.shape[0]
    out_shape = (
        jax.ShapeDtypeStruct((n, GDN_CONV_CH), F32),
        jax.ShapeDtypeStruct((n, GDN_VAL_DIM), F32),
        jax.ShapeDtypeStruct((n, DSA_DIM), BF16),
        jax.ShapeDtypeStruct((n, LANES), F32),
        jax.ShapeDtypeStruct((DSA_DIM, n), BF16),
        jax.ShapeDtypeStruct((n // KEY_TILE, DSA_DIM, KEY_TILE), BF16),
        jax.ShapeDtypeStruct((IDX_Q_DIM, n), BF16),
        jax.ShapeDtypeStruct((IDX_HEADS, n), F32),
    )
    return pl.pallas_call(
        _inproj_kernel,
        grid=(n // tm,),
        in_specs=[
            pl.BlockSpec((tm, d), lambda i: (i, 0)),
            _const_spec((1, d)),
            _const_spec((d, nrow)),
            _const_spec((ncol, d)),
            _const_spec((DSA_DIM, DSA_DIM)),
            _const_spec((1, DSA_DIM)),
            _const_spec((DSA_DIM, 1)),
        ],
        out_specs=(
            pl.BlockSpec((tm, GDN_CONV_CH), lambda i: (i, 0)),
            pl.BlockSpec((tm, GDN_VAL_DIM), lambda i: (i, 0)),
            pl.BlockSpec((tm, DSA_DIM), lambda i: (i, 0)),
            pl.BlockSpec((tm, LANES), lambda i: (i, 0)),
            pl.BlockSpec((DSA_DIM, tm), lambda i: (0, i)),
            pl.BlockSpec((tm // KEY_TILE, DSA_DIM, KEY_TILE), lambda i: (i, 0, 0)),
            pl.BlockSpec((IDX_Q_DIM, tm), lambda i: (0, i)),
            pl.BlockSpec((IDX_HEADS, tm), lambda i: (0, i)),
        ),
        out_shape=out_shape,
        compiler_params=pltpu.CompilerParams(dimension_semantics=("arbitrary",), vmem_limit_bytes=VMEM_LIMIT),
        name="inproj",
    )(x2, g_mix[None, :], w_row, w_col, bd, gk_row, gq_col)


def _gdn_kernel(qkv_ref, z_ref, rs_ref, cw_ref, alog_ref, dtb_ref, gout_ref, y_ref, xc_ref, s_ref):
    rows = qkv_ref.shape[0]
    nc = rows // CHUNK

    @pl.when(pl.program_id(1) == 0)
    def _start_of_sequence():
        xc_ref[0:SUBLANES, :] = jnp.zeros((SUBLANES, GDN_CONV_CH), F32)
        s_ref[...] = jnp.zeros(s_ref.shape, F32)

    xc_ref[SUBLANES:SUBLANES + rows, :] = qkv_ref[...]
    cw = cw_ref[...]
    conv = cw[0:1, :] * xc_ref[SUBLANES - 3:SUBLANES - 3 + rows, :]
    for j in range(1, CONV_K):
        conv = conv + cw[j:j + 1, :] * xc_ref[SUBLANES - 3 + j:SUBLANES - 3 + j + rows, :]
    xc_ref[0:SUBLANES, :] = xc_ref[rows:rows + SUBLANES, :]
    act = conv * jax.nn.sigmoid(conv)

    rs = rs_ref[...]
    beta_all = jax.nn.sigmoid(rs)
    sp_in = rs + dtb_ref[...]
    softplus = jnp.maximum(sp_in, 0.0) + jnp.log(1.0 + jnp.exp(-jnp.abs(sp_in)))
    g_all = -jnp.exp(alog_ref[...]) * softplus

    ri = lax.broadcasted_iota(I32, (CHUNK, CHUNK), 0)
    ci = lax.broadcasted_iota(I32, (CHUNK, CHUNK), 1)
    incl = ri >= ci
    strict = ri > ci
    ltri = incl.astype(F32)
    eye = (ri == ci).astype(F32)
    gout = gout_ref[...]

    chains = [(c, h) for c in range(nc) for h in range(GDN_HEADS)]
    dcum, dcum_t = [], []
    for c in range(nc):
        d = _dot_f32(ltri, g_all[c * CHUNK:(c + 1) * CHUNK, :])
        dcum.append(d)
        dcum_t.append(d.T)

    q16, k16, kb16, vb16, kw16, qdec16, kdect16, gamma, last = ([] for _ in range(9))
    for c, h in chains:
        r = slice(c * CHUNK, (c + 1) * CHUNK)
        q = act[r, h * GDN_DK:(h + 1) * GDN_DK]
        k = act[r, GDN_KEY_DIM + h * GDN_DK:GDN_KEY_DIM + (h + 1) * GDN_DK]
        v = act[r, 2 * GDN_KEY_DIM + h * GDN_DV:2 * GDN_KEY_DIM + (h + 1) * GDN_DV]
        q = q * lax.rsqrt(jnp.sum(q * q, axis=-1, keepdims=True) + EPS) * (GDN_DK ** -0.5)
        k = k * lax.rsqrt(jnp.sum(k * k, axis=-1, keepdims=True) + EPS)
        beta = beta_all[r, RS_BETA + h:RS_BETA + h + 1]
        d_col = dcum[c][:, RS_A + h:RS_A + h + 1]
        d_row = dcum_t[c][RS_A + h:RS_A + h + 1, :]
        d_last = dcum[c][CHUNK - 1:CHUNK, RS_A + h:RS_A + h + 1]
        e_col = jnp.exp(d_col)
        kb = k * beta
        gamma.append(jnp.exp(jnp.where(incl, d_col - d_row, -jnp.inf)))
        q16.append(q.astype(BF16))
        k16.append(k.astype(BF16))
        kb16.append(kb.astype(BF16))
        vb16.append((v * beta).astype(BF16))
        kw16.append((kb * e_col).astype(BF16))
        qdec16.append((q * e_col).astype(BF16))
        kdect16.append((k * jnp.exp(d_last - d_col)).T.astype(BF16))
        last.append(jnp.exp(d_last))

    n = len(chains)
    p16 = [(-jnp.where(strict, _dot_nt(kb16[i], k16[i]) * gamma[i], 0.0)).astype(BF16) for i in range(n)]
    qk16 = [(_dot_nt(q16[i], k16[i]) * gamma[i]).astype(BF16) for i in range(n)]
    t_mat = [eye + p16[i].astype(F32) for i in range(n)]
    p16 = [_dot(p16[i], p16[i]).astype(BF16) for i in range(n)]
    for _ in range(4):
        t_mat = [t_mat[i] + _dot(t_mat[i].astype(BF16), p16[i]) for i in range(n)]
        p16 = [_dot(p16[i], p16[i]).astype(BF16) for i in range(n)]
    t16 = [(t_mat[i] + _dot(t_mat[i].astype(BF16), p16[i])).astype(BF16) for i in range(n)]
    u = [_dot(t16[i], vb16[i]) for i in range(n)]
    w16 = [_dot(t16[i], kw16[i]).astype(BF16) for i in range(n)]

    s = [s_ref[h] for h in range(GDN_HEADS)]
    for c in range(nc):
        ids = [c * GDN_HEADS + h for h in range(GDN_HEADS)]
        s16 = [s[h].astype(BF16) for h in range(GDN_HEADS)]
        v_new16 = [(u[i] - _dot(w16[i], s16[h])).astype(BF16) for h, i in enumerate(ids)]
        o_state = [_dot(qdec16[i], s16[h]) for h, i in enumerate(ids)]
        o = [o_state[h] + _dot(qk16[i], v_new16[h]) for h, i in enumerate(ids)]
        s = [s[h] * last[i] + _dot(kdect16[i], v_new16[h]) for h, i in enumerate(ids)]
        for h in range(GDN_HEADS):
            zc = z_ref[c * CHUNK:(c + 1) * CHUNK, h * GDN_DV:(h + 1) * GDN_DV]
            y = _rms_rows(o[h], gout) * (zc * jax.nn.sigmoid(zc))
            y_ref[c * CHUNK:(c + 1) * CHUNK, h * GDN_DV:(h + 1) * GDN_DV] = y.astype(y_ref.dtype)
    for h in range(GDN_HEADS):
        s_ref[h] = s[h]


def _gdn(qkv, z, rs, conv_w, a_log, dt_bias, g_out, batch, seq):
    n = qkv.shape[0]
    rows = GDN_CHUNKS_PER_STEP * CHUNK
    steps = seq // rows
    lane_vec = lambda v: jnp.zeros((1, LANES), F32).at[0, RS_A:RS_A + GDN_HEADS].set(v)
    row_map = lambda b, i: (b * steps + i, 0)
    return pl.pallas_call(
        _gdn_kernel,
        grid=(batch, steps),
        in_specs=[
            pl.BlockSpec((rows, GDN_CONV_CH), row_map),
            pl.BlockSpec((rows, GDN_VAL_DIM), row_map),
            pl.BlockSpec((rows, LANES), row_map),
            _const_spec((CONV_K, GDN_CONV_CH)),
            _const_spec((1, LANES)),
            _const_spec((1, LANES)),
            _const_spec((1, GDN_DV)),
        ],
        out_specs=pl.BlockSpec((rows, GDN_VAL_DIM), row_map),
        out_shape=jax.ShapeDtypeStruct((n, GDN_VAL_DIM), BF16),
        scratch_shapes=[
            pltpu.VMEM((rows + SUBLANES, GDN_CONV_CH), F32),
            pltpu.VMEM((GDN_HEADS, GDN_DK, GDN_DV), F32),
        ],
        compiler_params=pltpu.CompilerParams(dimension_semantics=("arbitrary", "arbitrary"), vmem_limit_bytes=VMEM_LIMIT),
        name="gdn",
    )(qkv, z, rs, conv_w, lane_vec(a_log), lane_vec(dt_bias), g_out[None, :])


def _alibi_slope(h):
    return 2.0 ** (-8.0 * (h + 1) / DSA_HEADS)


def _dsa_kernel(qT_ref, qiT_ref, wT_ref, kn_ref, vT_ref, rs_ref, kpos_ref, slope_ref, o_ref,
                idxk_scr, sc_scr, thr_scr, cnt_scr, acc_scr, *, topk, n_pos_bits):
    j = pl.program_id(1)

    @pl.when(j == 0)
    def _new_sequence():
        idxk_scr[...] = rs_ref[:, RS_IDXK:RS_IDXK + IDX_HD].astype(BF16)

    n_tiles = (j * Q_BLOCK + Q_BLOCK + KEY_TILE - 1) // KEY_TILE
    lane = lax.broadcasted_iota(I32, (1, Q_BLOCK), 1)
    qpos = j * Q_BLOCK + lane
    key_limit = ((qpos >> 6) + 1) << 6
    row = lax.broadcasted_iota(I32, (KEY_TILE, Q_BLOCK), 0)

    w_t = wT_ref[...]
    qi_pairs = [jnp.concatenate([qiT_ref[(2 * p) * IDX_HD:(2 * p + 1) * IDX_HD, :],
                                 qiT_ref[(2 * p + 1) * IDX_HD:(2 * p + 2) * IDX_HD, :]], axis=1)
                for p in range(IDX_HEADS // 2)]

    def score_tile(t, carry):
        s_max, s_min = carry
        r0 = pl.multiple_of(t * KEY_TILE, KEY_TILE)
        kt = idxk_scr[pl.ds(r0, KEY_TILE), :]
        acc = jnp.zeros((KEY_TILE, Q_BLOCK), F32)
        for p in range(IDX_HEADS // 2):
            d = _dot(kt, qi_pairs[p])
            acc = acc + jnp.maximum(d[:, :Q_BLOCK], 0.0) * w_t[2 * p:2 * p + 1, :]
            acc = acc + jnp.maximum(d[:, Q_BLOCK:], 0.0) * w_t[2 * p + 1:2 * p + 2, :]
        admissible = r0 + row < key_limit
        stored = jnp.where(admissible, acc, -jnp.inf)
        sc_scr[t] = stored
        s_max = jnp.maximum(s_max, jnp.max(stored, axis=0, keepdims=True))
        s_min = jnp.minimum(s_min, jnp.min(jnp.where(admissible, acc, jnp.inf), axis=0, keepdims=True))
        return s_max, s_min

    s_max, s_min = lax.fori_loop(0, n_tiles, score_tile, (jnp.full((1, Q_BLOCK), -jnp.inf, F32),
                                                          jnp.full((1, Q_BLOCK), jnp.inf, F32)))

    def count_ge(thr):
        cnt_scr[...] = jnp.zeros(cnt_scr.shape, I32)
        base = jnp.int32(0)
        arm = 1 << (sc_scr.shape[0].bit_length() - 1)
        while arm >= 1:
            has = (n_tiles & arm) != 0

            @pl.when(has)
            def _run(base=base, arm=arm):
                parts = [jnp.zeros((SUBLANES, Q_BLOCK), I32) for _ in range(4)]
                for i in range(arm):
                    ge = jnp.where(sc_scr[base + i] >= thr, 1, 0)
                    for r in range(KEY_TILE // SUBLANES):
                        parts[r % 4] = parts[r % 4] + ge[r * SUBLANES:(r + 1) * SUBLANES, :]
                cnt_scr[...] += (parts[0] + parts[1]) + (parts[2] + parts[3])

            base = base + jnp.where(has, arm, 0)
            arm //= 2
        return jnp.sum(cnt_scr[...], axis=0, keepdims=True)

    def bisect_step(_, st):
        lo, hi, cnt_lo = st
        mid = lo * 0.5 + hi * 0.5
        cnt = count_ge(mid)
        take = cnt >= topk
        return jnp.where(take, mid, lo), jnp.where(take, hi, mid), jnp.where(take, cnt, cnt_lo)

    above_max = s_max + jnp.maximum(jnp.abs(s_max) * 1e-6, 1e-30)
    lo, _, cnt = lax.fori_loop(0, BISECT_ROUNDS, bisect_step, (s_min, above_max, key_limit))
    few = key_limit < topk
    thr_scr[...] = jnp.where(few, jnp.finfo(F32).min, lo)
    cnt = jnp.where(few, 0, cnt)

    @pl.when(jnp.max(cnt) > topk)
    def _resolve_ties():
        def min_above(thr):
            def tile(t, m):
                sc = sc_scr[t]
                return jnp.minimum(m, jnp.min(jnp.where(sc > thr, sc, jnp.inf), axis=0, keepdims=True))
            return lax.fori_loop(0, n_tiles, tile, jnp.full((1, Q_BLOCK), jnp.inf, F32))

        def count_eq_below(thr, pos_limit):
            def tile(t, c):
                r0 = t * KEY_TILE
                hit = (sc_scr[t] == thr) & (r0 + row < pos_limit)
                return c + jnp.sum(hit.astype(I32), axis=0, keepdims=True)
            return lax.fori_loop(0, n_tiles, tile, jnp.zeros((1, Q_BLOCK), I32))

        def body(st):
            thr, cnt = st
            nxt = min_above(thr)
            cnt_n = count_ge(nxt)
            active = cnt > topk
            advance = active & (cnt_n >= topk)
            tie = active & (cnt_n < topk)
            @pl.when(jnp.max(tie.astype(I32)) > 0)
            def _drop_surplus_ties():
                need = topk - cnt_n
                pos = jnp.zeros((1, Q_BLOCK), I32)
                for b in range(n_pos_bits - 1, -1, -1):
                    cand = pos + (1 << b)
                    pos = jnp.where(count_eq_below(thr, cand) < need, cand, pos)

                def drop_tile(t, carry):
                    r0 = t * KEY_TILE
                    sc = sc_scr[t]
                    sc_scr[t] = jnp.where(tie & (sc == thr) & (r0 + row > pos), -jnp.inf, sc)
                    return carry

                lax.fori_loop(0, n_tiles, drop_tile, 0)

            return jnp.where(advance, nxt, thr), jnp.where(advance, cnt_n, jnp.where(tie, topk, cnt))

        thr, _ = lax.while_loop(lambda st: jnp.max(st[1]) > topk, body, (thr_scr[...], cnt))
        thr_scr[...] = thr

    thr = thr_scr[...]

    zero = jnp.zeros((DSA_HD, Q_BLOCK), BF16)
    q_pairs = []
    for p in range(DSA_HEADS // 2):
        a = qT_ref[(2 * p) * DSA_HD:(2 * p + 1) * DSA_HD, :]
        b = qT_ref[(2 * p + 1) * DSA_HD:(2 * p + 2) * DSA_HD, :]
        q_pairs.append(jnp.concatenate([jnp.concatenate([a, zero], axis=1),
                                        jnp.concatenate([zero, b], axis=1),
                                        slope_ref[p]], axis=0))
    acc_scr[...] = jnp.zeros(acc_scr.shape, F32)
    qposf = qpos.astype(F32)

    def attend_tile(t, ml, last_tile):
        r0 = pl.multiple_of(t * KEY_TILE, KEY_TILE)
        sel = sc_scr[t] >= thr
        kt = kn_ref[pl.ds(r0, KEY_TILE), :]
        kp = kpos_ref[pl.ds(r0, KEY_TILE), :]
        vt = vT_ref[t]
        s2 = [_dot(jnp.concatenate([kt[:, p * 2 * DSA_HD:(p + 1) * 2 * DSA_HD], kp], axis=1), q_pairs[p])
              for p in range(DSA_HEADS // 2)]
        if last_tile:
            after = jnp.maximum((r0 + row).astype(F32) - qposf, 0.0)
        m_all, l_all = ml
        m_rows, l_rows = [], []
        for h in range(DSA_HEADS):
            s = s2[h // 2][:, (h % 2) * Q_BLOCK:(h % 2 + 1) * Q_BLOCK]
            if last_tile:
                s = s - (2.0 * _alibi_slope(h)) * after
            s = jnp.where(sel, s, NEG_BIG)
            m_old = m_all[h:h + 1, :]
            m_new = jnp.maximum(m_old, jnp.max(s, axis=0, keepdims=True))
            alpha = jnp.exp(m_old - m_new)
            pr = jnp.exp(s - m_new)
            l_rows.append(alpha * l_all[h:h + 1, :] + jnp.sum(pr, axis=0, keepdims=True))
            m_rows.append(m_new)
            hs = slice(h * DSA_HD, (h + 1) * DSA_HD)
            acc_scr[hs, :] = alpha * acc_scr[hs, :] + _dot(vt[hs, :], pr.astype(BF16))
        return jnp.concatenate(m_rows, axis=0), jnp.concatenate(l_rows, axis=0)

    ml = (jnp.full((DSA_HEADS, Q_BLOCK), NEG_BIG, F32), jnp.zeros((DSA_HEADS, Q_BLOCK), F32))
    ml = lax.fori_loop(0, n_tiles - 1, lambda t, ml: attend_tile(t, ml, last_tile=False), ml)
    _, l_all = attend_tile(n_tiles - 1, ml, last_tile=True)

    outs = [acc_scr[h * DSA_HD:(h + 1) * DSA_HD, :] / l_all[h:h + 1, :] for h in range(DSA_HEADS)]
    o_ref[...] = jnp.concatenate(outs, axis=0).T.astype(o_ref.dtype)


def _dsa(qT, qiT, wT, kn, vT, rs, batch, seq):
    n = kn.shape[0]
    blocks = seq // Q_BLOCK
    tiles = seq // KEY_TILE
    topk = min(TOPK_MAX, seq // 4)
    col_map = lambda b, j: (0, b * blocks + j)
    pos = jnp.arange(seq, dtype=I32)
    kpos = jnp.zeros((seq, LANES), F32).at[:, 0].set((pos >> 4).astype(F32)).at[:, 1].set((pos & 15).astype(F32))
    slopes = jnp.asarray([_alibi_slope(h) for h in range(DSA_HEADS)], F32).reshape(DSA_HEADS // 2, 2)
    slope_cols = jnp.repeat(slopes, Q_BLOCK, axis=1)
    slope_rows = jnp.zeros((DSA_HEADS // 2, LANES, 2 * Q_BLOCK), F32)
    slope_rows = slope_rows.at[:, 0, :].set(16.0 * slope_cols).at[:, 1, :].set(slope_cols)
    return pl.pallas_call(
        functools.partial(_dsa_kernel, topk=topk, n_pos_bits=seq.bit_length()),
        grid=(batch, blocks),
        in_specs=[
            pl.BlockSpec((DSA_DIM, Q_BLOCK), col_map),
            pl.BlockSpec((IDX_Q_DIM, Q_BLOCK), col_map),
            pl.BlockSpec((IDX_HEADS, Q_BLOCK), col_map),
            pl.BlockSpec((seq, DSA_DIM), lambda b, j: (b, 0)),
            pl.BlockSpec((tiles, DSA_DIM, KEY_TILE), lambda b, j: (b, 0, 0)),
            pl.BlockSpec((seq, LANES), lambda b, j: (b, 0)),
            _const_spec((seq, LANES)),
            _const_spec((DSA_HEADS // 2, LANES, 2 * Q_BLOCK)),
        ],
        out_specs=pl.BlockSpec((Q_BLOCK, DSA_DIM), lambda b, j: (b * blocks + j, 0)),
        out_shape=jax.ShapeDtypeStruct((n, DSA_DIM), BF16),
        scratch_shapes=[
            pltpu.VMEM((seq, IDX_HD), BF16),
            pltpu.VMEM((tiles, KEY_TILE, Q_BLOCK), F32),
            pltpu.VMEM((1, Q_BLOCK), F32),
            pltpu.VMEM((SUBLANES, Q_BLOCK), I32),
            pltpu.VMEM((DSA_DIM, Q_BLOCK), F32),
        ],
        compiler_params=pltpu.CompilerParams(dimension_semantics=("arbitrary", "arbitrary"), vmem_limit_bytes=VMEM_LIMIT),
        name="dsa",
    )(qT, qiT, wT, kn, vT, rs, kpos.astype(BF16), slope_rows.astype(BF16))


def _memkv_kernel(mem_ref, g_ref, w_ref, bd_ref, gk_ref, k_ref, v_ref):
    h = _rms_rows(mem_ref[...], g_ref[...]).astype(BF16)
    kv = _dot(h, w_ref[...])
    k = kv[:, :XA_DIM]
    kms = _dot((k * k).astype(BF16), bd_ref[...]) * (1.0 / XA_HD)
    k_ref[...] = (k * lax.rsqrt(kms + EPS) * gk_ref[...]).astype(BF16)
    v_ref[...] = kv[:, XA_DIM:].astype(BF16)


def _memkv(mem2, g_mem, w_xkv, g_xk):
    n, d = mem2.shape
    tm = min(ROW_TILE, n)
    return pl.pallas_call(
        _memkv_kernel,
        grid=(n // tm,),
        in_specs=[
            pl.BlockSpec((tm, d), lambda i: (i, 0)),
            _const_spec((1, d)),
            _const_spec((d, 2 * XA_DIM)),
            _const_spec((XA_DIM, XA_DIM)),
            _const_spec((1, XA_DIM)),
        ],
        out_specs=(pl.BlockSpec((tm, XA_DIM), lambda i: (i, 0)), pl.BlockSpec((tm, XA_DIM), lambda i: (i, 0))),
        out_shape=(jax.ShapeDtypeStruct((n, XA_DIM), BF16), jax.ShapeDtypeStruct((n, XA_DIM), BF16)),
        compiler_params=pltpu.CompilerParams(dimension_semantics=("arbitrary",), vmem_limit_bytes=VMEM_LIMIT),
        name="memkv",
    )(mem2, g_mem[None, :], w_xkv.astype(BF16), _block_diag_ones(XA_DIM, XA_HD), jnp.tile(g_xk, XA_HEADS)[None, :])


def _tail_kernel(x_ref, ya_ref, yb_ref, wout_ref, gx_ref, wxq_ref, bd_ref, gxq_ref, km_ref, vm_ref, wxo_ref,
                 gf_ref, wgu_ref, wd_ref, o_ref, *, ff_chunks):
    x1 = x_ref[...] + _dot(ya_ref[...], wout_ref[0:GDN_VAL_DIM, :]) + _dot(yb_ref[...], wout_ref[GDN_VAL_DIM:, :])

    h = _rms_rows(x1, gx_ref[...]).astype(BF16)
    q = _dot(h, wxq_ref[...])
    qms = _dot((q * q).astype(BF16), bd_ref[...]) * (1.0 / XA_HD)
    qn = (q * lax.rsqrt(qms + EPS) * gxq_ref[...]).astype(BF16)
    heads = []
    for hd in range(XA_HEADS):
        hs = slice(hd * XA_HD, (hd + 1) * XA_HD)
        s = _dot_nt(qn[:, hs], km_ref[0, :, hs]) * (XA_HD ** -0.5)
        pr = jnp.exp(s - jnp.max(s, axis=-1, keepdims=True))
        o = _dot(pr.astype(BF16), vm_ref[0, :, hs]) / jnp.sum(pr, axis=-1, keepdims=True)
        heads.append(o.astype(BF16))
    x2 = x1 + _dot(jnp.concatenate(heads, axis=1), wxo_ref[...])

    h = _rms_rows(x2, gf_ref[...]).astype(BF16)
    d_ff = wd_ref.shape[0]
    fc = d_ff // ff_chunks
    acc = x2
    for c in range(ff_chunks):
        gate = _dot(h, wgu_ref[:, c * fc:(c + 1) * fc])
        up = _dot(h, wgu_ref[:, d_ff + c * fc:d_ff + (c + 1) * fc])
        a = (gate * jax.nn.sigmoid(gate) * up).astype(BF16)
        acc = acc + _dot(a, wd_ref[c * fc:(c + 1) * fc, :])
    o_ref[...] = acc


def _tail(x2, ya, yb, w_out, g_xattn, w_xq, g_xq, km, vm, w_xo, g_ffn, w_gu, w_down, seq):
    n, d = x2.shape
    tm = ROW_TILE
    d_ff = w_down.shape[0]
    ff_chunks = 2 if (d_ff // 2) % LANES == 0 else 1
    n_mem = km.shape[1]
    per_seq = seq // tm
    return pl.pallas_call(
        functools.partial(_tail_kernel, ff_chunks=ff_chunks),
        grid=(n // tm,),
        in_specs=[
            pl.BlockSpec((tm, d), lambda i: (i, 0)),
            pl.BlockSpec((tm, GDN_VAL_DIM), lambda i: (i, 0)),
            pl.BlockSpec((tm, DSA_DIM), lambda i: (i, 0)),
            _const_spec((GDN_VAL_DIM + DSA_DIM, d)),
            _const_spec((1, d)),
            _const_spec((d, XA_DIM)),
            _const_spec((XA_DIM, XA_DIM)),
            _const_spec((1, XA_DIM)),
            pl.BlockSpec((1, n_mem, XA_DIM), lambda i: (i // per_seq, 0, 0)),
            pl.BlockSpec((1, n_mem, XA_DIM), lambda i: (i // per_seq, 0, 0)),
            _const_spec((XA_DIM, d)),
            _const_spec((1, d)),
            _const_spec((d, 2 * d_ff)),
            _const_spec((d_ff, d)),
        ],
        out_specs=pl.BlockSpec((tm, d), lambda i: (i, 0)),
        out_shape=jax.ShapeDtypeStruct((n, d), F32),
        compiler_params=pltpu.CompilerParams(dimension_semantics=("arbitrary",), vmem_limit_bytes=VMEM_LIMIT),
        name="tail",
    )(x2, ya, yb, w_out.astype(BF16), g_xattn[None, :], w_xq.astype(BF16), _block_diag_ones(XA_DIM, XA_HD),
      jnp.tile(g_xq, XA_HEADS)[None, :], km, vm, w_xo.astype(BF16), g_ffn[None, :], w_gu.astype(BF16),
      w_down.astype(BF16))


def kernel(x, mem, g_mix, w_in, conv_w, a_log, dt_bias, g_gdn_out, g_q_dsa, g_k_dsa, w_out, g_xattn, g_mem, w_xq,
           w_xkv, g_xq, g_xk, w_xo, g_ffn, w_gu, w_down):
    batch, seq, d = x.shape
    n_mem = mem.shape[1]
    assert seq % ROW_TILE == 0 and seq % KEY_TILE == 0 and ROW_TILE % KEY_TILE == 0
    for l in range(g_mix.shape[0]):
        x2 = x.reshape(batch * seq, d)
        qkv, z, kn, rs, qT, vT, qiT, wT = _inproj(x2, g_mix[l], w_in[l], g_q_dsa[l], g_k_dsa[l])
        ya = _gdn(qkv, z, rs, conv_w[l], a_log[l], dt_bias[l], g_gdn_out[l], batch, seq)
        yb = _dsa(qT, qiT, wT, kn, vT, rs, batch, seq)
        km, vm = _memkv(mem.reshape(batch * n_mem, d), g_mem[l], w_xkv[l], g_xk[l])
        km = km.reshape(batch, n_mem, XA_DIM)
        vm = vm.reshape(batch, n_mem, XA_DIM)
        x = _tail(x2, ya, yb, w_out[l], g_xattn[l], w_xq[l], g_xq[l], km, vm, w_xo[l], g_ffn[l], w_gu[l],
                  w_down[l], seq).reshape(batch, seq, d)
    return x
```

```python
import functools

import jax
import jax.numpy as jnp
from jax import lax
from jax.experimental import pallas as pl
from jax.experimental.pallas import tpu as pltpu

F32 = jnp.float32
BF16 = jnp.bfloat16
I32 = jnp.int32

EPS = 1e-6
CHUNK = 64
GDN_HEADS, GDN_DK, GDN_DV, CONV_K = 4, 128, 128, 4
DSA_HEADS, DSA_HD = 8, 64
IDX_HEADS, IDX_HD = 16, 64
TOPK_MAX = 256
XA_HEADS, XA_HD = 4, 128

GDN_KEY_DIM = GDN_HEADS * GDN_DK
GDN_VAL_DIM = GDN_HEADS * GDN_DV
GDN_CONV_CH = 2 * GDN_KEY_DIM + GDN_VAL_DIM
DSA_DIM = DSA_HEADS * DSA_HD
IDX_Q_DIM = IDX_HEADS * IDX_HD
XA_DIM = XA_HEADS * XA_HD

LANES = 128
SUBLANES = 8
VMEM_LIMIT = 56 * 1024 * 1024

RS_IDXK = 0
RS_BETA = IDX_HD
RS_A = IDX_HD + GDN_HEADS

ROW_TILE = 512
KEY_TILE = 256
Q_BLOCK = 128
GDN_CHUNKS_PER_STEP = 4

BISECT_ROUNDS = 20
COUNT_CHAINS = 4
NEG_BIG = -1e30


def _dot(a, b):
    return jnp.dot(a, b, preferred_element_type=F32)


def _dot_nt(a, b):
    return lax.dot_general(a, b, (((1,), (1,)), ((), ())), preferred_element_type=F32)


def _dot_f32(a, b):
    return jnp.dot(a, b, preferred_element_type=F32, precision=lax.Precision.HIGHEST)


def _rms_rows(x, g):
    ms = jnp.mean(x * x, axis=-1, keepdims=True)
    return x * lax.rsqrt(ms + EPS) * g


def _const_spec(shape):
    nd = len(shape)
    return pl.BlockSpec(shape, lambda *_: (0,) * nd, pipeline_mode=pl.Buffered(1))


def _block_diag_ones(n, blk):
    r = lax.broadcasted_iota(I32, (n, n), 0) // blk
    c = lax.broadcasted_iota(I32, (n, n), 1) // blk
    return (r == c).astype(BF16)


def _inproj_kernel(x_ref, g_ref, wrow_ref, wcol_ref, bd_ref, gk_ref, gq_ref,
                   qkv_ref, z_ref, kn_ref, rs_ref, qT_ref, vT_ref, qiT_ref, wT_ref):
    h = _rms_rows(x_ref[...], g_ref[...]).astype(BF16)
    c0, c1, c2, c3 = GDN_CONV_CH, GDN_CONV_CH + GDN_VAL_DIM, GDN_CONV_CH + GDN_VAL_DIM + DSA_DIM, \
        GDN_CONV_CH + GDN_VAL_DIM + DSA_DIM + LANES
    qkv_ref[...] = _dot(h, wrow_ref[:, 0:c0])
    z_ref[...] = _dot(h, wrow_ref[:, c0:c1])
    k = _dot(h, wrow_ref[:, c1:c2])
    kms = _dot((k * k).astype(BF16), bd_ref[...]) * (1.0 / DSA_HD)
    kn_ref[...] = (k * lax.rsqrt(kms + EPS) * gk_ref[...]).astype(BF16)
    rs_ref[...] = _dot(h, wrow_ref[:, c1 + DSA_DIM:c3])

    r0, r1, r2, r3 = DSA_DIM, 2 * DSA_DIM, 2 * DSA_DIM + IDX_Q_DIM, 2 * DSA_DIM + IDX_Q_DIM + IDX_HEADS
    qT = _dot_nt(wcol_ref[0:r0, :], h)
    qms = _dot(bd_ref[...], (qT * qT).astype(BF16)) * (1.0 / DSA_HD)
    qT_ref[...] = (qT * lax.rsqrt(qms + EPS) * gq_ref[...] * (DSA_HD ** -0.5)).astype(BF16)
    vT = _dot_nt(wcol_ref[r0:r1, :], h).astype(BF16)
    for i in range(vT_ref.shape[0]):
        vT_ref[i] = vT[:, i * KEY_TILE:(i + 1) * KEY_TILE]
    qiT_ref[...] = (_dot_nt(wcol_ref[r1:r2, :], h) * (IDX_HD ** -0.5)).astype(BF16)
    wT_ref[...] = _dot_nt(wcol_ref[r2:r3, :], h) * (IDX_HEADS ** -0.5)


def _inproj(x2, g_mix, w_in, g_q, g_k):
    n, d = x2.shape
    tm = ROW_TILE
    sizes = (GDN_CONV_CH, GDN_VAL_DIM, GDN_HEADS, GDN_HEADS, DSA_DIM, DSA_DIM, DSA_DIM, IDX_Q_DIM, IDX_HD, IDX_HEADS)
    offs = [0]
    for s in sizes:
        offs.append(offs[-1] + s)
    (w_qkv, w_z, w_b, w_a, w_q, w_k, w_v, w_iq, w_ik, w_iw) = [w_in[:, offs[i]:offs[i + 1]] for i in range(len(sizes))]
    pad = jnp.zeros((d, LANES - IDX_HD - 2 * GDN_HEADS), w_in.dtype)
    w_row = jnp.concatenate([w_qkv, w_z, w_k, w_ik, w_b, w_a, pad], axis=1).astype(BF16)
    w_col = jnp.concatenate([w_q, w_v, w_iq, w_iw], axis=1).T.astype(BF16)
    bd = _block_diag_ones(DSA_DIM, DSA_HD)
    gk_row = jnp.tile(g_k, DSA_HEADS)[None, :]
    gq_col = jnp.tile(g_q, DSA_HEADS)[:, None]
    nrow, ncol = w_row.shape[1], w_col.shape[0]
    out_shape = (
        jax.ShapeDtypeStruct((n, GDN_CONV_CH), F32),
        jax.ShapeDtypeStruct((n, GDN_VAL_DIM), F32),
        jax.ShapeDtypeStruct((n, DSA_DIM), BF16),
        jax.ShapeDtypeStruct((n, LANES), F32),
        jax.ShapeDtypeStruct((DSA_DIM, n), BF16),
        jax.ShapeDtypeStruct((n // KEY_TILE, DSA_DIM, KEY_TILE), BF16),
        jax.ShapeDtypeStruct((IDX_Q_DIM, n), BF16),
        jax.ShapeDtypeStruct((IDX_HEADS, n), F32),
    )
    return pl.pallas_call(
        _inproj_kernel,
        grid=(n // tm,),
        in_specs=[
            pl.BlockSpec((tm, d), lambda i: (i, 0)),
            _const_spec((1, d)),
            _const_spec((d, nrow)),
            _const_spec((ncol, d)),
            _const_spec((DSA_DIM, DSA_DIM)),
            _const_spec((1, DSA_DIM)),
            _const_spec((DSA_DIM, 1)),
        ],
        out_specs=(
            pl.BlockSpec((tm, GDN_CONV_CH), lambda i: (i, 0)),
            pl.BlockSpec((tm, GDN_VAL_DIM), lambda i: (i, 0)),
            pl.BlockSpec((tm, DSA_DIM), lambda i: (i, 0)),
            pl.BlockSpec((tm, LANES), lambda i: (i, 0)),
            pl.BlockSpec((DSA_DIM, tm), lambda i: (0, i)),
            pl.BlockSpec((tm // KEY_TILE, DSA_DIM, KEY_TILE), lambda i: (i, 0, 0)),
            pl.BlockSpec((IDX_Q_DIM, tm), lambda i: (0, i)),
            pl.BlockSpec((IDX_HEADS, tm), lambda i: (0, i)),
        ),
        out_shape=out_shape,
        compiler_params=pltpu.CompilerParams(dimension_semantics=("arbitrary",), vmem_limit_bytes=VMEM_LIMIT),
        name="inproj",
    )(x2, g_mix[None, :], w_row, w_col, bd, gk_row, gq_col)


def _gdn_kernel(qkv_ref, z_ref, rs_ref, cw_ref, alog_ref, dtb_ref, gout_ref, y_ref, xc_ref, s_ref):
    rows = qkv_ref.shape[0]
    nc = rows // CHUNK

    @pl.when(pl.program_id(1) == 0)
    def _start_of_sequence():
        xc_ref[0:SUBLANES, :] = jnp.zeros((SUBLANES, GDN_CONV_CH), F32)
        s_ref[...] = jnp.zeros(s_ref.shape, F32)

    xc_ref[SUBLANES:SUBLANES + rows, :] = qkv_ref[...]
    cw = cw_ref[...]
    conv = cw[0:1, :] * xc_ref[SUBLANES - 3:SUBLANES - 3 + rows, :]
    for j in range(1, CONV_K):
        conv = conv + cw[j:j + 1, :] * xc_ref[SUBLANES - 3 + j:SUBLANES - 3 + j + rows, :]
    xc_ref[0:SUBLANES, :] = xc_ref[rows:rows + SUBLANES, :]
    act = conv * jax.nn.sigmoid(conv)

    rs = rs_ref[...]
    beta_all = jax.nn.sigmoid(rs)
    sp_in = rs + dtb_ref[...]
    softplus = jnp.maximum(sp_in, 0.0) + jnp.log(1.0 + jnp.exp(-jnp.abs(sp_in)))
    g_all = -jnp.exp(alog_ref[...]) * softplus

    ri = lax.broadcasted_iota(I32, (CHUNK, CHUNK), 0)
    ci = lax.broadcasted_iota(I32, (CHUNK, CHUNK), 1)
    incl = ri >= ci
    strict = ri > ci
    ltri = incl.astype(F32)
    eye = (ri == ci).astype(F32)
    gout = gout_ref[...]

    chains = [(c, h) for c in range(nc) for h in range(GDN_HEADS)]
    dcum, dcum_t = [], []
    for c in range(nc):
        d = _dot_f32(ltri, g_all[c * CHUNK:(c + 1) * CHUNK, :])
        dcum.append(d)
        dcum_t.append(d.T)

    q16, k16, kb16, vb16, kw16, qdec16, kdect16, gamma, last = ([] for _ in range(9))
    for c, h in chains:
        r = slice(c * CHUNK, (c + 1) * CHUNK)
        q = act[r, h * GDN_DK:(h + 1) * GDN_DK]
        k = act[r, GDN_KEY_DIM + h * GDN_DK:GDN_KEY_DIM + (h + 1) * GDN_DK]
        v = act[r, 2 * GDN_KEY_DIM + h * GDN_DV:2 * GDN_KEY_DIM + (h + 1) * GDN_DV]
        q = q * lax.rsqrt(jnp.sum(q * q, axis=-1, keepdims=True) + EPS) * (GDN_DK ** -0.5)
        k = k * lax.rsqrt(jnp.sum(k * k, axis=-1, keepdims=True) + EPS)
        beta = beta_all[r, RS_BETA + h:RS_BETA + h + 1]
        d_col = dcum[c][:, RS_A + h:RS_A + h + 1]
        d_row = dcum_t[c][RS_A + h:RS_A + h + 1, :]
        d_last = dcum[c][CHUNK - 1:CHUNK, RS_A + h:RS_A + h + 1]
        e_col = jnp.exp(d_col)
        kb = k * beta
        gamma.append(jnp.exp(jnp.where(incl, d_col - d_row, -jnp.inf)))
        q16.append(q.astype(BF16))
        k16.append(k.astype(BF16))
        kb16.append(kb.astype(BF16))
        vb16.append((v * beta).astype(BF16))
        kw16.append((kb * e_col).astype(BF16))
        qdec16.append((q * e_col).astype(BF16))
        kdect16.append((k * jnp.exp(d_last - d_col)).T.astype(BF16))
        last.append(jnp.exp(d_last))

    n = len(chains)
    p16 = [(-jnp.where(strict, _dot_nt(kb16[i], k16[i]) * gamma[i], 0.0)).astype(BF16) for i in range(n)]
    qk16 = [(_dot_nt(q16[i], k16[i]) * gamma[i]).astype(BF16) for i in range(n)]
    t_mat = [eye + p16[i].astype(F32) for i in range(n)]
    p16 = [_dot(p16[i], p16[i]).astype(BF16) for i in range(n)]
    for _ in range(4):
        t_mat = [t_mat[i] + _dot(t_mat[i].astype(BF16), p16[i]) for i in range(n)]
        p16 = [_dot(p16[i], p16[i]).astype(BF16) for i in range(n)]
    t16 = [(t_mat[i] + _dot(t_mat[i].astype(BF16), p16[i])).astype(BF16) for i in range(n)]
    u = [_dot(t16[i], vb16[i]) for i in range(n)]
    w16 = [_dot(t16[i], kw16[i]).astype(BF16) for i in range(n)]

    s = [s_ref[h] for h in range(GDN_HEADS)]
    for c in range(nc):
        ids = [c * GDN_HEADS + h for h in range(GDN_HEADS)]
        s16 = [s[h].astype(BF16) for h in range(GDN_HEADS)]
        v_new16 = [(u[i] - _dot(w16[i], s16[h])).astype(BF16) for h, i in enumerate(ids)]
        o_state = [_dot(qdec16[i], s16[h]) for h, i in enumerate(ids)]
        o = [o_state[h] + _dot(qk16[i], v_new16[h]) for h, i in enumerate(ids)]
        s = [s[h] * last[i] + _dot(kdect16[i], v_new16[h]) for h, i in enumerate(ids)]
        for h in range(GDN_HEADS):
            zc = z_ref[c * CHUNK:(c + 1) * CHUNK, h * GDN_DV:(h + 1) * GDN_DV]
            y = _rms_rows(o[h], gout) * (zc * jax.nn.sigmoid(zc))
            y_ref[c * CHUNK:(c + 1) * CHUNK, h * GDN_DV:(h + 1) * GDN_DV] = y.astype(y_ref.dtype)
    for h in range(GDN_HEADS):
        s_ref[h] = s[h]


def _gdn(qkv, z, rs, conv_w, a_log, dt_bias, g_out, batch, seq):
    n = qkv.shape[0]
    rows = GDN_CHUNKS_PER_STEP * CHUNK
    steps = seq // rows
    lane_vec = lambda v: jnp.zeros((1, LANES), F32).at[0, RS_A:RS_A + GDN_HEADS].set(v)
    row_map = lambda b, i: (b * steps + i, 0)
    return pl.pallas_call(
        _gdn_kernel,
        grid=(batch, steps),
        in_specs=[
            pl.BlockSpec((rows, GDN_CONV_CH), row_map),
            pl.BlockSpec((rows, GDN_VAL_DIM), row_map),
            pl.BlockSpec((rows, LANES), row_map),
            _const_spec((CONV_K, GDN_CONV_CH)),
            _const_spec((1, LANES)),
            _const_spec((1, LANES)),
            _const_spec((1, GDN_DV)),
        ],
        out_specs=pl.BlockSpec((rows, GDN_VAL_DIM), row_map),
        out_shape=jax.ShapeDtypeStruct((n, GDN_VAL_DIM), BF16),
        scratch_shapes=[
            pltpu.VMEM((rows + SUBLANES, GDN_CONV_CH), F32),
            pltpu.VMEM((GDN_HEADS, GDN_DK, GDN_DV), F32),
        ],
        compiler_params=pltpu.CompilerParams(dimension_semantics=("arbitrary", "arbitrary"), vmem_limit_bytes=VMEM_LIMIT),
        name="gdn",
    )(qkv, z, rs, conv_w, lane_vec(a_log), lane_vec(dt_bias), g_out[None, :])


def _alibi_slope(h):
    return 2.0 ** (-8.0 * (h + 1) / DSA_HEADS)


def _order_key(x):
    bits = lax.bitcast_convert_type(x, I32)
    return (bits ^ ((bits >> 31) & jnp.int32(0x7FFFFFFF))) >> 1


def _order_key_to_f32(key):
    full = key << 1
    return lax.bitcast_convert_type(full ^ ((full >> 31) & jnp.int32(0x7FFFFFFF)), F32)


def _dsa_kernel(qT_ref, qiT_ref, wT_ref, kn_ref, vT_ref, rs_ref, kpos_ref, slope_ref, o_ref,
                idxk_scr, sc_scr, key_scr, thr_scr, cnt_scr, acc_scr, *, topk, n_pos_bits):
    j = pl.program_id(1)

    @pl.when(j == 0)
    def _new_sequence():
        idxk_scr[...] = rs_ref[:, RS_IDXK:RS_IDXK + IDX_HD].astype(BF16)

    n_tiles = (j * Q_BLOCK + Q_BLOCK + KEY_TILE - 1) // KEY_TILE
    lane = lax.broadcasted_iota(I32, (1, Q_BLOCK), 1)
    qpos = j * Q_BLOCK + lane
    key_limit = ((qpos >> 6) + 1) << 6
    row = lax.broadcasted_iota(I32, (KEY_TILE, Q_BLOCK), 0)

    w_t = wT_ref[...]
    qi_pairs = [jnp.concatenate([qiT_ref[(2 * p) * IDX_HD:(2 * p + 1) * IDX_HD, :],
                                 qiT_ref[(2 * p + 1) * IDX_HD:(2 * p + 2) * IDX_HD, :]], axis=1)
                for p in range(IDX_HEADS // 2)]

    def score_tile(t, carry):
        s_max, s_min = carry
        r0 = pl.multiple_of(t * KEY_TILE, KEY_TILE)
        kt = idxk_scr[pl.ds(r0, KEY_TILE), :]
        acc = jnp.zeros((KEY_TILE, Q_BLOCK), F32)
        for p in range(IDX_HEADS // 2):
            d = _dot(kt, qi_pairs[p])
            acc = acc + jnp.maximum(d[:, :Q_BLOCK], 0.0) * w_t[2 * p:2 * p + 1, :]
            acc = acc + jnp.maximum(d[:, Q_BLOCK:], 0.0) * w_t[2 * p + 1:2 * p + 2, :]
        admissible = r0 + row < key_limit
        stored = jnp.where(admissible, acc, -jnp.inf)
        sc_scr[t] = stored
        key_scr[t] = _order_key(stored)
        s_max = jnp.maximum(s_max, jnp.max(stored, axis=0, keepdims=True))
        s_min = jnp.minimum(s_min, jnp.min(jnp.where(admissible, acc, jnp.inf), axis=0, keepdims=True))
        return s_max, s_min

    s_max, s_min = lax.fori_loop(0, n_tiles, score_tile, (jnp.full((1, Q_BLOCK), -jnp.inf, F32),
                                                          jnp.full((1, Q_BLOCK), jnp.inf, F32)))

    def count_tiles(per_tile):
        cnt_scr[...] = jnp.zeros(cnt_scr.shape, F32)
        base = jnp.int32(0)
        arm = 1 << (sc_scr.shape[0].bit_length() - 1)
        while arm >= 1:
            has = (n_tiles & arm) != 0

            @pl.when(has)
            def _run(base=base, arm=arm):
                chains = [jnp.zeros((SUBLANES, Q_BLOCK), F32) for _ in range(COUNT_CHAINS)]
                for i in range(arm):
                    flags = per_tile(base + i)
                    for r in range(KEY_TILE // SUBLANES):
                        chains[r % COUNT_CHAINS] = chains[r % COUNT_CHAINS] + flags[r * SUBLANES:(r + 1) * SUBLANES, :]
                while len(chains) > 1:
                    chains = [chains[k] + chains[k + 1] for k in range(0, len(chains), 2)]
                cnt_scr[...] += chains[0]

            base = base + jnp.where(has, arm, 0)
            arm //= 2
        return jnp.sum(cnt_scr[...], axis=0, keepdims=True).astype(I32)

    def count_ge(thr):
        return count_tiles(lambda i: jnp.where(sc_scr[i] >= thr, 1.0, 0.0))

    def count_keys_ge(key):
        one = jnp.int32(0x3F800000)
        below = count_tiles(lambda i: lax.bitcast_convert_type(((key_scr[i] - key) >> 31) & one, F32))
        return n_tiles * KEY_TILE - below

    def bisect_step(_, st):
        lo, hi = st
        mid = lo * 0.5 + hi * 0.5
        take = count_keys_ge(_order_key(mid)) >= topk
        return jnp.where(take, mid, lo), jnp.where(take, hi, mid)

    above_max = s_max + jnp.maximum(jnp.abs(s_max) * 1e-6, 1e-30)
    lo, _ = lax.fori_loop(0, BISECT_ROUNDS, bisect_step, (s_min, above_max))
    few = key_limit < topk
    thr = jnp.where(few, jnp.finfo(F32).min, _order_key_to_f32(_order_key(lo)))
    cnt = jnp.where(few, topk, count_ge(thr))
    thr_scr[...] = thr

    @pl.when(jnp.max(jnp.abs(cnt - topk)) > 0)
    def _resolve():
        def max_below(thr):
            def tile(t, m):
                sc = sc_scr[t]
                return jnp.maximum(m, jnp.max(jnp.where(sc < thr, sc, -jnp.inf), axis=0, keepdims=True))
            return lax.fori_loop(0, n_tiles, tile, jnp.full((1, Q_BLOCK), -jnp.inf, F32))

        def lower(st):
            thr, cnt = st
            nxt = max_below(thr)
            short = cnt < topk
            return jnp.where(short, nxt, thr), jnp.where(short, count_ge(nxt), cnt)

        def min_above(thr):
            def tile(t, m):
                sc = sc_scr[t]
                return jnp.minimum(m, jnp.min(jnp.where(sc > thr, sc, jnp.inf), axis=0, keepdims=True))
            return lax.fori_loop(0, n_tiles, tile, jnp.full((1, Q_BLOCK), jnp.inf, F32))

        def count_eq_below(thr, pos_limit):
            def tile(t, c):
                r0 = t * KEY_TILE
                hit = (sc_scr[t] == thr) & (r0 + row < pos_limit)
                return c + jnp.sum(hit.astype(I32), axis=0, keepdims=True)
            return lax.fori_loop(0, n_tiles, tile, jnp.zeros((1, Q_BLOCK), I32))

        def body(st):
            thr, cnt = st
            nxt = min_above(thr)
            cnt_n = count_ge(nxt)
            active = cnt > topk
            advance = active & (cnt_n >= topk)
            tie = active & (cnt_n < topk)
            @pl.when(jnp.max(tie.astype(I32)) > 0)
            def _drop_surplus_ties():
                need = topk - cnt_n
                pos = jnp.zeros((1, Q_BLOCK), I32)
                for b in range(n_pos_bits - 1, -1, -1):
                    cand = pos + (1 << b)
                    pos = jnp.where(count_eq_below(thr, cand) < need, cand, pos)

                def drop_tile(t, carry):
                    r0 = t * KEY_TILE
                    sc = sc_scr[t]
                    sc_scr[t] = jnp.where(tie & (sc == thr) & (r0 + row > pos), -jnp.inf, sc)
                    return carry

                lax.fori_loop(0, n_tiles, drop_tile, 0)

            return jnp.where(advance, nxt, thr), jnp.where(advance, cnt_n, jnp.where(tie, topk, cnt))

        st = lax.while_loop(lambda st: jnp.min(st[1]) < topk, lower, (thr, cnt))
        thr_scr[...] = lax.while_loop(lambda st: jnp.max(st[1]) > topk, body, st)[0]

    thr = thr_scr[...]

    zero = jnp.zeros((DSA_HD, Q_BLOCK), BF16)
    q_pairs = []
    for p in range(DSA_HEADS // 2):
        a = qT_ref[(2 * p) * DSA_HD:(2 * p + 1) * DSA_HD, :]
        b = qT_ref[(2 * p + 1) * DSA_HD:(2 * p + 2) * DSA_HD, :]
        q_pairs.append(jnp.concatenate([jnp.concatenate([a, zero], axis=1),
                                        jnp.concatenate([zero, b], axis=1),
                                        slope_ref[p]], axis=0))
    acc_scr[...] = jnp.zeros(acc_scr.shape, F32)
    qposf = qpos.astype(F32)

    def qk_products(t):
        r0 = pl.multiple_of(t * KEY_TILE, KEY_TILE)
        kt = kn_ref[pl.ds(r0, KEY_TILE), :]
        kp = kpos_ref[pl.ds(r0, KEY_TILE), :]
        s2 = [_dot(jnp.concatenate([kt[:, p * 2 * DSA_HD:(p + 1) * 2 * DSA_HD], kp], axis=1), q_pairs[p])
              for p in range(DSA_HEADS // 2)]
        return t, sc_scr[t] >= thr, s2

    def softmax_pv(tile, ml, last_tile):
        t, sel, s2 = tile
        vt = vT_ref[t]
        if last_tile:
            after = jnp.maximum((t * KEY_TILE + row).astype(F32) - qposf, 0.0)
        m_all, l_all = ml
        m_rows, l_rows = [], []
        for h in range(DSA_HEADS):
            s = s2[h // 2][:, (h % 2) * Q_BLOCK:(h % 2 + 1) * Q_BLOCK]
            if last_tile:
                s = s - (2.0 * _alibi_slope(h)) * after
            s = jnp.where(sel, s, NEG_BIG)
            m_old = m_all[h:h + 1, :]
            m_new = jnp.maximum(m_old, jnp.max(s, axis=0, keepdims=True))
            alpha = jnp.exp(m_old - m_new)
            pr = jnp.exp(s - m_new)
            l_rows.append(alpha * l_all[h:h + 1, :] + jnp.sum(pr, axis=0, keepdims=True))
            m_rows.append(m_new)
            hs = slice(h * DSA_HD, (h + 1) * DSA_HD)
            acc_scr[hs, :] = alpha * acc_scr[hs, :] + _dot(vt[hs, :], pr.astype(BF16))
        return jnp.concatenate(m_rows, axis=0), jnp.concatenate(l_rows, axis=0)

    def attend_two(ta, tb, ml, b_is_last):
        a, b = qk_products(ta), qk_products(tb)
        return softmax_pv(b, softmax_pv(a, ml, False), b_is_last)

    ml = (jnp.full((DSA_HEADS, Q_BLOCK), NEG_BIG, F32), jnp.zeros((DSA_HEADS, Q_BLOCK), F32))
    earlier = n_tiles - 1
    ml = lax.fori_loop(0, earlier // 2, lambda u, ml: attend_two(2 * u, 2 * u + 1, ml, False), ml)
    _, l_all = lax.cond(earlier % 2 == 1,
                        lambda ml: attend_two(n_tiles - 2, n_tiles - 1, ml, True),
                        lambda ml: softmax_pv(qk_products(n_tiles - 1), ml, True), ml)

    outs = [acc_scr[h * DSA_HD:(h + 1) * DSA_HD, :] / l_all[h:h + 1, :] for h in range(DSA_HEADS)]
    o_ref[...] = jnp.concatenate(outs, axis=0).T.astype(o_ref.dtype)


def _dsa(qT, qiT, wT, kn, vT, rs, batch, seq):
    n = kn.shape[0]
    blocks = seq // Q_BLOCK
    tiles = seq // KEY_TILE
    topk = min(TOPK_MAX, seq // 4)
    col_map = lambda b, j: (0, b * blocks + j)
    pos = jnp.arange(seq, dtype=I32)
    kpos = jnp.zeros((seq, LANES), F32).at[:, 0].set((pos >> 4).astype(F32)).at[:, 1].set((pos & 15).astype(F32))
    slopes = jnp.asarray([_alibi_slope(h) for h in range(DSA_HEADS)], F32).reshape(DSA_HEADS // 2, 2)
    slope_cols = jnp.repeat(slopes, Q_BLOCK, axis=1)
    slope_rows = jnp.zeros((DSA_HEADS // 2, LANES, 2 * Q_BLOCK), F32)
    slope_rows = slope_rows.at[:, 0, :].set(16.0 * slope_cols).at[:, 1, :].set(slope_cols)
    return pl.pallas_call(
        functools.partial(_dsa_kernel, topk=topk, n_pos_bits=seq.bit_length()),
        grid=(batch, blocks),
        in_specs=[
            pl.BlockSpec((DSA_DIM, Q_BLOCK), col_map),
            pl.BlockSpec((IDX_Q_DIM, Q_BLOCK), col_map),
            pl.BlockSpec((IDX_HEADS, Q_BLOCK), col_map),
            pl.BlockSpec((seq, DSA_DIM), lambda b, j: (b, 0)),
            pl.BlockSpec((tiles, DSA_DIM, KEY_TILE), lambda b, j: (b, 0, 0)),
            pl.BlockSpec((seq, LANES), lambda b, j: (b, 0)),
            _const_spec((seq, LANES)),
            _const_spec((DSA_HEADS // 2, LANES, 2 * Q_BLOCK)),
        ],
        out_specs=pl.BlockSpec((Q_BLOCK, DSA_DIM), lambda b, j: (b * blocks + j, 0)),
        out_shape=jax.ShapeDtypeStruct((n, DSA_DIM), BF16),
        scratch_shapes=[
            pltpu.VMEM((seq, IDX_HD), BF16),
            pltpu.VMEM((tiles, KEY_TILE, Q_BLOCK), F32),
            pltpu.VMEM((tiles, KEY_TILE, Q_BLOCK), I32),
            pltpu.VMEM((1, Q_BLOCK), F32),
            pltpu.VMEM((SUBLANES, Q_BLOCK), F32),
            pltpu.VMEM((DSA_DIM, Q_BLOCK), F32),
        ],
        compiler_params=pltpu.CompilerParams(dimension_semantics=("arbitrary", "arbitrary"), vmem_limit_bytes=VMEM_LIMIT),
        name="dsa",
    )(qT, qiT, wT, kn, vT, rs, kpos.astype(BF16), slope_rows.astype(BF16))


def _memkv_kernel(mem_ref, g_ref, w_ref, bd_ref, gk_ref, k_ref, v_ref):
    h = _rms_rows(mem_ref[...], g_ref[...]).astype(BF16)
    kv = _dot(h, w_ref[...])
    k = kv[:, :XA_DIM]
    kms = _dot((k * k).astype(BF16), bd_ref[...]) * (1.0 / XA_HD)
    k_ref[...] = (k * lax.rsqrt(kms + EPS) * gk_ref[...]).astype(BF16)
    v_ref[...] = kv[:, XA_DIM:].astype(BF16)


def _memkv(mem2, g_mem, w_xkv, g_xk):
    n, d = mem2.shape
    tm = min(ROW_TILE, n)
    return pl.pallas_call(
        _memkv_kernel,
        grid=(n // tm,),
        in_specs=[
            pl.BlockSpec((tm, d), lambda i: (i, 0)),
            _const_spec((1, d)),
            _const_spec((d, 2 * XA_DIM)),
            _const_spec((XA_DIM, XA_DIM)),
            _const_spec((1, XA_DIM)),
        ],
        out_specs=(pl.BlockSpec((tm, XA_DIM), lambda i: (i, 0)), pl.BlockSpec((tm, XA_DIM), lambda i: (i, 0))),
        out_shape=(jax.ShapeDtypeStruct((n, XA_DIM), BF16), jax.ShapeDtypeStruct((n, XA_DIM), BF16)),
        compiler_params=pltpu.CompilerParams(dimension_semantics=("arbitrary",), vmem_limit_bytes=VMEM_LIMIT),
        name="memkv",
    )(mem2, g_mem[None, :], w_xkv.astype(BF16), _block_diag_ones(XA_DIM, XA_HD), jnp.tile(g_xk, XA_HEADS)[None, :])


def _tail_kernel(x_ref, ya_ref, yb_ref, wout_ref, gx_ref, wxq_ref, bd_ref, gxq_ref, km_ref, vm_ref, wxo_ref,
                 gf_ref, wgu_ref, wd_ref, o_ref, *, ff_chunks):
    x1 = x_ref[...] + _dot(ya_ref[...], wout_ref[0:GDN_VAL_DIM, :]) + _dot(yb_ref[...], wout_ref[GDN_VAL_DIM:, :])

    h = _rms_rows(x1, gx_ref[...]).astype(BF16)
    q = _dot(h, wxq_ref[...])
    qms = _dot((q * q).astype(BF16), bd_ref[...]) * (1.0 / XA_HD)
    qn = (q * lax.rsqrt(qms + EPS) * gxq_ref[...]).astype(BF16)
    heads = []
    for hd in range(XA_HEADS):
        hs = slice(hd * XA_HD, (hd + 1) * XA_HD)
        s = _dot_nt(qn[:, hs], km_ref[0, :, hs]) * (XA_HD ** -0.5)
        pr = jnp.exp(s - jnp.max(s, axis=-1, keepdims=True))
        o = _dot(pr.astype(BF16), vm_ref[0, :, hs]) / jnp.sum(pr, axis=-1, keepdims=True)
        heads.append(o.astype(BF16))
    x2 = x1 + _dot(jnp.concatenate(heads, axis=1), wxo_ref[...])

    h = _rms_rows(x2, gf_ref[...]).astype(BF16)
    d_ff = wd_ref.shape[0]
    fc = d_ff // ff_chunks
    acc = x2
    for c in range(ff_chunks):
        gate = _dot(h, wgu_ref[:, c * fc:(c + 1) * fc])
        up = _dot(h, wgu_ref[:, d_ff + c * fc:d_ff + (c + 1) * fc])
        a = (gate * jax.nn.sigmoid(gate) * up).astype(BF16)
        acc = acc + _dot(a, wd_ref[c * fc:(c + 1) * fc, :])
    o_ref[...] = acc


def _tail(x2, ya, yb, w_out, g_xattn, w_xq, g_xq, km, vm, w_xo, g_ffn, w_gu, w_down, seq):
    n, d = x2.shape
    tm = ROW_TILE
    d_ff = w_down.shape[0]
    ff_chunks = 2 if (d_ff // 2) % LANES == 0 else 1
    n_mem = km.shape[1]
    per_seq = seq // tm
    return pl.pallas_call(
        functools.partial(_tail_kernel, ff_chunks=ff_chunks),
        grid=(n // tm,),
        in_specs=[
            pl.BlockSpec((tm, d), lambda i: (i, 0)),
            pl.BlockSpec((tm, GDN_VAL_DIM), lambda i: (i, 0)),
            pl.BlockSpec((tm, DSA_DIM), lambda i: (i, 0)),
            _const_spec((GDN_VAL_DIM + DSA_DIM, d)),
            _const_spec((1, d)),
            _const_spec((d, XA_DIM)),
            _const_spec((XA_DIM, XA_DIM)),
            _const_spec((1, XA_DIM)),
            pl.BlockSpec((1, n_mem, XA_DIM), lambda i: (i // per_seq, 0, 0)),
            pl.BlockSpec((1, n_mem, XA_DIM), lambda i: (i // per_seq, 0, 0)),
            _const_spec((XA_DIM, d)),
            _const_spec((1, d)),
            _const_spec((d, 2 * d_ff)),
            _const_spec((d_ff, d)),
        ],
        out_specs=pl.BlockSpec((tm, d), lambda i: (i, 0)),
        out_shape=jax.ShapeDtypeStruct((n, d), F32),
        compiler_params=pltpu.CompilerParams(dimension_semantics=("arbitrary",), vmem_limit_bytes=VMEM_LIMIT),
        name="tail",
    )(x2, ya, yb, w_out.astype(BF16), g_xattn[None, :], w_xq.astype(BF16), _block_diag_ones(XA_DIM, XA_HD),
      jnp.tile(g_xq, XA_HEADS)[None, :], km, vm, w_xo.astype(BF16), g_ffn[None, :], w_gu.astype(BF16),
      w_down.astype(BF16))


def kernel(x, mem, g_mix, w_in, conv_w, a_log, dt_bias, g_gdn_out, g_q_dsa, g_k_dsa, w_out, g_xattn, g_mem, w_xq,
           w_xkv, g_xq, g_xk, w_xo, g_ffn, w_gu, w_down):
    batch, seq, d = x.shape
    n_mem = mem.shape[1]
    assert seq % ROW_TILE == 0 and seq % KEY_TILE == 0 and ROW_TILE % KEY_TILE == 0
    for l in range(g_mix.shape[0]):
        x2 = x.reshape(batch * seq, d)
        qkv, z, kn, rs, qT, vT, qiT, wT = _inproj(x2, g_mix[l], w_in[l], g_q_dsa[l], g_k_dsa[l])
        ya = _gdn(qkv, z, rs, conv_w[l], a_log[l], dt_bias[l], g_gdn_out[l], batch, seq)
        yb = _dsa(qT, qiT, wT, kn, vT, rs, batch, seq)
        km, vm = _memkv(mem.reshape(batch * n_mem, d), g_mem[l], w_xkv[l], g_xk[l])
        km = km.reshape(batch, n_mem, XA_DIM)
        vm = vm.reshape(batch, n_mem, XA_DIM)
        x = _tail(x2, ya, yb, w_out[l], g_xattn[l], w_xq[l], g_xq[l], km, vm, w_xo[l], g_ffn[l], w_gu[l],
                  w_down[l], seq).reshape(batch, seq, d)
    return x
```

```python
import functools

import jax
import jax.numpy as jnp
from jax import lax
from jax.experimental import pallas as pl
from jax.experimental.pallas import tpu as pltpu

F32 = jnp.float32
BF16 = jnp.bfloat16
I32 = jnp.int32

EPS = 1e-6
CHUNK = 64
GDN_HEADS, GDN_DK, GDN_DV, CONV_K = 4, 128, 128, 4
DSA_HEADS, DSA_HD = 8, 64
IDX_HEADS, IDX_HD = 16, 64
TOPK_MAX = 256
XA_HEADS, XA_HD = 4, 128

GDN_KEY_DIM = GDN_HEADS * GDN_DK
GDN_VAL_DIM = GDN_HEADS * GDN_DV
GDN_CONV_CH = 2 * GDN_KEY_DIM + GDN_VAL_DIM
DSA_DIM = DSA_HEADS * DSA_HD
IDX_Q_DIM = IDX_HEADS * IDX_HD
XA_DIM = XA_HEADS * XA_HD

LANES = 128
SUBLANES = 8
VMEM_LIMIT = 56 * 1024 * 1024

RS_IDXK = 0
RS_BETA = IDX_HD
RS_A = IDX_HD + GDN_HEADS

ROW_TILE = 512
KEY_TILE = 256
Q_BLOCK = 128
GDN_CHUNKS_PER_STEP = 4

BISECT_ROUNDS = 20
COUNT_CHAINS = 4
NEG_BIG = -1e30


def _dot(a, b):
    return jnp.dot(a, b, preferred_element_type=F32)


def _dot_nt(a, b):
    return lax.dot_general(a, b, (((1,), (1,)), ((), ())), preferred_element_type=F32)


def _dot_f32(a, b):
    return jnp.dot(a, b, preferred_element_type=F32, precision=lax.Precision.HIGHEST)


def _rms_rows(x, g):
    ms = jnp.mean(x * x, axis=-1, keepdims=True)
    return x * lax.rsqrt(ms + EPS) * g


def _const_spec(shape):
    nd = len(shape)
    return pl.BlockSpec(shape, lambda *_: (0,) * nd, pipeline_mode=pl.Buffered(1))


def _block_diag_ones(n, blk):
    r = lax.broadcasted_iota(I32, (n, n), 0) // blk
    c = lax.broadcasted_iota(I32, (n, n), 1) // blk
    return (r == c).astype(BF16)


def _inproj_kernel(x_ref, g_ref, wrow_ref, wcol_ref, bd_ref, gk_ref, gq_ref, cw_ref,
                   qkv_ref, z_ref, kn_ref, rs_ref, qT_ref, vT_ref, qiT_ref, wT_ref, xc_ref, *, tiles_per_seq):
    tm = x_ref.shape[0]

    @pl.when(pl.program_id(0) % tiles_per_seq == 0)
    def _start_of_sequence():
        xc_ref[0:SUBLANES, :] = jnp.zeros((SUBLANES, GDN_CONV_CH), F32)

    h = _rms_rows(x_ref[...], g_ref[...]).astype(BF16)
    c0, c1, c2, c3 = GDN_CONV_CH, GDN_CONV_CH + GDN_VAL_DIM, GDN_CONV_CH + GDN_VAL_DIM + DSA_DIM, \
        GDN_CONV_CH + GDN_VAL_DIM + DSA_DIM + LANES

    xc_ref[SUBLANES:SUBLANES + tm, :] = _dot(h, wrow_ref[:, 0:c0])

    z_ref[...] = _dot(h, wrow_ref[:, c0:c1])
    k = _dot(h, wrow_ref[:, c1:c2])
    kms = _dot((k * k).astype(BF16), bd_ref[...]) * (1.0 / DSA_HD)
    kn_ref[...] = (k * lax.rsqrt(kms + EPS) * gk_ref[...]).astype(BF16)
    rs_ref[...] = _dot(h, wrow_ref[:, c1 + DSA_DIM:c3])

    r0, r1, r2, r3 = DSA_DIM, 2 * DSA_DIM, 2 * DSA_DIM + IDX_Q_DIM, 2 * DSA_DIM + IDX_Q_DIM + IDX_HEADS
    qT = _dot_nt(wcol_ref[0:r0, :], h)
    qms = _dot(bd_ref[...], (qT * qT).astype(BF16)) * (1.0 / DSA_HD)
    qT_ref[...] = (qT * lax.rsqrt(qms + EPS) * gq_ref[...] * (DSA_HD ** -0.5)).astype(BF16)
    vT = _dot_nt(wcol_ref[r0:r1, :], h).astype(BF16)
    for i in range(vT_ref.shape[0]):
        vT_ref[i] = vT[:, i * KEY_TILE:(i + 1) * KEY_TILE]
    qiT_ref[...] = (_dot_nt(wcol_ref[r1:r2, :], h) * (IDX_HD ** -0.5)).astype(BF16)
    wT_ref[...] = _dot_nt(wcol_ref[r2:r3, :], h) * (IDX_HEADS ** -0.5)

    cw = cw_ref[...]
    conv = cw[0:1, :] * xc_ref[SUBLANES - 3:SUBLANES - 3 + tm, :]
    for j in range(1, CONV_K):
        conv = conv + cw[j:j + 1, :] * xc_ref[SUBLANES - 3 + j:SUBLANES - 3 + j + tm, :]
    xc_ref[0:SUBLANES, :] = xc_ref[tm:tm + SUBLANES, :]
    act = conv * jax.nn.sigmoid(conv)
    for hd in range(GDN_HEADS):
        qs = slice(hd * GDN_DK, (hd + 1) * GDN_DK)
        ks = slice(GDN_KEY_DIM + hd * GDN_DK, GDN_KEY_DIM + (hd + 1) * GDN_DK)
        q, k = act[:, qs], act[:, ks]
        qkv_ref[:, qs] = q * lax.rsqrt(jnp.sum(q * q, axis=-1, keepdims=True) + EPS) * (GDN_DK ** -0.5)
        qkv_ref[:, ks] = k * lax.rsqrt(jnp.sum(k * k, axis=-1, keepdims=True) + EPS)
    qkv_ref[:, 2 * GDN_KEY_DIM:] = act[:, 2 * GDN_KEY_DIM:]


def _inproj(x2, g_mix, w_in, g_q, g_k, conv_w, seq):
    n, d = x2.shape
    tm = ROW_TILE
    sizes = (GDN_CONV_CH, GDN_VAL_DIM, GDN_HEADS, GDN_HEADS, DSA_DIM, DSA_DIM, DSA_DIM, IDX_Q_DIM, IDX_HD, IDX_HEADS)
    offs = [0]
    for s in sizes:
        offs.append(offs[-1] + s)
    (w_qkv, w_z, w_b, w_a, w_q, w_k, w_v, w_iq, w_ik, w_iw) = [w_in[:, offs[i]:offs[i + 1]] for i in range(len(sizes))]
    pad = jnp.zeros((d, LANES - IDX_HD - 2 * GDN_HEADS), w_in.dtype)
    w_row = jnp.concatenate([w_qkv, w_z, w_k, w_ik, w_b, w_a, pad], axis=1).astype(BF16)
    w_col = jnp.concatenate([w_q, w_v, w_iq, w_iw], axis=1).T.astype(BF16)
    bd = _block_diag_ones(DSA_DIM, DSA_HD)
    gk_row = jnp.tile(g_k, DSA_HEADS)[None, :]
    gq_col = jnp.tile(g_q, DSA_HEADS)[:, None]
    nrow, ncol = w_row.shape[1], w_col.shape[0]
    out_shape = (
        jax.ShapeDtypeStruct((n, GDN_CONV_CH), F32),
        jax.ShapeDtypeStruct((n, GDN_VAL_DIM), F32),
        jax.ShapeDtypeStruct((n, DSA_DIM), BF16),
        jax.ShapeDtypeStruct((n, LANES), F32),
        jax.ShapeDtypeStruct((DSA_DIM, n), BF16),
        jax.ShapeDtypeStruct((n // KEY_TILE, DSA_DIM, KEY_TILE), BF16),
        jax.ShapeDtypeStruct((IDX_Q_DIM, n), BF16),
        jax.ShapeDtypeStruct((IDX_HEADS, n), F32),
    )
    return pl.pallas_call(
        functools.partial(_inproj_kernel, tiles_per_seq=seq // tm),
        grid=(n // tm,),
        in_specs=[
            pl.BlockSpec((tm, d), lambda i: (i, 0)),
            _const_spec((1, d)),
            _const_spec((d, nrow)),
            _const_spec((ncol, d)),
            _const_spec((DSA_DIM, DSA_DIM)),
            _const_spec((1, DSA_DIM)),
            _const_spec((DSA_DIM, 1)),
            _const_spec((CONV_K, GDN_CONV_CH)),
        ],
        out_specs=(
            pl.BlockSpec((tm, GDN_CONV_CH), lambda i: (i, 0)),
            pl.BlockSpec((tm, GDN_VAL_DIM), lambda i: (i, 0)),
            pl.BlockSpec((tm, DSA_DIM), lambda i: (i, 0)),
            pl.BlockSpec((tm, LANES), lambda i: (i, 0)),
            pl.BlockSpec((DSA_DIM, tm), lambda i: (0, i)),
            pl.BlockSpec((tm // KEY_TILE, DSA_DIM, KEY_TILE), lambda i: (i, 0, 0)),
            pl.BlockSpec((IDX_Q_DIM, tm), lambda i: (0, i)),
            pl.BlockSpec((IDX_HEADS, tm), lambda i: (0, i)),
        ),
        out_shape=out_shape,
        scratch_shapes=[pltpu.VMEM((tm + SUBLANES, GDN_CONV_CH), F32)],
        compiler_params=pltpu.CompilerParams(dimension_semantics=("arbitrary",), vmem_limit_bytes=VMEM_LIMIT),
        name="inproj",
    )(x2, g_mix[None, :], w_row, w_col, bd, gk_row, gq_col, conv_w)


def _gdn_kernel(qkv_ref, z_ref, rs_ref, alog_ref, dtb_ref, gout_ref, y_ref, s_ref):
    rows = qkv_ref.shape[0]
    nc = rows // CHUNK

    @pl.when(pl.program_id(1) == 0)
    def _start_of_sequence():
        s_ref[...] = jnp.zeros(s_ref.shape, F32)

    rs = rs_ref[...]
    beta_all = jax.nn.sigmoid(rs)
    sp_in = rs + dtb_ref[...]
    softplus = jnp.maximum(sp_in, 0.0) + jnp.log(1.0 + jnp.exp(-jnp.abs(sp_in)))
    g_all = -jnp.exp(alog_ref[...]) * softplus

    ri = lax.broadcasted_iota(I32, (CHUNK, CHUNK), 0)
    ci = lax.broadcasted_iota(I32, (CHUNK, CHUNK), 1)
    incl = ri >= ci
    strict = ri > ci
    ltri = incl.astype(F32)
    eye = (ri == ci).astype(F32)
    gout = gout_ref[...]

    chains = [(c, h) for c in range(nc) for h in range(GDN_HEADS)]
    dcum, dcum_t = [], []
    for c in range(nc):
        d = _dot_f32(ltri, g_all[c * CHUNK:(c + 1) * CHUNK, :])
        dcum.append(d)
        dcum_t.append(d.T)

    q16, k16, kb16, vb16, kw16, qdec16, kdect16, gamma, last = ([] for _ in range(9))
    for c, h in chains:
        r = slice(c * CHUNK, (c + 1) * CHUNK)
        q = qkv_ref[r, h * GDN_DK:(h + 1) * GDN_DK]
        k = qkv_ref[r, GDN_KEY_DIM + h * GDN_DK:GDN_KEY_DIM + (h + 1) * GDN_DK]
        v = qkv_ref[r, 2 * GDN_KEY_DIM + h * GDN_DV:2 * GDN_KEY_DIM + (h + 1) * GDN_DV]
        beta = beta_all[r, RS_BETA + h:RS_BETA + h + 1]
        d_col = dcum[c][:, RS_A + h:RS_A + h + 1]
        d_row = dcum_t[c][RS_A + h:RS_A + h + 1, :]
        d_last = dcum[c][CHUNK - 1:CHUNK, RS_A + h:RS_A + h + 1]
        e_col = jnp.exp(d_col)
        kb = k * beta
        gamma.append(jnp.exp(jnp.where(incl, d_col - d_row, -jnp.inf)))
        q16.append(q.astype(BF16))
        k16.append(k.astype(BF16))
        kb16.append(kb.astype(BF16))
        vb16.append((v * beta).astype(BF16))
        kw16.append((kb * e_col).astype(BF16))
        qdec16.append((q * e_col).astype(BF16))
        kdect16.append((k * jnp.exp(d_last - d_col)).T.astype(BF16))
        last.append(jnp.exp(d_last))

    n = len(chains)
    p16 = [(-jnp.where(strict, _dot_nt(kb16[i], k16[i]) * gamma[i], 0.0)).astype(BF16) for i in range(n)]
    qk16 = [(_dot_nt(q16[i], k16[i]) * gamma[i]).astype(BF16) for i in range(n)]
    t_mat = [eye + p16[i].astype(F32) for i in range(n)]
    p16 = [_dot(p16[i], p16[i]).astype(BF16) for i in range(n)]
    for _ in range(4):
        t_mat = [t_mat[i] + _dot(t_mat[i].astype(BF16), p16[i]) for i in range(n)]
        p16 = [_dot(p16[i], p16[i]).astype(BF16) for i in range(n)]
    t16 = [(t_mat[i] + _dot(t_mat[i].astype(BF16), p16[i])).astype(BF16) for i in range(n)]
    u = [_dot(t16[i], vb16[i]) for i in range(n)]
    w16 = [_dot(t16[i], kw16[i]).astype(BF16) for i in range(n)]

    s = [s_ref[h] for h in range(GDN_HEADS)]
    for c in range(nc):
        ids = [c * GDN_HEADS + h for h in range(GDN_HEADS)]
        s16 = [s[h].astype(BF16) for h in range(GDN_HEADS)]
        v_new16 = [(u[i] - _dot(w16[i], s16[h])).astype(BF16) for h, i in enumerate(ids)]
        o_state = [_dot(qdec16[i], s16[h]) for h, i in enumerate(ids)]
        o = [o_state[h] + _dot(qk16[i], v_new16[h]) for h, i in enumerate(ids)]
        s = [s[h] * last[i] + _dot(kdect16[i], v_new16[h]) for h, i in enumerate(ids)]
        for h in range(GDN_HEADS):
            zc = z_ref[c * CHUNK:(c + 1) * CHUNK, h * GDN_DV:(h + 1) * GDN_DV]
            y = _rms_rows(o[h], gout) * (zc * jax.nn.sigmoid(zc))
            y_ref[c * CHUNK:(c + 1) * CHUNK, h * GDN_DV:(h + 1) * GDN_DV] = y.astype(y_ref.dtype)
    for h in range(GDN_HEADS):
        s_ref[h] = s[h]


def _gdn(qkv, z, rs, a_log, dt_bias, g_out, batch, seq):
    n = qkv.shape[0]
    rows = GDN_CHUNKS_PER_STEP * CHUNK
    steps = seq // rows
    lane_vec = lambda v: jnp.zeros((1, LANES), F32).at[0, RS_A:RS_A + GDN_HEADS].set(v)
    row_map = lambda b, i: (b * steps + i, 0)
    return pl.pallas_call(
        _gdn_kernel,
        grid=(batch, steps),
        in_specs=[
            pl.BlockSpec((rows, GDN_CONV_CH), row_map),
            pl.BlockSpec((rows, GDN_VAL_DIM), row_map),
            pl.BlockSpec((rows, LANES), row_map),
            _const_spec((1, LANES)),
            _const_spec((1, LANES)),
            _const_spec((1, GDN_DV)),
        ],
        out_specs=pl.BlockSpec((rows, GDN_VAL_DIM), row_map),
        out_shape=jax.ShapeDtypeStruct((n, GDN_VAL_DIM), BF16),
        scratch_shapes=[
            pltpu.VMEM((GDN_HEADS, GDN_DK, GDN_DV), F32),
        ],
        compiler_params=pltpu.CompilerParams(dimension_semantics=("arbitrary", "arbitrary"), vmem_limit_bytes=VMEM_LIMIT),
        name="gdn",
    )(qkv, z, rs, lane_vec(a_log), lane_vec(dt_bias), g_out[None, :])


def _alibi_slope(h):
    return 2.0 ** (-8.0 * (h + 1) / DSA_HEADS)


def _order_key(x):
    bits = lax.bitcast_convert_type(x, I32)
    return (bits ^ ((bits >> 31) & jnp.int32(0x7FFFFFFF))) >> 1


def _order_key_to_f32(key):
    full = key << 1
    return lax.bitcast_convert_type(full ^ ((full >> 31) & jnp.int32(0x7FFFFFFF)), F32)


def _tiles_for_block(j):
    return (j * Q_BLOCK + Q_BLOCK + KEY_TILE - 1) // KEY_TILE


def _dsa_kernel(qT_ref, qiT_ref, wT_ref, kn_ref, vT_ref, rs_ref, kpos_ref, slope_ref, o_ref,
                idxk_scr, sc_scr, key_scr, thr_scr, cnt_scr, acc_scr, *, topk, n_pos_bits):
    j = pl.program_id(1)

    @pl.when(j == 0)
    def _new_sequence():
        idxk_scr[...] = rs_ref[:, RS_IDXK:RS_IDXK + IDX_HD].astype(BF16)

    n_tiles = _tiles_for_block(j)
    lane = lax.broadcasted_iota(I32, (1, Q_BLOCK), 1)
    qpos = j * Q_BLOCK + lane
    key_limit = ((qpos >> 6) + 1) << 6
    row = lax.broadcasted_iota(I32, (KEY_TILE, Q_BLOCK), 0)

    w_t = wT_ref[...]
    qi_pairs = [jnp.concatenate([qiT_ref[(2 * p) * IDX_HD:(2 * p + 1) * IDX_HD, :],
                                 qiT_ref[(2 * p + 1) * IDX_HD:(2 * p + 2) * IDX_HD, :]], axis=1)
                for p in range(IDX_HEADS // 2)]

    def score_tile(t, carry):
        s_max, s_min = carry
        r0 = pl.multiple_of(t * KEY_TILE, KEY_TILE)
        kt = idxk_scr[pl.ds(r0, KEY_TILE), :]
        acc = jnp.zeros((KEY_TILE, Q_BLOCK), F32)
        for p in range(IDX_HEADS // 2):
            d = _dot(kt, qi_pairs[p])
            acc = acc + jnp.maximum(d[:, :Q_BLOCK], 0.0) * w_t[2 * p:2 * p + 1, :]
            acc = acc + jnp.maximum(d[:, Q_BLOCK:], 0.0) * w_t[2 * p + 1:2 * p + 2, :]
        admissible = r0 + row < key_limit
        stored = jnp.where(admissible, acc, -jnp.inf)
        sc_scr[t] = stored
        key_scr[t] = _order_key(stored)
        s_max = jnp.maximum(s_max, jnp.max(stored, axis=0, keepdims=True))
        s_min = jnp.minimum(s_min, jnp.min(jnp.where(admissible, acc, jnp.inf), axis=0, keepdims=True))
        return s_max, s_min

    s_max, s_min = lax.fori_loop(0, n_tiles, score_tile, (jnp.full((1, Q_BLOCK), -jnp.inf, F32),
                                                          jnp.full((1, Q_BLOCK), jnp.inf, F32)))

    def count_tiles(per_tile):
        cnt_scr[...] = jnp.zeros(cnt_scr.shape, F32)
        base = jnp.int32(0)
        arm = 1 << (sc_scr.shape[0].bit_length() - 1)
        while arm >= 1:
            has = (n_tiles & arm) != 0

            @pl.when(has)
            def _run(base=base, arm=arm):
                chains = [jnp.zeros((SUBLANES, Q_BLOCK), F32) for _ in range(COUNT_CHAINS)]
                for i in range(arm):
                    flags = per_tile(base + i)
                    for r in range(KEY_TILE // SUBLANES):
                        chains[r % COUNT_CHAINS] = chains[r % COUNT_CHAINS] + flags[r * SUBLANES:(r + 1) * SUBLANES, :]
                while len(chains) > 1:
                    chains = [chains[k] + chains[k + 1] for k in range(0, len(chains), 2)]
                cnt_scr[...] += chains[0]

            base = base + jnp.where(has, arm, 0)
            arm //= 2
        return jnp.sum(cnt_scr[...], axis=0, keepdims=True).astype(I32)

    def count_ge(thr):
        return count_tiles(lambda i: jnp.where(sc_scr[i] >= thr, 1.0, 0.0))

    def count_keys_ge(key, tiles):
        one = jnp.int32(0x3F800000)
        chains = [jnp.zeros((SUBLANES, Q_BLOCK), F32) for _ in range(COUNT_CHAINS)]
        for i in range(tiles):
            below = lax.bitcast_convert_type(((key_scr[i] - key) >> 31) & one, F32)
            for r in range(KEY_TILE // SUBLANES):
                chains[r % COUNT_CHAINS] = chains[r % COUNT_CHAINS] + below[r * SUBLANES:(r + 1) * SUBLANES, :]
        while len(chains) > 1:
            chains = [chains[k] + chains[k + 1] for k in range(0, len(chains), 2)]
        return tiles * KEY_TILE - jnp.sum(chains[0], axis=0, keepdims=True).astype(I32)

    def search_with_tiles(tiles):
        def bisect_step(_, st):
            lo, hi = st
            mid = lo * 0.5 + hi * 0.5
            take = count_keys_ge(_order_key(mid), tiles) >= topk
            return jnp.where(take, mid, lo), jnp.where(take, hi, mid)

        return lambda st: lax.fori_loop(0, BISECT_ROUNDS, bisect_step, st)

    above_max = s_max + jnp.maximum(jnp.abs(s_max) * 1e-6, 1e-30)
    lo, _ = lax.switch(n_tiles - 1, [search_with_tiles(n) for n in range(1, sc_scr.shape[0] + 1)], (s_min, above_max))
    few = key_limit < topk
    thr = jnp.where(few, jnp.finfo(F32).min, _order_key_to_f32(_order_key(lo)))
    cnt = jnp.where(few, topk, count_ge(thr))
    thr_scr[...] = thr

    @pl.when(jnp.max(jnp.abs(cnt - topk)) > 0)
    def _resolve():
        def max_below(thr):
            def tile(t, m):
                sc = sc_scr[t]
                return jnp.maximum(m, jnp.max(jnp.where(sc < thr, sc, -jnp.inf), axis=0, keepdims=True))
            return lax.fori_loop(0, n_tiles, tile, jnp.full((1, Q_BLOCK), -jnp.inf, F32))

        def lower(st):
            thr, cnt = st
            nxt = max_below(thr)
            short = cnt < topk
            return jnp.where(short, nxt, thr), jnp.where(short, count_ge(nxt), cnt)

        def min_above(thr):
            def tile(t, m):
                sc = sc_scr[t]
                return jnp.minimum(m, jnp.min(jnp.where(sc > thr, sc, jnp.inf), axis=0, keepdims=True))
            return lax.fori_loop(0, n_tiles, tile, jnp.full((1, Q_BLOCK), jnp.inf, F32))

        def count_eq_below(thr, pos_limit):
            def tile(t, c):
                r0 = t * KEY_TILE
                hit = (sc_scr[t] == thr) & (r0 + row < pos_limit)
                return c + jnp.sum(hit.astype(I32), axis=0, keepdims=True)
            return lax.fori_loop(0, n_tiles, tile, jnp.zeros((1, Q_BLOCK), I32))

        def body(st):
            thr, cnt = st
            nxt = min_above(thr)
            cnt_n = count_ge(nxt)
            active = cnt > topk
            advance = active & (cnt_n >= topk)
            tie = active & (cnt_n < topk)
            @pl.when(jnp.max(tie.astype(I32)) > 0)
            def _drop_surplus_ties():
                need = topk - cnt_n
                pos = jnp.zeros((1, Q_BLOCK), I32)
                for b in range(n_pos_bits - 1, -1, -1):
                    cand = pos + (1 << b)
                    pos = jnp.where(count_eq_below(thr, cand) < need, cand, pos)

                def drop_tile(t, carry):
                    r0 = t * KEY_TILE
                    sc = sc_scr[t]
                    sc_scr[t] = jnp.where(tie & (sc == thr) & (r0 + row > pos), -jnp.inf, sc)
                    return carry

                lax.fori_loop(0, n_tiles, drop_tile, 0)

            return jnp.where(advance, nxt, thr), jnp.where(advance, cnt_n, jnp.where(tie, topk, cnt))

        st = lax.while_loop(lambda st: jnp.min(st[1]) < topk, lower, (thr, cnt))
        thr_scr[...] = lax.while_loop(lambda st: jnp.max(st[1]) > topk, body, st)[0]

    thr = thr_scr[...]

    zero = jnp.zeros((DSA_HD, Q_BLOCK), BF16)
    q_pairs = []
    for p in range(DSA_HEADS // 2):
        a = qT_ref[(2 * p) * DSA_HD:(2 * p + 1) * DSA_HD, :]
        b = qT_ref[(2 * p + 1) * DSA_HD:(2 * p + 2) * DSA_HD, :]
        q_pairs.append(jnp.concatenate([jnp.concatenate([a, zero], axis=1),
                                        jnp.concatenate([zero, b], axis=1),
                                        slope_ref[p]], axis=0))
    acc_scr[...] = jnp.zeros(acc_scr.shape, F32)
    qposf = qpos.astype(F32)

    def qk_products(t):
        r0 = pl.multiple_of(t * KEY_TILE, KEY_TILE)
        kt = kn_ref[pl.ds(r0, KEY_TILE), :]
        kp = kpos_ref[pl.ds(r0, KEY_TILE), :]
        s2 = [_dot(jnp.concatenate([kt[:, p * 2 * DSA_HD:(p + 1) * 2 * DSA_HD], kp], axis=1), q_pairs[p])
              for p in range(DSA_HEADS // 2)]
        return t, sc_scr[t] >= thr, s2

    def softmax_pv(tile, ml, last_tile):
        t, sel, s2 = tile
        vt = vT_ref[t]
        if last_tile:
            after = jnp.maximum((t * KEY_TILE + row).astype(F32) - qposf, 0.0)
        m_all, l_all = ml
        m_rows, l_rows = [], []
        for h in range(DSA_HEADS):
            s = s2[h // 2][:, (h % 2) * Q_BLOCK:(h % 2 + 1) * Q_BLOCK]
            if last_tile:
                s = s - (2.0 * _alibi_slope(h)) * after
            s = jnp.where(sel, s, NEG_BIG)
            m_old = m_all[h:h + 1, :]
            m_new = jnp.maximum(m_old, jnp.max(s, axis=0, keepdims=True))
            alpha = jnp.exp(m_old - m_new)
            pr = jnp.exp(s - m_new)
            l_rows.append(alpha * l_all[h:h + 1, :] + jnp.sum(pr, axis=0, keepdims=True))
            m_rows.append(m_new)
            hs = slice(h * DSA_HD, (h + 1) * DSA_HD)
            acc_scr[hs, :] = alpha * acc_scr[hs, :] + _dot(vt[hs, :], pr.astype(BF16))
        return jnp.concatenate(m_rows, axis=0), jnp.concatenate(l_rows, axis=0)

    def attend_two(ta, tb, ml, b_is_last):
        a, b = qk_products(ta), qk_products(tb)
        return softmax_pv(b, softmax_pv(a, ml, False), b_is_last)

    ml = (jnp.full((DSA_HEADS, Q_BLOCK), NEG_BIG, F32), jnp.zeros((DSA_HEADS, Q_BLOCK), F32))
    earlier = n_tiles - 1
    ml = lax.fori_loop(0, earlier // 2, lambda u, ml: attend_two(2 * u, 2 * u + 1, ml, False), ml)
    _, l_all = lax.cond(earlier % 2 == 1,
                        lambda ml: attend_two(n_tiles - 2, n_tiles - 1, ml, True),
                        lambda ml: softmax_pv(qk_products(n_tiles - 1), ml, True), ml)

    outs = [acc_scr[h * DSA_HD:(h + 1) * DSA_HD, :] / l_all[h:h + 1, :] for h in range(DSA_HEADS)]
    o_ref[...] = jnp.concatenate(outs, axis=0).T.astype(o_ref.dtype)


def _dsa(qT, qiT, wT, kn, vT, rs, batch, seq):
    n = kn.shape[0]
    blocks = seq // Q_BLOCK
    tiles = seq // KEY_TILE
    topk = min(TOPK_MAX, seq // 4)
    col_map = lambda b, j: (0, b * blocks + j)
    pos = jnp.arange(seq, dtype=I32)
    kpos = jnp.zeros((seq, LANES), F32).at[:, 0].set((pos >> 4).astype(F32)).at[:, 1].set((pos & 15).astype(F32))
    slopes = jnp.asarray([_alibi_slope(h) for h in range(DSA_HEADS)], F32).reshape(DSA_HEADS // 2, 2)
    slope_cols = jnp.repeat(slopes, Q_BLOCK, axis=1)
    slope_rows = jnp.zeros((DSA_HEADS // 2, LANES, 2 * Q_BLOCK), F32)
    slope_rows = slope_rows.at[:, 0, :].set(16.0 * slope_cols).at[:, 1, :].set(slope_cols)
    return pl.pallas_call(
        functools.partial(_dsa_kernel, topk=topk, n_pos_bits=seq.bit_length()),
        grid=(batch, blocks),
        in_specs=[
            pl.BlockSpec((DSA_DIM, Q_BLOCK), col_map),
            pl.BlockSpec((IDX_Q_DIM, Q_BLOCK), col_map),
            pl.BlockSpec((IDX_HEADS, Q_BLOCK), col_map),
            pl.BlockSpec((seq, DSA_DIM), lambda b, j: (b, 0)),
            pl.BlockSpec((tiles, DSA_DIM, KEY_TILE), lambda b, j: (b, 0, 0)),
            pl.BlockSpec((seq, LANES), lambda b, j: (b, 0)),
            _const_spec((seq, LANES)),
            _const_spec((DSA_HEADS // 2, LANES, 2 * Q_BLOCK)),
        ],
        out_specs=pl.BlockSpec((Q_BLOCK, DSA_DIM), lambda b, j: (b * blocks + j, 0)),
        out_shape=jax.ShapeDtypeStruct((n, DSA_DIM), BF16),
        scratch_shapes=[
            pltpu.VMEM((seq, IDX_HD), BF16),
            pltpu.VMEM((tiles, KEY_TILE, Q_BLOCK), F32),
            pltpu.VMEM((tiles, KEY_TILE, Q_BLOCK), I32),
            pltpu.VMEM((1, Q_BLOCK), F32),
            pltpu.VMEM((SUBLANES, Q_BLOCK), F32),
            pltpu.VMEM((DSA_DIM, Q_BLOCK), F32),
        ],
        compiler_params=pltpu.CompilerParams(dimension_semantics=("arbitrary", "arbitrary"), vmem_limit_bytes=VMEM_LIMIT),
        name="dsa",
    )(qT, qiT, wT, kn, vT, rs, kpos.astype(BF16), slope_rows.astype(BF16))


def _memkv_kernel(mem_ref, g_ref, w_ref, bd_ref, gk_ref, k_ref, v_ref):
    h = _rms_rows(mem_ref[...], g_ref[...]).astype(BF16)
    kv = _dot(h, w_ref[...])
    k = kv[:, :XA_DIM]
    kms = _dot((k * k).astype(BF16), bd_ref[...]) * (1.0 / XA_HD)
    k_ref[...] = (k * lax.rsqrt(kms + EPS) * gk_ref[...]).astype(BF16)
    v_ref[...] = kv[:, XA_DIM:].astype(BF16)


def _memkv(mem2, g_mem, w_xkv, g_xk):
    n, d = mem2.shape
    tm = min(ROW_TILE, n)
    return pl.pallas_call(
        _memkv_kernel,
        grid=(n // tm,),
        in_specs=[
            pl.BlockSpec((tm, d), lambda i: (i, 0)),
            _const_spec((1, d)),
            _const_spec((d, 2 * XA_DIM)),
            _const_spec((XA_DIM, XA_DIM)),
            _const_spec((1, XA_DIM)),
        ],
        out_specs=(pl.BlockSpec((tm, XA_DIM), lambda i: (i, 0)), pl.BlockSpec((tm, XA_DIM), lambda i: (i, 0))),
        out_shape=(jax.ShapeDtypeStruct((n, XA_DIM), BF16), jax.ShapeDtypeStruct((n, XA_DIM), BF16)),
        compiler_params=pltpu.CompilerParams(dimension_semantics=("arbitrary",), vmem_limit_bytes=VMEM_LIMIT),
        name="memkv",
    )(mem2, g_mem[None, :], w_xkv.astype(BF16), _block_diag_ones(XA_DIM, XA_HD), jnp.tile(g_xk, XA_HEADS)[None, :])


def _tail_kernel(x_ref, ya_ref, yb_ref, wout_ref, gx_ref, wxq_ref, bd_ref, gxq_ref, km_ref, vm_ref, wxo_ref,
                 gf_ref, wgu_ref, wd_ref, o_ref, *, ff_chunks):
    x1 = x_ref[...] + _dot(ya_ref[...], wout_ref[0:GDN_VAL_DIM, :]) + _dot(yb_ref[...], wout_ref[GDN_VAL_DIM:, :])

    h = _rms_rows(x1, gx_ref[...]).astype(BF16)
    q = _dot(h, wxq_ref[...])
    qms = _dot((q * q).astype(BF16), bd_ref[...]) * (1.0 / XA_HD)
    qn = (q * lax.rsqrt(qms + EPS) * gxq_ref[...]).astype(BF16)
    heads = []
    for hd in range(XA_HEADS):
        hs = slice(hd * XA_HD, (hd + 1) * XA_HD)
        s = _dot_nt(qn[:, hs], km_ref[0, :, hs]) * (XA_HD ** -0.5)
        pr = jnp.exp(s - jnp.max(s, axis=-1, keepdims=True))
        o = _dot(pr.astype(BF16), vm_ref[0, :, hs]) / jnp.sum(pr, axis=-1, keepdims=True)
        heads.append(o.astype(BF16))
    x2 = x1 + _dot(jnp.concatenate(heads, axis=1), wxo_ref[...])

    h = _rms_rows(x2, gf_ref[...]).astype(BF16)
    d_ff = wd_ref.shape[0]
    fc = d_ff // ff_chunks
    acc = x2
    for c in range(ff_chunks):
        gate = _dot(h, wgu_ref[:, c * fc:(c + 1) * fc])
        up = _dot(h, wgu_ref[:, d_ff + c * fc:d_ff + (c + 1) * fc])
        a = (gate * jax.nn.sigmoid(gate) * up).astype(BF16)
        acc = acc + _dot(a, wd_ref[c * fc:(c + 1) * fc, :])
    o_ref[...] = acc


def _tail(x2, ya, yb, w_out, g_xattn, w_xq, g_xq, km, vm, w_xo, g_ffn, w_gu, w_down, seq):
    n, d = x2.shape
    tm = ROW_TILE
    d_ff = w_down.shape[0]
    ff_chunks = 2 if (d_ff // 2) % LANES == 0 else 1
    n_mem = km.shape[1]
    per_seq = seq // tm
    return pl.pallas_call(
        functools.partial(_tail_kernel, ff_chunks=ff_chunks),
        grid=(n // tm,),
        in_specs=[
            pl.BlockSpec((tm, d), lambda i: (i, 0)),
            pl.BlockSpec((tm, GDN_VAL_DIM), lambda i: (i, 0)),
            pl.BlockSpec((tm, DSA_DIM), lambda i: (i, 0)),
            _const_spec((GDN_VAL_DIM + DSA_DIM, d)),
            _const_spec((1, d)),
            _const_spec((d, XA_DIM)),
            _const_spec((XA_DIM, XA_DIM)),
            _const_spec((1, XA_DIM)),
            pl.BlockSpec((1, n_mem, XA_DIM), lambda i: (i // per_seq, 0, 0)),
            pl.BlockSpec((1, n_mem, XA_DIM), lambda i: (i // per_seq, 0, 0)),
            _const_spec((XA_DIM, d)),
            _const_spec((1, d)),
            _const_spec((d, 2 * d_ff)),
            _const_spec((d_ff, d)),
        ],
        out_specs=pl.BlockSpec((tm, d), lambda i: (i, 0)),
        out_shape=jax.ShapeDtypeStruct((n, d), F32),
        compiler_params=pltpu.CompilerParams(dimension_semantics=("arbitrary",), vmem_limit_bytes=VMEM_LIMIT),
        name="tail",
    )(x2, ya, yb, w_out.astype(BF16), g_xattn[None, :], w_xq.astype(BF16), _block_diag_ones(XA_DIM, XA_HD),
      jnp.tile(g_xq, XA_HEADS)[None, :], km, vm, w_xo.astype(BF16), g_ffn[None, :], w_gu.astype(BF16),
      w_down.astype(BF16))


def kernel(x, mem, g_mix, w_in, conv_w, a_log, dt_bias, g_gdn_out, g_q_dsa, g_k_dsa, w_out, g_xattn, g_mem, w_xq,
           w_xkv, g_xq, g_xk, w_xo, g_ffn, w_gu, w_down):
    batch, seq, d = x.shape
    n_mem = mem.shape[1]
    assert seq % ROW_TILE == 0 and seq % KEY_TILE == 0 and ROW_TILE % KEY_TILE == 0
    for l in range(g_mix.shape[0]):
        x2 = x.reshape(batch * seq, d)
        qkv, z, kn, rs, qT, vT, qiT, wT = _inproj(x2, g_mix[l], w_in[l], g_q_dsa[l], g_k_dsa[l], conv_w[l], seq)
        ya = _gdn(qkv, z, rs, a_log[l], dt_bias[l], g_gdn_out[l], batch, seq)
        yb = _dsa(qT, qiT, wT, kn, vT, rs, batch, seq)
        km, vm = _memkv(mem.reshape(batch * n_mem, d), g_mem[l], w_xkv[l], g_xk[l])
        km = km.reshape(batch, n_mem, XA_DIM)
        vm = vm.reshape(batch, n_mem, XA_DIM)
        x = _tail(x2, ya, yb, w_out[l], g_xattn[l], w_xq[l], g_xq[l], km, vm, w_xo[l], g_ffn[l], w_gu[l],
                  w_down[l], seq).reshape(batch, seq, d)
    return x
```

```python
import functools

import jax
import jax.numpy as jnp
from jax import lax
from jax.experimental import pallas as pl
from jax.experimental.pallas import tpu as pltpu

F32 = jnp.float32
BF16 = jnp.bfloat16
I32 = jnp.int32

EPS = 1e-6
CHUNK = 64
GDN_HEADS, GDN_DK, GDN_DV, CONV_K = 4, 128, 128, 4
DSA_HEADS, DSA_HD = 8, 64
IDX_HEADS, IDX_HD = 16, 64
TOPK_MAX = 256
XA_HEADS, XA_HD = 4, 128

GDN_KEY_DIM = GDN_HEADS * GDN_DK
GDN_VAL_DIM = GDN_HEADS * GDN_DV
GDN_CONV_CH = 2 * GDN_KEY_DIM + GDN_VAL_DIM
DSA_DIM = DSA_HEADS * DSA_HD
IDX_Q_DIM = IDX_HEADS * IDX_HD
XA_DIM = XA_HEADS * XA_HD

LANES = 128
SUBLANES = 8
VMEM_LIMIT = 56 * 1024 * 1024

RS_IDXK = 0
RS_BETA = IDX_HD
RS_A = IDX_HD + GDN_HEADS

ROW_TILE = 512
ROW_PARTS = 2
KEY_TILE = 256
Q_BLOCK = 128
GDN_CHUNKS_PER_STEP = 4
GDN_SEQS_PER_STEP = 4

BISECT_ROUNDS = 20
COUNT_CHAINS = 4
NEG_BIG = -1e30


def _dot(a, b):
    return jnp.dot(a, b, preferred_element_type=F32)


def _dot_nt(a, b):
    return lax.dot_general(a, b, (((1,), (1,)), ((), ())), preferred_element_type=F32)


def _dot_f32(a, b):
    return jnp.dot(a, b, preferred_element_type=F32, precision=lax.Precision.HIGHEST)


def _rms_rows(x, g):
    ms = jnp.mean(x * x, axis=-1, keepdims=True)
    return x * lax.rsqrt(ms + EPS) * g


def _const_spec(shape):
    nd = len(shape)
    return pl.BlockSpec(shape, lambda *_: (0,) * nd, pipeline_mode=pl.Buffered(1))


def _block_diag_ones(n, blk):
    r = lax.broadcasted_iota(I32, (n, n), 0) // blk
    c = lax.broadcasted_iota(I32, (n, n), 1) // blk
    return (r == c).astype(BF16)


def _inproj_kernel(x_ref, g_ref, wrow_ref, wcol_ref, bd_ref, gk_ref, gq_ref, cw_ref,
                   qkv_ref, z_ref, kn_ref, rs_ref, qT_ref, vT_ref, qiT_ref, wT_ref, xc_ref, *, tiles_per_seq):
    tm = x_ref.shape[0]

    @pl.when(pl.program_id(0) % tiles_per_seq == 0)
    def _start_of_sequence():
        xc_ref[0:SUBLANES, :] = jnp.zeros((SUBLANES, GDN_CONV_CH), F32)

    h = _rms_rows(x_ref[...], g_ref[...]).astype(BF16)
    c0, c1, c2, c3 = GDN_CONV_CH, GDN_CONV_CH + GDN_VAL_DIM, GDN_CONV_CH + GDN_VAL_DIM + DSA_DIM, \
        GDN_CONV_CH + GDN_VAL_DIM + DSA_DIM + LANES

    xc_ref[SUBLANES:SUBLANES + tm, :] = _dot(h, wrow_ref[:, 0:c0])

    z_ref[...] = _dot(h, wrow_ref[:, c0:c1])
    k = _dot(h, wrow_ref[:, c1:c2])
    kms = _dot((k * k).astype(BF16), bd_ref[...]) * (1.0 / DSA_HD)
    kn_ref[...] = (k * lax.rsqrt(kms + EPS) * gk_ref[...]).astype(BF16)
    rs_ref[...] = _dot(h, wrow_ref[:, c1 + DSA_DIM:c3])

    r0, r1, r2, r3 = DSA_DIM, 2 * DSA_DIM, 2 * DSA_DIM + IDX_Q_DIM, 2 * DSA_DIM + IDX_Q_DIM + IDX_HEADS
    qT = _dot_nt(wcol_ref[0:r0, :], h)
    qms = _dot(bd_ref[...], (qT * qT).astype(BF16)) * (1.0 / DSA_HD)
    qT_ref[...] = (qT * lax.rsqrt(qms + EPS) * gq_ref[...] * (DSA_HD ** -0.5)).astype(BF16)
    vT = _dot_nt(wcol_ref[r0:r1, :], h).astype(BF16)
    for i in range(vT_ref.shape[0]):
        vT_ref[i] = vT[:, i * KEY_TILE:(i + 1) * KEY_TILE]
    qiT_ref[...] = (_dot_nt(wcol_ref[r1:r2, :], h) * (IDX_HD ** -0.5)).astype(BF16)
    wT_ref[...] = _dot_nt(wcol_ref[r2:r3, :], h) * (IDX_HEADS ** -0.5)

    cw = cw_ref[...]
    conv = cw[0:1, :] * xc_ref[SUBLANES - 3:SUBLANES - 3 + tm, :]
    for j in range(1, CONV_K):
        conv = conv + cw[j:j + 1, :] * xc_ref[SUBLANES - 3 + j:SUBLANES - 3 + j + tm, :]
    xc_ref[0:SUBLANES, :] = xc_ref[tm:tm + SUBLANES, :]
    act = conv * jax.nn.sigmoid(conv)
    for hd in range(GDN_HEADS):
        qs = slice(hd * GDN_DK, (hd + 1) * GDN_DK)
        ks = slice(GDN_KEY_DIM + hd * GDN_DK, GDN_KEY_DIM + (hd + 1) * GDN_DK)
        q, k = act[:, qs], act[:, ks]
        qkv_ref[:, qs] = q * lax.rsqrt(jnp.sum(q * q, axis=-1, keepdims=True) + EPS) * (GDN_DK ** -0.5)
        qkv_ref[:, ks] = k * lax.rsqrt(jnp.sum(k * k, axis=-1, keepdims=True) + EPS)
    qkv_ref[:, 2 * GDN_KEY_DIM:] = act[:, 2 * GDN_KEY_DIM:]


def _inproj(x2, g_mix, w_in, g_q, g_k, conv_w, seq):
    n, d = x2.shape
    tm = ROW_TILE
    sizes = (GDN_CONV_CH, GDN_VAL_DIM, GDN_HEADS, GDN_HEADS, DSA_DIM, DSA_DIM, DSA_DIM, IDX_Q_DIM, IDX_HD, IDX_HEADS)
    offs = [0]
    for s in sizes:
        offs.append(offs[-1] + s)
    (w_qkv, w_z, w_b, w_a, w_q, w_k, w_v, w_iq, w_ik, w_iw) = [w_in[:, offs[i]:offs[i + 1]] for i in range(len(sizes))]
    pad = jnp.zeros((d, LANES - IDX_HD - 2 * GDN_HEADS), w_in.dtype)
    w_row = jnp.concatenate([w_qkv, w_z, w_k, w_ik, w_b, w_a, pad], axis=1).astype(BF16)
    w_col = jnp.concatenate([w_q, w_v, w_iq, w_iw], axis=1).T.astype(BF16)
    bd = _block_diag_ones(DSA_DIM, DSA_HD)
    gk_row = jnp.tile(g_k, DSA_HEADS)[None, :]
    gq_col = jnp.tile(g_q, DSA_HEADS)[:, None]
    nrow, ncol = w_row.shape[1], w_col.shape[0]
    out_shape = (
        jax.ShapeDtypeStruct((n, GDN_CONV_CH), F32),
        jax.ShapeDtypeStruct((n, GDN_VAL_DIM), F32),
        jax.ShapeDtypeStruct((n, DSA_DIM), BF16),
        jax.ShapeDtypeStruct((n, LANES), F32),
        jax.ShapeDtypeStruct((DSA_DIM, n), BF16),
        jax.ShapeDtypeStruct((n // KEY_TILE, DSA_DIM, KEY_TILE), BF16),
        jax.ShapeDtypeStruct((IDX_Q_DIM, n), BF16),
        jax.ShapeDtypeStruct((IDX_HEADS, n), F32),
    )
    return pl.pallas_call(
        functools.partial(_inproj_kernel, tiles_per_seq=seq // tm),
        grid=(n // tm,),
        in_specs=[
            pl.BlockSpec((tm, d), lambda i: (i, 0)),
            _const_spec((1, d)),
            _const_spec((d, nrow)),
            _const_spec((ncol, d)),
            _const_spec((DSA_DIM, DSA_DIM)),
            _const_spec((1, DSA_DIM)),
            _const_spec((DSA_DIM, 1)),
            _const_spec((CONV_K, GDN_CONV_CH)),
        ],
        out_specs=(
            pl.BlockSpec((tm, GDN_CONV_CH), lambda i: (i, 0)),
            pl.BlockSpec((tm, GDN_VAL_DIM), lambda i: (i, 0)),
            pl.BlockSpec((tm, DSA_DIM), lambda i: (i, 0)),
            pl.BlockSpec((tm, LANES), lambda i: (i, 0)),
            pl.BlockSpec((DSA_DIM, tm), lambda i: (0, i)),
            pl.BlockSpec((tm // KEY_TILE, DSA_DIM, KEY_TILE), lambda i: (i, 0, 0)),
            pl.BlockSpec((IDX_Q_DIM, tm), lambda i: (0, i)),
            pl.BlockSpec((IDX_HEADS, tm), lambda i: (0, i)),
        ),
        out_shape=out_shape,
        scratch_shapes=[pltpu.VMEM((tm + SUBLANES, GDN_CONV_CH), F32)],
        compiler_params=pltpu.CompilerParams(dimension_semantics=("arbitrary",), vmem_limit_bytes=VMEM_LIMIT),
        name="inproj",
    )(x2, g_mix[None, :], w_row, w_col, bd, gk_row, gq_col, conv_w)


def _gdn_kernel(qkv_ref, z_ref, rs_ref, alog_ref, dtb_ref, gout_ref, y_ref, s_ref):
    nb, rows = qkv_ref.shape[0], qkv_ref.shape[1]
    nc = rows // CHUNK

    @pl.when(pl.program_id(1) == 0)
    def _start_of_sequence():
        s_ref[...] = jnp.zeros(s_ref.shape, F32)

    rs = rs_ref[...]
    beta_all = jax.nn.sigmoid(rs)
    sp_in = rs + dtb_ref[...]
    softplus = jnp.maximum(sp_in, 0.0) + jnp.log(1.0 + jnp.exp(-jnp.abs(sp_in)))
    g_all = -jnp.exp(alog_ref[...]) * softplus

    ri = lax.broadcasted_iota(I32, (CHUNK, CHUNK), 0)
    ci = lax.broadcasted_iota(I32, (CHUNK, CHUNK), 1)
    incl = ri >= ci
    strict = ri > ci
    ltri = incl.astype(F32)
    eye = (ri == ci).astype(F32)
    gout = gout_ref[...]

    chains = [(b, c, h) for c in range(nc) for b in range(nb) for h in range(GDN_HEADS)]
    dcum, dcum_t = {}, {}
    for c in range(nc):
        for b in range(nb):
            d = _dot_f32(ltri, g_all[b, c * CHUNK:(c + 1) * CHUNK, :])
            dcum[b, c] = d
            dcum_t[b, c] = d.T

    q16, k16, kb16, vb16, kw16, qdec16, kdect16, gamma, last = ([] for _ in range(9))
    for b, c, h in chains:
        r = slice(c * CHUNK, (c + 1) * CHUNK)
        q = qkv_ref[b, r, h * GDN_DK:(h + 1) * GDN_DK]
        k = qkv_ref[b, r, GDN_KEY_DIM + h * GDN_DK:GDN_KEY_DIM + (h + 1) * GDN_DK]
        v = qkv_ref[b, r, 2 * GDN_KEY_DIM + h * GDN_DV:2 * GDN_KEY_DIM + (h + 1) * GDN_DV]
        beta = beta_all[b, r, RS_BETA + h:RS_BETA + h + 1]
        d_col = dcum[b, c][:, RS_A + h:RS_A + h + 1]
        d_row = dcum_t[b, c][RS_A + h:RS_A + h + 1, :]
        d_last = dcum[b, c][CHUNK - 1:CHUNK, RS_A + h:RS_A + h + 1]
        e_col = jnp.exp(d_col)
        kb = k * beta
        gamma.append(jnp.exp(jnp.where(incl, d_col - d_row, -jnp.inf)))
        q16.append(q.astype(BF16))
        k16.append(k.astype(BF16))
        kb16.append(kb.astype(BF16))
        vb16.append((v * beta).astype(BF16))
        kw16.append((kb * e_col).astype(BF16))
        qdec16.append((q * e_col).astype(BF16))
        kdect16.append((k * jnp.exp(d_last - d_col)).T.astype(BF16))
        last.append(jnp.exp(d_last))

    n = len(chains)
    p16 = [(-jnp.where(strict, _dot_nt(kb16[i], k16[i]) * gamma[i], 0.0)).astype(BF16) for i in range(n)]
    qk16 = [(_dot_nt(q16[i], k16[i]) * gamma[i]).astype(BF16) for i in range(n)]
    t_mat = [eye + p16[i].astype(F32) for i in range(n)]
    p16 = [_dot(p16[i], p16[i]).astype(BF16) for i in range(n)]
    for _ in range(4):
        t_mat = [t_mat[i] + _dot(t_mat[i].astype(BF16), p16[i]) for i in range(n)]
        p16 = [_dot(p16[i], p16[i]).astype(BF16) for i in range(n)]
    t16 = [(t_mat[i] + _dot(t_mat[i].astype(BF16), p16[i])).astype(BF16) for i in range(n)]
    u = [_dot(t16[i], vb16[i]) for i in range(n)]
    w16 = [_dot(t16[i], kw16[i]).astype(BF16) for i in range(n)]

    per_chunk = nb * GDN_HEADS
    s = [s_ref[b, h] for b in range(nb) for h in range(GDN_HEADS)]
    for c in range(nc):
        ids = range(c * per_chunk, (c + 1) * per_chunk)
        s16 = [s[j].astype(BF16) for j in range(per_chunk)]
        v_new16 = [(u[i] - _dot(w16[i], s16[j])).astype(BF16) for j, i in enumerate(ids)]
        o_state = [_dot(qdec16[i], s16[j]) for j, i in enumerate(ids)]
        o = [o_state[j] + _dot(qk16[i], v_new16[j]) for j, i in enumerate(ids)]
        s = [s[j] * last[i] + _dot(kdect16[i], v_new16[j]) for j, i in enumerate(ids)]
        for j, i in enumerate(ids):
            b, _, h = chains[i]
            zc = z_ref[b, c * CHUNK:(c + 1) * CHUNK, h * GDN_DV:(h + 1) * GDN_DV]
            y = _rms_rows(o[j], gout) * (zc * jax.nn.sigmoid(zc))
            y_ref[b, c * CHUNK:(c + 1) * CHUNK, h * GDN_DV:(h + 1) * GDN_DV] = y.astype(y_ref.dtype)
    for j in range(per_chunk):
        s_ref[j // GDN_HEADS, j % GDN_HEADS] = s[j]


def _gdn(qkv, z, rs, a_log, dt_bias, g_out, batch, seq):
    n = qkv.shape[0]
    rows = GDN_CHUNKS_PER_STEP * CHUNK
    nb = GDN_SEQS_PER_STEP if batch % GDN_SEQS_PER_STEP == 0 else 1
    lane_vec = lambda v: jnp.zeros((1, LANES), F32).at[0, RS_A:RS_A + GDN_HEADS].set(v)
    per_seq = lambda a: a.reshape(batch, seq, a.shape[-1])
    block = lambda width: pl.BlockSpec((nb, rows, width), lambda b, i: (b, i, 0))
    y = pl.pallas_call(
        _gdn_kernel,
        grid=(batch // nb, seq // rows),
        in_specs=[
            block(GDN_CONV_CH),
            block(GDN_VAL_DIM),
            block(LANES),
            _const_spec((1, LANES)),
            _const_spec((1, LANES)),
            _const_spec((1, GDN_DV)),
        ],
        out_specs=block(GDN_VAL_DIM),
        out_shape=jax.ShapeDtypeStruct((batch, seq, GDN_VAL_DIM), BF16),
        scratch_shapes=[
            pltpu.VMEM((nb, GDN_HEADS, GDN_DK, GDN_DV), F32),
        ],
        compiler_params=pltpu.CompilerParams(dimension_semantics=("arbitrary", "arbitrary"), vmem_limit_bytes=VMEM_LIMIT),
        name="gdn",
    )(per_seq(qkv), per_seq(z), per_seq(rs), lane_vec(a_log), lane_vec(dt_bias), g_out[None, :])
    return y.reshape(n, GDN_VAL_DIM)


def _alibi_slope(h):
    return 2.0 ** (-8.0 * (h + 1) / DSA_HEADS)


def _order_key(x):
    bits = lax.bitcast_convert_type(x, I32)
    return (bits ^ ((bits >> 31) & jnp.int32(0x7FFFFFFF))) >> 1


def _order_key_to_f32(key):
    full = key << 1
    return lax.bitcast_convert_type(full ^ ((full >> 31) & jnp.int32(0x7FFFFFFF)), F32)


def _tiles_for_block(j):
    return (j * Q_BLOCK + Q_BLOCK + KEY_TILE - 1) // KEY_TILE


def _dsa_kernel(qT_ref, qiT_ref, wT_ref, kn_ref, vT_ref, rs_ref, kpos_ref, slope_ref, o_ref,
                idxk_scr, sc_scr, key_scr, thr_scr, cnt_scr, acc_scr, *, topk, n_pos_bits):
    j = pl.program_id(1)

    @pl.when(j == 0)
    def _new_sequence():
        idxk_scr[...] = rs_ref[:, RS_IDXK:RS_IDXK + IDX_HD].astype(BF16)

    n_tiles = _tiles_for_block(j)
    lane = lax.broadcasted_iota(I32, (1, Q_BLOCK), 1)
    qpos = j * Q_BLOCK + lane
    key_limit = ((qpos >> 6) + 1) << 6
    row = lax.broadcasted_iota(I32, (KEY_TILE, Q_BLOCK), 0)

    w_t = wT_ref[...]
    qi_pairs = [jnp.concatenate([qiT_ref[(2 * p) * IDX_HD:(2 * p + 1) * IDX_HD, :],
                                 qiT_ref[(2 * p + 1) * IDX_HD:(2 * p + 2) * IDX_HD, :]], axis=1)
                for p in range(IDX_HEADS // 2)]

    def score_tile(t, carry):
        s_max, s_min = carry
        r0 = pl.multiple_of(t * KEY_TILE, KEY_TILE)
        kt = idxk_scr[pl.ds(r0, KEY_TILE), :]
        acc = jnp.zeros((KEY_TILE, Q_BLOCK), F32)
        for p in range(IDX_HEADS // 2):
            d = _dot(kt, qi_pairs[p])
            acc = acc + jnp.maximum(d[:, :Q_BLOCK], 0.0) * w_t[2 * p:2 * p + 1, :]
            acc = acc + jnp.maximum(d[:, Q_BLOCK:], 0.0) * w_t[2 * p + 1:2 * p + 2, :]
        admissible = r0 + row < key_limit
        stored = jnp.where(admissible, acc, -jnp.inf)
        sc_scr[t] = stored
        key_scr[t] = _order_key(stored)
        s_max = jnp.maximum(s_max, jnp.max(stored, axis=0, keepdims=True))
        s_min = jnp.minimum(s_min, jnp.min(jnp.where(admissible, acc, jnp.inf), axis=0, keepdims=True))
        return s_max, s_min

    s_max, s_min = lax.fori_loop(0, n_tiles, score_tile, (jnp.full((1, Q_BLOCK), -jnp.inf, F32),
                                                          jnp.full((1, Q_BLOCK), jnp.inf, F32)))

    def count_tiles(per_tile):
        cnt_scr[...] = jnp.zeros(cnt_scr.shape, F32)
        base = jnp.int32(0)
        arm = 1 << (sc_scr.shape[0].bit_length() - 1)
        while arm >= 1:
            has = (n_tiles & arm) != 0

            @pl.when(has)
            def _run(base=base, arm=arm):
                chains = [jnp.zeros((SUBLANES, Q_BLOCK), F32) for _ in range(COUNT_CHAINS)]
                for i in range(arm):
                    flags = per_tile(base + i)
                    for r in range(KEY_TILE // SUBLANES):
                        chains[r % COUNT_CHAINS] = chains[r % COUNT_CHAINS] + flags[r * SUBLANES:(r + 1) * SUBLANES, :]
                while len(chains) > 1:
                    chains = [chains[k] + chains[k + 1] for k in range(0, len(chains), 2)]
                cnt_scr[...] += chains[0]

            base = base + jnp.where(has, arm, 0)
            arm //= 2
        return jnp.sum(cnt_scr[...], axis=0, keepdims=True).astype(I32)

    def count_ge(thr):
        return count_tiles(lambda i: jnp.where(sc_scr[i] >= thr, 1.0, 0.0))

    def count_keys_ge(key, tiles):
        one = jnp.int32(0x3F800000)
        chains = [jnp.zeros((SUBLANES, Q_BLOCK), F32) for _ in range(COUNT_CHAINS)]
        for i in range(tiles):
            below = lax.bitcast_convert_type(((key_scr[i] - key) >> 31) & one, F32)
            for r in range(KEY_TILE // SUBLANES):
                chains[r % COUNT_CHAINS] = chains[r % COUNT_CHAINS] + below[r * SUBLANES:(r + 1) * SUBLANES, :]
        while len(chains) > 1:
            chains = [chains[k] + chains[k + 1] for k in range(0, len(chains), 2)]
        return tiles * KEY_TILE - jnp.sum(chains[0], axis=0, keepdims=True).astype(I32)

    def search_with_tiles(tiles):
        def bisect_step(_, st):
            lo, hi = st
            mid = lo * 0.5 + hi * 0.5
            take = count_keys_ge(_order_key(mid), tiles) >= topk
            return jnp.where(take, mid, lo), jnp.where(take, hi, mid)

        return lambda st: lax.fori_loop(0, BISECT_ROUNDS, bisect_step, st)

    above_max = s_max + jnp.maximum(jnp.abs(s_max) * 1e-6, 1e-30)
    lo, _ = lax.switch(n_tiles - 1, [search_with_tiles(n) for n in range(1, sc_scr.shape[0] + 1)], (s_min, above_max))
    few = key_limit < topk
    thr = jnp.where(few, jnp.finfo(F32).min, _order_key_to_f32(_order_key(lo)))
    cnt = jnp.where(few, topk, count_ge(thr))
    thr_scr[...] = thr

    @pl.when(jnp.max(jnp.abs(cnt - topk)) > 0)
    def _resolve():
        def max_below(thr):
            def tile(t, m):
                sc = sc_scr[t]
                return jnp.maximum(m, jnp.max(jnp.where(sc < thr, sc, -jnp.inf), axis=0, keepdims=True))
            return lax.fori_loop(0, n_tiles, tile, jnp.full((1, Q_BLOCK), -jnp.inf, F32))

        def lower(st):
            thr, cnt = st
            nxt = max_below(thr)
            short = cnt < topk
            return jnp.where(short, nxt, thr), jnp.where(short, count_ge(nxt), cnt)

        def min_above(thr):
            def tile(t, m):
                sc = sc_scr[t]
                return jnp.minimum(m, jnp.min(jnp.where(sc > thr, sc, jnp.inf), axis=0, keepdims=True))
            return lax.fori_loop(0, n_tiles, tile, jnp.full((1, Q_BLOCK), jnp.inf, F32))

        def count_eq_below(thr, pos_limit):
            def tile(t, c):
                r0 = t * KEY_TILE
                hit = (sc_scr[t] == thr) & (r0 + row < pos_limit)
                return c + jnp.sum(hit.astype(I32), axis=0, keepdims=True)
            return lax.fori_loop(0, n_tiles, tile, jnp.zeros((1, Q_BLOCK), I32))

        def body(st):
            thr, cnt = st
            nxt = min_above(thr)
            cnt_n = count_ge(nxt)
            active = cnt > topk
            advance = active & (cnt_n >= topk)
            tie = active & (cnt_n < topk)
            @pl.when(jnp.max(tie.astype(I32)) > 0)
            def _drop_surplus_ties():
                need = topk - cnt_n
                pos = jnp.zeros((1, Q_BLOCK), I32)
                for b in range(n_pos_bits - 1, -1, -1):
                    cand = pos + (1 << b)
                    pos = jnp.where(count_eq_below(thr, cand) < need, cand, pos)

                def drop_tile(t, carry):
                    r0 = t * KEY_TILE
                    sc = sc_scr[t]
                    sc_scr[t] = jnp.where(tie & (sc == thr) & (r0 + row > pos), -jnp.inf, sc)
                    return carry

                lax.fori_loop(0, n_tiles, drop_tile, 0)

            return jnp.where(advance, nxt, thr), jnp.where(advance, cnt_n, jnp.where(tie, topk, cnt))

        st = lax.while_loop(lambda st: jnp.min(st[1]) < topk, lower, (thr, cnt))
        thr_scr[...] = lax.while_loop(lambda st: jnp.max(st[1]) > topk, body, st)[0]

    thr = thr_scr[...]

    zero = jnp.zeros((DSA_HD, Q_BLOCK), BF16)
    q_pairs = []
    for p in range(DSA_HEADS // 2):
        a = qT_ref[(2 * p) * DSA_HD:(2 * p + 1) * DSA_HD, :]
        b = qT_ref[(2 * p + 1) * DSA_HD:(2 * p + 2) * DSA_HD, :]
        q_pairs.append(jnp.concatenate([jnp.concatenate([a, zero], axis=1),
                                        jnp.concatenate([zero, b], axis=1),
                                        slope_ref[p]], axis=0))
    acc_scr[...] = jnp.zeros(acc_scr.shape, F32)
    qposf = qpos.astype(F32)

    def qk_products(t):
        r0 = pl.multiple_of(t * KEY_TILE, KEY_TILE)
        kt = kn_ref[pl.ds(r0, KEY_TILE), :]
        kp = kpos_ref[pl.ds(r0, KEY_TILE), :]
        s2 = [_dot(jnp.concatenate([kt[:, p * 2 * DSA_HD:(p + 1) * 2 * DSA_HD], kp], axis=1), q_pairs[p])
              for p in range(DSA_HEADS // 2)]
        return t, sc_scr[t] >= thr, s2

    def softmax_pv(tile, ml, last_tile):
        t, sel, s2 = tile
        vt = vT_ref[t]
        if last_tile:
            after = jnp.maximum((t * KEY_TILE + row).astype(F32) - qposf, 0.0)
        m_all, l_all = ml
        m_rows, l_rows = [], []
        for h in range(DSA_HEADS):
            s = s2[h // 2][:, (h % 2) * Q_BLOCK:(h % 2 + 1) * Q_BLOCK]
            if last_tile:
                s = s - (2.0 * _alibi_slope(h)) * after
            s = jnp.where(sel, s, NEG_BIG)
            m_old = m_all[h:h + 1, :]
            m_new = jnp.maximum(m_old, jnp.max(s, axis=0, keepdims=True))
            alpha = jnp.exp(m_old - m_new)
            pr = jnp.exp(s - m_new)
            l_rows.append(alpha * l_all[h:h + 1, :] + jnp.sum(pr, axis=0, keepdims=True))
            m_rows.append(m_new)
            hs = slice(h * DSA_HD, (h + 1) * DSA_HD)
            acc_scr[hs, :] = alpha * acc_scr[hs, :] + _dot(vt[hs, :], pr.astype(BF16))
        return jnp.concatenate(m_rows, axis=0), jnp.concatenate(l_rows, axis=0)

    def attend_two(ta, tb, ml, b_is_last):
        a, b = qk_products(ta), qk_products(tb)
        return softmax_pv(b, softmax_pv(a, ml, False), b_is_last)

    ml = (jnp.full((DSA_HEADS, Q_BLOCK), NEG_BIG, F32), jnp.zeros((DSA_HEADS, Q_BLOCK), F32))
    earlier = n_tiles - 1
    ml = lax.fori_loop(0, earlier // 2, lambda u, ml: attend_two(2 * u, 2 * u + 1, ml, False), ml)
    _, l_all = lax.cond(earlier % 2 == 1,
                        lambda ml: attend_two(n_tiles - 2, n_tiles - 1, ml, True),
                        lambda ml: softmax_pv(qk_products(n_tiles - 1), ml, True), ml)

    outs = [acc_scr[h * DSA_HD:(h + 1) * DSA_HD, :] / l_all[h:h + 1, :] for h in range(DSA_HEADS)]
    o_ref[...] = jnp.concatenate(outs, axis=0).T.astype(o_ref.dtype)


def _dsa(qT, qiT, wT, kn, vT, rs, batch, seq):
    n = kn.shape[0]
    blocks = seq // Q_BLOCK
    tiles = seq // KEY_TILE
    topk = min(TOPK_MAX, seq // 4)
    col_map = lambda b, j: (0, b * blocks + j)
    pos = jnp.arange(seq, dtype=I32)
    kpos = jnp.zeros((seq, LANES), F32).at[:, 0].set((pos >> 4).astype(F32)).at[:, 1].set((pos & 15).astype(F32))
    slopes = jnp.asarray([_alibi_slope(h) for h in range(DSA_HEADS)], F32).reshape(DSA_HEADS // 2, 2)
    slope_cols = jnp.repeat(slopes, Q_BLOCK, axis=1)
    slope_rows = jnp.zeros((DSA_HEADS // 2, LANES, 2 * Q_BLOCK), F32)
    slope_rows = slope_rows.at[:, 0, :].set(16.0 * slope_cols).at[:, 1, :].set(slope_cols)
    return pl.pallas_call(
        functools.partial(_dsa_kernel, topk=topk, n_pos_bits=seq.bit_length()),
        grid=(batch, blocks),
        in_specs=[
            pl.BlockSpec((DSA_DIM, Q_BLOCK), col_map),
            pl.BlockSpec((IDX_Q_DIM, Q_BLOCK), col_map),
            pl.BlockSpec((IDX_HEADS, Q_BLOCK), col_map),
            pl.BlockSpec((seq, DSA_DIM), lambda b, j: (b, 0)),
            pl.BlockSpec((tiles, DSA_DIM, KEY_TILE), lambda b, j: (b, 0, 0)),
            pl.BlockSpec((seq, LANES), lambda b, j: (b, 0)),
            _const_spec((seq, LANES)),
            _const_spec((DSA_HEADS // 2, LANES, 2 * Q_BLOCK)),
        ],
        out_specs=pl.BlockSpec((Q_BLOCK, DSA_DIM), lambda b, j: (b * blocks + j, 0)),
        out_shape=jax.ShapeDtypeStruct((n, DSA_DIM), BF16),
        scratch_shapes=[
            pltpu.VMEM((seq, IDX_HD), BF16),
            pltpu.VMEM((tiles, KEY_TILE, Q_BLOCK), F32),
            pltpu.VMEM((tiles, KEY_TILE, Q_BLOCK), I32),
            pltpu.VMEM((1, Q_BLOCK), F32),
            pltpu.VMEM((SUBLANES, Q_BLOCK), F32),
            pltpu.VMEM((DSA_DIM, Q_BLOCK), F32),
        ],
        compiler_params=pltpu.CompilerParams(dimension_semantics=("arbitrary", "arbitrary"), vmem_limit_bytes=VMEM_LIMIT),
        name="dsa",
    )(qT, qiT, wT, kn, vT, rs, kpos.astype(BF16), slope_rows.astype(BF16))


def _memkv_kernel(mem_ref, g_ref, w_ref, bd_ref, gk_ref, k_ref, v_ref):
    h = _rms_rows(mem_ref[...], g_ref[...]).astype(BF16)
    kv = _dot(h, w_ref[...])
    k = kv[:, :XA_DIM]
    kms = _dot((k * k).astype(BF16), bd_ref[...]) * (1.0 / XA_HD)
    k_ref[...] = (k * lax.rsqrt(kms + EPS) * gk_ref[...]).astype(BF16)
    v_ref[...] = kv[:, XA_DIM:].astype(BF16)


def _memkv(mem2, g_mem, w_xkv, g_xk):
    n, d = mem2.shape
    tm = min(ROW_TILE, n)
    return pl.pallas_call(
        _memkv_kernel,
        grid=(n // tm,),
        in_specs=[
            pl.BlockSpec((tm, d), lambda i: (i, 0)),
            _const_spec((1, d)),
            _const_spec((d, 2 * XA_DIM)),
            _const_spec((XA_DIM, XA_DIM)),
            _const_spec((1, XA_DIM)),
        ],
        out_specs=(pl.BlockSpec((tm, XA_DIM), lambda i: (i, 0)), pl.BlockSpec((tm, XA_DIM), lambda i: (i, 0))),
        out_shape=(jax.ShapeDtypeStruct((n, XA_DIM), BF16), jax.ShapeDtypeStruct((n, XA_DIM), BF16)),
        compiler_params=pltpu.CompilerParams(dimension_semantics=("arbitrary",), vmem_limit_bytes=VMEM_LIMIT),
        name="memkv",
    )(mem2, g_mem[None, :], w_xkv.astype(BF16), _block_diag_ones(XA_DIM, XA_HD), jnp.tile(g_xk, XA_HEADS)[None, :])


def _tail_kernel(x_ref, ya_ref, yb_ref, wout_ref, gx_ref, wxq_ref, bd_ref, gxq_ref, km_ref, vm_ref, wxo_ref,
                 gf_ref, wgu_ref, wd_ref, o_ref, *, ff_chunks):
    tm = x_ref.shape[0]
    part = tm // ROW_PARTS
    groups = [slice(g * part, (g + 1) * part) for g in range(ROW_PARTS)]
    d_ff = wd_ref.shape[0]
    fc = d_ff // ff_chunks

    x1 = [x_ref[r, :] + _dot(ya_ref[r, :], wout_ref[0:GDN_VAL_DIM, :]) + _dot(yb_ref[r, :], wout_ref[GDN_VAL_DIM:, :])
          for r in groups]

    qn = []
    for g in range(ROW_PARTS):
        h = _rms_rows(x1[g], gx_ref[...]).astype(BF16)
        q = _dot(h, wxq_ref[...])
        qms = _dot((q * q).astype(BF16), bd_ref[...]) * (1.0 / XA_HD)
        qn.append((q * lax.rsqrt(qms + EPS) * gxq_ref[...]).astype(BF16))

    attn = []
    for g in range(ROW_PARTS):
        heads = []
        for hd in range(XA_HEADS):
            hs = slice(hd * XA_HD, (hd + 1) * XA_HD)
            s = _dot_nt(qn[g][:, hs], km_ref[0, :, hs]) * (XA_HD ** -0.5)
            pr = jnp.exp(s - jnp.max(s, axis=-1, keepdims=True))
            o = _dot(pr.astype(BF16), vm_ref[0, :, hs]) / jnp.sum(pr, axis=-1, keepdims=True)
            heads.append(o.astype(BF16))
        attn.append(jnp.concatenate(heads, axis=1))

    x2 = [x1[g] + _dot(attn[g], wxo_ref[...]) for g in range(ROW_PARTS)]
    h = [_rms_rows(x2[g], gf_ref[...]).astype(BF16) for g in range(ROW_PARTS)]
    acc = x2
    for c in range(ff_chunks):
        for g in range(ROW_PARTS):
            gate = _dot(h[g], wgu_ref[:, c * fc:(c + 1) * fc])
            up = _dot(h[g], wgu_ref[:, d_ff + c * fc:d_ff + (c + 1) * fc])
            a = (gate * jax.nn.sigmoid(gate) * up).astype(BF16)
            acc[g] = acc[g] + _dot(a, wd_ref[c * fc:(c + 1) * fc, :])
    for g, r in enumerate(groups):
        o_ref[r, :] = acc[g]


def _tail(x2, ya, yb, w_out, g_xattn, w_xq, g_xq, km, vm, w_xo, g_ffn, w_gu, w_down, seq):
    n, d = x2.shape
    tm = ROW_TILE
    d_ff = w_down.shape[0]
    ff_chunks = 2 if (d_ff // 2) % LANES == 0 else 1
    n_mem = km.shape[1]
    per_seq = seq // tm
    return pl.pallas_call(
        functools.partial(_tail_kernel, ff_chunks=ff_chunks),
        grid=(n // tm,),
        in_specs=[
            pl.BlockSpec((tm, d), lambda i: (i, 0)),
            pl.BlockSpec((tm, GDN_VAL_DIM), lambda i: (i, 0)),
            pl.BlockSpec((tm, DSA_DIM), lambda i: (i, 0)),
            _const_spec((GDN_VAL_DIM + DSA_DIM, d)),
            _const_spec((1, d)),
            _const_spec((d, XA_DIM)),
            _const_spec((XA_DIM, XA_DIM)),
            _const_spec((1, XA_DIM)),
            pl.BlockSpec((1, n_mem, XA_DIM), lambda i: (i // per_seq, 0, 0)),
            pl.BlockSpec((1, n_mem, XA_DIM), lambda i: (i // per_seq, 0, 0)),
            _const_spec((XA_DIM, d)),
            _const_spec((1, d)),
            _const_spec((d, 2 * d_ff)),
            _const_spec((d_ff, d)),
        ],
        out_specs=pl.BlockSpec((tm, d), lambda i: (i, 0)),
        out_shape=jax.ShapeDtypeStruct((n, d), F32),
        compiler_params=pltpu.CompilerParams(dimension_semantics=("arbitrary",), vmem_limit_bytes=VMEM_LIMIT),
        name="tail",
    )(x2, ya, yb, w_out.astype(BF16), g_xattn[None, :], w_xq.astype(BF16), _block_diag_ones(XA_DIM, XA_HD),
      jnp.tile(g_xq, XA_HEADS)[None, :], km, vm, w_xo.astype(BF16), g_ffn[None, :], w_gu.astype(BF16),
      w_down.astype(BF16))


def kernel(x, mem, g_mix, w_in, conv_w, a_log, dt_bias, g_gdn_out, g_q_dsa, g_k_dsa, w_out, g_xattn, g_mem, w_xq,
           w_xkv, g_xq, g_xk, w_xo, g_ffn, w_gu, w_down):
    batch, seq, d = x.shape
    n_mem = mem.shape[1]
    assert seq % ROW_TILE == 0 and seq % KEY_TILE == 0 and ROW_TILE % KEY_TILE == 0
    for l in range(g_mix.shape[0]):
        x2 = x.reshape(batch * seq, d)
        qkv, z, kn, rs, qT, vT, qiT, wT = _inproj(x2, g_mix[l], w_in[l], g_q_dsa[l], g_k_dsa[l], conv_w[l], seq)
        ya = _gdn(qkv, z, rs, a_log[l], dt_bias[l], g_gdn_out[l], batch, seq)
        yb = _dsa(qT, qiT, wT, kn, vT, rs, batch, seq)
        km, vm = _memkv(mem.reshape(batch * n_mem, d), g_mem[l], w_xkv[l], g_xk[l])
        km = km.reshape(batch, n_mem, XA_DIM)
        vm = vm.reshape(batch, n_mem, XA_DIM)
        x = _tail(x2, ya, yb, w_out[l], g_xattn[l], w_xq[l], g_xq[l], km, vm, w_xo[l], g_ffn[l], w_gu[l],
                  w_down[l], seq).reshape(batch, seq, d)
    return x
```

```python
import functools

import jax
import jax.numpy as jnp
from jax import lax
from jax.experimental import pallas as pl
from jax.experimental.pallas import tpu as pltpu

F32 = jnp.float32
BF16 = jnp.bfloat16
I32 = jnp.int32

EPS = 1e-6
CHUNK = 64
GDN_HEADS, GDN_DK, GDN_DV, CONV_K = 4, 128, 128, 4
DSA_HEADS, DSA_HD = 8, 64
IDX_HEADS, IDX_HD = 16, 64
TOPK_MAX = 256
XA_HEADS, XA_HD = 4, 128

GDN_KEY_DIM = GDN_HEADS * GDN_DK
GDN_VAL_DIM = GDN_HEADS * GDN_DV
GDN_CONV_CH = 2 * GDN_KEY_DIM + GDN_VAL_DIM
DSA_DIM = DSA_HEADS * DSA_HD
IDX_Q_DIM = IDX_HEADS * IDX_HD
XA_DIM = XA_HEADS * XA_HD

LANES = 128
SUBLANES = 8
VMEM_LIMIT = 56 * 1024 * 1024

RS_IDXK = 0
RS_BETA = IDX_HD
RS_A = IDX_HD + GDN_HEADS

ROW_TILE = 512
ROW_PARTS = 2
KEY_TILE = 256
Q_BLOCK = 128
GDN_CHUNKS_PER_STEP = 4
GDN_SEQS_PER_STEP = 4

BISECT_ROUNDS = 20
COUNT_CHAINS = 4
NEG_BIG = -1e30


def _dot(a, b):
    return jnp.dot(a, b, preferred_element_type=F32)


def _dot_nt(a, b):
    return lax.dot_general(a, b, (((1,), (1,)), ((), ())), preferred_element_type=F32)


def _dot_f32(a, b):
    return jnp.dot(a, b, preferred_element_type=F32, precision=lax.Precision.HIGHEST)


def _silu(x):
    half = 0.5 * x
    return half + half * jnp.tanh(half)


def _rms_rows(x, g):
    ms = jnp.mean(x * x, axis=-1, keepdims=True)
    return x * lax.rsqrt(ms + EPS) * g


def _const_spec(shape):
    nd = len(shape)
    return pl.BlockSpec(shape, lambda *_: (0,) * nd, pipeline_mode=pl.Buffered(1))


def _block_diag_ones(n, blk):
    r = lax.broadcasted_iota(I32, (n, n), 0) // blk
    c = lax.broadcasted_iota(I32, (n, n), 1) // blk
    return (r == c).astype(BF16)


def _inproj_kernel(x_ref, g_ref, wrow_ref, wcol_ref, bd_ref, gk_ref, gq_ref, cw_ref,
                   qkv_ref, z_ref, kn_ref, rs_ref, qT_ref, vT_ref, qiT_ref, wT_ref, xc_ref, *, tiles_per_seq):
    tm = x_ref.shape[0]

    @pl.when(pl.program_id(0) % tiles_per_seq == 0)
    def _start_of_sequence():
        xc_ref[0:SUBLANES, :] = jnp.zeros((SUBLANES, GDN_CONV_CH), F32)

    h = _rms_rows(x_ref[...], g_ref[...]).astype(BF16)
    c0, c1, c2, c3 = GDN_CONV_CH, GDN_CONV_CH + GDN_VAL_DIM, GDN_CONV_CH + GDN_VAL_DIM + DSA_DIM, \
        GDN_CONV_CH + GDN_VAL_DIM + DSA_DIM + LANES

    xc_ref[SUBLANES:SUBLANES + tm, :] = _dot(h, wrow_ref[:, 0:c0])

    z_ref[...] = _dot(h, wrow_ref[:, c0:c1])
    k = _dot(h, wrow_ref[:, c1:c2])
    kms = _dot((k * k).astype(BF16), bd_ref[...]) * (1.0 / DSA_HD)
    kn_ref[...] = (k * lax.rsqrt(kms + EPS) * gk_ref[...]).astype(BF16)
    rs_ref[...] = _dot(h, wrow_ref[:, c1 + DSA_DIM:c3])

    r0, r1, r2, r3 = DSA_DIM, 2 * DSA_DIM, 2 * DSA_DIM + IDX_Q_DIM, 2 * DSA_DIM + IDX_Q_DIM + IDX_HEADS
    qT = _dot_nt(wcol_ref[0:r0, :], h)
    qms = _dot(bd_ref[...], (qT * qT).astype(BF16)) * (1.0 / DSA_HD)
    qT_ref[...] = (qT * lax.rsqrt(qms + EPS) * gq_ref[...] * (DSA_HD ** -0.5)).astype(BF16)
    vT = _dot_nt(wcol_ref[r0:r1, :], h).astype(BF16)
    for i in range(vT_ref.shape[0]):
        vT_ref[i] = vT[:, i * KEY_TILE:(i + 1) * KEY_TILE]
    qiT_ref[...] = (_dot_nt(wcol_ref[r1:r2, :], h) * (IDX_HD ** -0.5)).astype(BF16)
    wT_ref[...] = _dot_nt(wcol_ref[r2:r3, :], h) * (IDX_HEADS ** -0.5)

    cw = cw_ref[...]
    conv = cw[CONV_K - 1:CONV_K, :] * xc_ref[SUBLANES:SUBLANES + tm, :]
    for j in range(CONV_K - 2, -1, -1):
        back = CONV_K - 1 - j
        conv = conv + cw[j:j + 1, :] * xc_ref[SUBLANES - back:SUBLANES - back + tm, :]
    xc_ref[0:SUBLANES, :] = xc_ref[tm:tm + SUBLANES, :]
    act = _silu(conv)
    for hd in range(GDN_HEADS):
        qs = slice(hd * GDN_DK, (hd + 1) * GDN_DK)
        ks = slice(GDN_KEY_DIM + hd * GDN_DK, GDN_KEY_DIM + (hd + 1) * GDN_DK)
        q, k = act[:, qs], act[:, ks]
        qkv_ref[:, qs] = q * lax.rsqrt(jnp.sum(q * q, axis=-1, keepdims=True) + EPS) * (GDN_DK ** -0.5)
        qkv_ref[:, ks] = k * lax.rsqrt(jnp.sum(k * k, axis=-1, keepdims=True) + EPS)
    qkv_ref[:, 2 * GDN_KEY_DIM:] = act[:, 2 * GDN_KEY_DIM:]


def _inproj(x2, g_mix, w_in, g_q, g_k, conv_w, seq):
    n, d = x2.shape
    tm = ROW_TILE
    sizes = (GDN_CONV_CH, GDN_VAL_DIM, GDN_HEADS, GDN_HEADS, DSA_DIM, DSA_DIM, DSA_DIM, IDX_Q_DIM, IDX_HD, IDX_HEADS)
    offs = [0]
    for s in sizes:
        offs.append(offs[-1] + s)
    (w_qkv, w_z, w_b, w_a, w_q, w_k, w_v, w_iq, w_ik, w_iw) = [w_in[:, offs[i]:offs[i + 1]] for i in range(len(sizes))]
    pad = jnp.zeros((d, LANES - IDX_HD - 2 * GDN_HEADS), w_in.dtype)
    w_row = jnp.concatenate([w_qkv, w_z, w_k, w_ik, w_b, w_a, pad], axis=1).astype(BF16)
    w_col = jnp.concatenate([w_q, w_v, w_iq, w_iw], axis=1).T.astype(BF16)
    bd = _block_diag_ones(DSA_DIM, DSA_HD)
    gk_row = jnp.tile(g_k, DSA_HEADS)[None, :]
    gq_col = jnp.tile(g_q, DSA_HEADS)[:, None]
    nrow, ncol = w_row.shape[1], w_col.shape[0]
    out_shape = (
        jax.ShapeDtypeStruct((n, GDN_CONV_CH), F32),
        jax.ShapeDtypeStruct((n, GDN_VAL_DIM), F32),
        jax.ShapeDtypeStruct((n, DSA_DIM), BF16),
        jax.ShapeDtypeStruct((n, LANES), F32),
        jax.ShapeDtypeStruct((DSA_DIM, n), BF16),
        jax.ShapeDtypeStruct((n // KEY_TILE, DSA_DIM, KEY_TILE), BF16),
        jax.ShapeDtypeStruct((IDX_Q_DIM, n), BF16),
        jax.ShapeDtypeStruct((IDX_HEADS, n), F32),
    )
    return pl.pallas_call(
        functools.partial(_inproj_kernel, tiles_per_seq=seq // tm),
        grid=(n // tm,),
        in_specs=[
            pl.BlockSpec((tm, d), lambda i: (i, 0)),
            _const_spec((1, d)),
            _const_spec((d, nrow)),
            _const_spec((ncol, d)),
            _const_spec((DSA_DIM, DSA_DIM)),
            _const_spec((1, DSA_DIM)),
            _const_spec((DSA_DIM, 1)),
            _const_spec((CONV_K, GDN_CONV_CH)),
        ],
        out_specs=(
            pl.BlockSpec((tm, GDN_CONV_CH), lambda i: (i, 0)),
            pl.BlockSpec((tm, GDN_VAL_DIM), lambda i: (i, 0)),
            pl.BlockSpec((tm, DSA_DIM), lambda i: (i, 0)),
            pl.BlockSpec((tm, LANES), lambda i: (i, 0)),
            pl.BlockSpec((DSA_DIM, tm), lambda i: (0, i)),
            pl.BlockSpec((tm // KEY_TILE, DSA_DIM, KEY_TILE), lambda i: (i, 0, 0)),
            pl.BlockSpec((IDX_Q_DIM, tm), lambda i: (0, i)),
            pl.BlockSpec((IDX_HEADS, tm), lambda i: (0, i)),
        ),
        out_shape=out_shape,
        scratch_shapes=[pltpu.VMEM((tm + SUBLANES, GDN_CONV_CH), F32)],
        compiler_params=pltpu.CompilerParams(dimension_semantics=("arbitrary",), vmem_limit_bytes=VMEM_LIMIT),
        name="inproj",
    )(x2, g_mix[None, :], w_row, w_col, bd, gk_row, gq_col, conv_w)


def _gdn_kernel(qkv_ref, z_ref, rs_ref, alog_ref, dtb_ref, gout_ref, y_ref, s_ref):
    nb, rows = qkv_ref.shape[0], qkv_ref.shape[1]
    nc = rows // CHUNK

    @pl.when(pl.program_id(1) == 0)
    def _start_of_sequence():
        s_ref[...] = jnp.zeros(s_ref.shape, F32)

    rs = rs_ref[...]
    beta_all = jax.nn.sigmoid(rs)
    sp_in = rs + dtb_ref[...]
    softplus = jnp.maximum(sp_in, 0.0) + jnp.log(1.0 + jnp.exp(-jnp.abs(sp_in)))
    g_all = -jnp.exp(alog_ref[...]) * softplus

    ri = lax.broadcasted_iota(I32, (CHUNK, CHUNK), 0)
    ci = lax.broadcasted_iota(I32, (CHUNK, CHUNK), 1)
    incl = ri >= ci
    strict = ri > ci
    ltri = incl.astype(F32)
    eye = (ri == ci).astype(F32)
    gout = gout_ref[...]

    chains = [(b, c, h) for c in range(nc) for b in range(nb) for h in range(GDN_HEADS)]
    dcum, dcum_t = {}, {}
    for c in range(nc):
        for b in range(nb):
            d = _dot_f32(ltri, g_all[b, c * CHUNK:(c + 1) * CHUNK, :])
            dcum[b, c] = d
            dcum_t[b, c] = d.T

    q16, k16, kb16, vb16, kw16, qdec16, kdect16, gamma, last = ([] for _ in range(9))
    for b, c, h in chains:
        r = slice(c * CHUNK, (c + 1) * CHUNK)
        q = qkv_ref[b, r, h * GDN_DK:(h + 1) * GDN_DK]
        k = qkv_ref[b, r, GDN_KEY_DIM + h * GDN_DK:GDN_KEY_DIM + (h + 1) * GDN_DK]
        v = qkv_ref[b, r, 2 * GDN_KEY_DIM + h * GDN_DV:2 * GDN_KEY_DIM + (h + 1) * GDN_DV]
        beta = beta_all[b, r, RS_BETA + h:RS_BETA + h + 1]
        d_col = dcum[b, c][:, RS_A + h:RS_A + h + 1]
        d_row = dcum_t[b, c][RS_A + h:RS_A + h + 1, :]
        d_last = dcum[b, c][CHUNK - 1:CHUNK, RS_A + h:RS_A + h + 1]
        e_col = jnp.exp(d_col)
        kb = k * beta
        gamma.append(jnp.exp(jnp.where(incl, d_col - d_row, -jnp.inf)))
        q16.append(q.astype(BF16))
        k16.append(k.astype(BF16))
        kb16.append(kb.astype(BF16))
        vb16.append((v * beta).astype(BF16))
        kw16.append((kb * e_col).astype(BF16))
        qdec16.append((q * e_col).astype(BF16))
        kdect16.append((k * jnp.exp(d_last - d_col)).T.astype(BF16))
        last.append(jnp.exp(d_last))

    n = len(chains)
    p16 = [(-jnp.where(strict, _dot_nt(kb16[i], k16[i]) * gamma[i], 0.0)).astype(BF16) for i in range(n)]
    qk16 = [(_dot_nt(q16[i], k16[i]) * gamma[i]).astype(BF16) for i in range(n)]
    t_mat = [eye + p16[i].astype(F32) for i in range(n)]
    p16 = [_dot(p16[i], p16[i]).astype(BF16) for i in range(n)]
    for _ in range(4):
        t_mat = [t_mat[i] + _dot(t_mat[i].astype(BF16), p16[i]) for i in range(n)]
        p16 = [_dot(p16[i], p16[i]).astype(BF16) for i in range(n)]
    t16 = [(t_mat[i] + _dot(t_mat[i].astype(BF16), p16[i])).astype(BF16) for i in range(n)]
    u = [_dot(t16[i], vb16[i]) for i in range(n)]
    w16 = [_dot(t16[i], kw16[i]).astype(BF16) for i in range(n)]

    per_chunk = nb * GDN_HEADS
    s = [s_ref[b, h] for b in range(nb) for h in range(GDN_HEADS)]
    for c in range(nc):
        ids = range(c * per_chunk, (c + 1) * per_chunk)
        s16 = [s[j].astype(BF16) for j in range(per_chunk)]
        v_new16 = [(u[i] - _dot(w16[i], s16[j])).astype(BF16) for j, i in enumerate(ids)]
        o_state = [_dot(qdec16[i], s16[j]) for j, i in enumerate(ids)]
        o = [o_state[j] + _dot(qk16[i], v_new16[j]) for j, i in enumerate(ids)]
        s = [s[j] * last[i] + _dot(kdect16[i], v_new16[j]) for j, i in enumerate(ids)]
        for j, i in enumerate(ids):
            b, _, h = chains[i]
            zc = z_ref[b, c * CHUNK:(c + 1) * CHUNK, h * GDN_DV:(h + 1) * GDN_DV]
            y = _rms_rows(o[j], gout) * _silu(zc)
            y_ref[b, c * CHUNK:(c + 1) * CHUNK, h * GDN_DV:(h + 1) * GDN_DV] = y.astype(y_ref.dtype)
    for j in range(per_chunk):
        s_ref[j // GDN_HEADS, j % GDN_HEADS] = s[j]


def _gdn(qkv, z, rs, a_log, dt_bias, g_out, batch, seq):
    n = qkv.shape[0]
    rows = GDN_CHUNKS_PER_STEP * CHUNK
    nb = GDN_SEQS_PER_STEP if batch % GDN_SEQS_PER_STEP == 0 else 1
    lane_vec = lambda v: jnp.zeros((1, LANES), F32).at[0, RS_A:RS_A + GDN_HEADS].set(v)
    per_seq = lambda a: a.reshape(batch, seq, a.shape[-1])
    block = lambda width: pl.BlockSpec((nb, rows, width), lambda b, i: (b, i, 0))
    y = pl.pallas_call(
        _gdn_kernel,
        grid=(batch // nb, seq // rows),
        in_specs=[
            block(GDN_CONV_CH),
            block(GDN_VAL_DIM),
            block(LANES),
            _const_spec((1, LANES)),
            _const_spec((1, LANES)),
            _const_spec((1, GDN_DV)),
        ],
        out_specs=block(GDN_VAL_DIM),
        out_shape=jax.ShapeDtypeStruct((batch, seq, GDN_VAL_DIM), BF16),
        scratch_shapes=[
            pltpu.VMEM((nb, GDN_HEADS, GDN_DK, GDN_DV), F32),
        ],
        compiler_params=pltpu.CompilerParams(dimension_semantics=("arbitrary", "arbitrary"), vmem_limit_bytes=VMEM_LIMIT),
        name="gdn",
    )(per_seq(qkv), per_seq(z), per_seq(rs), lane_vec(a_log), lane_vec(dt_bias), g_out[None, :])
    return y.reshape(n, GDN_VAL_DIM)


def _alibi_slope(h):
    return 2.0 ** (-8.0 * (h + 1) / DSA_HEADS)


def _order_key(x):
    bits = lax.bitcast_convert_type(x, I32)
    return (bits ^ ((bits >> 31) & jnp.int32(0x7FFFFFFF))) >> 1


def _order_key_to_f32(key):
    full = key << 1
    return lax.bitcast_convert_type(full ^ ((full >> 31) & jnp.int32(0x7FFFFFFF)), F32)


def _tiles_for_block(j):
    return (j * Q_BLOCK + Q_BLOCK + KEY_TILE - 1) // KEY_TILE


def _dsa_kernel(qT_ref, qiT_ref, wT_ref, kn_ref, vT_ref, rs_ref, kpos_ref, slope_ref, o_ref,
                idxk_scr, sc_scr, key_scr, thr_scr, cnt_scr, acc_scr, *, topk, n_pos_bits):
    j = pl.program_id(1)

    @pl.when(j == 0)
    def _new_sequence():
        idxk_scr[...] = rs_ref[:, RS_IDXK:RS_IDXK + IDX_HD].astype(BF16)

    n_tiles = _tiles_for_block(j)
    lane = lax.broadcasted_iota(I32, (1, Q_BLOCK), 1)
    qpos = j * Q_BLOCK + lane
    key_limit = ((qpos >> 6) + 1) << 6
    row = lax.broadcasted_iota(I32, (KEY_TILE, Q_BLOCK), 0)

    w_t = wT_ref[...]
    qi_pairs = [jnp.concatenate([qiT_ref[(2 * p) * IDX_HD:(2 * p + 1) * IDX_HD, :],
                                 qiT_ref[(2 * p + 1) * IDX_HD:(2 * p + 2) * IDX_HD, :]], axis=1)
                for p in range(IDX_HEADS // 2)]

    def score_tiles(t, count, carry):
        s_max, s_min = carry
        r0 = pl.multiple_of(t * KEY_TILE, KEY_TILE)
        kt = idxk_scr[pl.ds(r0, count * KEY_TILE), :]
        accs = [jnp.zeros((KEY_TILE, Q_BLOCK), F32) for _ in range(count)]
        for p in range(IDX_HEADS // 2):
            d = _dot(kt, qi_pairs[p])
            for i in range(count):
                di = d[i * KEY_TILE:(i + 1) * KEY_TILE, :]
                accs[i] = accs[i] + jnp.maximum(di[:, :Q_BLOCK], 0.0) * w_t[2 * p:2 * p + 1, :]
                accs[i] = accs[i] + jnp.maximum(di[:, Q_BLOCK:], 0.0) * w_t[2 * p + 1:2 * p + 2, :]
        for i in range(count):
            admissible = r0 + i * KEY_TILE + row < key_limit
            stored = jnp.where(admissible, accs[i], -jnp.inf)
            sc_scr[t + i] = stored
            key_scr[t + i] = _order_key(stored)
            s_max = jnp.maximum(s_max, jnp.max(stored, axis=0, keepdims=True))
            s_min = jnp.minimum(s_min, jnp.min(jnp.where(admissible, accs[i], jnp.inf), axis=0, keepdims=True))
        return s_max, s_min

    stats = (jnp.full((1, Q_BLOCK), -jnp.inf, F32), jnp.full((1, Q_BLOCK), jnp.inf, F32))
    stats = lax.fori_loop(0, n_tiles // 2, lambda u, st: score_tiles(2 * u, 2, st), stats)
    s_max, s_min = lax.cond(n_tiles % 2 == 1, lambda st: score_tiles(n_tiles - 1, 1, st), lambda st: st, stats)

    def count_tiles(per_tile):
        cnt_scr[...] = jnp.zeros(cnt_scr.shape, F32)
        base = jnp.int32(0)
        arm = 1 << (sc_scr.shape[0].bit_length() - 1)
        while arm >= 1:
            has = (n_tiles & arm) != 0

            @pl.when(has)
            def _run(base=base, arm=arm):
                chains = [jnp.zeros((SUBLANES, Q_BLOCK), F32) for _ in range(COUNT_CHAINS)]
                for i in range(arm):
                    flags = per_tile(base + i)
                    for r in range(KEY_TILE // SUBLANES):
                        chains[r % COUNT_CHAINS] = chains[r % COUNT_CHAINS] + flags[r * SUBLANES:(r + 1) * SUBLANES, :]
                while len(chains) > 1:
                    chains = [chains[k] + chains[k + 1] for k in range(0, len(chains), 2)]
                cnt_scr[...] += chains[0]

            base = base + jnp.where(has, arm, 0)
            arm //= 2
        return jnp.sum(cnt_scr[...], axis=0, keepdims=True).astype(I32)

    def count_ge(thr):
        return count_tiles(lambda i: jnp.where(sc_scr[i] >= thr, 1.0, 0.0))

    def count_keys_ge(key, tiles):
        one = jnp.int32(0x3F800000)
        chains = [jnp.zeros((SUBLANES, Q_BLOCK), F32) for _ in range(COUNT_CHAINS)]
        for i in range(tiles):
            below = lax.bitcast_convert_type(((key_scr[i] - key) >> 31) & one, F32)
            for r in range(KEY_TILE // SUBLANES):
                chains[r % COUNT_CHAINS] = chains[r % COUNT_CHAINS] + below[r * SUBLANES:(r + 1) * SUBLANES, :]
        while len(chains) > 1:
            chains = [chains[k] + chains[k + 1] for k in range(0, len(chains), 2)]
        return tiles * KEY_TILE - jnp.sum(chains[0], axis=0, keepdims=True).astype(I32)

    def search_with_tiles(tiles):
        def bisect_step(_, st):
            lo, hi = st
            mid = lo * 0.5 + hi * 0.5
            take = count_keys_ge(_order_key(mid), tiles) >= topk
            return jnp.where(take, mid, lo), jnp.where(take, hi, mid)

        return lambda st: lax.fori_loop(0, BISECT_ROUNDS, bisect_step, st)

    above_max = s_max + jnp.maximum(jnp.abs(s_max) * 1e-6, 1e-30)
    lo, _ = lax.switch(n_tiles - 1, [search_with_tiles(n) for n in range(1, sc_scr.shape[0] + 1)], (s_min, above_max))
    few = key_limit < topk
    thr = jnp.where(few, jnp.finfo(F32).min, _order_key_to_f32(_order_key(lo)))
    cnt = jnp.where(few, topk, count_ge(thr))
    thr_scr[...] = thr

    @pl.when(jnp.max(jnp.abs(cnt - topk)) > 0)
    def _resolve():
        def max_below(thr):
            def tile(t, m):
                sc = sc_scr[t]
                return jnp.maximum(m, jnp.max(jnp.where(sc < thr, sc, -jnp.inf), axis=0, keepdims=True))
            return lax.fori_loop(0, n_tiles, tile, jnp.full((1, Q_BLOCK), -jnp.inf, F32))

        def lower(st):
            thr, cnt = st
            nxt = max_below(thr)
            short = cnt < topk
            return jnp.where(short, nxt, thr), jnp.where(short, count_ge(nxt), cnt)

        def min_above(thr):
            def tile(t, m):
                sc = sc_scr[t]
                return jnp.minimum(m, jnp.min(jnp.where(sc > thr, sc, jnp.inf), axis=0, keepdims=True))
            return lax.fori_loop(0, n_tiles, tile, jnp.full((1, Q_BLOCK), jnp.inf, F32))

        def count_eq_below(thr, pos_limit):
            def tile(t, c):
                r0 = t * KEY_TILE
                hit = (sc_scr[t] == thr) & (r0 + row < pos_limit)
                return c + jnp.sum(hit.astype(I32), axis=0, keepdims=True)
            return lax.fori_loop(0, n_tiles, tile, jnp.zeros((1, Q_BLOCK), I32))

        def body(st):
            thr, cnt = st
            nxt = min_above(thr)
            cnt_n = count_ge(nxt)
            active = cnt > topk
            advance = active & (cnt_n >= topk)
            tie = active & (cnt_n < topk)
            @pl.when(jnp.max(tie.astype(I32)) > 0)
            def _drop_surplus_ties():
                need = topk - cnt_n
                pos = jnp.zeros((1, Q_BLOCK), I32)
                for b in range(n_pos_bits - 1, -1, -1):
                    cand = pos + (1 << b)
                    pos = jnp.where(count_eq_below(thr, cand) < need, cand, pos)

                def drop_tile(t, carry):
                    r0 = t * KEY_TILE
                    sc = sc_scr[t]
                    sc_scr[t] = jnp.where(tie & (sc == thr) & (r0 + row > pos), -jnp.inf, sc)
                    return carry

                lax.fori_loop(0, n_tiles, drop_tile, 0)

            return jnp.where(advance, nxt, thr), jnp.where(advance, cnt_n, jnp.where(tie, topk, cnt))

        st = lax.while_loop(lambda st: jnp.min(st[1]) < topk, lower, (thr, cnt))
        thr_scr[...] = lax.while_loop(lambda st: jnp.max(st[1]) > topk, body, st)[0]

    thr = thr_scr[...]

    zero = jnp.zeros((DSA_HD, Q_BLOCK), BF16)
    q_pairs = []
    for p in range(DSA_HEADS // 2):
        a = qT_ref[(2 * p) * DSA_HD:(2 * p + 1) * DSA_HD, :]
        b = qT_ref[(2 * p + 1) * DSA_HD:(2 * p + 2) * DSA_HD, :]
        q_pairs.append(jnp.concatenate([jnp.concatenate([a, zero], axis=1),
                                        jnp.concatenate([zero, b], axis=1),
                                        slope_ref[p]], axis=0))
    acc_scr[...] = jnp.zeros(acc_scr.shape, F32)
    qposf = qpos.astype(F32)

    def qk_products(t):
        r0 = pl.multiple_of(t * KEY_TILE, KEY_TILE)
        kt = kn_ref[pl.ds(r0, KEY_TILE), :]
        kp = kpos_ref[pl.ds(r0, KEY_TILE), :]
        s2 = [_dot(jnp.concatenate([kt[:, p * 2 * DSA_HD:(p + 1) * 2 * DSA_HD], kp], axis=1), q_pairs[p])
              for p in range(DSA_HEADS // 2)]
        return t, sc_scr[t] >= thr, s2

    def softmax_pv(tile, ml, last_tile):
        t, sel, s2 = tile
        vt = vT_ref[t]
        if last_tile:
            after = jnp.maximum((t * KEY_TILE + row).astype(F32) - qposf, 0.0)
        m_all, l_all = ml
        m_rows, l_rows = [], []
        for h in range(DSA_HEADS):
            s = s2[h // 2][:, (h % 2) * Q_BLOCK:(h % 2 + 1) * Q_BLOCK]
            if last_tile:
                s = s - (2.0 * _alibi_slope(h)) * after
            s = jnp.where(sel, s, NEG_BIG)
            m_old = m_all[h:h + 1, :]
            m_new = jnp.maximum(m_old, jnp.max(s, axis=0, keepdims=True))
            alpha = jnp.exp(m_old - m_new)
            pr = jnp.exp(s - m_new)
            l_rows.append(alpha * l_all[h:h + 1, :] + jnp.sum(pr, axis=0, keepdims=True))
            m_rows.append(m_new)
            hs = slice(h * DSA_HD, (h + 1) * DSA_HD)
            acc_scr[hs, :] = alpha * acc_scr[hs, :] + _dot(vt[hs, :], pr.astype(BF16))
        return jnp.concatenate(m_rows, axis=0), jnp.concatenate(l_rows, axis=0)

    def attend_two(ta, tb, ml, b_is_last):
        a, b = qk_products(ta), qk_products(tb)
        return softmax_pv(b, softmax_pv(a, ml, False), b_is_last)

    ml = (jnp.full((DSA_HEADS, Q_BLOCK), NEG_BIG, F32), jnp.zeros((DSA_HEADS, Q_BLOCK), F32))
    earlier = n_tiles - 1
    ml = lax.fori_loop(0, earlier // 2, lambda u, ml: attend_two(2 * u, 2 * u + 1, ml, False), ml)
    _, l_all = lax.cond(earlier % 2 == 1,
                        lambda ml: attend_two(n_tiles - 2, n_tiles - 1, ml, True),
                        lambda ml: softmax_pv(qk_products(n_tiles - 1), ml, True), ml)

    outs = [acc_scr[h * DSA_HD:(h + 1) * DSA_HD, :] / l_all[h:h + 1, :] for h in range(DSA_HEADS)]
    o_ref[...] = jnp.concatenate(outs, axis=0).T.astype(o_ref.dtype)


def _dsa(qT, qiT, wT, kn, vT, rs, batch, seq):
    n = kn.shape[0]
    blocks = seq // Q_BLOCK
    tiles = seq // KEY_TILE
    topk = min(TOPK_MAX, seq // 4)
    col_map = lambda b, j: (0, b * blocks + j)
    pos = jnp.arange(seq, dtype=I32)
    kpos = jnp.zeros((seq, LANES), F32).at[:, 0].set((pos >> 4).astype(F32)).at[:, 1].set((pos & 15).astype(F32))
    slopes = jnp.asarray([_alibi_slope(h) for h in range(DSA_HEADS)], F32).reshape(DSA_HEADS // 2, 2)
    slope_cols = jnp.repeat(slopes, Q_BLOCK, axis=1)
    slope_rows = jnp.zeros((DSA_HEADS // 2, LANES, 2 * Q_BLOCK), F32)
    slope_rows = slope_rows.at[:, 0, :].set(16.0 * slope_cols).at[:, 1, :].set(slope_cols)
    return pl.pallas_call(
        functools.partial(_dsa_kernel, topk=topk, n_pos_bits=seq.bit_length()),
        grid=(batch, blocks),
        in_specs=[
            pl.BlockSpec((DSA_DIM, Q_BLOCK), col_map),
            pl.BlockSpec((IDX_Q_DIM, Q_BLOCK), col_map),
            pl.BlockSpec((IDX_HEADS, Q_BLOCK), col_map),
            pl.BlockSpec((seq, DSA_DIM), lambda b, j: (b, 0)),
            pl.BlockSpec((tiles, DSA_DIM, KEY_TILE), lambda b, j: (b, 0, 0)),
            pl.BlockSpec((seq, LANES), lambda b, j: (b, 0)),
            _const_spec((seq, LANES)),
            _const_spec((DSA_HEADS // 2, LANES, 2 * Q_BLOCK)),
        ],
        out_specs=pl.BlockSpec((Q_BLOCK, DSA_DIM), lambda b, j: (b * blocks + j, 0)),
        out_shape=jax.ShapeDtypeStruct((n, DSA_DIM), BF16),
        scratch_shapes=[
            pltpu.VMEM((seq, IDX_HD), BF16),
            pltpu.VMEM((tiles, KEY_TILE, Q_BLOCK), F32),
            pltpu.VMEM((tiles, KEY_TILE, Q_BLOCK), I32),
            pltpu.VMEM((1, Q_BLOCK), F32),
            pltpu.VMEM((SUBLANES, Q_BLOCK), F32),
            pltpu.VMEM((DSA_DIM, Q_BLOCK), F32),
        ],
        compiler_params=pltpu.CompilerParams(dimension_semantics=("arbitrary", "arbitrary"), vmem_limit_bytes=VMEM_LIMIT),
        name="dsa",
    )(qT, qiT, wT, kn, vT, rs, kpos.astype(BF16), slope_rows.astype(BF16))


def _memkv_kernel(mem_ref, g_ref, w_ref, bd_ref, gk_ref, k_ref, v_ref):
    h = _rms_rows(mem_ref[...], g_ref[...]).astype(BF16)
    kv = _dot(h, w_ref[...])
    k = kv[:, :XA_DIM]
    kms = _dot((k * k).astype(BF16), bd_ref[...]) * (1.0 / XA_HD)
    k_ref[...] = (k * lax.rsqrt(kms + EPS) * gk_ref[...]).astype(BF16)
    v_ref[...] = kv[:, XA_DIM:].astype(BF16)


def _memkv(mem2, g_mem, w_xkv, g_xk):
    n, d = mem2.shape
    tm = min(ROW_TILE, n)
    return pl.pallas_call(
        _memkv_kernel,
        grid=(n // tm,),
        in_specs=[
            pl.BlockSpec((tm, d), lambda i: (i, 0)),
            _const_spec((1, d)),
            _const_spec((d, 2 * XA_DIM)),
            _const_spec((XA_DIM, XA_DIM)),
            _const_spec((1, XA_DIM)),
        ],
        out_specs=(pl.BlockSpec((tm, XA_DIM), lambda i: (i, 0)), pl.BlockSpec((tm, XA_DIM), lambda i: (i, 0))),
        out_shape=(jax.ShapeDtypeStruct((n, XA_DIM), BF16), jax.ShapeDtypeStruct((n, XA_DIM), BF16)),
        compiler_params=pltpu.CompilerParams(dimension_semantics=("arbitrary",), vmem_limit_bytes=VMEM_LIMIT),
        name="memkv",
    )(mem2, g_mem[None, :], w_xkv.astype(BF16), _block_diag_ones(XA_DIM, XA_HD), jnp.tile(g_xk, XA_HEADS)[None, :])


def _tail_kernel(x_ref, ya_ref, yb_ref, wout_ref, gx_ref, wxq_ref, bd_ref, gxq_ref, km_ref, vm_ref, wxo_ref,
                 gf_ref, wgu_ref, wd_ref, o_ref, *, ff_chunks):
    tm = x_ref.shape[0]
    part = tm // ROW_PARTS
    groups = [slice(g * part, (g + 1) * part) for g in range(ROW_PARTS)]
    d_ff = wd_ref.shape[0]
    fc = d_ff // ff_chunks

    x1 = [x_ref[r, :] + _dot(ya_ref[r, :], wout_ref[0:GDN_VAL_DIM, :]) + _dot(yb_ref[r, :], wout_ref[GDN_VAL_DIM:, :])
          for r in groups]

    qn = []
    for g in range(ROW_PARTS):
        h = _rms_rows(x1[g], gx_ref[...]).astype(BF16)
        q = _dot(h, wxq_ref[...])
        qms = _dot((q * q).astype(BF16), bd_ref[...]) * (1.0 / XA_HD)
        qn.append((q * lax.rsqrt(qms + EPS) * gxq_ref[...]).astype(BF16))

    attn = []
    for g in range(ROW_PARTS):
        heads = []
        for hd in range(XA_HEADS):
            hs = slice(hd * XA_HD, (hd + 1) * XA_HD)
            s = _dot_nt(qn[g][:, hs], km_ref[0, :, hs]) * (XA_HD ** -0.5)
            pr = jnp.exp(s - jnp.max(s, axis=-1, keepdims=True))
            o = _dot(pr.astype(BF16), vm_ref[0, :, hs]) / jnp.sum(pr, axis=-1, keepdims=True)
            heads.append(o.astype(BF16))
        attn.append(jnp.concatenate(heads, axis=1))

    x2 = [x1[g] + _dot(attn[g], wxo_ref[...]) for g in range(ROW_PARTS)]
    h = [_rms_rows(x2[g], gf_ref[...]).astype(BF16) for g in range(ROW_PARTS)]
    acc = x2
    for c in range(ff_chunks):
        for g in range(ROW_PARTS):
            gate = _dot(h[g], wgu_ref[:, c * fc:(c + 1) * fc])
            up = _dot(h[g], wgu_ref[:, d_ff + c * fc:d_ff + (c + 1) * fc])
            a = (_silu(gate) * up).astype(BF16)
            acc[g] = acc[g] + _dot(a, wd_ref[c * fc:(c + 1) * fc, :])
    for g, r in enumerate(groups):
        o_ref[r, :] = acc[g]


def _tail(x2, ya, yb, w_out, g_xattn, w_xq, g_xq, km, vm, w_xo, g_ffn, w_gu, w_down, seq):
    n, d = x2.shape
    tm = ROW_TILE
    d_ff = w_down.shape[0]
    ff_chunks = 2 if (d_ff // 2) % LANES == 0 else 1
    n_mem = km.shape[1]
    per_seq = seq // tm
    return pl.pallas_call(
        functools.partial(_tail_kernel, ff_chunks=ff_chunks),
        grid=(n // tm,),
        in_specs=[
            pl.BlockSpec((tm, d), lambda i: (i, 0)),
            pl.BlockSpec((tm, GDN_VAL_DIM), lambda i: (i, 0)),
            pl.BlockSpec((tm, DSA_DIM), lambda i: (i, 0)),
            _const_spec((GDN_VAL_DIM + DSA_DIM, d)),
            _const_spec((1, d)),
            _const_spec((d, XA_DIM)),
            _const_spec((XA_DIM, XA_DIM)),
            _const_spec((1, XA_DIM)),
            pl.BlockSpec((1, n_mem, XA_DIM), lambda i: (i // per_seq, 0, 0)),
            pl.BlockSpec((1, n_mem, XA_DIM), lambda i: (i // per_seq, 0, 0)),
            _const_spec((XA_DIM, d)),
            _const_spec((1, d)),
            _const_spec((d, 2 * d_ff)),
            _const_spec((d_ff, d)),
        ],
        out_specs=pl.BlockSpec((tm, d), lambda i: (i, 0)),
        out_shape=jax.ShapeDtypeStruct((n, d), F32),
        compiler_params=pltpu.CompilerParams(dimension_semantics=("arbitrary",), vmem_limit_bytes=VMEM_LIMIT),
        name="tail",
    )(x2, ya, yb, w_out.astype(BF16), g_xattn[None, :], w_xq.astype(BF16), _block_diag_ones(XA_DIM, XA_HD),
      jnp.tile(g_xq, XA_HEADS)[None, :], km, vm, w_xo.astype(BF16), g_ffn[None, :], w_gu.astype(BF16),
      w_down.astype(BF16))


def kernel(x, mem, g_mix, w_in, conv_w, a_log, dt_bias, g_gdn_out, g_q_dsa, g_k_dsa, w_out, g_xattn, g_mem, w_xq,
           w_xkv, g_xq, g_xk, w_xo, g_ffn, w_gu, w_down):
    batch, seq, d = x.shape
    n_mem = mem.shape[1]
    assert seq % ROW_TILE == 0 and seq % KEY_TILE == 0 and ROW_TILE % KEY_TILE == 0
    for l in range(g_mix.shape[0]):
        x2 = x.reshape(batch * seq, d)
        qkv, z, kn, rs, qT, vT, qiT, wT = _inproj(x2, g_mix[l], w_in[l], g_q_dsa[l], g_k_dsa[l], conv_w[l], seq)
        ya = _gdn(qkv, z, rs, a_log[l], dt_bias[l], g_gdn_out[l], batch, seq)
        yb = _dsa(qT, qiT, wT, kn, vT, rs, batch, seq)
        km, vm = _memkv(mem.reshape(batch * n_mem, d), g_mem[l], w_xkv[l], g_xk[l])
        km = km.reshape(batch, n_mem, XA_DIM)
        vm = vm.reshape(batch, n_mem, XA_DIM)
        x = _tail(x2, ya, yb, w_out[l], g_xattn[l], w_xq[l], g_xq[l], km, vm, w_xo[l], g_ffn[l], w_gu[l],
                  w_down[l], seq).reshape(batch, seq, d)
    return x
```

```python
import functools

import jax
import jax.numpy as jnp
from jax import lax
from jax.experimental import pallas as pl
from jax.experimental.pallas import tpu as pltpu

F32 = jnp.float32
BF16 = jnp.bfloat16
I32 = jnp.int32

EPS = 1e-6
CHUNK = 64
GDN_HEADS, GDN_DK, GDN_DV, CONV_K = 4, 128, 128, 4
DSA_HEADS, DSA_HD = 8, 64
IDX_HEADS, IDX_HD = 16, 64
TOPK_MAX = 256
XA_HEADS, XA_HD = 4, 128

GDN_KEY_DIM = GDN_HEADS * GDN_DK
GDN_VAL_DIM = GDN_HEADS * GDN_DV
GDN_CONV_CH = 2 * GDN_KEY_DIM + GDN_VAL_DIM
DSA_DIM = DSA_HEADS * DSA_HD
IDX_Q_DIM = IDX_HEADS * IDX_HD
XA_DIM = XA_HEADS * XA_HD

LANES = 128
SUBLANES = 8
VMEM_LIMIT = 56 * 1024 * 1024

RS_IDXK = 0
RS_BETA = IDX_HD
RS_A = IDX_HD + GDN_HEADS

ROW_TILE = 512
ROW_PARTS = 2
KEY_TILE = 256
Q_BLOCK = 128
GDN_CHUNKS_PER_STEP = 4
GDN_SEQS_PER_STEP = 4

BISECT_ROUNDS = 20
COUNT_CHAINS = 4
ATTEND_RUN = 4
NEG_BIG = -1e30


def _dot(a, b):
    return jnp.dot(a, b, preferred_element_type=F32)


def _dot_nt(a, b):
    return lax.dot_general(a, b, (((1,), (1,)), ((), ())), preferred_element_type=F32)


def _dot_f32(a, b):
    return jnp.dot(a, b, preferred_element_type=F32, precision=lax.Precision.HIGHEST)


def _silu(x):
    half = 0.5 * x
    return half + half * jnp.tanh(half)


def _rms_rows(x, g):
    ms = jnp.mean(x * x, axis=-1, keepdims=True)
    return x * lax.rsqrt(ms + EPS) * g


def _const_spec(shape):
    nd = len(shape)
    return pl.BlockSpec(shape, lambda *_: (0,) * nd, pipeline_mode=pl.Buffered(1))


def _block_diag_ones(n, blk):
    r = lax.broadcasted_iota(I32, (n, n), 0) // blk
    c = lax.broadcasted_iota(I32, (n, n), 1) // blk
    return (r == c).astype(BF16)


def _inproj_kernel(x_ref, g_ref, wrow_ref, wcol_ref, bd_ref, gk_ref, gq_ref, cw_ref,
                   qkv_ref, z_ref, kn_ref, rs_ref, qT_ref, vT_ref, qiT_ref, wT_ref, xc_ref, *, tiles_per_seq):
    tm = x_ref.shape[0]

    @pl.when(pl.program_id(0) % tiles_per_seq == 0)
    def _start_of_sequence():
        xc_ref[0:SUBLANES, :] = jnp.zeros((SUBLANES, GDN_CONV_CH), F32)

    h = _rms_rows(x_ref[...], g_ref[...]).astype(BF16)
    c0, c1, c2, c3 = GDN_CONV_CH, GDN_CONV_CH + GDN_VAL_DIM, GDN_CONV_CH + GDN_VAL_DIM + DSA_DIM, \
        GDN_CONV_CH + GDN_VAL_DIM + DSA_DIM + LANES

    xc_ref[SUBLANES:SUBLANES + tm, :] = _dot(h, wrow_ref[:, 0:c0])

    z_ref[...] = _dot(h, wrow_ref[:, c0:c1])
    k = _dot(h, wrow_ref[:, c1:c2])
    kms = _dot((k * k).astype(BF16), bd_ref[...]) * (1.0 / DSA_HD)
    kn_ref[...] = (k * lax.rsqrt(kms + EPS) * gk_ref[...]).astype(BF16)
    rs_ref[...] = _dot(h, wrow_ref[:, c1 + DSA_DIM:c3])

    r0, r1, r2, r3 = DSA_DIM, 2 * DSA_DIM, 2 * DSA_DIM + IDX_Q_DIM, 2 * DSA_DIM + IDX_Q_DIM + IDX_HEADS
    qT = _dot_nt(wcol_ref[0:r0, :], h)
    qms = _dot(bd_ref[...], (qT * qT).astype(BF16)) * (1.0 / DSA_HD)
    qT_ref[...] = (qT * lax.rsqrt(qms + EPS) * gq_ref[...] * (DSA_HD ** -0.5)).astype(BF16)
    vT = _dot_nt(wcol_ref[r0:r1, :], h).astype(BF16)
    for i in range(vT_ref.shape[0]):
        vT_ref[i] = vT[:, i * KEY_TILE:(i + 1) * KEY_TILE]
    qiT_ref[...] = (_dot_nt(wcol_ref[r1:r2, :], h) * (IDX_HD ** -0.5)).astype(BF16)
    wT_ref[...] = _dot_nt(wcol_ref[r2:r3, :], h) * (IDX_HEADS ** -0.5)

    cw = cw_ref[...]
    conv = cw[CONV_K - 1:CONV_K, :] * xc_ref[SUBLANES:SUBLANES + tm, :]
    for j in range(CONV_K - 2, -1, -1):
        back = CONV_K - 1 - j
        conv = conv + cw[j:j + 1, :] * xc_ref[SUBLANES - back:SUBLANES - back + tm, :]
    xc_ref[0:SUBLANES, :] = xc_ref[tm:tm + SUBLANES, :]
    act = _silu(conv)
    for hd in range(GDN_HEADS):
        qs = slice(hd * GDN_DK, (hd + 1) * GDN_DK)
        ks = slice(GDN_KEY_DIM + hd * GDN_DK, GDN_KEY_DIM + (hd + 1) * GDN_DK)
        q, k = act[:, qs], act[:, ks]
        qkv_ref[:, qs] = q * lax.rsqrt(jnp.sum(q * q, axis=-1, keepdims=True) + EPS) * (GDN_DK ** -0.5)
        qkv_ref[:, ks] = k * lax.rsqrt(jnp.sum(k * k, axis=-1, keepdims=True) + EPS)
    qkv_ref[:, 2 * GDN_KEY_DIM:] = act[:, 2 * GDN_KEY_DIM:]


def _inproj(x2, g_mix, w_in, g_q, g_k, conv_w, seq):
    n, d = x2.shape
    tm = ROW_TILE
    sizes = (GDN_CONV_CH, GDN_VAL_DIM, GDN_HEADS, GDN_HEADS, DSA_DIM, DSA_DIM, DSA_DIM, IDX_Q_DIM, IDX_HD, IDX_HEADS)
    offs = [0]
    for s in sizes:
        offs.append(offs[-1] + s)
    (w_qkv, w_z, w_b, w_a, w_q, w_k, w_v, w_iq, w_ik, w_iw) = [w_in[:, offs[i]:offs[i + 1]] for i in range(len(sizes))]
    pad = jnp.zeros((d, LANES - IDX_HD - 2 * GDN_HEADS), w_in.dtype)
    w_row = jnp.concatenate([w_qkv, w_z, w_k, w_ik, w_b, w_a, pad], axis=1).astype(BF16)
    w_col = jnp.concatenate([w_q, w_v, w_iq, w_iw], axis=1).T.astype(BF16)
    bd = _block_diag_ones(DSA_DIM, DSA_HD)
    gk_row = jnp.tile(g_k, DSA_HEADS)[None, :]
    gq_col = jnp.tile(g_q, DSA_HEADS)[:, None]
    nrow, ncol = w_row.shape[1], w_col.shape[0]
    out_shape = (
        jax.ShapeDtypeStruct((n, GDN_CONV_CH), F32),
        jax.ShapeDtypeStruct((n, GDN_VAL_DIM), F32),
        jax.ShapeDtypeStruct((n, DSA_DIM), BF16),
        jax.ShapeDtypeStruct((n, LANES), F32),
        jax.ShapeDtypeStruct((DSA_DIM, n), BF16),
        jax.ShapeDtypeStruct((n // KEY_TILE, DSA_DIM, KEY_TILE), BF16),
        jax.ShapeDtypeStruct((IDX_Q_DIM, n), BF16),
        jax.ShapeDtypeStruct((IDX_HEADS, n), F32),
    )
    return pl.pallas_call(
        functools.partial(_inproj_kernel, tiles_per_seq=seq // tm),
        grid=(n // tm,),
        in_specs=[
            pl.BlockSpec((tm, d), lambda i: (i, 0)),
            _const_spec((1, d)),
            _const_spec((d, nrow)),
            _const_spec((ncol, d)),
            _const_spec((DSA_DIM, DSA_DIM)),
            _const_spec((1, DSA_DIM)),
            _const_spec((DSA_DIM, 1)),
            _const_spec((CONV_K, GDN_CONV_CH)),
        ],
        out_specs=(
            pl.BlockSpec((tm, GDN_CONV_CH), lambda i: (i, 0)),
            pl.BlockSpec((tm, GDN_VAL_DIM), lambda i: (i, 0)),
            pl.BlockSpec((tm, DSA_DIM), lambda i: (i, 0)),
            pl.BlockSpec((tm, LANES), lambda i: (i, 0)),
            pl.BlockSpec((DSA_DIM, tm), lambda i: (0, i)),
            pl.BlockSpec((tm // KEY_TILE, DSA_DIM, KEY_TILE), lambda i: (i, 0, 0)),
            pl.BlockSpec((IDX_Q_DIM, tm), lambda i: (0, i)),
            pl.BlockSpec((IDX_HEADS, tm), lambda i: (0, i)),
        ),
        out_shape=out_shape,
        scratch_shapes=[pltpu.VMEM((tm + SUBLANES, GDN_CONV_CH), F32)],
        compiler_params=pltpu.CompilerParams(dimension_semantics=("arbitrary",), vmem_limit_bytes=VMEM_LIMIT),
        name="inproj",
    )(x2, g_mix[None, :], w_row, w_col, bd, gk_row, gq_col, conv_w)


def _gdn_kernel(qkv_ref, z_ref, rs_ref, alog_ref, dtb_ref, gout_ref, y_ref, s_ref):
    nb, rows = qkv_ref.shape[0], qkv_ref.shape[1]
    nc = rows // CHUNK

    @pl.when(pl.program_id(1) == 0)
    def _start_of_sequence():
        s_ref[...] = jnp.zeros(s_ref.shape, F32)

    rs = rs_ref[...]
    beta_all = jax.nn.sigmoid(rs)
    sp_in = rs + dtb_ref[...]
    softplus = jnp.maximum(sp_in, 0.0) + jnp.log(1.0 + jnp.exp(-jnp.abs(sp_in)))
    g_all = -jnp.exp(alog_ref[...]) * softplus

    ri = lax.broadcasted_iota(I32, (CHUNK, CHUNK), 0)
    ci = lax.broadcasted_iota(I32, (CHUNK, CHUNK), 1)
    incl = ri >= ci
    strict = ri > ci
    ltri = incl.astype(F32)
    eye = (ri == ci).astype(F32)
    gout = gout_ref[...]

    chains = [(b, c, h) for c in range(nc) for b in range(nb) for h in range(GDN_HEADS)]
    dcum, dcum_t = {}, {}
    for c in range(nc):
        for b in range(nb):
            d = _dot_f32(ltri, g_all[b, c * CHUNK:(c + 1) * CHUNK, :])
            dcum[b, c] = d
            dcum_t[b, c] = d.T

    q16, k16, kb16, vb16, kw16, qdec16, kdect16, gamma, last = ([] for _ in range(9))
    for b, c, h in chains:
        r = slice(c * CHUNK, (c + 1) * CHUNK)
        q = qkv_ref[b, r, h * GDN_DK:(h + 1) * GDN_DK]
        k = qkv_ref[b, r, GDN_KEY_DIM + h * GDN_DK:GDN_KEY_DIM + (h + 1) * GDN_DK]
        v = qkv_ref[b, r, 2 * GDN_KEY_DIM + h * GDN_DV:2 * GDN_KEY_DIM + (h + 1) * GDN_DV]
        beta = beta_all[b, r, RS_BETA + h:RS_BETA + h + 1]
        d_col = dcum[b, c][:, RS_A + h:RS_A + h + 1]
        d_row = dcum_t[b, c][RS_A + h:RS_A + h + 1, :]
        d_last = dcum[b, c][CHUNK - 1:CHUNK, RS_A + h:RS_A + h + 1]
        e_col = jnp.exp(d_col)
        kb = k * beta
        gamma.append(jnp.exp(jnp.where(incl, d_col - d_row, -jnp.inf)))
        q16.append(q.astype(BF16))
        k16.append(k.astype(BF16))
        kb16.append(kb.astype(BF16))
        vb16.append((v * beta).astype(BF16))
        kw16.append((kb * e_col).astype(BF16))
        qdec16.append((q * e_col).astype(BF16))
        kdect16.append((k * jnp.exp(d_last - d_col)).T.astype(BF16))
        last.append(jnp.exp(d_last))

    n = len(chains)
    p16 = [(-jnp.where(strict, _dot_nt(kb16[i], k16[i]) * gamma[i], 0.0)).astype(BF16) for i in range(n)]
    qk16 = [(_dot_nt(q16[i], k16[i]) * gamma[i]).astype(BF16) for i in range(n)]
    t_mat = [eye + p16[i].astype(F32) for i in range(n)]
    p16 = [_dot(p16[i], p16[i]).astype(BF16) for i in range(n)]
    for _ in range(4):
        t_mat = [t_mat[i] + _dot(t_mat[i].astype(BF16), p16[i]) for i in range(n)]
        p16 = [_dot(p16[i], p16[i]).astype(BF16) for i in range(n)]
    t16 = [(t_mat[i] + _dot(t_mat[i].astype(BF16), p16[i])).astype(BF16) for i in range(n)]
    u = [_dot(t16[i], vb16[i]) for i in range(n)]
    w16 = [_dot(t16[i], kw16[i]).astype(BF16) for i in range(n)]

    per_chunk = nb * GDN_HEADS
    s = [s_ref[b, h] for b in range(nb) for h in range(GDN_HEADS)]
    for c in range(nc):
        ids = range(c * per_chunk, (c + 1) * per_chunk)
        s16 = [s[j].astype(BF16) for j in range(per_chunk)]
        v_new16 = [(u[i] - _dot(w16[i], s16[j])).astype(BF16) for j, i in enumerate(ids)]
        o_state = [_dot(qdec16[i], s16[j]) for j, i in enumerate(ids)]
        o = [o_state[j] + _dot(qk16[i], v_new16[j]) for j, i in enumerate(ids)]
        s = [s[j] * last[i] + _dot(kdect16[i], v_new16[j]) for j, i in enumerate(ids)]
        for j, i in enumerate(ids):
            b, _, h = chains[i]
            zc = z_ref[b, c * CHUNK:(c + 1) * CHUNK, h * GDN_DV:(h + 1) * GDN_DV]
            y = _rms_rows(o[j], gout) * _silu(zc)
            y_ref[b, c * CHUNK:(c + 1) * CHUNK, h * GDN_DV:(h + 1) * GDN_DV] = y.astype(y_ref.dtype)
    for j in range(per_chunk):
        s_ref[j // GDN_HEADS, j % GDN_HEADS] = s[j]


def _gdn(qkv, z, rs, a_log, dt_bias, g_out, batch, seq):
    n = qkv.shape[0]
    rows = GDN_CHUNKS_PER_STEP * CHUNK
    nb = GDN_SEQS_PER_STEP if batch % GDN_SEQS_PER_STEP == 0 else 1
    lane_vec = lambda v: jnp.zeros((1, LANES), F32).at[0, RS_A:RS_A + GDN_HEADS].set(v)
    per_seq = lambda a: a.reshape(batch, seq, a.shape[-1])
    block = lambda width: pl.BlockSpec((nb, rows, width), lambda b, i: (b, i, 0))
    y = pl.pallas_call(
        _gdn_kernel,
        grid=(batch // nb, seq // rows),
        in_specs=[
            block(GDN_CONV_CH),
            block(GDN_VAL_DIM),
            block(LANES),
            _const_spec((1, LANES)),
            _const_spec((1, LANES)),
            _const_spec((1, GDN_DV)),
        ],
        out_specs=block(GDN_VAL_DIM),
        out_shape=jax.ShapeDtypeStruct((batch, seq, GDN_VAL_DIM), BF16),
        scratch_shapes=[
            pltpu.VMEM((nb, GDN_HEADS, GDN_DK, GDN_DV), F32),
        ],
        compiler_params=pltpu.CompilerParams(dimension_semantics=("arbitrary", "arbitrary"), vmem_limit_bytes=VMEM_LIMIT),
        name="gdn",
    )(per_seq(qkv), per_seq(z), per_seq(rs), lane_vec(a_log), lane_vec(dt_bias), g_out[None, :])
    return y.reshape(n, GDN_VAL_DIM)


def _alibi_slope(h):
    return 2.0 ** (-8.0 * (h + 1) / DSA_HEADS)


def _order_key(x):
    bits = lax.bitcast_convert_type(x, I32)
    return (bits ^ ((bits >> 31) & jnp.int32(0x7FFFFFFF))) >> 1


def _order_key_to_f32(key):
    full = key << 1
    return lax.bitcast_convert_type(full ^ ((full >> 31) & jnp.int32(0x7FFFFFFF)), F32)


def _tiles_for_block(j):
    return (j * Q_BLOCK + Q_BLOCK + KEY_TILE - 1) // KEY_TILE


def _dsa_kernel(qT_ref, qiT_ref, wT_ref, kn_ref, vT_ref, rs_ref, kpos_ref, slope_ref, o_ref,
                idxk_scr, sc_scr, key_scr, thr_scr, cnt_scr, acc_scr, *, topk, n_pos_bits):
    j = pl.program_id(1)

    @pl.when(j == 0)
    def _new_sequence():
        idxk_scr[...] = rs_ref[:, RS_IDXK:RS_IDXK + IDX_HD].astype(BF16)

    n_tiles = _tiles_for_block(j)
    lane = lax.broadcasted_iota(I32, (1, Q_BLOCK), 1)
    qpos = j * Q_BLOCK + lane
    key_limit = ((qpos >> 6) + 1) << 6
    row = lax.broadcasted_iota(I32, (KEY_TILE, Q_BLOCK), 0)

    w_t = wT_ref[...]
    qi_pairs = [jnp.concatenate([qiT_ref[(2 * p) * IDX_HD:(2 * p + 1) * IDX_HD, :],
                                 qiT_ref[(2 * p + 1) * IDX_HD:(2 * p + 2) * IDX_HD, :]], axis=1)
                for p in range(IDX_HEADS // 2)]

    def score_tiles(t, count, carry):
        s_max, s_min = carry
        r0 = pl.multiple_of(t * KEY_TILE, KEY_TILE)
        kt = idxk_scr[pl.ds(r0, count * KEY_TILE), :]
        accs = [jnp.zeros((KEY_TILE, Q_BLOCK), F32) for _ in range(count)]
        for p in range(IDX_HEADS // 2):
            d = _dot(kt, qi_pairs[p])
            for i in range(count):
                di = d[i * KEY_TILE:(i + 1) * KEY_TILE, :]
                accs[i] = accs[i] + jnp.maximum(di[:, :Q_BLOCK], 0.0) * w_t[2 * p:2 * p + 1, :]
                accs[i] = accs[i] + jnp.maximum(di[:, Q_BLOCK:], 0.0) * w_t[2 * p + 1:2 * p + 2, :]
        for i in range(count):
            admissible = r0 + i * KEY_TILE + row < key_limit
            stored = jnp.where(admissible, accs[i], -jnp.inf)
            sc_scr[t + i] = stored
            key_scr[t + i] = _order_key(stored)
            s_max = jnp.maximum(s_max, jnp.max(stored, axis=0, keepdims=True))
            s_min = jnp.minimum(s_min, jnp.min(jnp.where(admissible, accs[i], jnp.inf), axis=0, keepdims=True))
        return s_max, s_min

    stats = (jnp.full((1, Q_BLOCK), -jnp.inf, F32), jnp.full((1, Q_BLOCK), jnp.inf, F32))
    stats = lax.fori_loop(0, n_tiles // 2, lambda u, st: score_tiles(2 * u, 2, st), stats)
    s_max, s_min = lax.cond(n_tiles % 2 == 1, lambda st: score_tiles(n_tiles - 1, 1, st), lambda st: st, stats)

    def count_tiles(per_tile):
        cnt_scr[...] = jnp.zeros(cnt_scr.shape, F32)
        base = jnp.int32(0)
        arm = 1 << (sc_scr.shape[0].bit_length() - 1)
        while arm >= 1:
            has = (n_tiles & arm) != 0

            @pl.when(has)
            def _run(base=base, arm=arm):
                chains = [jnp.zeros((SUBLANES, Q_BLOCK), F32) for _ in range(COUNT_CHAINS)]
                for i in range(arm):
                    flags = per_tile(base + i)
                    for r in range(KEY_TILE // SUBLANES):
                        chains[r % COUNT_CHAINS] = chains[r % COUNT_CHAINS] + flags[r * SUBLANES:(r + 1) * SUBLANES, :]
                while len(chains) > 1:
                    chains = [chains[k] + chains[k + 1] for k in range(0, len(chains), 2)]
                cnt_scr[...] += chains[0]

            base = base + jnp.where(has, arm, 0)
            arm //= 2
        return jnp.sum(cnt_scr[...], axis=0, keepdims=True).astype(I32)

    def count_ge(thr):
        return count_tiles(lambda i: jnp.where(sc_scr[i] >= thr, 1.0, 0.0))

    def count_keys_below(key, tiles):
        one = jnp.int32(0x3F800000)
        chains = [jnp.zeros((SUBLANES, Q_BLOCK), F32) for _ in range(COUNT_CHAINS)]
        for i in range(tiles):
            below = lax.bitcast_convert_type(((key_scr[i] - key) >> 31) & one, F32)
            for r in range(KEY_TILE // SUBLANES):
                chains[r % COUNT_CHAINS] = chains[r % COUNT_CHAINS] + below[r * SUBLANES:(r + 1) * SUBLANES, :]
        while len(chains) > 1:
            chains = [chains[k] + chains[k + 1] for k in range(0, len(chains), 2)]
        return jnp.sum(chains[0], axis=0, keepdims=True)

    def search_with_tiles(tiles):
        most_below = float(tiles * KEY_TILE - topk)

        def bisect_step(_, st):
            lo, hi, mid, mid_key = st
            up, down = mid * 0.5 + hi * 0.5, lo * 0.5 + mid * 0.5
            up_key, down_key = _order_key(up), _order_key(down)
            take = count_keys_below(mid_key, tiles) <= most_below
            return (jnp.where(take, mid, lo), jnp.where(take, hi, mid),
                    jnp.where(take, up, down), jnp.where(take, up_key, down_key))

        return lambda st: lax.fori_loop(0, BISECT_ROUNDS, bisect_step, st)

    above_max = s_max + jnp.maximum(jnp.abs(s_max) * 1e-6, 1e-30)
    first_mid = s_min * 0.5 + above_max * 0.5
    lo = lax.switch(n_tiles - 1, [search_with_tiles(n) for n in range(1, sc_scr.shape[0] + 1)],
                    (s_min, above_max, first_mid, _order_key(first_mid)))[0]
    few = key_limit < topk
    thr = jnp.where(few, jnp.finfo(F32).min, _order_key_to_f32(_order_key(lo)))
    cnt = jnp.where(few, topk, count_ge(thr))
    thr_scr[...] = thr

    @pl.when(jnp.max(jnp.abs(cnt - topk)) > 0)
    def _resolve():
        def max_below(thr):
            def tile(t, m):
                sc = sc_scr[t]
                return jnp.maximum(m, jnp.max(jnp.where(sc < thr, sc, -jnp.inf), axis=0, keepdims=True))
            return lax.fori_loop(0, n_tiles, tile, jnp.full((1, Q_BLOCK), -jnp.inf, F32))

        def lower(st):
            thr, cnt = st
            nxt = max_below(thr)
            short = cnt < topk
            return jnp.where(short, nxt, thr), jnp.where(short, count_ge(nxt), cnt)

        def min_above(thr):
            def tile(t, m):
                sc = sc_scr[t]
                return jnp.minimum(m, jnp.min(jnp.where(sc > thr, sc, jnp.inf), axis=0, keepdims=True))
            return lax.fori_loop(0, n_tiles, tile, jnp.full((1, Q_BLOCK), jnp.inf, F32))

        def count_eq_below(thr, pos_limit):
            def tile(t, c):
                r0 = t * KEY_TILE
                hit = (sc_scr[t] == thr) & (r0 + row < pos_limit)
                return c + jnp.sum(hit.astype(I32), axis=0, keepdims=True)
            return lax.fori_loop(0, n_tiles, tile, jnp.zeros((1, Q_BLOCK), I32))

        def body(st):
            thr, cnt = st
            nxt = min_above(thr)
            cnt_n = count_ge(nxt)
            active = cnt > topk
            advance = active & (cnt_n >= topk)
            tie = active & (cnt_n < topk)
            @pl.when(jnp.max(tie.astype(I32)) > 0)
            def _drop_surplus_ties():
                need = topk - cnt_n
                pos = jnp.zeros((1, Q_BLOCK), I32)
                for b in range(n_pos_bits - 1, -1, -1):
                    cand = pos + (1 << b)
                    pos = jnp.where(count_eq_below(thr, cand) < need, cand, pos)

                def drop_tile(t, carry):
                    r0 = t * KEY_TILE
                    sc = sc_scr[t]
                    sc_scr[t] = jnp.where(tie & (sc == thr) & (r0 + row > pos), -jnp.inf, sc)
                    return carry

                lax.fori_loop(0, n_tiles, drop_tile, 0)

            return jnp.where(advance, nxt, thr), jnp.where(advance, cnt_n, jnp.where(tie, topk, cnt))

        st = lax.while_loop(lambda st: jnp.min(st[1]) < topk, lower, (thr, cnt))
        thr_scr[...] = lax.while_loop(lambda st: jnp.max(st[1]) > topk, body, st)[0]

    thr = thr_scr[...]

    zero = jnp.zeros((DSA_HD, Q_BLOCK), BF16)
    q_pairs = []
    for p in range(DSA_HEADS // 2):
        a = qT_ref[(2 * p) * DSA_HD:(2 * p + 1) * DSA_HD, :]
        b = qT_ref[(2 * p + 1) * DSA_HD:(2 * p + 2) * DSA_HD, :]
        q_pairs.append(jnp.concatenate([jnp.concatenate([a, zero], axis=1),
                                        jnp.concatenate([zero, b], axis=1),
                                        slope_ref[p]], axis=0))
    acc_scr[...] = jnp.zeros(acc_scr.shape, F32)
    qposf = qpos.astype(F32)

    def qk_products(t):
        r0 = pl.multiple_of(t * KEY_TILE, KEY_TILE)
        kt = kn_ref[pl.ds(r0, KEY_TILE), :]
        kp = kpos_ref[pl.ds(r0, KEY_TILE), :]
        s2 = [_dot(jnp.concatenate([kt[:, p * 2 * DSA_HD:(p + 1) * 2 * DSA_HD], kp], axis=1), q_pairs[p])
              for p in range(DSA_HEADS // 2)]
        return t, sc_scr[t] >= thr, s2

    def softmax_pv(tile, ml, last_tile):
        t, sel, s2 = tile
        vt = vT_ref[t]
        if last_tile:
            after = jnp.maximum((t * KEY_TILE + row).astype(F32) - qposf, 0.0)
        m_all, l_all = ml
        m_rows, l_rows = [], []
        for h in range(DSA_HEADS):
            s = s2[h // 2][:, (h % 2) * Q_BLOCK:(h % 2 + 1) * Q_BLOCK]
            if last_tile:
                s = s - (2.0 * _alibi_slope(h)) * after
            s = jnp.where(sel, s, NEG_BIG)
            m_old = m_all[h:h + 1, :]
            m_new = jnp.maximum(m_old, jnp.max(s, axis=0, keepdims=True))
            alpha = jnp.exp(m_old - m_new)
            pr = jnp.exp(s - m_new)
            l_rows.append(alpha * l_all[h:h + 1, :] + jnp.sum(pr, axis=0, keepdims=True))
            m_rows.append(m_new)
            hs = slice(h * DSA_HD, (h + 1) * DSA_HD)
            acc_scr[hs, :] = alpha * acc_scr[hs, :] + _dot(vt[hs, :], pr.astype(BF16))
        return jnp.concatenate(m_rows, axis=0), jnp.concatenate(l_rows, axis=0)

    def attend_run(first, count, ml, ends_block):
        tiles = [qk_products(first + i) for i in range(count)]
        for i, tile in enumerate(tiles):
            ml = softmax_pv(tile, ml, ends_block and i == count - 1)
        return ml

    ml = (jnp.full((DSA_HEADS, Q_BLOCK), NEG_BIG, F32), jnp.zeros((DSA_HEADS, Q_BLOCK), F32))
    full_runs = (n_tiles - 1) // ATTEND_RUN
    ml = lax.fori_loop(0, full_runs, lambda u, ml: attend_run(u * ATTEND_RUN, ATTEND_RUN, ml, False), ml)
    rest = n_tiles - full_runs * ATTEND_RUN
    _, l_all = lax.switch(rest - 1, [(lambda ml, c=c: attend_run(n_tiles - c, c, ml, True))
                                     for c in range(1, ATTEND_RUN + 1)], ml)

    outs = [acc_scr[h * DSA_HD:(h + 1) * DSA_HD, :] / l_all[h:h + 1, :] for h in range(DSA_HEADS)]
    o_ref[...] = jnp.concatenate(outs, axis=0).T.astype(o_ref.dtype)


def _dsa(qT, qiT, wT, kn, vT, rs, batch, seq):
    n = kn.shape[0]
    blocks = seq // Q_BLOCK
    tiles = seq // KEY_TILE
    topk = min(TOPK_MAX, seq // 4)
    col_map = lambda b, j: (0, b * blocks + j)
    pos = jnp.arange(seq, dtype=I32)
    kpos = jnp.zeros((seq, LANES), F32).at[:, 0].set((pos >> 4).astype(F32)).at[:, 1].set((pos & 15).astype(F32))
    slopes = jnp.asarray([_alibi_slope(h) for h in range(DSA_HEADS)], F32).reshape(DSA_HEADS // 2, 2)
    slope_cols = jnp.repeat(slopes, Q_BLOCK, axis=1)
    slope_rows = jnp.zeros((DSA_HEADS // 2, LANES, 2 * Q_BLOCK), F32)
    slope_rows = slope_rows.at[:, 0, :].set(16.0 * slope_cols).at[:, 1, :].set(slope_cols)
    return pl.pallas_call(
        functools.partial(_dsa_kernel, topk=topk, n_pos_bits=seq.bit_length()),
        grid=(batch, blocks),
        in_specs=[
            pl.BlockSpec((DSA_DIM, Q_BLOCK), col_map),
            pl.BlockSpec((IDX_Q_DIM, Q_BLOCK), col_map),
            pl.BlockSpec((IDX_HEADS, Q_BLOCK), col_map),
            pl.BlockSpec((seq, DSA_DIM), lambda b, j: (b, 0)),
            pl.BlockSpec((tiles, DSA_DIM, KEY_TILE), lambda b, j: (b, 0, 0)),
            pl.BlockSpec((seq, LANES), lambda b, j: (b, 0)),
            _const_spec((seq, LANES)),
            _const_spec((DSA_HEADS // 2, LANES, 2 * Q_BLOCK)),
        ],
        out_specs=pl.BlockSpec((Q_BLOCK, DSA_DIM), lambda b, j: (b * blocks + j, 0)),
        out_shape=jax.ShapeDtypeStruct((n, DSA_DIM), BF16),
        scratch_shapes=[
            pltpu.VMEM((seq, IDX_HD), BF16),
            pltpu.VMEM((tiles, KEY_TILE, Q_BLOCK), F32),
            pltpu.VMEM((tiles, KEY_TILE, Q_BLOCK), I32),
            pltpu.VMEM((1, Q_BLOCK), F32),
            pltpu.VMEM((SUBLANES, Q_BLOCK), F32),
            pltpu.VMEM((DSA_DIM, Q_BLOCK), F32),
        ],
        compiler_params=pltpu.CompilerParams(dimension_semantics=("arbitrary", "arbitrary"), vmem_limit_bytes=VMEM_LIMIT),
        name="dsa",
    )(qT, qiT, wT, kn, vT, rs, kpos.astype(BF16), slope_rows.astype(BF16))


def _memkv_kernel(mem_ref, g_ref, w_ref, bd_ref, gk_ref, k_ref, v_ref):
    h = _rms_rows(mem_ref[...], g_ref[...]).astype(BF16)
    kv = _dot(h, w_ref[...])
    k = kv[:, :XA_DIM]
    kms = _dot((k * k).astype(BF16), bd_ref[...]) * (1.0 / XA_HD)
    k_ref[...] = (k * lax.rsqrt(kms + EPS) * gk_ref[...]).astype(BF16)
    v_ref[...] = kv[:, XA_DIM:].astype(BF16)


def _memkv(mem2, g_mem, w_xkv, g_xk):
    n, d = mem2.shape
    tm = min(ROW_TILE, n)
    return pl.pallas_call(
        _memkv_kernel,
        grid=(n // tm,),
        in_specs=[
            pl.BlockSpec((tm, d), lambda i: (i, 0)),
            _const_spec((1, d)),
            _const_spec((d, 2 * XA_DIM)),
            _const_spec((XA_DIM, XA_DIM)),
            _const_spec((1, XA_DIM)),
        ],
        out_specs=(pl.BlockSpec((tm, XA_DIM), lambda i: (i, 0)), pl.BlockSpec((tm, XA_DIM), lambda i: (i, 0))),
        out_shape=(jax.ShapeDtypeStruct((n, XA_DIM), BF16), jax.ShapeDtypeStruct((n, XA_DIM), BF16)),
        compiler_params=pltpu.CompilerParams(dimension_semantics=("arbitrary",), vmem_limit_bytes=VMEM_LIMIT),
        name="memkv",
    )(mem2, g_mem[None, :], w_xkv.astype(BF16), _block_diag_ones(XA_DIM, XA_HD), jnp.tile(g_xk, XA_HEADS)[None, :])


def _tail_kernel(x_ref, ya_ref, yb_ref, wout_ref, gx_ref, wxq_ref, bd_ref, gxq_ref, km_ref, vm_ref, wxo_ref,
                 gf_ref, wgu_ref, wd_ref, o_ref, *, ff_chunks):
    tm = x_ref.shape[0]
    part = tm // ROW_PARTS
    groups = [slice(g * part, (g + 1) * part) for g in range(ROW_PARTS)]
    d_ff = wd_ref.shape[0]
    fc = d_ff // ff_chunks

    x1 = [x_ref[r, :] + _dot(ya_ref[r, :], wout_ref[0:GDN_VAL_DIM, :]) + _dot(yb_ref[r, :], wout_ref[GDN_VAL_DIM:, :])
          for r in groups]

    qn = []
    for g in range(ROW_PARTS):
        h = _rms_rows(x1[g], gx_ref[...]).astype(BF16)
        q = _dot(h, wxq_ref[...])
        qms = _dot((q * q).astype(BF16), bd_ref[...]) * (1.0 / XA_HD)
        qn.append((q * lax.rsqrt(qms + EPS) * gxq_ref[...]).astype(BF16))

    attn = []
    for g in range(ROW_PARTS):
        heads = []
        for hd in range(XA_HEADS):
            hs = slice(hd * XA_HD, (hd + 1) * XA_HD)
            s = _dot_nt(qn[g][:, hs], km_ref[0, :, hs]) * (XA_HD ** -0.5)
            pr = jnp.exp(s - jnp.max(s, axis=-1, keepdims=True))
            o = _dot(pr.astype(BF16), vm_ref[0, :, hs]) / jnp.sum(pr, axis=-1, keepdims=True)
            heads.append(o.astype(BF16))
        attn.append(jnp.concatenate(heads, axis=1))

    x2 = [x1[g] + _dot(attn[g], wxo_ref[...]) for g in range(ROW_PARTS)]
    h = [_rms_rows(x2[g], gf_ref[...]).astype(BF16) for g in range(ROW_PARTS)]
    acc = x2
    for c in range(ff_chunks):
        for g in range(ROW_PARTS):
            gate = _dot(h[g], wgu_ref[:, c * fc:(c + 1) * fc])
            up = _dot(h[g], wgu_ref[:, d_ff + c * fc:d_ff + (c + 1) * fc])
            a = (_silu(gate) * up).astype(BF16)
            acc[g] = acc[g] + _dot(a, wd_ref[c * fc:(c + 1) * fc, :])
    for g, r in enumerate(groups):
        o_ref[r, :] = acc[g]


def _tail(x2, ya, yb, w_out, g_xattn, w_xq, g_xq, km, vm, w_xo, g_ffn, w_gu, w_down, seq):
    n, d = x2.shape
    tm = ROW_TILE
    d_ff = w_down.shape[0]
    ff_chunks = 2 if (d_ff // 2) % LANES == 0 else 1
    n_mem = km.shape[1]
    per_seq = seq // tm
    return pl.pallas_call(
        functools.partial(_tail_kernel, ff_chunks=ff_chunks),
        grid=(n // tm,),
        in_specs=[
            pl.BlockSpec((tm, d), lambda i: (i, 0)),
            pl.BlockSpec((tm, GDN_VAL_DIM), lambda i: (i, 0)),
            pl.BlockSpec((tm, DSA_DIM), lambda i: (i, 0)),
            _const_spec((GDN_VAL_DIM + DSA_DIM, d)),
            _const_spec((1, d)),
            _const_spec((d, XA_DIM)),
            _const_spec((XA_DIM, XA_DIM)),
            _const_spec((1, XA_DIM)),
            pl.BlockSpec((1, n_mem, XA_DIM), lambda i: (i // per_seq, 0, 0)),
            pl.BlockSpec((1, n_mem, XA_DIM), lambda i: (i // per_seq, 0, 0)),
            _const_spec((XA_DIM, d)),
            _const_spec((1, d)),
            _const_spec((d, 2 * d_ff)),
            _const_spec((d_ff, d)),
        ],
        out_specs=pl.BlockSpec((tm, d), lambda i: (i, 0)),
        out_shape=jax.ShapeDtypeStruct((n, d), F32),
        compiler_params=pltpu.CompilerParams(dimension_semantics=("arbitrary",), vmem_limit_bytes=VMEM_LIMIT),
        name="tail",
    )(x2, ya, yb, w_out.astype(BF16), g_xattn[None, :], w_xq.astype(BF16), _block_diag_ones(XA_DIM, XA_HD),
      jnp.tile(g_xq, XA_HEADS)[None, :], km, vm, w_xo.astype(BF16), g_ffn[None, :], w_gu.astype(BF16),
      w_down.astype(BF16))


def kernel(x, mem, g_mix, w_in, conv_w, a_log, dt_bias, g_gdn_out, g_q_dsa, g_k_dsa, w_out, g_xattn, g_mem, w_xq,
           w_xkv, g_xq, g_xk, w_xo, g_ffn, w_gu, w_down):
    batch, seq, d = x.shape
    n_mem = mem.shape[1]
    assert seq % ROW_TILE == 0 and seq % KEY_TILE == 0 and ROW_TILE % KEY_TILE == 0
    for l in range(g_mix.shape[0]):
        x2 = x.reshape(batch * seq, d)
        qkv, z, kn, rs, qT, vT, qiT, wT = _inproj(x2, g_mix[l], w_in[l], g_q_dsa[l], g_k_dsa[l], conv_w[l], seq)
        ya = _gdn(qkv, z, rs, a_log[l], dt_bias[l], g_gdn_out[l], batch, seq)
        yb = _dsa(qT, qiT, wT, kn, vT, rs, batch, seq)
        km, vm = _memkv(mem.reshape(batch * n_mem, d), g_mem[l], w_xkv[l], g_xk[l])
        km = km.reshape(batch, n_mem, XA_DIM)
        vm = vm.reshape(batch, n_mem, XA_DIM)
        x = _tail(x2, ya, yb, w_out[l], g_xattn[l], w_xq[l], g_xq[l], km, vm, w_xo[l], g_ffn[l], w_gu[l],
                  w_down[l], seq).reshape(batch, seq, d)
    return x
```

```python
import functools

import jax
import jax.numpy as jnp
from jax import lax
from jax.experimental import pallas as pl
from jax.experimental.pallas import tpu as pltpu

F32 = jnp.float32
BF16 = jnp.bfloat16
I32 = jnp.int32

EPS = 1e-6
CHUNK = 64
CHUNK_SHIFT = CHUNK.bit_length() - 1
assert CHUNK == 1 << CHUNK_SHIFT
GDN_HEADS, GDN_DK, GDN_DV, CONV_K = 4, 128, 128, 4
DSA_HEADS, DSA_HD = 8, 64
IDX_HEADS, IDX_HD = 16, 64
TOPK_MAX = 256
XA_HEADS, XA_HD = 4, 128

GDN_KEY_DIM = GDN_HEADS * GDN_DK
GDN_VAL_DIM = GDN_HEADS * GDN_DV
GDN_CONV_CH = 2 * GDN_KEY_DIM + GDN_VAL_DIM
DSA_DIM = DSA_HEADS * DSA_HD
IDX_Q_DIM = IDX_HEADS * IDX_HD
XA_DIM = XA_HEADS * XA_HD

LANES = 128
SUBLANES = 8
MXU_WIDTH = 256
VMEM_LIMIT = 56 * 1024 * 1024

RS_IDXK = 0
RS_BETA = IDX_HD
RS_A = IDX_HD + GDN_HEADS

ROW_TILE = 512
ROW_PARTS = 2
KEY_TILE = 256
Q_BLOCK = 128
GDN_CHUNKS_PER_STEP = 4
GDN_SEQS_PER_STEP = 4

BISECT_ROUNDS = 20
COUNT_CHAINS = 4
ATTEND_RUN = 4
KPOS_DIGIT_BITS = 4
NEG_BIG = -1e30


def _dot(a, b):
    return jnp.dot(a, b, preferred_element_type=F32)


def _dot_nt(a, b):
    return lax.dot_general(a, b, (((1,), (1,)), ((), ())), preferred_element_type=F32)


def _dot_f32(a, b):
    return jnp.dot(a, b, preferred_element_type=F32, precision=lax.Precision.HIGHEST)


def _silu(x):
    half = 0.5 * x
    return half + half * jnp.tanh(half)


def _rms_rows(x, g):
    ms = jnp.mean(x * x, axis=-1, keepdims=True)
    return x * lax.rsqrt(ms + EPS) * g


def _const_spec(shape):
    nd = len(shape)
    return pl.BlockSpec(shape, lambda *_: (0,) * nd, pipeline_mode=pl.Buffered(1))


def _block_diag_ones(n, blk):
    r = lax.broadcasted_iota(I32, (n, n), 0) // blk
    c = lax.broadcasted_iota(I32, (n, n), 1) // blk
    return (r == c).astype(BF16)


def _inproj_kernel(x_ref, g_ref, wrow_ref, wcol_ref, bd_ref, gk_ref, gq_ref, cw_ref,
                   qkv_ref, z_ref, kn_ref, rs_ref, qT_ref, vT_ref, qiT_ref, wT_ref, xc_ref, *, tiles_per_seq):
    tm = x_ref.shape[0]

    @pl.when(pl.program_id(0) % tiles_per_seq == 0)
    def _start_of_sequence():
        xc_ref[0:SUBLANES, :] = jnp.zeros((SUBLANES, GDN_CONV_CH), F32)

    h = _rms_rows(x_ref[...], g_ref[...]).astype(BF16)
    c0, c1, c2, c3 = GDN_CONV_CH, GDN_CONV_CH + GDN_VAL_DIM, GDN_CONV_CH + GDN_VAL_DIM + DSA_DIM, \
        GDN_CONV_CH + GDN_VAL_DIM + DSA_DIM + LANES

    xc_ref[SUBLANES:SUBLANES + tm, :] = _dot(h, wrow_ref[:, 0:c0])

    zkr = _dot(h, wrow_ref[:, c0:c3])
    z_ref[...] = zkr[:, 0:c1 - c0]
    k = zkr[:, c1 - c0:c2 - c0]
    kms = _dot((k * k).astype(BF16), bd_ref[...]) * (1.0 / DSA_HD)
    kn_ref[...] = (k * lax.rsqrt(kms + EPS) * gk_ref[...]).astype(BF16)
    rs_ref[...] = zkr[:, c2 - c0:c3 - c0]

    r0, r1, r2, r3 = DSA_DIM, 2 * DSA_DIM, 2 * DSA_DIM + IDX_Q_DIM, 2 * DSA_DIM + IDX_Q_DIM + IDX_HEADS
    qvT = _dot_nt(wcol_ref[0:r1, :], h)
    qT = qvT[0:r0, :]
    qms = _dot(bd_ref[...], (qT * qT).astype(BF16)) * (1.0 / DSA_HD)
    qT_ref[...] = (qT * lax.rsqrt(qms + EPS) * gq_ref[...] * (DSA_HD ** -0.5)).astype(BF16)
    vT = qvT[r0:r1, :].astype(BF16)
    for i in range(vT_ref.shape[0]):
        vT_ref[i] = vT[:, i * KEY_TILE:(i + 1) * KEY_TILE]
    qwT = _dot_nt(wcol_ref[r1:r3, :], h)
    qiT_ref[...] = (qwT[0:IDX_Q_DIM, :] * (IDX_HD ** -0.5)).astype(BF16)
    wT_ref[...] = qwT[IDX_Q_DIM:, :] * (IDX_HEADS ** -0.5)

    cw = cw_ref[...]
    conv = cw[CONV_K - 1:CONV_K, :] * xc_ref[SUBLANES:SUBLANES + tm, :]
    for j in range(CONV_K - 2, -1, -1):
        back = CONV_K - 1 - j
        conv = conv + cw[j:j + 1, :] * xc_ref[SUBLANES - back:SUBLANES - back + tm, :]
    xc_ref[0:SUBLANES, :] = xc_ref[tm:tm + SUBLANES, :]
    act = _silu(conv)
    for hd in range(GDN_HEADS):
        qs = slice(hd * GDN_DK, (hd + 1) * GDN_DK)
        ks = slice(GDN_KEY_DIM + hd * GDN_DK, GDN_KEY_DIM + (hd + 1) * GDN_DK)
        q, k = act[:, qs], act[:, ks]
        qkv_ref[:, qs] = q * lax.rsqrt(jnp.sum(q * q, axis=-1, keepdims=True) + EPS) * (GDN_DK ** -0.5)
        qkv_ref[:, ks] = k * lax.rsqrt(jnp.sum(k * k, axis=-1, keepdims=True) + EPS)
    qkv_ref[:, 2 * GDN_KEY_DIM:] = act[:, 2 * GDN_KEY_DIM:]


def _inproj(x2, g_mix, w_in, g_q, g_k, conv_w, seq):
    n, d = x2.shape
    tm = ROW_TILE
    sizes = (GDN_CONV_CH, GDN_VAL_DIM, GDN_HEADS, GDN_HEADS, DSA_DIM, DSA_DIM, DSA_DIM, IDX_Q_DIM, IDX_HD, IDX_HEADS)
    offs = [0]
    for s in sizes:
        offs.append(offs[-1] + s)
    (w_qkv, w_z, w_b, w_a, w_q, w_k, w_v, w_iq, w_ik, w_iw) = [w_in[:, offs[i]:offs[i + 1]] for i in range(len(sizes))]
    pad = jnp.zeros((d, LANES - IDX_HD - 2 * GDN_HEADS), w_in.dtype)
    w_row = jnp.concatenate([w_qkv, w_z, w_k, w_ik, w_b, w_a, pad], axis=1).astype(BF16)
    w_col = jnp.concatenate([w_q, w_v, w_iq, w_iw], axis=1).T.astype(BF16)
    bd = _block_diag_ones(DSA_DIM, DSA_HD)
    gk_row = jnp.tile(g_k, DSA_HEADS)[None, :]
    gq_col = jnp.tile(g_q, DSA_HEADS)[:, None]
    nrow, ncol = w_row.shape[1], w_col.shape[0]
    out_shape = (
        jax.ShapeDtypeStruct((n, GDN_CONV_CH), F32),
        jax.ShapeDtypeStruct((n, GDN_VAL_DIM), F32),
        jax.ShapeDtypeStruct((n, DSA_DIM), BF16),
        jax.ShapeDtypeStruct((n, LANES), F32),
        jax.ShapeDtypeStruct((DSA_DIM, n), BF16),
        jax.ShapeDtypeStruct((n // KEY_TILE, DSA_DIM, KEY_TILE), BF16),
        jax.ShapeDtypeStruct((IDX_Q_DIM, n), BF16),
        jax.ShapeDtypeStruct((IDX_HEADS, n), F32),
    )
    return pl.pallas_call(
        functools.partial(_inproj_kernel, tiles_per_seq=seq // tm),
        grid=(n // tm,),
        in_specs=[
            pl.BlockSpec((tm, d), lambda i: (i, 0)),
            _const_spec((1, d)),
            _const_spec((d, nrow)),
            _const_spec((ncol, d)),
            _const_spec((DSA_DIM, DSA_DIM)),
            _const_spec((1, DSA_DIM)),
            _const_spec((DSA_DIM, 1)),
            _const_spec((CONV_K, GDN_CONV_CH)),
        ],
        out_specs=(
            pl.BlockSpec((tm, GDN_CONV_CH), lambda i: (i, 0)),
            pl.BlockSpec((tm, GDN_VAL_DIM), lambda i: (i, 0)),
            pl.BlockSpec((tm, DSA_DIM), lambda i: (i, 0)),
            pl.BlockSpec((tm, LANES), lambda i: (i, 0)),
            pl.BlockSpec((DSA_DIM, tm), lambda i: (0, i)),
            pl.BlockSpec((tm // KEY_TILE, DSA_DIM, KEY_TILE), lambda i: (i, 0, 0)),
            pl.BlockSpec((IDX_Q_DIM, tm), lambda i: (0, i)),
            pl.BlockSpec((IDX_HEADS, tm), lambda i: (0, i)),
        ),
        out_shape=out_shape,
        scratch_shapes=[pltpu.VMEM((tm + SUBLANES, GDN_CONV_CH), F32)],
        compiler_params=pltpu.CompilerParams(dimension_semantics=("arbitrary",), vmem_limit_bytes=VMEM_LIMIT),
        name="inproj",
    )(x2, g_mix[None, :], w_row, w_col, bd, gk_row, gq_col, conv_w)


def _gdn_kernel(qkv_ref, z_ref, rs_ref, alog_ref, dtb_ref, gout_ref, y_ref, s_ref):
    nb, rows = qkv_ref.shape[0], qkv_ref.shape[1]
    nc = rows // CHUNK

    @pl.when(pl.program_id(1) == 0)
    def _start_of_sequence():
        s_ref[...] = jnp.zeros(s_ref.shape, F32)

    rs = rs_ref[...]
    beta_all = jax.nn.sigmoid(rs)
    sp_in = rs + dtb_ref[...]
    softplus = jnp.maximum(sp_in, 0.0) + jnp.log(1.0 + jnp.exp(-jnp.abs(sp_in)))
    g_all = -jnp.exp(alog_ref[...]) * softplus

    ri = lax.broadcasted_iota(I32, (CHUNK, CHUNK), 0)
    ci = lax.broadcasted_iota(I32, (CHUNK, CHUNK), 1)
    incl = ri >= ci
    strict = ri > ci
    ltri = incl.astype(F32)
    eye = (ri == ci).astype(F32)
    gout = gout_ref[...]

    chains = [(b, c, h) for c in range(nc) for b in range(nb) for h in range(GDN_HEADS)]
    dcum, dcum_t = {}, {}
    for c in range(nc):
        for b in range(nb):
            d = _dot_f32(ltri, g_all[b, c * CHUNK:(c + 1) * CHUNK, :])
            dcum[b, c] = d
            dcum_t[b, c] = d.T

    q16, k16, kb16, vb16, kw16, qdec16, kdect16, gamma, last = ([] for _ in range(9))
    for b, c, h in chains:
        r = slice(c * CHUNK, (c + 1) * CHUNK)
        q = qkv_ref[b, r, h * GDN_DK:(h + 1) * GDN_DK]
        k = qkv_ref[b, r, GDN_KEY_DIM + h * GDN_DK:GDN_KEY_DIM + (h + 1) * GDN_DK]
        v = qkv_ref[b, r, 2 * GDN_KEY_DIM + h * GDN_DV:2 * GDN_KEY_DIM + (h + 1) * GDN_DV]
        beta = beta_all[b, r, RS_BETA + h:RS_BETA + h + 1]
        d_col = dcum[b, c][:, RS_A + h:RS_A + h + 1]
        d_row = dcum_t[b, c][RS_A + h:RS_A + h + 1, :]
        d_last = dcum[b, c][CHUNK - 1:CHUNK, RS_A + h:RS_A + h + 1]
        e_col = jnp.exp(d_col)
        kb = k * beta
        gamma.append(jnp.exp(jnp.where(incl, d_col - d_row, -jnp.inf)))
        q16.append(q.astype(BF16))
        k16.append(k.astype(BF16))
        kb16.append(kb.astype(BF16))
        vb16.append((v * beta).astype(BF16))
        kw16.append((kb * e_col).astype(BF16))
        qdec16.append((q * e_col).astype(BF16))
        kdect16.append((k * jnp.exp(d_last - d_col)).T.astype(BF16))
        last.append(jnp.exp(d_last))

    n = len(chains)
    p16 = [(-jnp.where(strict, _dot_nt(kb16[i], k16[i]) * gamma[i], 0.0)).astype(BF16) for i in range(n)]
    qk16 = [(_dot_nt(q16[i], k16[i]) * gamma[i]).astype(BF16) for i in range(n)]
    t_mat = [eye + p16[i].astype(F32) for i in range(n)]
    p16 = [_dot(p16[i], p16[i]).astype(BF16) for i in range(n)]
    for _ in range(4):
        t_mat = [t_mat[i] + _dot(t_mat[i].astype(BF16), p16[i]) for i in range(n)]
        p16 = [_dot(p16[i], p16[i]).astype(BF16) for i in range(n)]
    t16 = [(t_mat[i] + _dot(t_mat[i].astype(BF16), p16[i])).astype(BF16) for i in range(n)]
    u = [_dot(t16[i], vb16[i]) for i in range(n)]
    w16 = [_dot(t16[i], kw16[i]).astype(BF16) for i in range(n)]

    per_chunk = nb * GDN_HEADS
    s = [s_ref[b, h] for b in range(nb) for h in range(GDN_HEADS)]
    for c in range(nc):
        ids = range(c * per_chunk, (c + 1) * per_chunk)
        s16 = [s[j].astype(BF16) for j in range(per_chunk)]
        v_new16 = [(u[i] - _dot(w16[i], s16[j])).astype(BF16) for j, i in enumerate(ids)]
        o_state = [_dot(qdec16[i], s16[j]) for j, i in enumerate(ids)]
        o = [o_state[j] + _dot(qk16[i], v_new16[j]) for j, i in enumerate(ids)]
        s = [s[j] * last[i] + _dot(kdect16[i], v_new16[j]) for j, i in enumerate(ids)]
        for j, i in enumerate(ids):
            b, _, h = chains[i]
            zc = z_ref[b, c * CHUNK:(c + 1) * CHUNK, h * GDN_DV:(h + 1) * GDN_DV]
            y = _rms_rows(o[j], gout) * _silu(zc)
            y_ref[b, c * CHUNK:(c + 1) * CHUNK, h * GDN_DV:(h + 1) * GDN_DV] = y.astype(y_ref.dtype)
    for j in range(per_chunk):
        s_ref[j // GDN_HEADS, j % GDN_HEADS] = s[j]


def _gdn(qkv, z, rs, a_log, dt_bias, g_out, batch, seq):
    n = qkv.shape[0]
    rows = GDN_CHUNKS_PER_STEP * CHUNK
    nb = GDN_SEQS_PER_STEP if batch % GDN_SEQS_PER_STEP == 0 else 1
    lane_vec = lambda v: jnp.zeros((1, LANES), F32).at[0, RS_A:RS_A + GDN_HEADS].set(v)
    per_seq = lambda a: a.reshape(batch, seq, a.shape[-1])
    block = lambda width: pl.BlockSpec((nb, rows, width), lambda b, i: (b, i, 0))
    y = pl.pallas_call(
        _gdn_kernel,
        grid=(batch // nb, seq // rows),
        in_specs=[
            block(GDN_CONV_CH),
            block(GDN_VAL_DIM),
            block(LANES),
            _const_spec((1, LANES)),
            _const_spec((1, LANES)),
            _const_spec((1, GDN_DV)),
        ],
        out_specs=block(GDN_VAL_DIM),
        out_shape=jax.ShapeDtypeStruct((batch, seq, GDN_VAL_DIM), BF16),
        scratch_shapes=[
            pltpu.VMEM((nb, GDN_HEADS, GDN_DK, GDN_DV), F32),
        ],
        compiler_params=pltpu.CompilerParams(dimension_semantics=("arbitrary", "arbitrary"), vmem_limit_bytes=VMEM_LIMIT),
        name="gdn",
    )(per_seq(qkv), per_seq(z), per_seq(rs), lane_vec(a_log), lane_vec(dt_bias), g_out[None, :])
    return y.reshape(n, GDN_VAL_DIM)


def _alibi_slope(h):
    return 2.0 ** (-8.0 * (h + 1) / DSA_HEADS)


def _order_key(x):
    bits = lax.bitcast_convert_type(x, I32)
    return (bits ^ ((bits >> 31) & jnp.int32(0x7FFFFFFF))) >> 1


def _order_key_to_f32(key):
    full = key << 1
    return lax.bitcast_convert_type(full ^ ((full >> 31) & jnp.int32(0x7FFFFFFF)), F32)


def _tiles_for_block(j):
    return (j * Q_BLOCK + Q_BLOCK + KEY_TILE - 1) // KEY_TILE


def _dsa_kernel(qT_ref, qiT_ref, wT_ref, kn_ref, vT_ref, rs_ref, kpos_ref, slope_ref, o_ref,
                idxk_scr, sc_scr, key_scr, thr_scr, cnt_scr, acc_scr, *, topk, n_pos_bits):
    j = pl.program_id(1)

    @pl.when(j == 0)
    def _new_sequence():
        idxk_scr[...] = rs_ref[:, RS_IDXK:RS_IDXK + IDX_HD].astype(BF16)

    n_tiles = _tiles_for_block(j)
    lane = lax.broadcasted_iota(I32, (1, Q_BLOCK), 1)
    qpos = j * Q_BLOCK + lane
    key_limit = ((qpos >> CHUNK_SHIFT) + 1) << CHUNK_SHIFT
    row = lax.broadcasted_iota(I32, (KEY_TILE, Q_BLOCK), 0)

    w_t = wT_ref[...]
    qi_pairs = [jnp.concatenate([qiT_ref[(2 * p) * IDX_HD:(2 * p + 1) * IDX_HD, :],
                                 qiT_ref[(2 * p + 1) * IDX_HD:(2 * p + 2) * IDX_HD, :]], axis=1)
                for p in range(IDX_HEADS // 2)]

    def score_tiles(t, count, carry):
        s_max, s_min = carry
        r0 = pl.multiple_of(t * KEY_TILE, KEY_TILE)
        kt = idxk_scr[pl.ds(r0, count * KEY_TILE), :]
        accs = [jnp.zeros((KEY_TILE, Q_BLOCK), F32) for _ in range(count)]
        for p in range(IDX_HEADS // 2):
            d = _dot(kt, qi_pairs[p])
            for i in range(count):
                di = d[i * KEY_TILE:(i + 1) * KEY_TILE, :]
                accs[i] = accs[i] + jnp.maximum(di[:, :Q_BLOCK], 0.0) * w_t[2 * p:2 * p + 1, :]
                accs[i] = accs[i] + jnp.maximum(di[:, Q_BLOCK:], 0.0) * w_t[2 * p + 1:2 * p + 2, :]
        for i in range(count):
            admissible = r0 + i * KEY_TILE + row < key_limit
            stored = jnp.where(admissible, accs[i], -jnp.inf)
            sc_scr[t + i] = stored
            key_scr[t + i] = _order_key(stored)
            s_max = jnp.maximum(s_max, jnp.max(stored, axis=0, keepdims=True))
            s_min = jnp.minimum(s_min, jnp.min(jnp.where(admissible, accs[i], jnp.inf), axis=0, keepdims=True))
        return s_max, s_min

    stats = (jnp.full((1, Q_BLOCK), -jnp.inf, F32), jnp.full((1, Q_BLOCK), jnp.inf, F32))
    stats = lax.fori_loop(0, n_tiles // 2, lambda u, st: score_tiles(2 * u, 2, st), stats)
    s_max, s_min = lax.cond(n_tiles % 2 == 1, lambda st: score_tiles(n_tiles - 1, 1, st), lambda st: st, stats)

    def count_tiles(per_tile):
        cnt_scr[...] = jnp.zeros(cnt_scr.shape, F32)
        base = jnp.int32(0)
        arm = 1 << (sc_scr.shape[0].bit_length() - 1)
        while arm >= 1:
            has = (n_tiles & arm) != 0

            @pl.when(has)
            def _run(base=base, arm=arm):
                chains = [jnp.zeros((SUBLANES, Q_BLOCK), F32) for _ in range(COUNT_CHAINS)]
                for i in range(arm):
                    flags = per_tile(base + i)
                    for r in range(KEY_TILE // SUBLANES):
                        chains[r % COUNT_CHAINS] = chains[r % COUNT_CHAINS] + flags[r * SUBLANES:(r + 1) * SUBLANES, :]
                while len(chains) > 1:
                    chains = [chains[k] + chains[k + 1] for k in range(0, len(chains), 2)]
                cnt_scr[...] += chains[0]

            base = base + jnp.where(has, arm, 0)
            arm //= 2
        return jnp.sum(cnt_scr[...], axis=0, keepdims=True).astype(I32)

    def count_ge(thr):
        return count_tiles(lambda i: jnp.where(sc_scr[i] >= thr, 1.0, 0.0))

    def count_keys_below(key, tiles):
        one = jnp.int32(0x3F800000)
        chains = [jnp.zeros((SUBLANES, Q_BLOCK), F32) for _ in range(COUNT_CHAINS)]
        for i in range(tiles):
            below = lax.bitcast_convert_type(((key_scr[i] - key) >> 31) & one, F32)
            for r in range(KEY_TILE // SUBLANES):
                chains[r % COUNT_CHAINS] = chains[r % COUNT_CHAINS] + below[r * SUBLANES:(r + 1) * SUBLANES, :]
        while len(chains) > 1:
            chains = [chains[k] + chains[k + 1] for k in range(0, len(chains), 2)]
        return jnp.sum(chains[0], axis=0, keepdims=True)

    def search_with_tiles(tiles):
        most_below = float(tiles * KEY_TILE - topk)

        def bisect_step(_, st):
            lo, hi, mid, mid_key = st
            up, down = mid * 0.5 + hi * 0.5, lo * 0.5 + mid * 0.5
            up_key, down_key = _order_key(up), _order_key(down)
            take = count_keys_below(mid_key, tiles) <= most_below
            return (jnp.where(take, mid, lo), jnp.where(take, hi, mid),
                    jnp.where(take, up, down), jnp.where(take, up_key, down_key))

        return lambda st: lax.fori_loop(0, BISECT_ROUNDS, bisect_step, st)

    above_max = s_max + jnp.maximum(jnp.abs(s_max) * 1e-6, 1e-30)
    first_mid = s_min * 0.5 + above_max * 0.5
    lo = lax.switch(n_tiles - 1, [search_with_tiles(n) for n in range(1, sc_scr.shape[0] + 1)],
                    (s_min, above_max, first_mid, _order_key(first_mid)))[0]
    few = key_limit < topk
    thr = jnp.where(few, jnp.finfo(F32).min, _order_key_to_f32(_order_key(lo)))
    cnt = jnp.where(few, topk, count_ge(thr))
    thr_scr[...] = thr

    @pl.when(jnp.max(jnp.abs(cnt - topk)) > 0)
    def _resolve():
        def max_below(thr):
            def tile(t, m):
                sc = sc_scr[t]
                return jnp.maximum(m, jnp.max(jnp.where(sc < thr, sc, -jnp.inf), axis=0, keepdims=True))
            return lax.fori_loop(0, n_tiles, tile, jnp.full((1, Q_BLOCK), -jnp.inf, F32))

        def lower(st):
            thr, cnt = st
            nxt = max_below(thr)
            short = cnt < topk
            return jnp.where(short, nxt, thr), jnp.where(short, count_ge(nxt), cnt)

        def min_above(thr):
            def tile(t, m):
                sc = sc_scr[t]
                return jnp.minimum(m, jnp.min(jnp.where(sc > thr, sc, jnp.inf), axis=0, keepdims=True))
            return lax.fori_loop(0, n_tiles, tile, jnp.full((1, Q_BLOCK), jnp.inf, F32))

        def count_eq_below(thr, pos_limit):
            def tile(t, c):
                r0 = t * KEY_TILE
                hit = (sc_scr[t] == thr) & (r0 + row < pos_limit)
                return c + jnp.sum(hit.astype(I32), axis=0, keepdims=True)
            return lax.fori_loop(0, n_tiles, tile, jnp.zeros((1, Q_BLOCK), I32))

        def body(st):
            thr, cnt = st
            nxt = min_above(thr)
            cnt_n = count_ge(nxt)
            active = cnt > topk
            advance = active & (cnt_n >= topk)
            tie = active & (cnt_n < topk)
            @pl.when(jnp.max(tie.astype(I32)) > 0)
            def _drop_surplus_ties():
                need = topk - cnt_n
                pos = jnp.zeros((1, Q_BLOCK), I32)
                for b in range(n_pos_bits - 1, -1, -1):
                    cand = pos + (1 << b)
                    pos = jnp.where(count_eq_below(thr, cand) < need, cand, pos)

                def drop_tile(t, carry):
                    r0 = t * KEY_TILE
                    sc = sc_scr[t]
                    sc_scr[t] = jnp.where(tie & (sc == thr) & (r0 + row > pos), -jnp.inf, sc)
                    return carry

                lax.fori_loop(0, n_tiles, drop_tile, 0)

            return jnp.where(advance, nxt, thr), jnp.where(advance, cnt_n, jnp.where(tie, topk, cnt))

        st = lax.while_loop(lambda st: jnp.min(st[1]) < topk, lower, (thr, cnt))
        thr_scr[...] = lax.while_loop(lambda st: jnp.max(st[1]) > topk, body, st)[0]

    thr = thr_scr[...]

    zero = jnp.zeros((DSA_HD, Q_BLOCK), BF16)
    q_pairs = []
    for p in range(DSA_HEADS // 2):
        a = qT_ref[(2 * p) * DSA_HD:(2 * p + 1) * DSA_HD, :]
        b = qT_ref[(2 * p + 1) * DSA_HD:(2 * p + 2) * DSA_HD, :]
        q_pairs.append(jnp.concatenate([jnp.concatenate([a, zero], axis=1),
                                        jnp.concatenate([zero, b], axis=1),
                                        slope_ref[p]], axis=0))
    acc_scr[...] = jnp.zeros(acc_scr.shape, F32)
    qposf = qpos.astype(F32)

    def qk_products(t):
        r0 = pl.multiple_of(t * KEY_TILE, KEY_TILE)
        kt = kn_ref[pl.ds(r0, KEY_TILE), :]
        kp = kpos_ref[pl.ds(r0, KEY_TILE), :]
        s2 = [_dot(jnp.concatenate([kt[:, p * 2 * DSA_HD:(p + 1) * 2 * DSA_HD], kp], axis=1), q_pairs[p])
              for p in range(DSA_HEADS // 2)]
        return t, sc_scr[t] >= thr, s2

    def softmax_pv(tile, ml, last_tile):
        t, sel, s2 = tile
        vt = vT_ref[t]
        if last_tile:
            after = jnp.maximum((t * KEY_TILE + row).astype(F32) - qposf, 0.0)
        m_all, l_all = ml
        m_rows, l_rows = [], []
        for h in range(DSA_HEADS):
            s = s2[h // 2][:, (h % 2) * Q_BLOCK:(h % 2 + 1) * Q_BLOCK]
            if last_tile:
                s = s - (2.0 * _alibi_slope(h)) * after
            s = jnp.where(sel, s, NEG_BIG)
            m_old = m_all[h:h + 1, :]
            m_new = jnp.maximum(m_old, jnp.max(s, axis=0, keepdims=True))
            alpha = jnp.exp(m_old - m_new)
            pr = jnp.exp(s - m_new)
            l_rows.append(alpha * l_all[h:h + 1, :] + jnp.sum(pr, axis=0, keepdims=True))
            m_rows.append(m_new)
            hs = slice(h * DSA_HD, (h + 1) * DSA_HD)
            acc_scr[hs, :] = alpha * acc_scr[hs, :] + _dot(vt[hs, :], pr.astype(BF16))
        return jnp.concatenate(m_rows, axis=0), jnp.concatenate(l_rows, axis=0)

    def attend_run(first, count, ml, ends_block):
        tiles = [qk_products(first + i) for i in range(count)]
        for i, tile in enumerate(tiles):
            ml = softmax_pv(tile, ml, ends_block and i == count - 1)
        return ml

    ml = (jnp.full((DSA_HEADS, Q_BLOCK), NEG_BIG, F32), jnp.zeros((DSA_HEADS, Q_BLOCK), F32))
    full_runs = (n_tiles - 1) // ATTEND_RUN
    ml = lax.fori_loop(0, full_runs, lambda u, ml: attend_run(u * ATTEND_RUN, ATTEND_RUN, ml, False), ml)
    rest = n_tiles - full_runs * ATTEND_RUN
    _, l_all = lax.switch(rest - 1, [(lambda ml, c=c: attend_run(n_tiles - c, c, ml, True))
                                     for c in range(1, ATTEND_RUN + 1)], ml)

    outs = [acc_scr[h * DSA_HD:(h + 1) * DSA_HD, :] / l_all[h:h + 1, :] for h in range(DSA_HEADS)]
    o_ref[...] = jnp.concatenate(outs, axis=0).T.astype(o_ref.dtype)


def _dsa(qT, qiT, wT, kn, vT, rs, batch, seq):
    n = kn.shape[0]
    blocks = seq // Q_BLOCK
    tiles = seq // KEY_TILE
    topk = min(TOPK_MAX, seq // 4)
    col_map = lambda b, j: (0, b * blocks + j)
    pos = jnp.arange(seq, dtype=I32)
    digit = 1 << KPOS_DIGIT_BITS
    kpos = jnp.zeros((seq, LANES), F32).at[:, 0].set((pos // digit).astype(F32)).at[:, 1].set((pos % digit).astype(F32))
    slopes = jnp.asarray([_alibi_slope(h) for h in range(DSA_HEADS)], F32).reshape(DSA_HEADS // 2, 2)
    slope_cols = jnp.repeat(slopes, Q_BLOCK, axis=1)
    slope_rows = jnp.zeros((DSA_HEADS // 2, LANES, 2 * Q_BLOCK), F32)
    slope_rows = slope_rows.at[:, 0, :].set(float(digit) * slope_cols).at[:, 1, :].set(slope_cols)
    return pl.pallas_call(
        functools.partial(_dsa_kernel, topk=topk, n_pos_bits=seq.bit_length()),
        grid=(batch, blocks),
        in_specs=[
            pl.BlockSpec((DSA_DIM, Q_BLOCK), col_map),
            pl.BlockSpec((IDX_Q_DIM, Q_BLOCK), col_map),
            pl.BlockSpec((IDX_HEADS, Q_BLOCK), col_map),
            pl.BlockSpec((seq, DSA_DIM), lambda b, j: (b, 0)),
            pl.BlockSpec((tiles, DSA_DIM, KEY_TILE), lambda b, j: (b, 0, 0)),
            pl.BlockSpec((seq, LANES), lambda b, j: (b, 0)),
            _const_spec((seq, LANES)),
            _const_spec((DSA_HEADS // 2, LANES, 2 * Q_BLOCK)),
        ],
        out_specs=pl.BlockSpec((Q_BLOCK, DSA_DIM), lambda b, j: (b * blocks + j, 0)),
        out_shape=jax.ShapeDtypeStruct((n, DSA_DIM), BF16),
        scratch_shapes=[
            pltpu.VMEM((seq, IDX_HD), BF16),
            pltpu.VMEM((tiles, KEY_TILE, Q_BLOCK), F32),
            pltpu.VMEM((tiles, KEY_TILE, Q_BLOCK), I32),
            pltpu.VMEM((1, Q_BLOCK), F32),
            pltpu.VMEM((SUBLANES, Q_BLOCK), F32),
            pltpu.VMEM((DSA_DIM, Q_BLOCK), F32),
        ],
        compiler_params=pltpu.CompilerParams(dimension_semantics=("arbitrary", "arbitrary"), vmem_limit_bytes=VMEM_LIMIT),
        name="dsa",
    )(qT, qiT, wT, kn, vT, rs, kpos.astype(BF16), slope_rows.astype(BF16))


def _memkv_kernel(mem_ref, g_ref, w_ref, bd_ref, gk_ref, k_ref, v_ref):
    h = _rms_rows(mem_ref[...], g_ref[...]).astype(BF16)
    kv = _dot(h, w_ref[...])
    k = kv[:, :XA_DIM]
    kms = _dot((k * k).astype(BF16), bd_ref[...]) * (1.0 / XA_HD)
    k_ref[...] = (k * lax.rsqrt(kms + EPS) * gk_ref[...]).astype(BF16)
    v_ref[...] = kv[:, XA_DIM:].astype(BF16)


def _memkv(mem2, g_mem, w_xkv, g_xk):
    n, d = mem2.shape
    tm = min(ROW_TILE, n)
    return pl.pallas_call(
        _memkv_kernel,
        grid=(n // tm,),
        in_specs=[
            pl.BlockSpec((tm, d), lambda i: (i, 0)),
            _const_spec((1, d)),
            _const_spec((d, 2 * XA_DIM)),
            _const_spec((XA_DIM, XA_DIM)),
            _const_spec((1, XA_DIM)),
        ],
        out_specs=(pl.BlockSpec((tm, XA_DIM), lambda i: (i, 0)), pl.BlockSpec((tm, XA_DIM), lambda i: (i, 0))),
        out_shape=(jax.ShapeDtypeStruct((n, XA_DIM), BF16), jax.ShapeDtypeStruct((n, XA_DIM), BF16)),
        compiler_params=pltpu.CompilerParams(dimension_semantics=("arbitrary",), vmem_limit_bytes=VMEM_LIMIT),
        name="memkv",
    )(mem2, g_mem[None, :], w_xkv.astype(BF16), _block_diag_ones(XA_DIM, XA_HD), jnp.tile(g_xk, XA_HEADS)[None, :])


def _tail_kernel(x_ref, ya_ref, yb_ref, wout_ref, gx_ref, wxq_ref, bd_ref, gxq_ref, km_ref, vm_ref, wxo_ref,
                 gf_ref, wgu_ref, wd_ref, o_ref, *, ff_bounds):
    tm = x_ref.shape[0]
    part = tm // ROW_PARTS
    groups = [slice(g * part, (g + 1) * part) for g in range(ROW_PARTS)]
    d_ff = wd_ref.shape[0]

    x1 = [x_ref[r, :] + _dot(jnp.concatenate([ya_ref[r, :], yb_ref[r, :]], axis=1), wout_ref[...]) for r in groups]

    qn = []
    for g in range(ROW_PARTS):
        h = _rms_rows(x1[g], gx_ref[...]).astype(BF16)
        q = _dot(h, wxq_ref[...])
        qms = _dot((q * q).astype(BF16), bd_ref[...]) * (1.0 / XA_HD)
        qn.append((q * lax.rsqrt(qms + EPS) * gxq_ref[...]).astype(BF16))

    attn = []
    for g in range(ROW_PARTS):
        heads = []
        for hd in range(XA_HEADS):
            hs = slice(hd * XA_HD, (hd + 1) * XA_HD)
            s = _dot_nt(qn[g][:, hs], km_ref[0, :, hs]) * (XA_HD ** -0.5)
            pr = jnp.exp(s - jnp.max(s, axis=-1, keepdims=True))
            o = _dot(pr.astype(BF16), vm_ref[0, :, hs]) / jnp.sum(pr, axis=-1, keepdims=True)
            heads.append(o.astype(BF16))
        attn.append(jnp.concatenate(heads, axis=1))

    x2 = [x1[g] + _dot(attn[g], wxo_ref[...]) for g in range(ROW_PARTS)]
    h = [_rms_rows(x2[g], gf_ref[...]).astype(BF16) for g in range(ROW_PARTS)]
    acc = x2
    for c0, c1 in zip(ff_bounds[:-1], ff_bounds[1:]):
        for g in range(ROW_PARTS):
            gate = _dot(h[g], wgu_ref[:, c0:c1])
            up = _dot(h[g], wgu_ref[:, d_ff + c0:d_ff + c1])
            a = (_silu(gate) * up).astype(BF16)
            acc[g] = acc[g] + _dot(a, wd_ref[c0:c1, :])
    for g, r in enumerate(groups):
        o_ref[r, :] = acc[g]


def _tail(x2, ya, yb, w_out, g_xattn, w_xq, g_xq, km, vm, w_xo, g_ffn, w_gu, w_down, seq):
    n, d = x2.shape
    tm = ROW_TILE
    d_ff = w_down.shape[0]
    assert d_ff % MXU_WIDTH == 0
    ff_bounds = (0, (d_ff // MXU_WIDTH + 1) // 2 * MXU_WIDTH, d_ff)
    n_mem = km.shape[1]
    per_seq = seq // tm
    return pl.pallas_call(
        functools.partial(_tail_kernel, ff_bounds=ff_bounds),
        grid=(n // tm,),
        in_specs=[
            pl.BlockSpec((tm, d), lambda i: (i, 0)),
            pl.BlockSpec((tm, GDN_VAL_DIM), lambda i: (i, 0)),
            pl.BlockSpec((tm, DSA_DIM), lambda i: (i, 0)),
            _const_spec((GDN_VAL_DIM + DSA_DIM, d)),
            _const_spec((1, d)),
            _const_spec((d, XA_DIM)),
            _const_spec((XA_DIM, XA_DIM)),
            _const_spec((1, XA_DIM)),
            pl.BlockSpec((1, n_mem, XA_DIM), lambda i: (i // per_seq, 0, 0)),
            pl.BlockSpec((1, n_mem, XA_DIM), lambda i: (i // per_seq, 0, 0)),
            _const_spec((XA_DIM, d)),
            _const_spec((1, d)),
            _const_spec((d, 2 * d_ff)),
            _const_spec((d_ff, d)),
        ],
        out_specs=pl.BlockSpec((tm, d), lambda i: (i, 0)),
        out_shape=jax.ShapeDtypeStruct((n, d), F32),
        compiler_params=pltpu.CompilerParams(dimension_semantics=("arbitrary",), vmem_limit_bytes=VMEM_LIMIT),
        name="tail",
    )(x2, ya, yb, w_out.astype(BF16), g_xattn[None, :], w_xq.astype(BF16), _block_diag_ones(XA_DIM, XA_HD),
      jnp.tile(g_xq, XA_HEADS)[None, :], km, vm, w_xo.astype(BF16), g_ffn[None, :], w_gu.astype(BF16),
      w_down.astype(BF16))


def kernel(x, mem, g_mix, w_in, conv_w, a_log, dt_bias, g_gdn_out, g_q_dsa, g_k_dsa, w_out, g_xattn, g_mem, w_xq,
           w_xkv, g_xq, g_xk, w_xo, g_ffn, w_gu, w_down):
    batch, seq, d = x.shape
    n_mem = mem.shape[1]
    assert seq % ROW_TILE == 0 and seq % KEY_TILE == 0 and ROW_TILE % KEY_TILE == 0
    for l in range(g_mix.shape[0]):
        x2 = x.reshape(batch * seq, d)
        qkv, z, kn, rs, qT, vT, qiT, wT = _inproj(x2, g_mix[l], w_in[l], g_q_dsa[l], g_k_dsa[l], conv_w[l], seq)
        ya = _gdn(qkv, z, rs, a_log[l], dt_bias[l], g_gdn_out[l], batch, seq)
        yb = _dsa(qT, qiT, wT, kn, vT, rs, batch, seq)
        km, vm = _memkv(mem.reshape(batch * n_mem, d), g_mem[l], w_xkv[l], g_xk[l])
        km = km.reshape(batch, n_mem, XA_DIM)
        vm = vm.reshape(batch, n_mem, XA_DIM)
        x = _tail(x2, ya, yb, w_out[l], g_xattn[l], w_xq[l], g_xq[l], km, vm, w_xo[l], g_ffn[l], w_gu[l],
                  w_down[l], seq).reshape(batch, seq, d)
    return x
```

```python
import functools

import jax
import jax.numpy as jnp
from jax import lax
from jax.experimental import pallas as pl
from jax.experimental.pallas import tpu as pltpu

F32 = jnp.float32
BF16 = jnp.bfloat16
I32 = jnp.int32

EPS = 1e-6
CHUNK = 64
CHUNK_SHIFT = CHUNK.bit_length() - 1
assert CHUNK == 1 << CHUNK_SHIFT
GDN_HEADS, GDN_DK, GDN_DV, CONV_K = 4, 128, 128, 4
DSA_HEADS, DSA_HD = 8, 64
IDX_HEADS, IDX_HD = 16, 64
TOPK_MAX = 256
XA_HEADS, XA_HD = 4, 128

GDN_KEY_DIM = GDN_HEADS * GDN_DK
GDN_VAL_DIM = GDN_HEADS * GDN_DV
GDN_CONV_CH = 2 * GDN_KEY_DIM + GDN_VAL_DIM
DSA_DIM = DSA_HEADS * DSA_HD
IDX_Q_DIM = IDX_HEADS * IDX_HD
XA_DIM = XA_HEADS * XA_HD

LANES = 128
SUBLANES = 8
MXU_WIDTH = 256
VMEM_LIMIT = 56 * 1024 * 1024

RS_IDXK = 0
RS_BETA = IDX_HD
RS_A = IDX_HD + GDN_HEADS

ROW_TILE = 512
ROW_PARTS = 2
KEY_TILE = 256
Q_BLOCK = 128
GDN_CHUNKS_PER_STEP = 4
GDN_SEQS_PER_STEP = 4

BISECT_ROUNDS = 20
COARSE_ROUNDS = 14
COARSE_SHIFT = 8
COUNT_CHAINS = 4
ATTEND_RUN = 4
KPOS_DIGIT_BITS = 4
NEG_BIG = -1e30


def _dot(a, b):
    return jnp.dot(a, b, preferred_element_type=F32)


def _dot_nt(a, b):
    return lax.dot_general(a, b, (((1,), (1,)), ((), ())), preferred_element_type=F32)


def _dot_f32(a, b):
    return jnp.dot(a, b, preferred_element_type=F32, precision=lax.Precision.HIGHEST)


def _silu(x):
    half = 0.5 * x
    return half + half * jnp.tanh(half)


def _rms_rows(x, g):
    ms = jnp.mean(x * x, axis=-1, keepdims=True)
    return x * lax.rsqrt(ms + EPS) * g


def _const_spec(shape):
    nd = len(shape)
    return pl.BlockSpec(shape, lambda *_: (0,) * nd, pipeline_mode=pl.Buffered(1))


def _block_diag_ones(n, blk):
    r = lax.broadcasted_iota(I32, (n, n), 0) // blk
    c = lax.broadcasted_iota(I32, (n, n), 1) // blk
    return (r == c).astype(BF16)


def _inproj_kernel(x_ref, g_ref, wrow_ref, wcol_ref, bd_ref, gk_ref, gq_ref, cw_ref,
                   qkv_ref, z_ref, kn_ref, rs_ref, qT_ref, vT_ref, qiT_ref, wT_ref, xc_ref, *, tiles_per_seq):
    tm = x_ref.shape[0]

    @pl.when(pl.program_id(0) % tiles_per_seq == 0)
    def _start_of_sequence():
        xc_ref[0:SUBLANES, :] = jnp.zeros((SUBLANES, GDN_CONV_CH), F32)

    h = _rms_rows(x_ref[...], g_ref[...]).astype(BF16)
    c0, c1, c2, c3 = GDN_CONV_CH, GDN_CONV_CH + GDN_VAL_DIM, GDN_CONV_CH + GDN_VAL_DIM + DSA_DIM, \
        GDN_CONV_CH + GDN_VAL_DIM + DSA_DIM + LANES

    xc_ref[SUBLANES:SUBLANES + tm, :] = _dot(h, wrow_ref[:, 0:c0])

    zkr = _dot(h, wrow_ref[:, c0:c3])
    z_ref[...] = zkr[:, 0:c1 - c0]
    k = zkr[:, c1 - c0:c2 - c0]
    kms = _dot((k * k).astype(BF16), bd_ref[...]) * (1.0 / DSA_HD)
    kn_ref[...] = (k * lax.rsqrt(kms + EPS) * gk_ref[...]).astype(BF16)
    rs_ref[...] = zkr[:, c2 - c0:c3 - c0]

    r0, r1, r2, r3 = DSA_DIM, 2 * DSA_DIM, 2 * DSA_DIM + IDX_Q_DIM, 2 * DSA_DIM + IDX_Q_DIM + IDX_HEADS
    qvT = _dot_nt(wcol_ref[0:r1, :], h)
    qT = qvT[0:r0, :]
    qms = _dot(bd_ref[...], (qT * qT).astype(BF16)) * (1.0 / DSA_HD)
    qT_ref[...] = (qT * lax.rsqrt(qms + EPS) * gq_ref[...] * (DSA_HD ** -0.5)).astype(BF16)
    vT = qvT[r0:r1, :].astype(BF16)
    for i in range(vT_ref.shape[0]):
        vT_ref[i] = vT[:, i * KEY_TILE:(i + 1) * KEY_TILE]
    qwT = _dot_nt(wcol_ref[r1:r3, :], h)
    qiT_ref[...] = (qwT[0:IDX_Q_DIM, :] * (IDX_HD ** -0.5)).astype(BF16)
    wT_ref[...] = qwT[IDX_Q_DIM:, :] * (IDX_HEADS ** -0.5)

    cw = cw_ref[...]
    conv = cw[CONV_K - 1:CONV_K, :] * xc_ref[SUBLANES:SUBLANES + tm, :]
    for j in range(CONV_K - 2, -1, -1):
        back = CONV_K - 1 - j
        conv = conv + cw[j:j + 1, :] * xc_ref[SUBLANES - back:SUBLANES - back + tm, :]
    xc_ref[0:SUBLANES, :] = xc_ref[tm:tm + SUBLANES, :]
    act = _silu(conv)
    for hd in range(GDN_HEADS):
        qs = slice(hd * GDN_DK, (hd + 1) * GDN_DK)
        ks = slice(GDN_KEY_DIM + hd * GDN_DK, GDN_KEY_DIM + (hd + 1) * GDN_DK)
        q, k = act[:, qs], act[:, ks]
        qkv_ref[:, qs] = q * lax.rsqrt(jnp.sum(q * q, axis=-1, keepdims=True) + EPS) * (GDN_DK ** -0.5)
        qkv_ref[:, ks] = k * lax.rsqrt(jnp.sum(k * k, axis=-1, keepdims=True) + EPS)
    qkv_ref[:, 2 * GDN_KEY_DIM:] = act[:, 2 * GDN_KEY_DIM:]


def _inproj(x2, g_mix, w_in, g_q, g_k, conv_w, seq):
    n, d = x2.shape
    tm = ROW_TILE
    sizes = (GDN_CONV_CH, GDN_VAL_DIM, GDN_HEADS, GDN_HEADS, DSA_DIM, DSA_DIM, DSA_DIM, IDX_Q_DIM, IDX_HD, IDX_HEADS)
    offs = [0]
    for s in sizes:
        offs.append(offs[-1] + s)
    (w_qkv, w_z, w_b, w_a, w_q, w_k, w_v, w_iq, w_ik, w_iw) = [w_in[:, offs[i]:offs[i + 1]] for i in range(len(sizes))]
    pad = jnp.zeros((d, LANES - IDX_HD - 2 * GDN_HEADS), w_in.dtype)
    w_row = jnp.concatenate([w_qkv, w_z, w_k, w_ik, w_b, w_a, pad], axis=1).astype(BF16)
    w_col = jnp.concatenate([w_q, w_v, w_iq, w_iw], axis=1).T.astype(BF16)
    bd = _block_diag_ones(DSA_DIM, DSA_HD)
    gk_row = jnp.tile(g_k, DSA_HEADS)[None, :]
    gq_col = jnp.tile(g_q, DSA_HEADS)[:, None]
    nrow, ncol = w_row.shape[1], w_col.shape[0]
    out_shape = (
        jax.ShapeDtypeStruct((n, GDN_CONV_CH), F32),
        jax.ShapeDtypeStruct((n, GDN_VAL_DIM), F32),
        jax.ShapeDtypeStruct((n, DSA_DIM), BF16),
        jax.ShapeDtypeStruct((n, LANES), F32),
        jax.ShapeDtypeStruct((DSA_DIM, n), BF16),
        jax.ShapeDtypeStruct((n // KEY_TILE, DSA_DIM, KEY_TILE), BF16),
        jax.ShapeDtypeStruct((IDX_Q_DIM, n), BF16),
        jax.ShapeDtypeStruct((IDX_HEADS, n), F32),
    )
    return pl.pallas_call(
        functools.partial(_inproj_kernel, tiles_per_seq=seq // tm),
        grid=(n // tm,),
        in_specs=[
            pl.BlockSpec((tm, d), lambda i: (i, 0)),
            _const_spec((1, d)),
            _const_spec((d, nrow)),
            _const_spec((ncol, d)),
            _const_spec((DSA_DIM, DSA_DIM)),
            _const_spec((1, DSA_DIM)),
            _const_spec((DSA_DIM, 1)),
            _const_spec((CONV_K, GDN_CONV_CH)),
        ],
        out_specs=(
            pl.BlockSpec((tm, GDN_CONV_CH), lambda i: (i, 0)),
            pl.BlockSpec((tm, GDN_VAL_DIM), lambda i: (i, 0)),
            pl.BlockSpec((tm, DSA_DIM), lambda i: (i, 0)),
            pl.BlockSpec((tm, LANES), lambda i: (i, 0)),
            pl.BlockSpec((DSA_DIM, tm), lambda i: (0, i)),
            pl.BlockSpec((tm // KEY_TILE, DSA_DIM, KEY_TILE), lambda i: (i, 0, 0)),
            pl.BlockSpec((IDX_Q_DIM, tm), lambda i: (0, i)),
            pl.BlockSpec((IDX_HEADS, tm), lambda i: (0, i)),
        ),
        out_shape=out_shape,
        scratch_shapes=[pltpu.VMEM((tm + SUBLANES, GDN_CONV_CH), F32)],
        compiler_params=pltpu.CompilerParams(dimension_semantics=("arbitrary",), vmem_limit_bytes=VMEM_LIMIT),
        name="inproj",
    )(x2, g_mix[None, :], w_row, w_col, bd, gk_row, gq_col, conv_w)


def _gdn_kernel(qkv_ref, z_ref, rs_ref, alog_ref, dtb_ref, gout_ref, y_ref, s_ref):
    nb, rows = qkv_ref.shape[0], qkv_ref.shape[1]
    nc = rows // CHUNK

    @pl.when(pl.program_id(1) == 0)
    def _start_of_sequence():
        s_ref[...] = jnp.zeros(s_ref.shape, F32)

    rs = rs_ref[...]
    beta_all = jax.nn.sigmoid(rs)
    sp_in = rs + dtb_ref[...]
    softplus = jnp.maximum(sp_in, 0.0) + jnp.log(1.0 + jnp.exp(-jnp.abs(sp_in)))
    g_all = -jnp.exp(alog_ref[...]) * softplus

    ri = lax.broadcasted_iota(I32, (CHUNK, CHUNK), 0)
    ci = lax.broadcasted_iota(I32, (CHUNK, CHUNK), 1)
    incl = ri >= ci
    strict = ri > ci
    ltri = incl.astype(F32)
    eye = (ri == ci).astype(F32)
    gout = gout_ref[...]

    chains = [(b, c, h) for c in range(nc) for b in range(nb) for h in range(GDN_HEADS)]
    dcum, dcum_t = {}, {}
    for c in range(nc):
        for b in range(nb):
            d = _dot_f32(ltri, g_all[b, c * CHUNK:(c + 1) * CHUNK, :])
            dcum[b, c] = d
            dcum_t[b, c] = d.T

    q16, k16, kb16, vb16, kw16, qdec16, kdect16, gamma, last = ([] for _ in range(9))
    for b, c, h in chains:
        r = slice(c * CHUNK, (c + 1) * CHUNK)
        q = qkv_ref[b, r, h * GDN_DK:(h + 1) * GDN_DK]
        k = qkv_ref[b, r, GDN_KEY_DIM + h * GDN_DK:GDN_KEY_DIM + (h + 1) * GDN_DK]
        v = qkv_ref[b, r, 2 * GDN_KEY_DIM + h * GDN_DV:2 * GDN_KEY_DIM + (h + 1) * GDN_DV]
        beta = beta_all[b, r, RS_BETA + h:RS_BETA + h + 1]
        d_col = dcum[b, c][:, RS_A + h:RS_A + h + 1]
        d_row = dcum_t[b, c][RS_A + h:RS_A + h + 1, :]
        d_last = dcum[b, c][CHUNK - 1:CHUNK, RS_A + h:RS_A + h + 1]
        e_col = jnp.exp(d_col)
        kb = k * beta
        gamma.append(jnp.exp(jnp.where(incl, d_col - d_row, -jnp.inf)))
        q16.append(q.astype(BF16))
        k16.append(k.astype(BF16))
        kb16.append(kb.astype(BF16))
        vb16.append((v * beta).astype(BF16))
        kw16.append((kb * e_col).astype(BF16))
        qdec16.append((q * e_col).astype(BF16))
        kdect16.append((k * jnp.exp(d_last - d_col)).T.astype(BF16))
        last.append(jnp.exp(d_last))

    n = len(chains)
    p16 = [(-jnp.where(strict, _dot_nt(kb16[i], k16[i]) * gamma[i], 0.0)).astype(BF16) for i in range(n)]
    qk16 = [(_dot_nt(q16[i], k16[i]) * gamma[i]).astype(BF16) for i in range(n)]
    t_mat = [eye + p16[i].astype(F32) for i in range(n)]
    p16 = [_dot(p16[i], p16[i]).astype(BF16) for i in range(n)]
    for _ in range(4):
        t_mat = [t_mat[i] + _dot(t_mat[i].astype(BF16), p16[i]) for i in range(n)]
        p16 = [_dot(p16[i], p16[i]).astype(BF16) for i in range(n)]
    t16 = [(t_mat[i] + _dot(t_mat[i].astype(BF16), p16[i])).astype(BF16) for i in range(n)]
    u = [_dot(t16[i], vb16[i]) for i in range(n)]
    w16 = [_dot(t16[i], kw16[i]).astype(BF16) for i in range(n)]

    per_chunk = nb * GDN_HEADS
    s = [s_ref[b, h] for b in range(nb) for h in range(GDN_HEADS)]
    for c in range(nc):
        ids = range(c * per_chunk, (c + 1) * per_chunk)
        s16 = [s[j].astype(BF16) for j in range(per_chunk)]
        v_new16 = [(u[i] - _dot(w16[i], s16[j])).astype(BF16) for j, i in enumerate(ids)]
        o_state = [_dot(qdec16[i], s16[j]) for j, i in enumerate(ids)]
        o = [o_state[j] + _dot(qk16[i], v_new16[j]) for j, i in enumerate(ids)]
        s = [s[j] * last[i] + _dot(kdect16[i], v_new16[j]) for j, i in enumerate(ids)]
        for j, i in enumerate(ids):
            b, _, h = chains[i]
            zc = z_ref[b, c * CHUNK:(c + 1) * CHUNK, h * GDN_DV:(h + 1) * GDN_DV]
            y = _rms_rows(o[j], gout) * _silu(zc)
            y_ref[b, c * CHUNK:(c + 1) * CHUNK, h * GDN_DV:(h + 1) * GDN_DV] = y.astype(y_ref.dtype)
    for j in range(per_chunk):
        s_ref[j // GDN_HEADS, j % GDN_HEADS] = s[j]


def _gdn(qkv, z, rs, a_log, dt_bias, g_out, batch, seq):
    n = qkv.shape[0]
    rows = GDN_CHUNKS_PER_STEP * CHUNK
    nb = GDN_SEQS_PER_STEP if batch % GDN_SEQS_PER_STEP == 0 else 1
    lane_vec = lambda v: jnp.zeros((1, LANES), F32).at[0, RS_A:RS_A + GDN_HEADS].set(v)
    per_seq = lambda a: a.reshape(batch, seq, a.shape[-1])
    block = lambda width: pl.BlockSpec((nb, rows, width), lambda b, i: (b, i, 0))
    y = pl.pallas_call(
        _gdn_kernel,
        grid=(batch // nb, seq // rows),
        in_specs=[
            block(GDN_CONV_CH),
            block(GDN_VAL_DIM),
            block(LANES),
            _const_spec((1, LANES)),
            _const_spec((1, LANES)),
            _const_spec((1, GDN_DV)),
        ],
        out_specs=block(GDN_VAL_DIM),
        out_shape=jax.ShapeDtypeStruct((batch, seq, GDN_VAL_DIM), BF16),
        scratch_shapes=[
            pltpu.VMEM((nb, GDN_HEADS, GDN_DK, GDN_DV), F32),
        ],
        compiler_params=pltpu.CompilerParams(dimension_semantics=("arbitrary", "arbitrary"), vmem_limit_bytes=VMEM_LIMIT),
        name="gdn",
    )(per_seq(qkv), per_seq(z), per_seq(rs), lane_vec(a_log), lane_vec(dt_bias), g_out[None, :])
    return y.reshape(n, GDN_VAL_DIM)


def _alibi_slope(h):
    return 2.0 ** (-8.0 * (h + 1) / DSA_HEADS)


def _order_key(x):
    bits = lax.bitcast_convert_type(x, I32)
    return (bits ^ ((bits >> 31) & jnp.int32(0x7FFFFFFF))) >> 1


def _order_key_to_f32(key):
    full = key << 1
    return lax.bitcast_convert_type(full ^ ((full >> 31) & jnp.int32(0x7FFFFFFF)), F32)


def _tiles_for_block(j):
    return (j * Q_BLOCK + Q_BLOCK + KEY_TILE - 1) // KEY_TILE


def _dsa_kernel(qT_ref, qiT_ref, wT_ref, kn_ref, vT_ref, rs_ref, kpos_ref, slope_ref, o_ref,
                idxk_scr, sc_scr, key_scr, coarse_scr, thr_scr, cnt_scr, acc_scr, *, topk, n_pos_bits):
    j = pl.program_id(1)

    @pl.when(j == 0)
    def _new_sequence():
        idxk_scr[...] = rs_ref[:, RS_IDXK:RS_IDXK + IDX_HD].astype(BF16)

    n_tiles = _tiles_for_block(j)
    lane = lax.broadcasted_iota(I32, (1, Q_BLOCK), 1)
    qpos = j * Q_BLOCK + lane
    key_limit = ((qpos >> CHUNK_SHIFT) + 1) << CHUNK_SHIFT
    row = lax.broadcasted_iota(I32, (KEY_TILE, Q_BLOCK), 0)

    w_t = wT_ref[...]
    qi_pairs = [jnp.concatenate([qiT_ref[(2 * p) * IDX_HD:(2 * p + 1) * IDX_HD, :],
                                 qiT_ref[(2 * p + 1) * IDX_HD:(2 * p + 2) * IDX_HD, :]], axis=1)
                for p in range(IDX_HEADS // 2)]

    def score_tiles(t, count, carry):
        s_max, s_min = carry
        r0 = pl.multiple_of(t * KEY_TILE, KEY_TILE)
        kt = idxk_scr[pl.ds(r0, count * KEY_TILE), :]
        accs = [jnp.zeros((KEY_TILE, Q_BLOCK), F32) for _ in range(count)]
        for p in range(IDX_HEADS // 2):
            d = _dot(kt, qi_pairs[p])
            for i in range(count):
                di = d[i * KEY_TILE:(i + 1) * KEY_TILE, :]
                accs[i] = accs[i] + jnp.maximum(di[:, :Q_BLOCK], 0.0) * w_t[2 * p:2 * p + 1, :]
                accs[i] = accs[i] + jnp.maximum(di[:, Q_BLOCK:], 0.0) * w_t[2 * p + 1:2 * p + 2, :]
        for i in range(count):
            admissible = r0 + i * KEY_TILE + row < key_limit
            stored = jnp.where(admissible, accs[i], -jnp.inf)
            sc_scr[t + i] = stored
            key = _order_key(stored)
            key_scr[t + i] = key
            coarse_scr[t + i] = (key >> COARSE_SHIFT).astype(F32)
            s_max = jnp.maximum(s_max, jnp.max(stored, axis=0, keepdims=True))
            s_min = jnp.minimum(s_min, jnp.min(jnp.where(admissible, accs[i], jnp.inf), axis=0, keepdims=True))
        return s_max, s_min

    stats = (jnp.full((1, Q_BLOCK), -jnp.inf, F32), jnp.full((1, Q_BLOCK), jnp.inf, F32))
    stats = lax.fori_loop(0, n_tiles // 2, lambda u, st: score_tiles(2 * u, 2, st), stats)
    s_max, s_min = lax.cond(n_tiles % 2 == 1, lambda st: score_tiles(n_tiles - 1, 1, st), lambda st: st, stats)

    def count_tiles(per_tile):
        cnt_scr[...] = jnp.zeros(cnt_scr.shape, F32)
        base = jnp.int32(0)
        arm = 1 << (sc_scr.shape[0].bit_length() - 1)
        while arm >= 1:
            has = (n_tiles & arm) != 0

            @pl.when(has)
            def _run(base=base, arm=arm):
                chains = [jnp.zeros((SUBLANES, Q_BLOCK), F32) for _ in range(COUNT_CHAINS)]
                for i in range(arm):
                    flags = per_tile(base + i)
                    for r in range(KEY_TILE // SUBLANES):
                        chains[r % COUNT_CHAINS] = chains[r % COUNT_CHAINS] + flags[r * SUBLANES:(r + 1) * SUBLANES, :]
                while len(chains) > 1:
                    chains = [chains[k] + chains[k + 1] for k in range(0, len(chains), 2)]
                cnt_scr[...] += chains[0]

            base = base + jnp.where(has, arm, 0)
            arm //= 2
        return jnp.sum(cnt_scr[...], axis=0, keepdims=True).astype(I32)

    def count_ge(thr):
        return count_tiles(lambda i: jnp.where(sc_scr[i] >= thr, 1.0, 0.0))

    def count_keys_below(key, tiles):
        one = jnp.int32(0x3F800000)
        chains = [jnp.zeros((SUBLANES, Q_BLOCK), F32) for _ in range(COUNT_CHAINS)]
        for i in range(tiles):
            below = lax.bitcast_convert_type(((key_scr[i] - key) >> 31) & one, F32)
            for r in range(KEY_TILE // SUBLANES):
                chains[r % COUNT_CHAINS] = chains[r % COUNT_CHAINS] + below[r * SUBLANES:(r + 1) * SUBLANES, :]
        while len(chains) > 1:
            chains = [chains[k] + chains[k + 1] for k in range(0, len(chains), 2)]
        return jnp.sum(chains[0], axis=0, keepdims=True)

    def search_with_tiles(tiles):
        most_below = float(tiles * KEY_TILE - topk)

        def bisect_step(_, st):
            lo, hi, mid, mid_key = st
            up, down = mid * 0.5 + hi * 0.5, lo * 0.5 + mid * 0.5
            up_key, down_key = _order_key(up), _order_key(down)
            take = count_keys_below(mid_key, tiles) <= most_below
            return (jnp.where(take, mid, lo), jnp.where(take, hi, mid),
                    jnp.where(take, up, down), jnp.where(take, up_key, down_key))

        def coarse_step(_, st):
            lo, hi, mid, mid_key = st
            bucket = mid_key >> COARSE_SHIFT
            edge = jnp.minimum(hi, _order_key_to_f32((bucket + 1) << COARSE_SHIFT))
            up, down = mid * 0.5 + hi * 0.5, lo * 0.5 + edge * 0.5
            up_key, down_key = _order_key(up), _order_key(down)
            c = bucket.astype(F32) + 0.5
            chains = [jnp.zeros((SUBLANES, Q_BLOCK), F32) for _ in range(COUNT_CHAINS)]
            for i in range(tiles):
                side = lax.clamp(-0.5, c - coarse_scr[i], 0.5)
                for r in range(KEY_TILE // SUBLANES):
                    chains[r % COUNT_CHAINS] = chains[r % COUNT_CHAINS] + side[r * SUBLANES:(r + 1) * SUBLANES, :]
            while len(chains) > 1:
                chains = [chains[k] + chains[k + 1] for k in range(0, len(chains), 2)]
            below = jnp.sum(chains[0], axis=0, keepdims=True) + 0.5 * (tiles * KEY_TILE)
            take = below <= most_below
            return (jnp.where(take, mid, lo), jnp.where(take, hi, edge),
                    jnp.where(take, up, down), jnp.where(take, up_key, down_key))

        def search(st):
            st = lax.fori_loop(0, COARSE_ROUNDS, coarse_step, st)
            return lax.fori_loop(COARSE_ROUNDS, BISECT_ROUNDS, bisect_step, st)

        return search

    above_max = s_max + jnp.maximum(jnp.abs(s_max) * 1e-6, 1e-30)
    first_mid = s_min * 0.5 + above_max * 0.5
    lo = lax.switch(n_tiles - 1, [search_with_tiles(n) for n in range(1, sc_scr.shape[0] + 1)],
                    (s_min, above_max, first_mid, _order_key(first_mid)))[0]
    few = key_limit < topk
    thr = jnp.where(few, jnp.finfo(F32).min, _order_key_to_f32(_order_key(lo)))
    cnt = jnp.where(few, topk, count_ge(thr))
    thr_scr[...] = thr

    @pl.when(jnp.max(jnp.abs(cnt - topk)) > 0)
    def _resolve():
        def max_below(thr):
            def tile(t, m):
                sc = sc_scr[t]
                return jnp.maximum(m, jnp.max(jnp.where(sc < thr, sc, -jnp.inf), axis=0, keepdims=True))
            return lax.fori_loop(0, n_tiles, tile, jnp.full((1, Q_BLOCK), -jnp.inf, F32))

        def lower(st):
            thr, cnt = st
            nxt = max_below(thr)
            short = cnt < topk
            return jnp.where(short, nxt, thr), jnp.where(short, count_ge(nxt), cnt)

        def min_above(thr):
            def tile(t, m):
                sc = sc_scr[t]
                return jnp.minimum(m, jnp.min(jnp.where(sc > thr, sc, jnp.inf), axis=0, keepdims=True))
            return lax.fori_loop(0, n_tiles, tile, jnp.full((1, Q_BLOCK), jnp.inf, F32))

        def count_eq_below(thr, pos_limit):
            def tile(t, c):
                r0 = t * KEY_TILE
                hit = (sc_scr[t] == thr) & (r0 + row < pos_limit)
                return c + jnp.sum(hit.astype(I32), axis=0, keepdims=True)
            return lax.fori_loop(0, n_tiles, tile, jnp.zeros((1, Q_BLOCK), I32))

        def body(st):
            thr, cnt = st
            nxt = min_above(thr)
            cnt_n = count_ge(nxt)
            active = cnt > topk
            advance = active & (cnt_n >= topk)
            tie = active & (cnt_n < topk)
            @pl.when(jnp.max(tie.astype(I32)) > 0)
            def _drop_surplus_ties():
                need = topk - cnt_n
                pos = jnp.zeros((1, Q_BLOCK), I32)
                for b in range(n_pos_bits - 1, -1, -1):
                    cand = pos + (1 << b)
                    pos = jnp.where(count_eq_below(thr, cand) < need, cand, pos)

                def drop_tile(t, carry):
                    r0 = t * KEY_TILE
                    sc = sc_scr[t]
                    sc_scr[t] = jnp.where(tie & (sc == thr) & (r0 + row > pos), -jnp.inf, sc)
                    return carry

                lax.fori_loop(0, n_tiles, drop_tile, 0)

            return jnp.where(advance, nxt, thr), jnp.where(advance, cnt_n, jnp.where(tie, topk, cnt))

        st = lax.while_loop(lambda st: jnp.min(st[1]) < topk, lower, (thr, cnt))
        thr_scr[...] = lax.while_loop(lambda st: jnp.max(st[1]) > topk, body, st)[0]

    thr = thr_scr[...]

    zero = jnp.zeros((DSA_HD, Q_BLOCK), BF16)
    q_pairs = []
    for p in range(DSA_HEADS // 2):
        a = qT_ref[(2 * p) * DSA_HD:(2 * p + 1) * DSA_HD, :]
        b = qT_ref[(2 * p + 1) * DSA_HD:(2 * p + 2) * DSA_HD, :]
        q_pairs.append(jnp.concatenate([jnp.concatenate([a, zero], axis=1),
                                        jnp.concatenate([zero, b], axis=1),
                                        slope_ref[p]], axis=0))
    acc_scr[...] = jnp.zeros(acc_scr.shape, F32)
    qposf = qpos.astype(F32)

    def qk_products(t):
        r0 = pl.multiple_of(t * KEY_TILE, KEY_TILE)
        kt = kn_ref[pl.ds(r0, KEY_TILE), :]
        kp = kpos_ref[pl.ds(r0, KEY_TILE), :]
        s2 = [_dot(jnp.concatenate([kt[:, p * 2 * DSA_HD:(p + 1) * 2 * DSA_HD], kp], axis=1), q_pairs[p])
              for p in range(DSA_HEADS // 2)]
        return t, sc_scr[t] >= thr, s2

    def softmax_pv(tile, ml, last_tile):
        t, sel, s2 = tile
        vt = vT_ref[t]
        if last_tile:
            after = jnp.maximum((t * KEY_TILE + row).astype(F32) - qposf, 0.0)
        m_all, l_all = ml
        m_rows, l_rows = [], []
        for h in range(DSA_HEADS):
            s = s2[h // 2][:, (h % 2) * Q_BLOCK:(h % 2 + 1) * Q_BLOCK]
            if last_tile:
                s = s - (2.0 * _alibi_slope(h)) * after
            s = jnp.where(sel, s, NEG_BIG)
            m_old = m_all[h:h + 1, :]
            m_new = jnp.maximum(m_old, jnp.max(s, axis=0, keepdims=True))
            alpha = jnp.exp(m_old - m_new)
            pr = jnp.exp(s - m_new)
            l_rows.append(alpha * l_all[h:h + 1, :] + jnp.sum(pr, axis=0, keepdims=True))
            m_rows.append(m_new)
            hs = slice(h * DSA_HD, (h + 1) * DSA_HD)
            acc_scr[hs, :] = alpha * acc_scr[hs, :] + _dot(vt[hs, :], pr.astype(BF16))
        return jnp.concatenate(m_rows, axis=0), jnp.concatenate(l_rows, axis=0)

    def attend_run(first, count, ml, ends_block):
        tiles = [qk_products(first + i) for i in range(count)]
        for i, tile in enumerate(tiles):
            ml = softmax_pv(tile, ml, ends_block and i == count - 1)
        return ml

    ml = (jnp.full((DSA_HEADS, Q_BLOCK), NEG_BIG, F32), jnp.zeros((DSA_HEADS, Q_BLOCK), F32))
    full_runs = (n_tiles - 1) // ATTEND_RUN
    ml = lax.fori_loop(0, full_runs, lambda u, ml: attend_run(u * ATTEND_RUN, ATTEND_RUN, ml, False), ml)
    rest = n_tiles - full_runs * ATTEND_RUN
    _, l_all = lax.switch(rest - 1, [(lambda ml, c=c: attend_run(n_tiles - c, c, ml, True))
                                     for c in range(1, ATTEND_RUN + 1)], ml)

    outs = [acc_scr[h * DSA_HD:(h + 1) * DSA_HD, :] / l_all[h:h + 1, :] for h in range(DSA_HEADS)]
    o_ref[...] = jnp.concatenate(outs, axis=0).T.astype(o_ref.dtype)


def _dsa(qT, qiT, wT, kn, vT, rs, batch, seq):
    n = kn.shape[0]
    blocks = seq // Q_BLOCK
    tiles = seq // KEY_TILE
    topk = min(TOPK_MAX, seq // 4)
    col_map = lambda b, j: (0, b * blocks + j)
    pos = jnp.arange(seq, dtype=I32)
    digit = 1 << KPOS_DIGIT_BITS
    kpos = jnp.zeros((seq, LANES), F32).at[:, 0].set((pos // digit).astype(F32)).at[:, 1].set((pos % digit).astype(F32))
    slopes = jnp.asarray([_alibi_slope(h) for h in range(DSA_HEADS)], F32).reshape(DSA_HEADS // 2, 2)
    slope_cols = jnp.repeat(slopes, Q_BLOCK, axis=1)
    slope_rows = jnp.zeros((DSA_HEADS // 2, LANES, 2 * Q_BLOCK), F32)
    slope_rows = slope_rows.at[:, 0, :].set(float(digit) * slope_cols).at[:, 1, :].set(slope_cols)
    return pl.pallas_call(
        functools.partial(_dsa_kernel, topk=topk, n_pos_bits=seq.bit_length()),
        grid=(batch, blocks),
        in_specs=[
            pl.BlockSpec((DSA_DIM, Q_BLOCK), col_map),
            pl.BlockSpec((IDX_Q_DIM, Q_BLOCK), col_map),
            pl.BlockSpec((IDX_HEADS, Q_BLOCK), col_map),
            pl.BlockSpec((seq, DSA_DIM), lambda b, j: (b, 0)),
            pl.BlockSpec((tiles, DSA_DIM, KEY_TILE), lambda b, j: (b, 0, 0)),
            pl.BlockSpec((seq, LANES), lambda b, j: (b, 0)),
            _const_spec((seq, LANES)),
            _const_spec((DSA_HEADS // 2, LANES, 2 * Q_BLOCK)),
        ],
        out_specs=pl.BlockSpec((Q_BLOCK, DSA_DIM), lambda b, j: (b * blocks + j, 0)),
        out_shape=jax.ShapeDtypeStruct((n, DSA_DIM), BF16),
        scratch_shapes=[
            pltpu.VMEM((seq, IDX_HD), BF16),
            pltpu.VMEM((tiles, KEY_TILE, Q_BLOCK), F32),
            pltpu.VMEM((tiles, KEY_TILE, Q_BLOCK), I32),
            pltpu.VMEM((tiles, KEY_TILE, Q_BLOCK), F32),
            pltpu.VMEM((1, Q_BLOCK), F32),
            pltpu.VMEM((SUBLANES, Q_BLOCK), F32),
            pltpu.VMEM((DSA_DIM, Q_BLOCK), F32),
        ],
        compiler_params=pltpu.CompilerParams(dimension_semantics=("arbitrary", "arbitrary"), vmem_limit_bytes=VMEM_LIMIT),
        name="dsa",
    )(qT, qiT, wT, kn, vT, rs, kpos.astype(BF16), slope_rows.astype(BF16))


def _memkv_kernel(mem_ref, g_ref, w_ref, bd_ref, gk_ref, k_ref, v_ref):
    h = _rms_rows(mem_ref[...], g_ref[...]).astype(BF16)
    kv = _dot(h, w_ref[...])
    k = kv[:, :XA_DIM]
    kms = _dot((k * k).astype(BF16), bd_ref[...]) * (1.0 / XA_HD)
    k_ref[...] = (k * lax.rsqrt(kms + EPS) * gk_ref[...]).astype(BF16)
    v_ref[...] = kv[:, XA_DIM:].astype(BF16)


def _memkv(mem2, g_mem, w_xkv, g_xk):
    n, d = mem2.shape
    tm = min(ROW_TILE, n)
    return pl.pallas_call(
        _memkv_kernel,
        grid=(n // tm,),
        in_specs=[
            pl.BlockSpec((tm, d), lambda i: (i, 0)),
            _const_spec((1, d)),
            _const_spec((d, 2 * XA_DIM)),
            _const_spec((XA_DIM, XA_DIM)),
            _const_spec((1, XA_DIM)),
        ],
        out_specs=(pl.BlockSpec((tm, XA_DIM), lambda i: (i, 0)), pl.BlockSpec((tm, XA_DIM), lambda i: (i, 0))),
        out_shape=(jax.ShapeDtypeStruct((n, XA_DIM), BF16), jax.ShapeDtypeStruct((n, XA_DIM), BF16)),
        compiler_params=pltpu.CompilerParams(dimension_semantics=("arbitrary",), vmem_limit_bytes=VMEM_LIMIT),
        name="memkv",
    )(mem2, g_mem[None, :], w_xkv.astype(BF16), _block_diag_ones(XA_DIM, XA_HD), jnp.tile(g_xk, XA_HEADS)[None, :])


def _tail_kernel(x_ref, ya_ref, yb_ref, wout_ref, gx_ref, wxq_ref, bd_ref, gxq_ref, km_ref, vm_ref, wxo_ref,
                 gf_ref, wgu_ref, wd_ref, o_ref, *, ff_bounds):
    tm = x_ref.shape[0]
    part = tm // ROW_PARTS
    groups = [slice(g * part, (g + 1) * part) for g in range(ROW_PARTS)]
    d_ff = wd_ref.shape[0]

    x1 = [x_ref[r, :] + _dot(jnp.concatenate([ya_ref[r, :], yb_ref[r, :]], axis=1), wout_ref[...]) for r in groups]

    qn = []
    for g in range(ROW_PARTS):
        h = _rms_rows(x1[g], gx_ref[...]).astype(BF16)
        q = _dot(h, wxq_ref[...])
        qms = _dot((q * q).astype(BF16), bd_ref[...]) * (1.0 / XA_HD)
        qn.append((q * lax.rsqrt(qms + EPS) * gxq_ref[...]).astype(BF16))

    attn = []
    for g in range(ROW_PARTS):
        heads = []
        for hd in range(XA_HEADS):
            hs = slice(hd * XA_HD, (hd + 1) * XA_HD)
            s = _dot_nt(qn[g][:, hs], km_ref[0, :, hs]) * (XA_HD ** -0.5)
            pr = jnp.exp(s - jnp.max(s, axis=-1, keepdims=True))
            o = _dot(pr.astype(BF16), vm_ref[0, :, hs]) / jnp.sum(pr, axis=-1, keepdims=True)
            heads.append(o.astype(BF16))
        attn.append(jnp.concatenate(heads, axis=1))

    x2 = [x1[g] + _dot(attn[g], wxo_ref[...]) for g in range(ROW_PARTS)]
    h = [_rms_rows(x2[g], gf_ref[...]).astype(BF16) for g in range(ROW_PARTS)]
    acc = x2
    for c0, c1 in zip(ff_bounds[:-1], ff_bounds[1:]):
        for g in range(ROW_PARTS):
            gate = _dot(h[g], wgu_ref[:, c0:c1])
            up = _dot(h[g], wgu_ref[:, d_ff + c0:d_ff + c1])
            a = (_silu(gate) * up).astype(BF16)
            acc[g] = acc[g] + _dot(a, wd_ref[c0:c1, :])
    for g, r in enumerate(groups):
        o_ref[r, :] = acc[g]


def _tail(x2, ya, yb, w_out, g_xattn, w_xq, g_xq, km, vm, w_xo, g_ffn, w_gu, w_down, seq):
    n, d = x2.shape
    tm = ROW_TILE
    d_ff = w_down.shape[0]
    assert d_ff % MXU_WIDTH == 0
    ff_bounds = (0, (d_ff // MXU_WIDTH + 1) // 2 * MXU_WIDTH, d_ff)
    n_mem = km.shape[1]
    per_seq = seq // tm
    return pl.pallas_call(
        functools.partial(_tail_kernel, ff_bounds=ff_bounds),
        grid=(n // tm,),
        in_specs=[
            pl.BlockSpec((tm, d), lambda i: (i, 0)),
            pl.BlockSpec((tm, GDN_VAL_DIM), lambda i: (i, 0)),
            pl.BlockSpec((tm, DSA_DIM), lambda i: (i, 0)),
            _const_spec((GDN_VAL_DIM + DSA_DIM, d)),
            _const_spec((1, d)),
            _const_spec((d, XA_DIM)),
            _const_spec((XA_DIM, XA_DIM)),
            _const_spec((1, XA_DIM)),
            pl.BlockSpec((1, n_mem, XA_DIM), lambda i: (i // per_seq, 0, 0)),
            pl.BlockSpec((1, n_mem, XA_DIM), lambda i: (i // per_seq, 0, 0)),
            _const_spec((XA_DIM, d)),
            _const_spec((1, d)),
            _const_spec((d, 2 * d_ff)),
            _const_spec((d_ff, d)),
        ],
        out_specs=pl.BlockSpec((tm, d), lambda i: (i, 0)),
        out_shape=jax.ShapeDtypeStruct((n, d), F32),
        compiler_params=pltpu.CompilerParams(dimension_semantics=("arbitrary",), vmem_limit_bytes=VMEM_LIMIT),
        name="tail",
    )(x2, ya, yb, w_out.astype(BF16), g_xattn[None, :], w_xq.astype(BF16), _block_diag_ones(XA_DIM, XA_HD),
      jnp.tile(g_xq, XA_HEADS)[None, :], km, vm, w_xo.astype(BF16), g_ffn[None, :], w_gu.astype(BF16),
      w_down.astype(BF16))


def kernel(x, mem, g_mix, w_in, conv_w, a_log, dt_bias, g_gdn_out, g_q_dsa, g_k_dsa, w_out, g_xattn, g_mem, w_xq,
           w_xkv, g_xq, g_xk, w_xo, g_ffn, w_gu, w_down):
    batch, seq, d = x.shape
    n_mem = mem.shape[1]
    assert seq % ROW_TILE == 0 and seq % KEY_TILE == 0 and ROW_TILE % KEY_TILE == 0
    for l in range(g_mix.shape[0]):
        x2 = x.reshape(batch * seq, d)
        qkv, z, kn, rs, qT, vT, qiT, wT = _inproj(x2, g_mix[l], w_in[l], g_q_dsa[l], g_k_dsa[l], conv_w[l], seq)
        ya = _gdn(qkv, z, rs, a_log[l], dt_bias[l], g_gdn_out[l], batch, seq)
        yb = _dsa(qT, qiT, wT, kn, vT, rs, batch, seq)
        km, vm = _memkv(mem.reshape(batch * n_mem, d), g_mem[l], w_xkv[l], g_xk[l])
        km = km.reshape(batch, n_mem, XA_DIM)
        vm = vm.reshape(batch, n_mem, XA_DIM)
        x = _tail(x2, ya, yb, w_out[l], g_xattn[l], w_xq[l], g_xq[l], km, vm, w_xo[l], g_ffn[l], w_gu[l],
                  w_down[l], seq).reshape(batch, seq, d)
    return x
```

```python
import functools

import jax
import jax.numpy as jnp
from jax import lax
from jax.experimental import pallas as pl
from jax.experimental.pallas import tpu as pltpu

F32 = jnp.float32
BF16 = jnp.bfloat16
I32 = jnp.int32

EPS = 1e-6
CHUNK = 64
CHUNK_SHIFT = CHUNK.bit_length() - 1
assert CHUNK == 1 << CHUNK_SHIFT
GDN_HEADS, GDN_DK, GDN_DV, CONV_K = 4, 128, 128, 4
DSA_HEADS, DSA_HD = 8, 64
IDX_HEADS, IDX_HD = 16, 64
TOPK_MAX = 256
XA_HEADS, XA_HD = 4, 128

GDN_KEY_DIM = GDN_HEADS * GDN_DK
GDN_VAL_DIM = GDN_HEADS * GDN_DV
GDN_CONV_CH = 2 * GDN_KEY_DIM + GDN_VAL_DIM
DSA_DIM = DSA_HEADS * DSA_HD
IDX_Q_DIM = IDX_HEADS * IDX_HD
XA_DIM = XA_HEADS * XA_HD

LANES = 128
SUBLANES = 8
MXU_WIDTH = 256
VMEM_LIMIT = 56 * 1024 * 1024

RS_IDXK = 0
RS_BETA = IDX_HD
RS_A = IDX_HD + GDN_HEADS

ROW_TILE = 512
ROW_PARTS = 2
KEY_TILE = 256
Q_BLOCK = 128
GDN_CHUNKS_PER_STEP = 4
GDN_SEQS_PER_STEP = 4

BISECT_ROUNDS = 20
COARSE_ROUNDS = 14
COARSE_SHIFT = 8
COUNT_CHAINS = 4
ATTEND_RUN = 4
KPOS_DIGIT_BITS = 4
SLOPE_PIECES = 3
LOG2E = 1.4426950408889634
NEG_BIG = -1e30


def _dot(a, b):
    return jnp.dot(a, b, preferred_element_type=F32)


def _dot_nt(a, b):
    return lax.dot_general(a, b, (((1,), (1,)), ((), ())), preferred_element_type=F32)


def _dot_f32(a, b):
    return jnp.dot(a, b, preferred_element_type=F32, precision=lax.Precision.HIGHEST)


def _silu(x):
    half = 0.5 * x
    return half + half * jnp.tanh(half)


def _rms_rows(x, g):
    ms = jnp.mean(x * x, axis=-1, keepdims=True)
    return x * lax.rsqrt(ms + EPS) * g


def _const_spec(shape):
    nd = len(shape)
    return pl.BlockSpec(shape, lambda *_: (0,) * nd, pipeline_mode=pl.Buffered(1))


def _block_diag_ones(n, blk):
    r = lax.broadcasted_iota(I32, (n, n), 0) // blk
    c = lax.broadcasted_iota(I32, (n, n), 1) // blk
    return (r == c).astype(BF16)


def _inproj_kernel(x_ref, g_ref, wrow_ref, wcol_ref, bd_ref, gk_ref, gq_ref, cw_ref,
                   qkv_ref, z_ref, kn_ref, rs_ref, qT_ref, vT_ref, qiT_ref, wT_ref, xc_ref, *, tiles_per_seq):
    tm = x_ref.shape[0]

    @pl.when(pl.program_id(0) % tiles_per_seq == 0)
    def _start_of_sequence():
        xc_ref[0:SUBLANES, :] = jnp.zeros((SUBLANES, GDN_CONV_CH), F32)

    h = _rms_rows(x_ref[...], g_ref[...]).astype(BF16)
    c0, c1, c2, c3 = GDN_CONV_CH, GDN_CONV_CH + GDN_VAL_DIM, GDN_CONV_CH + GDN_VAL_DIM + DSA_DIM, \
        GDN_CONV_CH + GDN_VAL_DIM + DSA_DIM + LANES

    xc_ref[SUBLANES:SUBLANES + tm, :] = _dot(h, wrow_ref[:, 0:c0])

    zkr = _dot(h, wrow_ref[:, c0:c3])
    z_ref[...] = zkr[:, 0:c1 - c0]
    k = zkr[:, c1 - c0:c2 - c0]
    kms = _dot((k * k).astype(BF16), bd_ref[...]) * (1.0 / DSA_HD)
    kn_ref[...] = (k * lax.rsqrt(kms + EPS) * gk_ref[...]).astype(BF16)
    rs_ref[...] = zkr[:, c2 - c0:c3 - c0]

    r0, r1, r2, r3 = DSA_DIM, 2 * DSA_DIM, 2 * DSA_DIM + IDX_Q_DIM, 2 * DSA_DIM + IDX_Q_DIM + IDX_HEADS
    qvT = _dot_nt(wcol_ref[0:r1, :], h)
    qT = qvT[0:r0, :]
    qms = _dot(bd_ref[...], (qT * qT).astype(BF16)) * (1.0 / DSA_HD)
    qT_ref[...] = (qT * lax.rsqrt(qms + EPS) * gq_ref[...] * (DSA_HD ** -0.5 * LOG2E)).astype(BF16)
    vT = qvT[r0:r1, :].astype(BF16)
    for i in range(vT_ref.shape[0]):
        vT_ref[i] = vT[:, i * KEY_TILE:(i + 1) * KEY_TILE]
    qwT = _dot_nt(wcol_ref[r1:r3, :], h)
    qiT_ref[...] = (qwT[0:IDX_Q_DIM, :] * (IDX_HD ** -0.5)).astype(BF16)
    wT_ref[...] = qwT[IDX_Q_DIM:, :] * (IDX_HEADS ** -0.5)

    cw = cw_ref[...]
    conv = cw[CONV_K - 1:CONV_K, :] * xc_ref[SUBLANES:SUBLANES + tm, :]
    for j in range(CONV_K - 2, -1, -1):
        back = CONV_K - 1 - j
        conv = conv + cw[j:j + 1, :] * xc_ref[SUBLANES - back:SUBLANES - back + tm, :]
    xc_ref[0:SUBLANES, :] = xc_ref[tm:tm + SUBLANES, :]
    act = _silu(conv)
    for hd in range(GDN_HEADS):
        qs = slice(hd * GDN_DK, (hd + 1) * GDN_DK)
        ks = slice(GDN_KEY_DIM + hd * GDN_DK, GDN_KEY_DIM + (hd + 1) * GDN_DK)
        q, k = act[:, qs], act[:, ks]
        qkv_ref[:, qs] = q * lax.rsqrt(jnp.sum(q * q, axis=-1, keepdims=True) + EPS) * (GDN_DK ** -0.5)
        qkv_ref[:, ks] = k * lax.rsqrt(jnp.sum(k * k, axis=-1, keepdims=True) + EPS)
    qkv_ref[:, 2 * GDN_KEY_DIM:] = act[:, 2 * GDN_KEY_DIM:]


def _inproj(x2, g_mix, w_in, g_q, g_k, conv_w, seq):
    n, d = x2.shape
    tm = ROW_TILE
    sizes = (GDN_CONV_CH, GDN_VAL_DIM, GDN_HEADS, GDN_HEADS, DSA_DIM, DSA_DIM, DSA_DIM, IDX_Q_DIM, IDX_HD, IDX_HEADS)
    offs = [0]
    for s in sizes:
        offs.append(offs[-1] + s)
    (w_qkv, w_z, w_b, w_a, w_q, w_k, w_v, w_iq, w_ik, w_iw) = [w_in[:, offs[i]:offs[i + 1]] for i in range(len(sizes))]
    pad = jnp.zeros((d, LANES - IDX_HD - 2 * GDN_HEADS), w_in.dtype)
    w_row = jnp.concatenate([w_qkv, w_z, w_k, w_ik, w_b, w_a, pad], axis=1).astype(BF16)
    w_col = jnp.concatenate([w_q, w_v, w_iq, w_iw], axis=1).T.astype(BF16)
    bd = _block_diag_ones(DSA_DIM, DSA_HD)
    gk_row = jnp.tile(g_k, DSA_HEADS)[None, :]
    gq_col = jnp.tile(g_q, DSA_HEADS)[:, None]
    nrow, ncol = w_row.shape[1], w_col.shape[0]
    out_shape = (
        jax.ShapeDtypeStruct((n, GDN_CONV_CH), F32),
        jax.ShapeDtypeStruct((n, GDN_VAL_DIM), F32),
        jax.ShapeDtypeStruct((n, DSA_DIM), BF16),
        jax.ShapeDtypeStruct((n, LANES), F32),
        jax.ShapeDtypeStruct((DSA_DIM, n), BF16),
        jax.ShapeDtypeStruct((n // KEY_TILE, DSA_DIM, KEY_TILE), BF16),
        jax.ShapeDtypeStruct((IDX_Q_DIM, n), BF16),
        jax.ShapeDtypeStruct((IDX_HEADS, n), F32),
    )
    return pl.pallas_call(
        functools.partial(_inproj_kernel, tiles_per_seq=seq // tm),
        grid=(n // tm,),
        in_specs=[
            pl.BlockSpec((tm, d), lambda i: (i, 0)),
            _const_spec((1, d)),
            _const_spec((d, nrow)),
            _const_spec((ncol, d)),
            _const_spec((DSA_DIM, DSA_DIM)),
            _const_spec((1, DSA_DIM)),
            _const_spec((DSA_DIM, 1)),
            _const_spec((CONV_K, GDN_CONV_CH)),
        ],
        out_specs=(
            pl.BlockSpec((tm, GDN_CONV_CH), lambda i: (i, 0)),
            pl.BlockSpec((tm, GDN_VAL_DIM), lambda i: (i, 0)),
            pl.BlockSpec((tm, DSA_DIM), lambda i: (i, 0)),
            pl.BlockSpec((tm, LANES), lambda i: (i, 0)),
            pl.BlockSpec((DSA_DIM, tm), lambda i: (0, i)),
            pl.BlockSpec((tm // KEY_TILE, DSA_DIM, KEY_TILE), lambda i: (i, 0, 0)),
            pl.BlockSpec((IDX_Q_DIM, tm), lambda i: (0, i)),
            pl.BlockSpec((IDX_HEADS, tm), lambda i: (0, i)),
        ),
        out_shape=out_shape,
        scratch_shapes=[pltpu.VMEM((tm + SUBLANES, GDN_CONV_CH), F32)],
        compiler_params=pltpu.CompilerParams(dimension_semantics=("arbitrary",), vmem_limit_bytes=VMEM_LIMIT),
        name="inproj",
    )(x2, g_mix[None, :], w_row, w_col, bd, gk_row, gq_col, conv_w)


def _gdn_kernel(qkv_ref, z_ref, rs_ref, alog_ref, dtb_ref, gout_ref, y_ref, s_ref):
    nb, rows = qkv_ref.shape[0], qkv_ref.shape[1]
    nc = rows // CHUNK

    @pl.when(pl.program_id(1) == 0)
    def _start_of_sequence():
        s_ref[...] = jnp.zeros(s_ref.shape, F32)

    rs = rs_ref[...]
    beta_all = jax.nn.sigmoid(rs)
    sp_in = rs + dtb_ref[...]
    softplus = jnp.maximum(sp_in, 0.0) + jnp.log(1.0 + jnp.exp(-jnp.abs(sp_in)))
    g_all = -jnp.exp(alog_ref[...]) * softplus

    ri = lax.broadcasted_iota(I32, (CHUNK, CHUNK), 0)
    ci = lax.broadcasted_iota(I32, (CHUNK, CHUNK), 1)
    incl = ri >= ci
    strict = ri > ci
    ltri = incl.astype(F32)
    eye = (ri == ci).astype(F32)
    gout = gout_ref[...]

    chains = [(b, c, h) for c in range(nc) for b in range(nb) for h in range(GDN_HEADS)]
    dcum, dcum_t = {}, {}
    for c in range(nc):
        for b in range(nb):
            d = _dot_f32(ltri, g_all[b, c * CHUNK:(c + 1) * CHUNK, :])
            dcum[b, c] = d
            dcum_t[b, c] = d.T

    q16, k16, kb16, vb16, kw16, qdec16, kdect16, gamma, last = ([] for _ in range(9))
    for b, c, h in chains:
        r = slice(c * CHUNK, (c + 1) * CHUNK)
        q = qkv_ref[b, r, h * GDN_DK:(h + 1) * GDN_DK]
        k = qkv_ref[b, r, GDN_KEY_DIM + h * GDN_DK:GDN_KEY_DIM + (h + 1) * GDN_DK]
        v = qkv_ref[b, r, 2 * GDN_KEY_DIM + h * GDN_DV:2 * GDN_KEY_DIM + (h + 1) * GDN_DV]
        beta = beta_all[b, r, RS_BETA + h:RS_BETA + h + 1]
        d_col = dcum[b, c][:, RS_A + h:RS_A + h + 1]
        d_row = dcum_t[b, c][RS_A + h:RS_A + h + 1, :]
        d_last = dcum[b, c][CHUNK - 1:CHUNK, RS_A + h:RS_A + h + 1]
        e_col = jnp.exp(d_col)
        kb = k * beta
        gamma.append(jnp.exp(jnp.where(incl, d_col - d_row, -jnp.inf)))
        q16.append(q.astype(BF16))
        k16.append(k.astype(BF16))
        kb16.append(kb.astype(BF16))
        vb16.append((v * beta).astype(BF16))
        kw16.append((kb * e_col).astype(BF16))
        qdec16.append((q * e_col).astype(BF16))
        kdect16.append((k * jnp.exp(d_last - d_col)).T.astype(BF16))
        last.append(jnp.exp(d_last))

    n = len(chains)
    p16 = [(-jnp.where(strict, _dot_nt(kb16[i], k16[i]) * gamma[i], 0.0)).astype(BF16) for i in range(n)]
    qk16 = [(_dot_nt(q16[i], k16[i]) * gamma[i]).astype(BF16) for i in range(n)]
    t_mat = [eye + p16[i].astype(F32) for i in range(n)]
    p16 = [_dot(p16[i], p16[i]).astype(BF16) for i in range(n)]
    for _ in range(4):
        t_mat = [t_mat[i] + _dot(t_mat[i].astype(BF16), p16[i]) for i in range(n)]
        p16 = [_dot(p16[i], p16[i]).astype(BF16) for i in range(n)]
    t16 = [(t_mat[i] + _dot(t_mat[i].astype(BF16), p16[i])).astype(BF16) for i in range(n)]
    u = [_dot(t16[i], vb16[i]) for i in range(n)]
    w16 = [_dot(t16[i], kw16[i]).astype(BF16) for i in range(n)]

    per_chunk = nb * GDN_HEADS
    s = [s_ref[b, h] for b in range(nb) for h in range(GDN_HEADS)]
    for c in range(nc):
        ids = range(c * per_chunk, (c + 1) * per_chunk)
        s16 = [s[j].astype(BF16) for j in range(per_chunk)]
        v_new16 = [(u[i] - _dot(w16[i], s16[j])).astype(BF16) for j, i in enumerate(ids)]
        o_state = [_dot(qdec16[i], s16[j]) for j, i in enumerate(ids)]
        o = [o_state[j] + _dot(qk16[i], v_new16[j]) for j, i in enumerate(ids)]
        s = [s[j] * last[i] + _dot(kdect16[i], v_new16[j]) for j, i in enumerate(ids)]
        for j, i in enumerate(ids):
            b, _, h = chains[i]
            zc = z_ref[b, c * CHUNK:(c + 1) * CHUNK, h * GDN_DV:(h + 1) * GDN_DV]
            y = _rms_rows(o[j], gout) * _silu(zc)
            y_ref[b, c * CHUNK:(c + 1) * CHUNK, h * GDN_DV:(h + 1) * GDN_DV] = y.astype(y_ref.dtype)
    for j in range(per_chunk):
        s_ref[j // GDN_HEADS, j % GDN_HEADS] = s[j]


def _gdn(qkv, z, rs, a_log, dt_bias, g_out, batch, seq):
    n = qkv.shape[0]
    rows = GDN_CHUNKS_PER_STEP * CHUNK
    nb = GDN_SEQS_PER_STEP if batch % GDN_SEQS_PER_STEP == 0 else 1
    lane_vec = lambda v: jnp.zeros((1, LANES), F32).at[0, RS_A:RS_A + GDN_HEADS].set(v)
    per_seq = lambda a: a.reshape(batch, seq, a.shape[-1])
    block = lambda width: pl.BlockSpec((nb, rows, width), lambda b, i: (b, i, 0))
    y = pl.pallas_call(
        _gdn_kernel,
        grid=(batch // nb, seq // rows),
        in_specs=[
            block(GDN_CONV_CH),
            block(GDN_VAL_DIM),
            block(LANES),
            _const_spec((1, LANES)),
            _const_spec((1, LANES)),
            _const_spec((1, GDN_DV)),
        ],
        out_specs=block(GDN_VAL_DIM),
        out_shape=jax.ShapeDtypeStruct((batch, seq, GDN_VAL_DIM), BF16),
        scratch_shapes=[
            pltpu.VMEM((nb, GDN_HEADS, GDN_DK, GDN_DV), F32),
        ],
        compiler_params=pltpu.CompilerParams(dimension_semantics=("arbitrary", "arbitrary"), vmem_limit_bytes=VMEM_LIMIT),
        name="gdn",
    )(per_seq(qkv), per_seq(z), per_seq(rs), lane_vec(a_log), lane_vec(dt_bias), g_out[None, :])
    return y.reshape(n, GDN_VAL_DIM)


def _alibi_slope(h):
    return 2.0 ** (-8.0 * (h + 1) / DSA_HEADS)


def _order_key(x):
    bits = lax.bitcast_convert_type(x, I32)
    return (bits ^ ((bits >> 31) & jnp.int32(0x7FFFFFFF))) >> 1


def _order_key_to_f32(key):
    full = key << 1
    return lax.bitcast_convert_type(full ^ ((full >> 31) & jnp.int32(0x7FFFFFFF)), F32)


def _tiles_for_block(j):
    return (j * Q_BLOCK + Q_BLOCK + KEY_TILE - 1) // KEY_TILE


def _dsa_kernel(qT_ref, qiT_ref, wT_ref, kn_ref, vT_ref, rs_ref, kpos_ref, slope_ref, o_ref,
                idxk_scr, sc_scr, key_scr, coarse_scr, thr_scr, cnt_scr, acc_scr, *, topk, n_pos_bits):
    j = pl.program_id(1)

    @pl.when(j == 0)
    def _new_sequence():
        idxk_scr[...] = rs_ref[:, RS_IDXK:RS_IDXK + IDX_HD].astype(BF16)

    n_tiles = _tiles_for_block(j)
    lane = lax.broadcasted_iota(I32, (1, Q_BLOCK), 1)
    qpos = j * Q_BLOCK + lane
    key_limit = ((qpos >> CHUNK_SHIFT) + 1) << CHUNK_SHIFT
    row = lax.broadcasted_iota(I32, (KEY_TILE, Q_BLOCK), 0)

    w_t = wT_ref[...]
    qi_pairs = [jnp.concatenate([qiT_ref[(2 * p) * IDX_HD:(2 * p + 1) * IDX_HD, :],
                                 qiT_ref[(2 * p + 1) * IDX_HD:(2 * p + 2) * IDX_HD, :]], axis=1)
                for p in range(IDX_HEADS // 2)]

    def score_tiles(t, count, carry):
        s_max, s_min = carry
        r0 = pl.multiple_of(t * KEY_TILE, KEY_TILE)
        kt = idxk_scr[pl.ds(r0, count * KEY_TILE), :]
        accs = [jnp.zeros((KEY_TILE, Q_BLOCK), F32) for _ in range(count)]
        for p in range(IDX_HEADS // 2):
            d = _dot(kt, qi_pairs[p])
            for i in range(count):
                di = d[i * KEY_TILE:(i + 1) * KEY_TILE, :]
                accs[i] = accs[i] + jnp.maximum(di[:, :Q_BLOCK], 0.0) * w_t[2 * p:2 * p + 1, :]
                accs[i] = accs[i] + jnp.maximum(di[:, Q_BLOCK:], 0.0) * w_t[2 * p + 1:2 * p + 2, :]
        for i in range(count):
            admissible = r0 + i * KEY_TILE + row < key_limit
            stored = jnp.where(admissible, accs[i], -jnp.inf)
            sc_scr[t + i] = stored
            key = _order_key(stored)
            key_scr[t + i] = key
            coarse_scr[t + i] = (key >> COARSE_SHIFT).astype(F32)
            s_max = jnp.maximum(s_max, jnp.max(stored, axis=0, keepdims=True))
            s_min = jnp.minimum(s_min, jnp.min(jnp.where(admissible, accs[i], jnp.inf), axis=0, keepdims=True))
        return s_max, s_min

    stats = (jnp.full((1, Q_BLOCK), -jnp.inf, F32), jnp.full((1, Q_BLOCK), jnp.inf, F32))
    stats = lax.fori_loop(0, n_tiles // 2, lambda u, st: score_tiles(2 * u, 2, st), stats)
    s_max, s_min = lax.cond(n_tiles % 2 == 1, lambda st: score_tiles(n_tiles - 1, 1, st), lambda st: st, stats)

    def count_tiles(per_tile):
        cnt_scr[...] = jnp.zeros(cnt_scr.shape, F32)
        base = jnp.int32(0)
        arm = 1 << (sc_scr.shape[0].bit_length() - 1)
        while arm >= 1:
            has = (n_tiles & arm) != 0

            @pl.when(has)
            def _run(base=base, arm=arm):
                chains = [jnp.zeros((SUBLANES, Q_BLOCK), F32) for _ in range(COUNT_CHAINS)]
                for i in range(arm):
                    flags = per_tile(base + i)
                    for r in range(KEY_TILE // SUBLANES):
                        chains[r % COUNT_CHAINS] = chains[r % COUNT_CHAINS] + flags[r * SUBLANES:(r + 1) * SUBLANES, :]
                while len(chains) > 1:
                    chains = [chains[k] + chains[k + 1] for k in range(0, len(chains), 2)]
                cnt_scr[...] += chains[0]

            base = base + jnp.where(has, arm, 0)
            arm //= 2
        return jnp.sum(cnt_scr[...], axis=0, keepdims=True).astype(I32)

    def count_ge(thr):
        return count_tiles(lambda i: jnp.where(sc_scr[i] >= thr, 1.0, 0.0))

    def count_keys_below(key, tiles):
        one = jnp.int32(0x3F800000)
        chains = [jnp.zeros((SUBLANES, Q_BLOCK), F32) for _ in range(COUNT_CHAINS)]
        for i in range(tiles):
            below = lax.bitcast_convert_type(((key_scr[i] - key) >> 31) & one, F32)
            for r in range(KEY_TILE // SUBLANES):
                chains[r % COUNT_CHAINS] = chains[r % COUNT_CHAINS] + below[r * SUBLANES:(r + 1) * SUBLANES, :]
        while len(chains) > 1:
            chains = [chains[k] + chains[k + 1] for k in range(0, len(chains), 2)]
        return jnp.sum(chains[0], axis=0, keepdims=True)

    def search_with_tiles(tiles):
        most_below = float(tiles * KEY_TILE - topk)

        def bisect_step(_, st):
            lo, hi, mid, mid_key = st
            up, down = mid * 0.5 + hi * 0.5, lo * 0.5 + mid * 0.5
            up_key, down_key = _order_key(up), _order_key(down)
            take = count_keys_below(mid_key, tiles) <= most_below
            return (jnp.where(take, mid, lo), jnp.where(take, hi, mid),
                    jnp.where(take, up, down), jnp.where(take, up_key, down_key))

        def coarse_step(_, st):
            lo, hi, mid, mid_key = st
            bucket = mid_key >> COARSE_SHIFT
            edge = jnp.minimum(hi, _order_key_to_f32((bucket + 1) << COARSE_SHIFT))
            up, down = mid * 0.5 + hi * 0.5, lo * 0.5 + edge * 0.5
            up_key, down_key = _order_key(up), _order_key(down)
            c = bucket.astype(F32) + 0.5
            chains = [jnp.zeros((SUBLANES, Q_BLOCK), F32) for _ in range(COUNT_CHAINS)]
            for i in range(tiles):
                side = lax.clamp(-0.5, c - coarse_scr[i], 0.5)
                for r in range(KEY_TILE // SUBLANES):
                    chains[r % COUNT_CHAINS] = chains[r % COUNT_CHAINS] + side[r * SUBLANES:(r + 1) * SUBLANES, :]
            while len(chains) > 1:
                chains = [chains[k] + chains[k + 1] for k in range(0, len(chains), 2)]
            below = jnp.sum(chains[0], axis=0, keepdims=True) + 0.5 * (tiles * KEY_TILE)
            take = below <= most_below
            return (jnp.where(take, mid, lo), jnp.where(take, hi, edge),
                    jnp.where(take, up, down), jnp.where(take, up_key, down_key))

        def search(st):
            st = lax.fori_loop(0, COARSE_ROUNDS, coarse_step, st)
            return lax.fori_loop(COARSE_ROUNDS, BISECT_ROUNDS, bisect_step, st)

        return search

    above_max = s_max + jnp.maximum(jnp.abs(s_max) * 1e-6, 1e-30)
    first_mid = s_min * 0.5 + above_max * 0.5
    lo = lax.switch(n_tiles - 1, [search_with_tiles(n) for n in range(1, sc_scr.shape[0] + 1)],
                    (s_min, above_max, first_mid, _order_key(first_mid)))[0]
    few = key_limit < topk
    thr = jnp.where(few, jnp.finfo(F32).min, _order_key_to_f32(_order_key(lo)))
    cnt = jnp.where(few, topk, count_ge(thr))
    thr_scr[...] = thr

    @pl.when(jnp.max(jnp.abs(cnt - topk)) > 0)
    def _resolve():
        def max_below(thr):
            def tile(t, m):
                sc = sc_scr[t]
                return jnp.maximum(m, jnp.max(jnp.where(sc < thr, sc, -jnp.inf), axis=0, keepdims=True))
            return lax.fori_loop(0, n_tiles, tile, jnp.full((1, Q_BLOCK), -jnp.inf, F32))

        def lower(st):
            thr, cnt = st
            nxt = max_below(thr)
            short = cnt < topk
            return jnp.where(short, nxt, thr), jnp.where(short, count_ge(nxt), cnt)

        def min_above(thr):
            def tile(t, m):
                sc = sc_scr[t]
                return jnp.minimum(m, jnp.min(jnp.where(sc > thr, sc, jnp.inf), axis=0, keepdims=True))
            return lax.fori_loop(0, n_tiles, tile, jnp.full((1, Q_BLOCK), jnp.inf, F32))

        def count_eq_below(thr, pos_limit):
            def tile(t, c):
                r0 = t * KEY_TILE
                hit = (sc_scr[t] == thr) & (r0 + row < pos_limit)
                return c + jnp.sum(hit.astype(I32), axis=0, keepdims=True)
            return lax.fori_loop(0, n_tiles, tile, jnp.zeros((1, Q_BLOCK), I32))

        def body(st):
            thr, cnt = st
            nxt = min_above(thr)
            cnt_n = count_ge(nxt)
            active = cnt > topk
            advance = active & (cnt_n >= topk)
            tie = active & (cnt_n < topk)
            @pl.when(jnp.max(tie.astype(I32)) > 0)
            def _drop_surplus_ties():
                need = topk - cnt_n
                pos = jnp.zeros((1, Q_BLOCK), I32)
                for b in range(n_pos_bits - 1, -1, -1):
                    cand = pos + (1 << b)
                    pos = jnp.where(count_eq_below(thr, cand) < need, cand, pos)

                def drop_tile(t, carry):
                    r0 = t * KEY_TILE
                    sc = sc_scr[t]
                    sc_scr[t] = jnp.where(tie & (sc == thr) & (r0 + row > pos), -jnp.inf, sc)
                    return carry

                lax.fori_loop(0, n_tiles, drop_tile, 0)

            return jnp.where(advance, nxt, thr), jnp.where(advance, cnt_n, jnp.where(tie, topk, cnt))

        st = lax.while_loop(lambda st: jnp.min(st[1]) < topk, lower, (thr, cnt))
        thr_scr[...] = lax.while_loop(lambda st: jnp.max(st[1]) > topk, body, st)[0]

    thr = thr_scr[...]

    zero = jnp.zeros((DSA_HD, Q_BLOCK), BF16)
    q_pairs = []
    for p in range(DSA_HEADS // 2):
        a = qT_ref[(2 * p) * DSA_HD:(2 * p + 1) * DSA_HD, :]
        b = qT_ref[(2 * p + 1) * DSA_HD:(2 * p + 2) * DSA_HD, :]
        q_pairs.append(jnp.concatenate([jnp.concatenate([a, zero], axis=1),
                                        jnp.concatenate([zero, b], axis=1),
                                        slope_ref[p]], axis=0))
    acc_scr[...] = jnp.zeros(acc_scr.shape, F32)
    qposf = qpos.astype(F32)

    def qk_products(t):
        r0 = pl.multiple_of(t * KEY_TILE, KEY_TILE)
        kt = kn_ref[pl.ds(r0, KEY_TILE), :]
        kp = kpos_ref[pl.ds(r0, KEY_TILE), :]
        s2 = [_dot(jnp.concatenate([kt[:, p * 2 * DSA_HD:(p + 1) * 2 * DSA_HD], kp], axis=1), q_pairs[p])
              for p in range(DSA_HEADS // 2)]
        return t, sc_scr[t] >= thr, s2

    def softmax_pv(tile, ml, last_tile):
        t, sel, s2 = tile
        vt = vT_ref[t]
        if last_tile:
            after = jnp.maximum((t * KEY_TILE + row).astype(F32) - qposf, 0.0)
        m_all, l_all = ml
        m_rows, l_rows = [], []
        for h in range(DSA_HEADS):
            s = s2[h // 2][:, (h % 2) * Q_BLOCK:(h % 2 + 1) * Q_BLOCK]
            if last_tile:
                s = s - (2.0 * LOG2E * _alibi_slope(h)) * after
            s = jnp.where(sel, s, NEG_BIG)
            m_old = m_all[h:h + 1, :]
            m_new = jnp.maximum(m_old, jnp.max(s, axis=0, keepdims=True))
            alpha = jnp.exp2(m_old - m_new)
            pr = jnp.exp2(s - m_new)
            l_rows.append(alpha * l_all[h:h + 1, :] + jnp.sum(pr, axis=0, keepdims=True))
            m_rows.append(m_new)
            hs = slice(h * DSA_HD, (h + 1) * DSA_HD)
            acc_scr[hs, :] = alpha * acc_scr[hs, :] + _dot(vt[hs, :], pr.astype(BF16))
        return jnp.concatenate(m_rows, axis=0), jnp.concatenate(l_rows, axis=0)

    def attend_run(first, count, ml, ends_block):
        tiles = [qk_products(first + i) for i in range(count)]
        for i, tile in enumerate(tiles):
            ml = softmax_pv(tile, ml, ends_block and i == count - 1)
        return ml

    ml = (jnp.full((DSA_HEADS, Q_BLOCK), NEG_BIG, F32), jnp.zeros((DSA_HEADS, Q_BLOCK), F32))
    full_runs = (n_tiles - 1) // ATTEND_RUN
    ml = lax.fori_loop(0, full_runs, lambda u, ml: attend_run(u * ATTEND_RUN, ATTEND_RUN, ml, False), ml)
    rest = n_tiles - full_runs * ATTEND_RUN
    _, l_all = lax.switch(rest - 1, [(lambda ml, c=c: attend_run(n_tiles - c, c, ml, True))
                                     for c in range(1, ATTEND_RUN + 1)], ml)

    outs = [acc_scr[h * DSA_HD:(h + 1) * DSA_HD, :] / l_all[h:h + 1, :] for h in range(DSA_HEADS)]
    o_ref[...] = jnp.concatenate(outs, axis=0).T.astype(o_ref.dtype)


def _dsa(qT, qiT, wT, kn, vT, rs, batch, seq):
    n = kn.shape[0]
    blocks = seq // Q_BLOCK
    tiles = seq // KEY_TILE
    topk = min(TOPK_MAX, seq // 4)
    col_map = lambda b, j: (0, b * blocks + j)
    pos = jnp.arange(seq, dtype=I32)
    digit = 1 << KPOS_DIGIT_BITS
    kpos = jnp.zeros((seq, LANES), F32)
    slope_rows = jnp.zeros((DSA_HEADS // 2, LANES, 2 * Q_BLOCK), F32)
    rest = jnp.asarray([LOG2E * _alibi_slope(h) for h in range(DSA_HEADS)], F32).reshape(DSA_HEADS // 2, 2)
    for i in range(SLOPE_PIECES):
        piece = rest.astype(BF16).astype(F32)
        rest = rest - piece
        cols = jnp.repeat(piece, Q_BLOCK, axis=1)
        kpos = kpos.at[:, 2 * i].set((pos // digit).astype(F32)).at[:, 2 * i + 1].set((pos % digit).astype(F32))
        slope_rows = slope_rows.at[:, 2 * i, :].set(float(digit) * cols).at[:, 2 * i + 1, :].set(cols)
    return pl.pallas_call(
        functools.partial(_dsa_kernel, topk=topk, n_pos_bits=seq.bit_length()),
        grid=(batch, blocks),
        in_specs=[
            pl.BlockSpec((DSA_DIM, Q_BLOCK), col_map),
            pl.BlockSpec((IDX_Q_DIM, Q_BLOCK), col_map),
            pl.BlockSpec((IDX_HEADS, Q_BLOCK), col_map),
            pl.BlockSpec((seq, DSA_DIM), lambda b, j: (b, 0)),
            pl.BlockSpec((tiles, DSA_DIM, KEY_TILE), lambda b, j: (b, 0, 0)),
            pl.BlockSpec((seq, LANES), lambda b, j: (b, 0)),
            _const_spec((seq, LANES)),
            _const_spec((DSA_HEADS // 2, LANES, 2 * Q_BLOCK)),
        ],
        out_specs=pl.BlockSpec((Q_BLOCK, DSA_DIM), lambda b, j: (b * blocks + j, 0)),
        out_shape=jax.ShapeDtypeStruct((n, DSA_DIM), BF16),
        scratch_shapes=[
            pltpu.VMEM((seq, IDX_HD), BF16),
            pltpu.VMEM((tiles, KEY_TILE, Q_BLOCK), F32),
            pltpu.VMEM((tiles, KEY_TILE, Q_BLOCK), I32),
            pltpu.VMEM((tiles, KEY_TILE, Q_BLOCK), F32),
            pltpu.VMEM((1, Q_BLOCK), F32),
            pltpu.VMEM((SUBLANES, Q_BLOCK), F32),
            pltpu.VMEM((DSA_DIM, Q_BLOCK), F32),
        ],
        compiler_params=pltpu.CompilerParams(dimension_semantics=("arbitrary", "arbitrary"), vmem_limit_bytes=VMEM_LIMIT),
        name="dsa",
    )(qT, qiT, wT, kn, vT, rs, kpos.astype(BF16), slope_rows.astype(BF16))


def _memkv_kernel(mem_ref, g_ref, w_ref, bd_ref, gk_ref, k_ref, v_ref):
    h = _rms_rows(mem_ref[...], g_ref[...]).astype(BF16)
    kv = _dot(h, w_ref[...])
    k = kv[:, :XA_DIM]
    kms = _dot((k * k).astype(BF16), bd_ref[...]) * (1.0 / XA_HD)
    k_ref[...] = (k * lax.rsqrt(kms + EPS) * gk_ref[...]).astype(BF16)
    v_ref[...] = kv[:, XA_DIM:].astype(BF16)


def _memkv(mem2, g_mem, w_xkv, g_xk):
    n, d = mem2.shape
    tm = min(ROW_TILE, n)
    return pl.pallas_call(
        _memkv_kernel,
        grid=(n // tm,),
        in_specs=[
            pl.BlockSpec((tm, d), lambda i: (i, 0)),
            _const_spec((1, d)),
            _const_spec((d, 2 * XA_DIM)),
            _const_spec((XA_DIM, XA_DIM)),
            _const_spec((1, XA_DIM)),
        ],
        out_specs=(pl.BlockSpec((tm, XA_DIM), lambda i: (i, 0)), pl.BlockSpec((tm, XA_DIM), lambda i: (i, 0))),
        out_shape=(jax.ShapeDtypeStruct((n, XA_DIM), BF16), jax.ShapeDtypeStruct((n, XA_DIM), BF16)),
        compiler_params=pltpu.CompilerParams(dimension_semantics=("arbitrary",), vmem_limit_bytes=VMEM_LIMIT),
        name="memkv",
    )(mem2, g_mem[None, :], w_xkv.astype(BF16), _block_diag_ones(XA_DIM, XA_HD), jnp.tile(g_xk, XA_HEADS)[None, :])


def _tail_kernel(x_ref, ya_ref, yb_ref, wout_ref, gx_ref, wxq_ref, bd_ref, gxq_ref, km_ref, vm_ref, wxo_ref,
                 gf_ref, wgu_ref, wd_ref, o_ref, *, ff_bounds):
    tm = x_ref.shape[0]
    part = tm // ROW_PARTS
    groups = [slice(g * part, (g + 1) * part) for g in range(ROW_PARTS)]
    d_ff = wd_ref.shape[0]

    x1 = [x_ref[r, :] + _dot(jnp.concatenate([ya_ref[r, :], yb_ref[r, :]], axis=1), wout_ref[...]) for r in groups]

    qn = []
    for g in range(ROW_PARTS):
        h = _rms_rows(x1[g], gx_ref[...]).astype(BF16)
        q = _dot(h, wxq_ref[...])
        qms = _dot((q * q).astype(BF16), bd_ref[...]) * (1.0 / XA_HD)
        qn.append((q * lax.rsqrt(qms + EPS) * gxq_ref[...]).astype(BF16))

    attn = []
    for g in range(ROW_PARTS):
        heads = []
        for hd in range(XA_HEADS):
            hs = slice(hd * XA_HD, (hd + 1) * XA_HD)
            s = _dot_nt(qn[g][:, hs], km_ref[0, :, hs]) * (XA_HD ** -0.5)
            pr = jnp.exp(s - jnp.max(s, axis=-1, keepdims=True))
            o = _dot(pr.astype(BF16), vm_ref[0, :, hs]) / jnp.sum(pr, axis=-1, keepdims=True)
            heads.append(o.astype(BF16))
        attn.append(jnp.concatenate(heads, axis=1))

    x2 = [x1[g] + _dot(attn[g], wxo_ref[...]) for g in range(ROW_PARTS)]
    h = [_rms_rows(x2[g], gf_ref[...]).astype(BF16) for g in range(ROW_PARTS)]
    acc = x2
    for c0, c1 in zip(ff_bounds[:-1], ff_bounds[1:]):
        for g in range(ROW_PARTS):
            gate = _dot(h[g], wgu_ref[:, c0:c1])
            up = _dot(h[g], wgu_ref[:, d_ff + c0:d_ff + c1])
            a = (_silu(gate) * up).astype(BF16)
            acc[g] = acc[g] + _dot(a, wd_ref[c0:c1, :])
    for g, r in enumerate(groups):
        o_ref[r, :] = acc[g]


def _tail(x2, ya, yb, w_out, g_xattn, w_xq, g_xq, km, vm, w_xo, g_ffn, w_gu, w_down, seq):
    n, d = x2.shape
    tm = ROW_TILE
    d_ff = w_down.shape[0]
    assert d_ff % MXU_WIDTH == 0
    ff_bounds = (0, (d_ff // MXU_WIDTH + 1) // 2 * MXU_WIDTH, d_ff)
    n_mem = km.shape[1]
    per_seq = seq // tm
    return pl.pallas_call(
        functools.partial(_tail_kernel, ff_bounds=ff_bounds),
        grid=(n // tm,),
        in_specs=[
            pl.BlockSpec((tm, d), lambda i: (i, 0)),
            pl.BlockSpec((tm, GDN_VAL_DIM), lambda i: (i, 0)),
            pl.BlockSpec((tm, DSA_DIM), lambda i: (i, 0)),
            _const_spec((GDN_VAL_DIM + DSA_DIM, d)),
            _const_spec((1, d)),
            _const_spec((d, XA_DIM)),
            _const_spec((XA_DIM, XA_DIM)),
            _const_spec((1, XA_DIM)),
            pl.BlockSpec((1, n_mem, XA_DIM), lambda i: (i // per_seq, 0, 0)),
            pl.BlockSpec((1, n_mem, XA_DIM), lambda i: (i // per_seq, 0, 0)),
            _const_spec((XA_DIM, d)),
            _const_spec((1, d)),
            _const_spec((d, 2 * d_ff)),
            _const_spec((d_ff, d)),
        ],
        out_specs=pl.BlockSpec((tm, d), lambda i: (i, 0)),
        out_shape=jax.ShapeDtypeStruct((n, d), F32),
        compiler_params=pltpu.CompilerParams(dimension_semantics=("arbitrary",), vmem_limit_bytes=VMEM_LIMIT),
        name="tail",
    )(x2, ya, yb, w_out.astype(BF16), g_xattn[None, :], w_xq.astype(BF16), _block_diag_ones(XA_DIM, XA_HD),
      jnp.tile(g_xq, XA_HEADS)[None, :], km, vm, w_xo.astype(BF16), g_ffn[None, :], w_gu.astype(BF16),
      w_down.astype(BF16))


def kernel(x, mem, g_mix, w_in, conv_w, a_log, dt_bias, g_gdn_out, g_q_dsa, g_k_dsa, w_out, g_xattn, g_mem, w_xq,
           w_xkv, g_xq, g_xk, w_xo, g_ffn, w_gu, w_down):
    batch, seq, d = x.shape
    n_mem = mem.shape[1]
    assert seq % ROW_TILE == 0 and seq % KEY_TILE == 0 and ROW_TILE % KEY_TILE == 0
    for l in range(g_mix.shape[0]):
        x2 = x.reshape(batch * seq, d)
        qkv, z, kn, rs, qT, vT, qiT, wT = _inproj(x2, g_mix[l], w_in[l], g_q_dsa[l], g_k_dsa[l], conv_w[l], seq)
        ya = _gdn(qkv, z, rs, a_log[l], dt_bias[l], g_gdn_out[l], batch, seq)
        yb = _dsa(qT, qiT, wT, kn, vT, rs, batch, seq)
        km, vm = _memkv(mem.reshape(batch * n_mem, d), g_mem[l], w_xkv[l], g_xk[l])
        km = km.reshape(batch, n_mem, XA_DIM)
        vm = vm.reshape(batch, n_mem, XA_DIM)
        x = _tail(x2, ya, yb, w_out[l], g_xattn[l], w_xq[l], g_xq[l], km, vm, w_xo[l], g_ffn[l], w_gu[l],
                  w_down[l], seq).reshape(batch, seq, d)
    return x
```

```python
import functools

import jax
import jax.numpy as jnp
import numpy as np
from jax import lax
from jax.experimental import pallas as pl
from jax.experimental.pallas import tpu as pltpu

F32 = jnp.float32
BF16 = jnp.bfloat16
I32 = jnp.int32

EPS = 1e-6
CHUNK = 64
CHUNK_SHIFT = CHUNK.bit_length() - 1
assert CHUNK == 1 << CHUNK_SHIFT
GDN_HEADS, GDN_DK, GDN_DV, CONV_K = 4, 128, 128, 4
DSA_HEADS, DSA_HD = 8, 64
IDX_HEADS, IDX_HD = 16, 64
TOPK_MAX = 256
XA_HEADS, XA_HD = 4, 128

GDN_KEY_DIM = GDN_HEADS * GDN_DK
GDN_VAL_DIM = GDN_HEADS * GDN_DV
GDN_CONV_CH = 2 * GDN_KEY_DIM + GDN_VAL_DIM
DSA_DIM = DSA_HEADS * DSA_HD
IDX_Q_DIM = IDX_HEADS * IDX_HD
XA_DIM = XA_HEADS * XA_HD

LANES = 128
SUBLANES = 8
MXU_WIDTH = 256
VMEM_LIMIT = 56 * 1024 * 1024

RS_IDXK = 0
RS_BETA = IDX_HD
RS_A = IDX_HD + GDN_HEADS

ROW_TILE = 512
ROW_PARTS = 2
KEY_TILE = 256
Q_BLOCK = 128
GDN_CHUNKS_PER_STEP = 4
GDN_SEQS_PER_STEP = 4

BISECT_ROUNDS = 20
COARSE_ROUNDS = 14
COARSE_SHIFT = 8
COUNT_CHAINS = 4
ATTEND_RUN = 4
KPOS_DIGIT_BITS = 4
SLOPE_PIECES = 3
LOG2E = 1.4426950408889634
NEG_BIG = -1e30


def _dot(a, b):
    return jnp.dot(a, b, preferred_element_type=F32)


def _dot_nt(a, b):
    return lax.dot_general(a, b, (((1,), (1,)), ((), ())), preferred_element_type=F32)


def _dot_f32(a, b):
    return jnp.dot(a, b, preferred_element_type=F32, precision=lax.Precision.HIGHEST)


def _silu(x):
    half = 0.5 * x
    return half + half * jnp.tanh(half)


def _rms_rows(x, g):
    ms = jnp.mean(x * x, axis=-1, keepdims=True)
    return x * lax.rsqrt(ms + EPS) * g


def _const_spec(shape):
    nd = len(shape)
    return pl.BlockSpec(shape, lambda *_: (0,) * nd, pipeline_mode=pl.Buffered(1))


def _block_diag_ones(n, blk):
    group = np.arange(n) // blk
    return jnp.asarray((group[:, None] == group[None, :]).astype(BF16))


def _inproj_kernel(x_ref, g_ref, wrow_ref, wcol_ref, bd_ref, gk_ref, gq_ref, cw_ref,
                   qkv_ref, z_ref, kn_ref, rs_ref, qT_ref, vT_ref, qiT_ref, wT_ref, xc_ref, *, tiles_per_seq):
    tm = x_ref.shape[0]

    @pl.when(pl.program_id(0) % tiles_per_seq == 0)
    def _start_of_sequence():
        xc_ref[0:SUBLANES, :] = jnp.zeros((SUBLANES, GDN_CONV_CH), F32)

    h = _rms_rows(x_ref[...], g_ref[...]).astype(BF16)
    c0, c1, c2, c3 = GDN_CONV_CH, GDN_CONV_CH + GDN_VAL_DIM, GDN_CONV_CH + GDN_VAL_DIM + DSA_DIM, \
        GDN_CONV_CH + GDN_VAL_DIM + DSA_DIM + LANES

    xc_ref[SUBLANES:SUBLANES + tm, :] = _dot(h, wrow_ref[:, 0:c0])

    zkr = _dot(h, wrow_ref[:, c0:c3])
    z_ref[...] = zkr[:, 0:c1 - c0]
    k = zkr[:, c1 - c0:c2 - c0]
    kms = _dot((k * k).astype(BF16), bd_ref[...]) * (1.0 / DSA_HD)
    kn_ref[...] = (k * lax.rsqrt(kms + EPS) * gk_ref[...]).astype(BF16)
    rs_ref[...] = zkr[:, c2 - c0:c3 - c0]

    r0, r1, r2, r3 = DSA_DIM, 2 * DSA_DIM, 2 * DSA_DIM + IDX_Q_DIM, 2 * DSA_DIM + IDX_Q_DIM + IDX_HEADS
    qvT = _dot_nt(wcol_ref[0:r1, :], h)
    qT = qvT[0:r0, :]
    qms = _dot(bd_ref[...], (qT * qT).astype(BF16)) * (1.0 / DSA_HD)
    qT_ref[...] = (qT * lax.rsqrt(qms + EPS) * gq_ref[...] * (DSA_HD ** -0.5 * LOG2E)).astype(BF16)
    vT = qvT[r0:r1, :].astype(BF16)
    for i in range(vT_ref.shape[0]):
        vT_ref[i] = vT[:, i * KEY_TILE:(i + 1) * KEY_TILE]
    qwT = _dot_nt(wcol_ref[r1:r3, :], h)
    qiT_ref[...] = (qwT[0:IDX_Q_DIM, :] * (IDX_HD ** -0.5)).astype(BF16)
    wT_ref[...] = qwT[IDX_Q_DIM:, :] * (IDX_HEADS ** -0.5)

    cw = cw_ref[...]
    conv = cw[CONV_K - 1:CONV_K, :] * xc_ref[SUBLANES:SUBLANES + tm, :]
    for j in range(CONV_K - 2, -1, -1):
        back = CONV_K - 1 - j
        conv = conv + cw[j:j + 1, :] * xc_ref[SUBLANES - back:SUBLANES - back + tm, :]
    xc_ref[0:SUBLANES, :] = xc_ref[tm:tm + SUBLANES, :]
    act = _silu(conv)
    for hd in range(GDN_HEADS):
        qs = slice(hd * GDN_DK, (hd + 1) * GDN_DK)
        ks = slice(GDN_KEY_DIM + hd * GDN_DK, GDN_KEY_DIM + (hd + 1) * GDN_DK)
        q, k = act[:, qs], act[:, ks]
        qkv_ref[:, qs] = q * lax.rsqrt(jnp.sum(q * q, axis=-1, keepdims=True) + EPS) * (GDN_DK ** -0.5)
        qkv_ref[:, ks] = k * lax.rsqrt(jnp.sum(k * k, axis=-1, keepdims=True) + EPS)
    qkv_ref[:, 2 * GDN_KEY_DIM:] = act[:, 2 * GDN_KEY_DIM:]


def _inproj(x2, g_mix, w_in, g_q, g_k, conv_w, seq):
    n, d = x2.shape
    tm = ROW_TILE
    sizes = (GDN_CONV_CH, GDN_VAL_DIM, GDN_HEADS, GDN_HEADS, DSA_DIM, DSA_DIM, DSA_DIM, IDX_Q_DIM, IDX_HD, IDX_HEADS)
    offs = [0]
    for s in sizes:
        offs.append(offs[-1] + s)
    (w_qkv, w_z, w_b, w_a, w_q, w_k, w_v, w_iq, w_ik, w_iw) = [w_in[:, offs[i]:offs[i + 1]] for i in range(len(sizes))]
    pad = jnp.zeros((d, LANES - IDX_HD - 2 * GDN_HEADS), w_in.dtype)
    w_row = jnp.concatenate([w_qkv, w_z, w_k, w_ik, w_b, w_a, pad], axis=1).astype(BF16)
    w_col = jnp.concatenate([w_q, w_v, w_iq, w_iw], axis=1).T.astype(BF16)
    bd = _block_diag_ones(DSA_DIM, DSA_HD)
    gk_row = jnp.tile(g_k, DSA_HEADS)[None, :]
    gq_col = jnp.tile(g_q, DSA_HEADS)[:, None]
    nrow, ncol = w_row.shape[1], w_col.shape[0]
    out_shape = (
        jax.ShapeDtypeStruct((n, GDN_CONV_CH), F32),
        jax.ShapeDtypeStruct((n, GDN_VAL_DIM), F32),
        jax.ShapeDtypeStruct((n, DSA_DIM), BF16),
        jax.ShapeDtypeStruct((n, LANES), F32),
        jax.ShapeDtypeStruct((DSA_DIM, n), BF16),
        jax.ShapeDtypeStruct((n // KEY_TILE, DSA_DIM, KEY_TILE), BF16),
        jax.ShapeDtypeStruct((IDX_Q_DIM, n), BF16),
        jax.ShapeDtypeStruct((IDX_HEADS, n), F32),
    )
    return pl.pallas_call(
        functools.partial(_inproj_kernel, tiles_per_seq=seq // tm),
        grid=(n // tm,),
        in_specs=[
            pl.BlockSpec((tm, d), lambda i: (i, 0)),
            _const_spec((1, d)),
            _const_spec((d, nrow)),
            _const_spec((ncol, d)),
            _const_spec((DSA_DIM, DSA_DIM)),
            _const_spec((1, DSA_DIM)),
            _const_spec((DSA_DIM, 1)),
            _const_spec((CONV_K, GDN_CONV_CH)),
        ],
        out_specs=(
            pl.BlockSpec((tm, GDN_CONV_CH), lambda i: (i, 0)),
            pl.BlockSpec((tm, GDN_VAL_DIM), lambda i: (i, 0)),
            pl.BlockSpec((tm, DSA_DIM), lambda i: (i, 0)),
            pl.BlockSpec((tm, LANES), lambda i: (i, 0)),
            pl.BlockSpec((DSA_DIM, tm), lambda i: (0, i)),
            pl.BlockSpec((tm // KEY_TILE, DSA_DIM, KEY_TILE), lambda i: (i, 0, 0)),
            pl.BlockSpec((IDX_Q_DIM, tm), lambda i: (0, i)),
            pl.BlockSpec((IDX_HEADS, tm), lambda i: (0, i)),
        ),
        out_shape=out_shape,
        scratch_shapes=[pltpu.VMEM((tm + SUBLANES, GDN_CONV_CH), F32)],
        compiler_params=pltpu.CompilerParams(dimension_semantics=("arbitrary",), vmem_limit_bytes=VMEM_LIMIT),
        name="inproj",
    )(x2, g_mix[None, :], w_row, w_col, bd, gk_row, gq_col, conv_w)


def _gdn_kernel(qkv_ref, z_ref, rs_ref, alog_ref, dtb_ref, gout_ref, y_ref, s_ref):
    nb, rows = qkv_ref.shape[0], qkv_ref.shape[1]
    nc = rows // CHUNK

    @pl.when(pl.program_id(1) == 0)
    def _start_of_sequence():
        s_ref[...] = jnp.zeros(s_ref.shape, F32)

    rs = rs_ref[...]
    beta_all = jax.nn.sigmoid(rs)
    sp_in = rs + dtb_ref[...]
    softplus = jnp.maximum(sp_in, 0.0) + jnp.log(1.0 + jnp.exp(-jnp.abs(sp_in)))
    g_all = -jnp.exp(alog_ref[...]) * softplus

    ri = lax.broadcasted_iota(I32, (CHUNK, CHUNK), 0)
    ci = lax.broadcasted_iota(I32, (CHUNK, CHUNK), 1)
    incl = ri >= ci
    strict = ri > ci
    ltri = incl.astype(F32)
    eye = (ri == ci).astype(F32)
    gout = gout_ref[...]

    chains = [(b, c, h) for c in range(nc) for b in range(nb) for h in range(GDN_HEADS)]
    dcum, dcum_t = {}, {}
    for c in range(nc):
        for b in range(nb):
            d = _dot_f32(ltri, g_all[b, c * CHUNK:(c + 1) * CHUNK, :])
            dcum[b, c] = d
            dcum_t[b, c] = d.T

    q16, k16, kb16, vb16, kw16, qdec16, kdect16, gamma, last = ([] for _ in range(9))
    for b, c, h in chains:
        r = slice(c * CHUNK, (c + 1) * CHUNK)
        q = qkv_ref[b, r, h * GDN_DK:(h + 1) * GDN_DK]
        k = qkv_ref[b, r, GDN_KEY_DIM + h * GDN_DK:GDN_KEY_DIM + (h + 1) * GDN_DK]
        v = qkv_ref[b, r, 2 * GDN_KEY_DIM + h * GDN_DV:2 * GDN_KEY_DIM + (h + 1) * GDN_DV]
        beta = beta_all[b, r, RS_BETA + h:RS_BETA + h + 1]
        d_col = dcum[b, c][:, RS_A + h:RS_A + h + 1]
        d_row = dcum_t[b, c][RS_A + h:RS_A + h + 1, :]
        d_last = dcum[b, c][CHUNK - 1:CHUNK, RS_A + h:RS_A + h + 1]
        e_col = jnp.exp(d_col)
        kb = k * beta
        gamma.append(jnp.exp(jnp.where(incl, d_col - d_row, -jnp.inf)))
        q16.append(q.astype(BF16))
        k16.append(k.astype(BF16))
        kb16.append(kb.astype(BF16))
        vb16.append((v * beta).astype(BF16))
        kw16.append((kb * e_col).astype(BF16))
        qdec16.append((q * e_col).astype(BF16))
        kdect16.append((k * jnp.exp(d_last - d_col)).T.astype(BF16))
        last.append(jnp.exp(d_last))

    n = len(chains)
    p16 = [(-jnp.where(strict, _dot_nt(kb16[i], k16[i]) * gamma[i], 0.0)).astype(BF16) for i in range(n)]
    qk16 = [(_dot_nt(q16[i], k16[i]) * gamma[i]).astype(BF16) for i in range(n)]
    t_mat = [eye + p16[i].astype(F32) for i in range(n)]
    p16 = [_dot(p16[i], p16[i]).astype(BF16) for i in range(n)]
    for _ in range(4):
        t_mat = [t_mat[i] + _dot(t_mat[i].astype(BF16), p16[i]) for i in range(n)]
        p16 = [_dot(p16[i], p16[i]).astype(BF16) for i in range(n)]
    t16 = [(t_mat[i] + _dot(t_mat[i].astype(BF16), p16[i])).astype(BF16) for i in range(n)]
    u = [_dot(t16[i], vb16[i]) for i in range(n)]
    w16 = [_dot(t16[i], kw16[i]).astype(BF16) for i in range(n)]

    per_chunk = nb * GDN_HEADS
    s = [s_ref[b, h] for b in range(nb) for h in range(GDN_HEADS)]
    for c in range(nc):
        ids = range(c * per_chunk, (c + 1) * per_chunk)
        s16 = [s[j].astype(BF16) for j in range(per_chunk)]
        v_new16 = [(u[i] - _dot(w16[i], s16[j])).astype(BF16) for j, i in enumerate(ids)]
        o_state = [_dot(qdec16[i], s16[j]) for j, i in enumerate(ids)]
        o = [o_state[j] + _dot(qk16[i], v_new16[j]) for j, i in enumerate(ids)]
        s = [s[j] * last[i] + _dot(kdect16[i], v_new16[j]) for j, i in enumerate(ids)]
        for j, i in enumerate(ids):
            b, _, h = chains[i]
            zc = z_ref[b, c * CHUNK:(c + 1) * CHUNK, h * GDN_DV:(h + 1) * GDN_DV]
            y = _rms_rows(o[j], gout) * _silu(zc)
            y_ref[b, c * CHUNK:(c + 1) * CHUNK, h * GDN_DV:(h + 1) * GDN_DV] = y.astype(y_ref.dtype)
    for j in range(per_chunk):
        s_ref[j // GDN_HEADS, j % GDN_HEADS] = s[j]


def _gdn(qkv, z, rs, a_log, dt_bias, g_out, batch, seq):
    n = qkv.shape[0]
    rows = GDN_CHUNKS_PER_STEP * CHUNK
    nb = GDN_SEQS_PER_STEP if batch % GDN_SEQS_PER_STEP == 0 else 1
    lane_vec = lambda v: jnp.zeros((1, LANES), F32).at[0, RS_A:RS_A + GDN_HEADS].set(v)
    per_seq = lambda a: a.reshape(batch, seq, a.shape[-1])
    block = lambda width: pl.BlockSpec((nb, rows, width), lambda b, i: (b, i, 0))
    y = pl.pallas_call(
        _gdn_kernel,
        grid=(batch // nb, seq // rows),
        in_specs=[
            block(GDN_CONV_CH),
            block(GDN_VAL_DIM),
            block(LANES),
            _const_spec((1, LANES)),
            _const_spec((1, LANES)),
            _const_spec((1, GDN_DV)),
        ],
        out_specs=block(GDN_VAL_DIM),
        out_shape=jax.ShapeDtypeStruct((batch, seq, GDN_VAL_DIM), BF16),
        scratch_shapes=[
            pltpu.VMEM((nb, GDN_HEADS, GDN_DK, GDN_DV), F32),
        ],
        compiler_params=pltpu.CompilerParams(dimension_semantics=("arbitrary", "arbitrary"), vmem_limit_bytes=VMEM_LIMIT),
        name="gdn",
    )(per_seq(qkv), per_seq(z), per_seq(rs), lane_vec(a_log), lane_vec(dt_bias), g_out[None, :])
    return y.reshape(n, GDN_VAL_DIM)


def _alibi_slope(h):
    return 2.0 ** (-8.0 * (h + 1) / DSA_HEADS)


def _order_key(x):
    bits = lax.bitcast_convert_type(x, I32)
    return (bits ^ ((bits >> 31) & jnp.int32(0x7FFFFFFF))) >> 1


def _order_key_to_f32(key):
    full = key << 1
    return lax.bitcast_convert_type(full ^ ((full >> 31) & jnp.int32(0x7FFFFFFF)), F32)


def _tiles_for_block(j):
    return (j * Q_BLOCK + Q_BLOCK + KEY_TILE - 1) // KEY_TILE


def _dsa_kernel(qT_ref, qiT_ref, wT_ref, kn_ref, vT_ref, rs_ref, kpos_ref, slope_ref, o_ref,
                idxk_scr, sc_scr, key_scr, coarse_scr, thr_scr, cnt_scr, acc_scr, *, topk, n_pos_bits):
    j = pl.program_id(1)

    @pl.when(j == 0)
    def _new_sequence():
        idxk_scr[...] = rs_ref[:, RS_IDXK:RS_IDXK + IDX_HD].astype(BF16)

    n_tiles = _tiles_for_block(j)
    lane = lax.broadcasted_iota(I32, (1, Q_BLOCK), 1)
    qpos = j * Q_BLOCK + lane
    key_limit = ((qpos >> CHUNK_SHIFT) + 1) << CHUNK_SHIFT
    row = lax.broadcasted_iota(I32, (KEY_TILE, Q_BLOCK), 0)

    w_t = wT_ref[...]
    qi_pairs = [jnp.concatenate([qiT_ref[(2 * p) * IDX_HD:(2 * p + 1) * IDX_HD, :],
                                 qiT_ref[(2 * p + 1) * IDX_HD:(2 * p + 2) * IDX_HD, :]], axis=1)
                for p in range(IDX_HEADS // 2)]

    def score_tiles(t, count, carry):
        s_max, s_min = carry
        r0 = pl.multiple_of(t * KEY_TILE, KEY_TILE)
        kt = idxk_scr[pl.ds(r0, count * KEY_TILE), :]
        accs = [jnp.zeros((KEY_TILE, Q_BLOCK), F32) for _ in range(count)]
        for p in range(IDX_HEADS // 2):
            d = _dot(kt, qi_pairs[p])
            for i in range(count):
                di = d[i * KEY_TILE:(i + 1) * KEY_TILE, :]
                accs[i] = accs[i] + jnp.maximum(di[:, :Q_BLOCK], 0.0) * w_t[2 * p:2 * p + 1, :]
                accs[i] = accs[i] + jnp.maximum(di[:, Q_BLOCK:], 0.0) * w_t[2 * p + 1:2 * p + 2, :]
        for i in range(count):
            admissible = r0 + i * KEY_TILE + row < key_limit
            stored = jnp.where(admissible, accs[i], -jnp.inf)
            sc_scr[t + i] = stored
            key = _order_key(stored)
            key_scr[t + i] = key
            coarse_scr[t + i] = (key >> COARSE_SHIFT).astype(F32)
            s_max = jnp.maximum(s_max, jnp.max(stored, axis=0, keepdims=True))
            s_min = jnp.minimum(s_min, jnp.min(jnp.where(admissible, accs[i], jnp.inf), axis=0, keepdims=True))
        return s_max, s_min

    stats = (jnp.full((1, Q_BLOCK), -jnp.inf, F32), jnp.full((1, Q_BLOCK), jnp.inf, F32))
    stats = lax.fori_loop(0, n_tiles // 2, lambda u, st: score_tiles(2 * u, 2, st), stats)
    s_max, s_min = lax.cond(n_tiles % 2 == 1, lambda st: score_tiles(n_tiles - 1, 1, st), lambda st: st, stats)

    def count_tiles(per_tile):
        cnt_scr[...] = jnp.zeros(cnt_scr.shape, F32)
        base = jnp.int32(0)
        arm = 1 << (sc_scr.shape[0].bit_length() - 1)
        while arm >= 1:
            has = (n_tiles & arm) != 0

            @pl.when(has)
            def _run(base=base, arm=arm):
                chains = [jnp.zeros((SUBLANES, Q_BLOCK), F32) for _ in range(COUNT_CHAINS)]
                for i in range(arm):
                    flags = per_tile(base + i)
                    for r in range(KEY_TILE // SUBLANES):
                        chains[r % COUNT_CHAINS] = chains[r % COUNT_CHAINS] + flags[r * SUBLANES:(r + 1) * SUBLANES, :]
                while len(chains) > 1:
                    chains = [chains[k] + chains[k + 1] for k in range(0, len(chains), 2)]
                cnt_scr[...] += chains[0]

            base = base + jnp.where(has, arm, 0)
            arm //= 2
        return jnp.sum(cnt_scr[...], axis=0, keepdims=True).astype(I32)

    def count_ge(thr):
        return count_tiles(lambda i: jnp.where(sc_scr[i] >= thr, 1.0, 0.0))

    def count_keys_below(key, tiles):
        one = jnp.int32(0x3F800000)
        chains = [jnp.zeros((SUBLANES, Q_BLOCK), F32) for _ in range(COUNT_CHAINS)]
        for i in range(tiles):
            below = lax.bitcast_convert_type(((key_scr[i] - key) >> 31) & one, F32)
            for r in range(KEY_TILE // SUBLANES):
                chains[r % COUNT_CHAINS] = chains[r % COUNT_CHAINS] + below[r * SUBLANES:(r + 1) * SUBLANES, :]
        while len(chains) > 1:
            chains = [chains[k] + chains[k + 1] for k in range(0, len(chains), 2)]
        return jnp.sum(chains[0], axis=0, keepdims=True)

    def search_with_tiles(tiles):
        most_below = float(tiles * KEY_TILE - topk)

        def bisect_step(_, st):
            lo, hi, mid, mid_key = st
            up, down = mid * 0.5 + hi * 0.5, lo * 0.5 + mid * 0.5
            up_key, down_key = _order_key(up), _order_key(down)
            take = count_keys_below(mid_key, tiles) <= most_below
            return (jnp.where(take, mid, lo), jnp.where(take, hi, mid),
                    jnp.where(take, up, down), jnp.where(take, up_key, down_key))

        def coarse_step(_, st):
            lo, hi, mid, mid_key = st
            bucket = mid_key >> COARSE_SHIFT
            edge = jnp.minimum(hi, _order_key_to_f32((bucket + 1) << COARSE_SHIFT))
            up, down = mid * 0.5 + hi * 0.5, lo * 0.5 + edge * 0.5
            up_key, down_key = _order_key(up), _order_key(down)
            c = bucket.astype(F32) + 0.5
            chains = [jnp.zeros((SUBLANES, Q_BLOCK), F32) for _ in range(COUNT_CHAINS)]
            for i in range(tiles):
                side = lax.clamp(-0.5, c - coarse_scr[i], 0.5)
                for r in range(KEY_TILE // SUBLANES):
                    chains[r % COUNT_CHAINS] = chains[r % COUNT_CHAINS] + side[r * SUBLANES:(r + 1) * SUBLANES, :]
            while len(chains) > 1:
                chains = [chains[k] + chains[k + 1] for k in range(0, len(chains), 2)]
            below = jnp.sum(chains[0], axis=0, keepdims=True) + 0.5 * (tiles * KEY_TILE)
            take = below <= most_below
            return (jnp.where(take, mid, lo), jnp.where(take, hi, edge),
                    jnp.where(take, up, down), jnp.where(take, up_key, down_key))

        def search(st):
            st = lax.fori_loop(0, COARSE_ROUNDS, coarse_step, st)
            return lax.fori_loop(COARSE_ROUNDS, BISECT_ROUNDS, bisect_step, st)

        return search

    above_max = s_max + jnp.maximum(jnp.abs(s_max) * 1e-6, 1e-30)
    first_mid = s_min * 0.5 + above_max * 0.5
    lo = lax.switch(n_tiles - 1, [search_with_tiles(n) for n in range(1, sc_scr.shape[0] + 1)],
                    (s_min, above_max, first_mid, _order_key(first_mid)))[0]
    few = key_limit < topk
    thr = jnp.where(few, jnp.finfo(F32).min, _order_key_to_f32(_order_key(lo)))
    cnt = jnp.where(few, topk, count_ge(thr))
    thr_scr[...] = thr

    @pl.when(jnp.max(jnp.abs(cnt - topk)) > 0)
    def _resolve():
        def max_below(thr):
            def tile(t, m):
                sc = sc_scr[t]
                return jnp.maximum(m, jnp.max(jnp.where(sc < thr, sc, -jnp.inf), axis=0, keepdims=True))
            return lax.fori_loop(0, n_tiles, tile, jnp.full((1, Q_BLOCK), -jnp.inf, F32))

        def lower(st):
            thr, cnt = st
            nxt = max_below(thr)
            short = cnt < topk
            return jnp.where(short, nxt, thr), jnp.where(short, count_ge(nxt), cnt)

        def min_above(thr):
            def tile(t, m):
                sc = sc_scr[t]
                return jnp.minimum(m, jnp.min(jnp.where(sc > thr, sc, jnp.inf), axis=0, keepdims=True))
            return lax.fori_loop(0, n_tiles, tile, jnp.full((1, Q_BLOCK), jnp.inf, F32))

        def count_eq_below(thr, pos_limit):
            def tile(t, c):
                r0 = t * KEY_TILE
                hit = (sc_scr[t] == thr) & (r0 + row < pos_limit)
                return c + jnp.sum(hit.astype(I32), axis=0, keepdims=True)
            return lax.fori_loop(0, n_tiles, tile, jnp.zeros((1, Q_BLOCK), I32))

        def body(st):
            thr, cnt = st
            nxt = min_above(thr)
            cnt_n = count_ge(nxt)
            active = cnt > topk
            advance = active & (cnt_n >= topk)
            tie = active & (cnt_n < topk)
            @pl.when(jnp.max(tie.astype(I32)) > 0)
            def _drop_surplus_ties():
                need = topk - cnt_n
                pos = jnp.zeros((1, Q_BLOCK), I32)
                for b in range(n_pos_bits - 1, -1, -1):
                    cand = pos + (1 << b)
                    pos = jnp.where(count_eq_below(thr, cand) < need, cand, pos)

                def drop_tile(t, carry):
                    r0 = t * KEY_TILE
                    sc = sc_scr[t]
                    sc_scr[t] = jnp.where(tie & (sc == thr) & (r0 + row > pos), -jnp.inf, sc)
                    return carry

                lax.fori_loop(0, n_tiles, drop_tile, 0)

            return jnp.where(advance, nxt, thr), jnp.where(advance, cnt_n, jnp.where(tie, topk, cnt))

        st = lax.while_loop(lambda st: jnp.min(st[1]) < topk, lower, (thr, cnt))
        thr_scr[...] = lax.while_loop(lambda st: jnp.max(st[1]) > topk, body, st)[0]

    thr = thr_scr[...]

    zero = jnp.zeros((DSA_HD, Q_BLOCK), BF16)
    q_pairs = []
    for p in range(DSA_HEADS // 2):
        a = qT_ref[(2 * p) * DSA_HD:(2 * p + 1) * DSA_HD, :]
        b = qT_ref[(2 * p + 1) * DSA_HD:(2 * p + 2) * DSA_HD, :]
        q_pairs.append(jnp.concatenate([jnp.concatenate([a, zero], axis=1),
                                        jnp.concatenate([zero, b], axis=1),
                                        slope_ref[p]], axis=0))
    acc_scr[...] = jnp.zeros(acc_scr.shape, F32)
    qposf = qpos.astype(F32)

    def qk_products(t):
        r0 = pl.multiple_of(t * KEY_TILE, KEY_TILE)
        kt = kn_ref[pl.ds(r0, KEY_TILE), :]
        kp = kpos_ref[pl.ds(r0, KEY_TILE), :]
        s2 = [_dot(jnp.concatenate([kt[:, p * 2 * DSA_HD:(p + 1) * 2 * DSA_HD], kp], axis=1), q_pairs[p])
              for p in range(DSA_HEADS // 2)]
        return t, sc_scr[t] >= thr, s2

    def softmax_pv(tile, ml, last_tile):
        t, sel, s2 = tile
        vt = vT_ref[t]
        if last_tile:
            after = jnp.maximum((t * KEY_TILE + row).astype(F32) - qposf, 0.0)
        m_all, l_all = ml
        m_rows, l_rows = [], []
        for h in range(DSA_HEADS):
            s = s2[h // 2][:, (h % 2) * Q_BLOCK:(h % 2 + 1) * Q_BLOCK]
            if last_tile:
                s = s - (2.0 * LOG2E * _alibi_slope(h)) * after
            s = jnp.where(sel, s, NEG_BIG)
            m_old = m_all[h:h + 1, :]
            m_new = jnp.maximum(m_old, jnp.max(s, axis=0, keepdims=True))
            alpha = jnp.exp2(m_old - m_new)
            pr = jnp.exp2(s - m_new)
            l_rows.append(alpha * l_all[h:h + 1, :] + jnp.sum(pr, axis=0, keepdims=True))
            m_rows.append(m_new)
            hs = slice(h * DSA_HD, (h + 1) * DSA_HD)
            acc_scr[hs, :] = alpha * acc_scr[hs, :] + _dot(vt[hs, :], pr.astype(BF16))
        return jnp.concatenate(m_rows, axis=0), jnp.concatenate(l_rows, axis=0)

    def attend_run(first, count, ml, ends_block):
        tiles = [qk_products(first + i) for i in range(count)]
        for i, tile in enumerate(tiles):
            ml = softmax_pv(tile, ml, ends_block and i == count - 1)
        return ml

    ml = (jnp.full((DSA_HEADS, Q_BLOCK), NEG_BIG, F32), jnp.zeros((DSA_HEADS, Q_BLOCK), F32))
    full_runs = (n_tiles - 1) // ATTEND_RUN
    ml = lax.fori_loop(0, full_runs, lambda u, ml: attend_run(u * ATTEND_RUN, ATTEND_RUN, ml, False), ml)
    rest = n_tiles - full_runs * ATTEND_RUN
    _, l_all = lax.switch(rest - 1, [(lambda ml, c=c: attend_run(n_tiles - c, c, ml, True))
                                     for c in range(1, ATTEND_RUN + 1)], ml)

    outs = [acc_scr[h * DSA_HD:(h + 1) * DSA_HD, :] / l_all[h:h + 1, :] for h in range(DSA_HEADS)]
    o_ref[...] = jnp.concatenate(outs, axis=0).T.astype(o_ref.dtype)


def _dsa(qT, qiT, wT, kn, vT, rs, batch, seq):
    n = kn.shape[0]
    blocks = seq // Q_BLOCK
    tiles = seq // KEY_TILE
    topk = min(TOPK_MAX, seq // 4)
    col_map = lambda b, j: (0, b * blocks + j)
    pos = np.arange(seq)
    digit = 1 << KPOS_DIGIT_BITS
    kpos = np.zeros((seq, LANES), np.float32)
    slope_rows = np.zeros((DSA_HEADS // 2, LANES, 2 * Q_BLOCK), np.float32)
    rest = np.asarray([LOG2E * _alibi_slope(h) for h in range(DSA_HEADS)], np.float32).reshape(DSA_HEADS // 2, 2)
    for i in range(SLOPE_PIECES):
        piece = rest.astype(BF16).astype(np.float32)
        rest = rest - piece
        cols = np.repeat(piece, Q_BLOCK, axis=1)
        kpos[:, 2 * i], kpos[:, 2 * i + 1] = pos // digit, pos % digit
        slope_rows[:, 2 * i, :], slope_rows[:, 2 * i + 1, :] = digit * cols, cols
    kpos, slope_rows = jnp.asarray(kpos.astype(BF16)), jnp.asarray(slope_rows.astype(BF16))
    return pl.pallas_call(
        functools.partial(_dsa_kernel, topk=topk, n_pos_bits=seq.bit_length()),
        grid=(batch, blocks),
        in_specs=[
            pl.BlockSpec((DSA_DIM, Q_BLOCK), col_map),
            pl.BlockSpec((IDX_Q_DIM, Q_BLOCK), col_map),
            pl.BlockSpec((IDX_HEADS, Q_BLOCK), col_map),
            pl.BlockSpec((seq, DSA_DIM), lambda b, j: (b, 0)),
            pl.BlockSpec((tiles, DSA_DIM, KEY_TILE), lambda b, j: (b, 0, 0)),
            pl.BlockSpec((seq, LANES), lambda b, j: (b, 0)),
            _const_spec((seq, LANES)),
            _const_spec((DSA_HEADS // 2, LANES, 2 * Q_BLOCK)),
        ],
        out_specs=pl.BlockSpec((Q_BLOCK, DSA_DIM), lambda b, j: (b * blocks + j, 0)),
        out_shape=jax.ShapeDtypeStruct((n, DSA_DIM), BF16),
        scratch_shapes=[
            pltpu.VMEM((seq, IDX_HD), BF16),
            pltpu.VMEM((tiles, KEY_TILE, Q_BLOCK), F32),
            pltpu.VMEM((tiles, KEY_TILE, Q_BLOCK), I32),
            pltpu.VMEM((tiles, KEY_TILE, Q_BLOCK), F32),
            pltpu.VMEM((1, Q_BLOCK), F32),
            pltpu.VMEM((SUBLANES, Q_BLOCK), F32),
            pltpu.VMEM((DSA_DIM, Q_BLOCK), F32),
        ],
        compiler_params=pltpu.CompilerParams(dimension_semantics=("arbitrary", "arbitrary"), vmem_limit_bytes=VMEM_LIMIT),
        name="dsa",
    )(qT, qiT, wT, kn, vT, rs, kpos, slope_rows)


def _memkv_kernel(mem_ref, g_ref, w_ref, bd_ref, gk_ref, k_ref, v_ref):
    h = _rms_rows(mem_ref[...], g_ref[...]).astype(BF16)
    kv = _dot(h, w_ref[...])
    k = kv[:, :XA_DIM]
    kms = _dot((k * k).astype(BF16), bd_ref[...]) * (1.0 / XA_HD)
    k_ref[...] = (k * lax.rsqrt(kms + EPS) * gk_ref[...]).astype(BF16)
    v_ref[...] = kv[:, XA_DIM:].astype(BF16)


def _memkv(mem2, g_mem, w_xkv, g_xk):
    n, d = mem2.shape
    tm = min(ROW_TILE, n)
    return pl.pallas_call(
        _memkv_kernel,
        grid=(n // tm,),
        in_specs=[
            pl.BlockSpec((tm, d), lambda i: (i, 0)),
            _const_spec((1, d)),
            _const_spec((d, 2 * XA_DIM)),
            _const_spec((XA_DIM, XA_DIM)),
            _const_spec((1, XA_DIM)),
        ],
        out_specs=(pl.BlockSpec((tm, XA_DIM), lambda i: (i, 0)), pl.BlockSpec((tm, XA_DIM), lambda i: (i, 0))),
        out_shape=(jax.ShapeDtypeStruct((n, XA_DIM), BF16), jax.ShapeDtypeStruct((n, XA_DIM), BF16)),
        compiler_params=pltpu.CompilerParams(dimension_semantics=("arbitrary",), vmem_limit_bytes=VMEM_LIMIT),
        name="memkv",
    )(mem2, g_mem[None, :], w_xkv.astype(BF16), _block_diag_ones(XA_DIM, XA_HD), jnp.tile(g_xk, XA_HEADS)[None, :])


def _tail_kernel(x_ref, ya_ref, yb_ref, wout_ref, gx_ref, wxq_ref, bd_ref, gxq_ref, km_ref, vm_ref, wxo_ref,
                 gf_ref, wgu_ref, wd_ref, o_ref, *, ff_bounds):
    tm = x_ref.shape[0]
    part = tm // ROW_PARTS
    groups = [slice(g * part, (g + 1) * part) for g in range(ROW_PARTS)]
    d_ff = wd_ref.shape[0]

    x1 = [x_ref[r, :] + _dot(jnp.concatenate([ya_ref[r, :], yb_ref[r, :]], axis=1), wout_ref[...]) for r in groups]

    qn = []
    for g in range(ROW_PARTS):
        h = _rms_rows(x1[g], gx_ref[...]).astype(BF16)
        q = _dot(h, wxq_ref[...])
        qms = _dot((q * q).astype(BF16), bd_ref[...]) * (1.0 / XA_HD)
        qn.append((q * lax.rsqrt(qms + EPS) * gxq_ref[...]).astype(BF16))

    attn = []
    for g in range(ROW_PARTS):
        heads = []
        for hd in range(XA_HEADS):
            hs = slice(hd * XA_HD, (hd + 1) * XA_HD)
            s = _dot_nt(qn[g][:, hs], km_ref[0, :, hs]) * (XA_HD ** -0.5)
            pr = jnp.exp(s - jnp.max(s, axis=-1, keepdims=True))
            o = _dot(pr.astype(BF16), vm_ref[0, :, hs]) / jnp.sum(pr, axis=-1, keepdims=True)
            heads.append(o.astype(BF16))
        attn.append(jnp.concatenate(heads, axis=1))

    x2 = [x1[g] + _dot(attn[g], wxo_ref[...]) for g in range(ROW_PARTS)]
    h = [_rms_rows(x2[g], gf_ref[...]).astype(BF16) for g in range(ROW_PARTS)]
    acc = x2
    for c0, c1 in zip(ff_bounds[:-1], ff_bounds[1:]):
        for g in range(ROW_PARTS):
            gate = _dot(h[g], wgu_ref[:, c0:c1])
            up = _dot(h[g], wgu_ref[:, d_ff + c0:d_ff + c1])
            a = (_silu(gate) * up).astype(BF16)
            acc[g] = acc[g] + _dot(a, wd_ref[c0:c1, :])
    for g, r in enumerate(groups):
        o_ref[r, :] = acc[g]


def _tail(x2, ya, yb, w_out, g_xattn, w_xq, g_xq, km, vm, w_xo, g_ffn, w_gu, w_down, seq):
    n, d = x2.shape
    tm = ROW_TILE
    d_ff = w_down.shape[0]
    assert d_ff % MXU_WIDTH == 0
    ff_bounds = (0, (d_ff // MXU_WIDTH + 1) // 2 * MXU_WIDTH, d_ff)
    n_mem = km.shape[1]
    per_seq = seq // tm
    return pl.pallas_call(
        functools.partial(_tail_kernel, ff_bounds=ff_bounds),
        grid=(n // tm,),
        in_specs=[
            pl.BlockSpec((tm, d), lambda i: (i, 0)),
            pl.BlockSpec((tm, GDN_VAL_DIM), lambda i: (i, 0)),
            pl.BlockSpec((tm, DSA_DIM), lambda i: (i, 0)),
            _const_spec((GDN_VAL_DIM + DSA_DIM, d)),
            _const_spec((1, d)),
            _const_spec((d, XA_DIM)),
            _const_spec((XA_DIM, XA_DIM)),
            _const_spec((1, XA_DIM)),
            pl.BlockSpec((1, n_mem, XA_DIM), lambda i: (i // per_seq, 0, 0)),
            pl.BlockSpec((1, n_mem, XA_DIM), lambda i: (i // per_seq, 0, 0)),
            _const_spec((XA_DIM, d)),
            _const_spec((1, d)),
            _const_spec((d, 2 * d_ff)),
            _const_spec((d_ff, d)),
        ],
        out_specs=pl.BlockSpec((tm, d), lambda i: (i, 0)),
        out_shape=jax.ShapeDtypeStruct((n, d), F32),
        compiler_params=pltpu.CompilerParams(dimension_semantics=("arbitrary",), vmem_limit_bytes=VMEM_LIMIT),
        name="tail",
    )(x2, ya, yb, w_out.astype(BF16), g_xattn[None, :], w_xq.astype(BF16), _block_diag_ones(XA_DIM, XA_HD),
      jnp.tile(g_xq, XA_HEADS)[None, :], km, vm, w_xo.astype(BF16), g_ffn[None, :], w_gu.astype(BF16),
      w_down.astype(BF16))


def kernel(x, mem, g_mix, w_in, conv_w, a_log, dt_bias, g_gdn_out, g_q_dsa, g_k_dsa, w_out, g_xattn, g_mem, w_xq,
           w_xkv, g_xq, g_xk, w_xo, g_ffn, w_gu, w_down):
    batch, seq, d = x.shape
    n_mem = mem.shape[1]
    assert seq % ROW_TILE == 0 and seq % KEY_TILE == 0 and ROW_TILE % KEY_TILE == 0
    for l in range(g_mix.shape[0]):
        x2 = x.reshape(batch * seq, d)
        qkv, z, kn, rs, qT, vT, qiT, wT = _inproj(x2, g_mix[l], w_in[l], g_q_dsa[l], g_k_dsa[l], conv_w[l], seq)
        ya = _gdn(qkv, z, rs, a_log[l], dt_bias[l], g_gdn_out[l], batch, seq)
        yb = _dsa(qT, qiT, wT, kn, vT, rs, batch, seq)
        km, vm = _memkv(mem.reshape(batch * n_mem, d), g_mem[l], w_xkv[l], g_xk[l])
        km = km.reshape(batch, n_mem, XA_DIM)
        vm = vm.reshape(batch, n_mem, XA_DIM)
        x = _tail(x2, ya, yb, w_out[l], g_xattn[l], w_xq[l], g_xq[l], km, vm, w_xo[l], g_ffn[l], w_gu[l],
                  w_down[l], seq).reshape(batch, seq, d)
    return x
```

```python
import functools

import jax
import jax.numpy as jnp
import numpy as np
from jax import lax
from jax.experimental import pallas as pl
from jax.experimental.pallas import tpu as pltpu

F32 = jnp.float32
BF16 = jnp.bfloat16
I32 = jnp.int32

EPS = 1e-6
CHUNK = 64
CHUNK_SHIFT = CHUNK.bit_length() - 1
assert CHUNK == 1 << CHUNK_SHIFT
GDN_HEADS, GDN_DK, GDN_DV, CONV_K = 4, 128, 128, 4
DSA_HEADS, DSA_HD = 8, 64
IDX_HEADS, IDX_HD = 16, 64
TOPK_MAX = 256
XA_HEADS, XA_HD = 4, 128

GDN_KEY_DIM = GDN_HEADS * GDN_DK
GDN_VAL_DIM = GDN_HEADS * GDN_DV
GDN_CONV_CH = 2 * GDN_KEY_DIM + GDN_VAL_DIM
DSA_DIM = DSA_HEADS * DSA_HD
IDX_Q_DIM = IDX_HEADS * IDX_HD
XA_DIM = XA_HEADS * XA_HD

LANES = 128
SUBLANES = 8
MXU_WIDTH = 256
VMEM_LIMIT = 56 * 1024 * 1024

RS_IDXK = 0
RS_BETA = IDX_HD
RS_A = IDX_HD + GDN_HEADS

ROW_TILE = 512
ROW_PARTS = 2
KEY_TILE = 256
Q_BLOCK = 128
GDN_CHUNKS_PER_STEP = 4
GDN_SEQS_PER_STEP = 4

BISECT_ROUNDS = 20
COARSE_ROUNDS = 14
COARSE_SHIFT = 8
COUNT_CHAINS = 4
SCORE_RUN = 4
ATTEND_RUN = 4
KPOS_DIGIT_BITS = 4
SLOPE_PIECES = 3
LOG2E = 1.4426950408889634
NEG_BIG = -1e30


def _dot(a, b):
    return jnp.dot(a, b, preferred_element_type=F32)


def _dot_nt(a, b):
    return lax.dot_general(a, b, (((1,), (1,)), ((), ())), preferred_element_type=F32)


def _dot_f32(a, b):
    return jnp.dot(a, b, preferred_element_type=F32, precision=lax.Precision.HIGHEST)


def _silu(x):
    half = 0.5 * x
    return half + half * jnp.tanh(half)


def _rms_rows(x, g):
    ms = jnp.mean(x * x, axis=-1, keepdims=True)
    return x * lax.rsqrt(ms + EPS) * g


def _const_spec(shape):
    nd = len(shape)
    return pl.BlockSpec(shape, lambda *_: (0,) * nd, pipeline_mode=pl.Buffered(1))


def _block_diag_ones(n, blk):
    group = np.arange(n) // blk
    return jnp.asarray((group[:, None] == group[None, :]).astype(BF16))


def _inproj_kernel(x_ref, g_ref, wrow_ref, wcol_ref, bd_ref, gk_ref, gq_ref, cw_ref,
                   qkv_ref, z_ref, kn_ref, rs_ref, qT_ref, vT_ref, qiT_ref, wT_ref, xc_ref, *, tiles_per_seq):
    tm = x_ref.shape[0]

    @pl.when(pl.program_id(0) % tiles_per_seq == 0)
    def _start_of_sequence():
        xc_ref[0:SUBLANES, :] = jnp.zeros((SUBLANES, GDN_CONV_CH), F32)

    h = _rms_rows(x_ref[...], g_ref[...]).astype(BF16)
    c0, c1, c2, c3 = GDN_CONV_CH, GDN_CONV_CH + GDN_VAL_DIM, GDN_CONV_CH + GDN_VAL_DIM + DSA_DIM, \
        GDN_CONV_CH + GDN_VAL_DIM + DSA_DIM + LANES

    xc_ref[SUBLANES:SUBLANES + tm, :] = _dot(h, wrow_ref[:, 0:c0])

    zkr = _dot(h, wrow_ref[:, c0:c3])
    z_ref[...] = zkr[:, 0:c1 - c0]
    k = zkr[:, c1 - c0:c2 - c0]
    kms = _dot((k * k).astype(BF16), bd_ref[...]) * (1.0 / DSA_HD)
    kn_ref[...] = (k * lax.rsqrt(kms + EPS) * gk_ref[...]).astype(BF16)
    rs_ref[...] = zkr[:, c2 - c0:c3 - c0]

    r0, r1, r2, r3 = DSA_DIM, 2 * DSA_DIM, 2 * DSA_DIM + IDX_Q_DIM, 2 * DSA_DIM + IDX_Q_DIM + IDX_HEADS
    qvT = _dot_nt(wcol_ref[0:r1, :], h)
    qT = qvT[0:r0, :]
    qms = _dot(bd_ref[...], (qT * qT).astype(BF16)) * (1.0 / DSA_HD)
    qT_ref[...] = (qT * lax.rsqrt(qms + EPS) * gq_ref[...] * (DSA_HD ** -0.5 * LOG2E)).astype(BF16)
    vT = qvT[r0:r1, :].astype(BF16)
    for i in range(vT_ref.shape[0]):
        vT_ref[i] = vT[:, i * KEY_TILE:(i + 1) * KEY_TILE]
    qwT = _dot_nt(wcol_ref[r1:r3, :], h)
    qiT_ref[...] = (qwT[0:IDX_Q_DIM, :] * (IDX_HD ** -0.5)).astype(BF16)
    wT_ref[...] = qwT[IDX_Q_DIM:, :] * (IDX_HEADS ** -0.5)

    cw = cw_ref[...]
    conv = cw[CONV_K - 1:CONV_K, :] * xc_ref[SUBLANES:SUBLANES + tm, :]
    for j in range(CONV_K - 2, -1, -1):
        back = CONV_K - 1 - j
        conv = conv + cw[j:j + 1, :] * xc_ref[SUBLANES - back:SUBLANES - back + tm, :]
    xc_ref[0:SUBLANES, :] = xc_ref[tm:tm + SUBLANES, :]
    act = _silu(conv)
    for hd in range(GDN_HEADS):
        qs = slice(hd * GDN_DK, (hd + 1) * GDN_DK)
        ks = slice(GDN_KEY_DIM + hd * GDN_DK, GDN_KEY_DIM + (hd + 1) * GDN_DK)
        q, k = act[:, qs], act[:, ks]
        qkv_ref[:, qs] = q * lax.rsqrt(jnp.sum(q * q, axis=-1, keepdims=True) + EPS) * (GDN_DK ** -0.5)
        qkv_ref[:, ks] = k * lax.rsqrt(jnp.sum(k * k, axis=-1, keepdims=True) + EPS)
    qkv_ref[:, 2 * GDN_KEY_DIM:] = act[:, 2 * GDN_KEY_DIM:]


def _inproj(x2, g_mix, w_in, g_q, g_k, conv_w, seq):
    n, d = x2.shape
    tm = ROW_TILE
    sizes = (GDN_CONV_CH, GDN_VAL_DIM, GDN_HEADS, GDN_HEADS, DSA_DIM, DSA_DIM, DSA_DIM, IDX_Q_DIM, IDX_HD, IDX_HEADS)
    offs = [0]
    for s in sizes:
        offs.append(offs[-1] + s)
    (w_qkv, w_z, w_b, w_a, w_q, w_k, w_v, w_iq, w_ik, w_iw) = [w_in[:, offs[i]:offs[i + 1]] for i in range(len(sizes))]
    pad = jnp.zeros((d, LANES - IDX_HD - 2 * GDN_HEADS), w_in.dtype)
    w_row = jnp.concatenate([w_qkv, w_z, w_k, w_ik, w_b, w_a, pad], axis=1).astype(BF16)
    w_col = jnp.concatenate([w_q, w_v, w_iq, w_iw], axis=1).T.astype(BF16)
    bd = _block_diag_ones(DSA_DIM, DSA_HD)
    gk_row = jnp.tile(g_k, DSA_HEADS)[None, :]
    gq_col = jnp.tile(g_q, DSA_HEADS)[:, None]
    nrow, ncol = w_row.shape[1], w_col.shape[0]
    out_shape = (
        jax.ShapeDtypeStruct((n, GDN_CONV_CH), F32),
        jax.ShapeDtypeStruct((n, GDN_VAL_DIM), F32),
        jax.ShapeDtypeStruct((n, DSA_DIM), BF16),
        jax.ShapeDtypeStruct((n, LANES), F32),
        jax.ShapeDtypeStruct((DSA_DIM, n), BF16),
        jax.ShapeDtypeStruct((n // KEY_TILE, DSA_DIM, KEY_TILE), BF16),
        jax.ShapeDtypeStruct((IDX_Q_DIM, n), BF16),
        jax.ShapeDtypeStruct((IDX_HEADS, n), F32),
    )
    return pl.pallas_call(
        functools.partial(_inproj_kernel, tiles_per_seq=seq // tm),
        grid=(n // tm,),
        in_specs=[
            pl.BlockSpec((tm, d), lambda i: (i, 0)),
            _const_spec((1, d)),
            _const_spec((d, nrow)),
            _const_spec((ncol, d)),
            _const_spec((DSA_DIM, DSA_DIM)),
            _const_spec((1, DSA_DIM)),
            _const_spec((DSA_DIM, 1)),
            _const_spec((CONV_K, GDN_CONV_CH)),
        ],
        out_specs=(
            pl.BlockSpec((tm, GDN_CONV_CH), lambda i: (i, 0)),
            pl.BlockSpec((tm, GDN_VAL_DIM), lambda i: (i, 0)),
            pl.BlockSpec((tm, DSA_DIM), lambda i: (i, 0)),
            pl.BlockSpec((tm, LANES), lambda i: (i, 0)),
            pl.BlockSpec((DSA_DIM, tm), lambda i: (0, i)),
            pl.BlockSpec((tm // KEY_TILE, DSA_DIM, KEY_TILE), lambda i: (i, 0, 0)),
            pl.BlockSpec((IDX_Q_DIM, tm), lambda i: (0, i)),
            pl.BlockSpec((IDX_HEADS, tm), lambda i: (0, i)),
        ),
        out_shape=out_shape,
        scratch_shapes=[pltpu.VMEM((tm + SUBLANES, GDN_CONV_CH), F32)],
        compiler_params=pltpu.CompilerParams(dimension_semantics=("arbitrary",), vmem_limit_bytes=VMEM_LIMIT),
        name="inproj",
    )(x2, g_mix[None, :], w_row, w_col, bd, gk_row, gq_col, conv_w)


def _gdn_kernel(qkv_ref, z_ref, rs_ref, alog_ref, dtb_ref, gout_ref, y_ref, s_ref):
    nb, rows = qkv_ref.shape[0], qkv_ref.shape[1]
    nc = rows // CHUNK

    @pl.when(pl.program_id(1) == 0)
    def _start_of_sequence():
        s_ref[...] = jnp.zeros(s_ref.shape, F32)

    rs = rs_ref[...]
    beta_all = jax.nn.sigmoid(rs)
    sp_in = rs + dtb_ref[...]
    softplus = jnp.maximum(sp_in, 0.0) + jnp.log(1.0 + jnp.exp(-jnp.abs(sp_in)))
    g_all = -jnp.exp(alog_ref[...]) * softplus

    ri = lax.broadcasted_iota(I32, (CHUNK, CHUNK), 0)
    ci = lax.broadcasted_iota(I32, (CHUNK, CHUNK), 1)
    incl = ri >= ci
    strict = ri > ci
    ltri = incl.astype(F32)
    eye = (ri == ci).astype(F32)
    gout = gout_ref[...]

    chains = [(b, c, h) for c in range(nc) for b in range(nb) for h in range(GDN_HEADS)]
    dcum, dcum_t = {}, {}
    for c in range(nc):
        for b in range(nb):
            d = _dot_f32(ltri, g_all[b, c * CHUNK:(c + 1) * CHUNK, :])
            dcum[b, c] = d
            dcum_t[b, c] = d.T

    q16, k16, kb16, vb16, kw16, qdec16, kdect16, gamma, last = ([] for _ in range(9))
    for b, c, h in chains:
        r = slice(c * CHUNK, (c + 1) * CHUNK)
        q = qkv_ref[b, r, h * GDN_DK:(h + 1) * GDN_DK]
        k = qkv_ref[b, r, GDN_KEY_DIM + h * GDN_DK:GDN_KEY_DIM + (h + 1) * GDN_DK]
        v = qkv_ref[b, r, 2 * GDN_KEY_DIM + h * GDN_DV:2 * GDN_KEY_DIM + (h + 1) * GDN_DV]
        beta = beta_all[b, r, RS_BETA + h:RS_BETA + h + 1]
        d_col = dcum[b, c][:, RS_A + h:RS_A + h + 1]
        d_row = dcum_t[b, c][RS_A + h:RS_A + h + 1, :]
        d_last = dcum[b, c][CHUNK - 1:CHUNK, RS_A + h:RS_A + h + 1]
        e_col = jnp.exp(d_col)
        kb = k * beta
        gamma.append(jnp.exp(jnp.where(incl, d_col - d_row, -jnp.inf)))
        q16.append(q.astype(BF16))
        k16.append(k.astype(BF16))
        kb16.append(kb.astype(BF16))
        vb16.append((v * beta).astype(BF16))
        kw16.append((kb * e_col).astype(BF16))
        qdec16.append((q * e_col).astype(BF16))
        kdect16.append((k * jnp.exp(d_last - d_col)).T.astype(BF16))
        last.append(jnp.exp(d_last))

    n = len(chains)
    p16 = [(-jnp.where(strict, _dot_nt(kb16[i], k16[i]) * gamma[i], 0.0)).astype(BF16) for i in range(n)]
    qk16 = [(_dot_nt(q16[i], k16[i]) * gamma[i]).astype(BF16) for i in range(n)]
    t_mat = [eye + p16[i].astype(F32) for i in range(n)]
    p16 = [_dot(p16[i], p16[i]).astype(BF16) for i in range(n)]
    for _ in range(4):
        t_mat = [t_mat[i] + _dot(t_mat[i].astype(BF16), p16[i]) for i in range(n)]
        p16 = [_dot(p16[i], p16[i]).astype(BF16) for i in range(n)]
    t16 = [(t_mat[i] + _dot(t_mat[i].astype(BF16), p16[i])).astype(BF16) for i in range(n)]
    u = [_dot(t16[i], vb16[i]) for i in range(n)]
    w16 = [_dot(t16[i], kw16[i]).astype(BF16) for i in range(n)]

    per_chunk = nb * GDN_HEADS
    s = [s_ref[b, h] for b in range(nb) for h in range(GDN_HEADS)]
    for c in range(nc):
        ids = range(c * per_chunk, (c + 1) * per_chunk)
        s16 = [s[j].astype(BF16) for j in range(per_chunk)]
        v_new16 = [(u[i] - _dot(w16[i], s16[j])).astype(BF16) for j, i in enumerate(ids)]
        o_state = [_dot(qdec16[i], s16[j]) for j, i in enumerate(ids)]
        o = [o_state[j] + _dot(qk16[i], v_new16[j]) for j, i in enumerate(ids)]
        s = [s[j] * last[i] + _dot(kdect16[i], v_new16[j]) for j, i in enumerate(ids)]
        for j, i in enumerate(ids):
            b, _, h = chains[i]
            zc = z_ref[b, c * CHUNK:(c + 1) * CHUNK, h * GDN_DV:(h + 1) * GDN_DV]
            y = _rms_rows(o[j], gout) * _silu(zc)
            y_ref[b, c * CHUNK:(c + 1) * CHUNK, h * GDN_DV:(h + 1) * GDN_DV] = y.astype(y_ref.dtype)
    for j in range(per_chunk):
        s_ref[j // GDN_HEADS, j % GDN_HEADS] = s[j]


def _gdn(qkv, z, rs, a_log, dt_bias, g_out, batch, seq):
    n = qkv.shape[0]
    rows = GDN_CHUNKS_PER_STEP * CHUNK
    nb = GDN_SEQS_PER_STEP if batch % GDN_SEQS_PER_STEP == 0 else 1
    lane_vec = lambda v: jnp.zeros((1, LANES), F32).at[0, RS_A:RS_A + GDN_HEADS].set(v)
    per_seq = lambda a: a.reshape(batch, seq, a.shape[-1])
    block = lambda width: pl.BlockSpec((nb, rows, width), lambda b, i: (b, i, 0))
    y = pl.pallas_call(
        _gdn_kernel,
        grid=(batch // nb, seq // rows),
        in_specs=[
            block(GDN_CONV_CH),
            block(GDN_VAL_DIM),
            block(LANES),
            _const_spec((1, LANES)),
            _const_spec((1, LANES)),
            _const_spec((1, GDN_DV)),
        ],
        out_specs=block(GDN_VAL_DIM),
        out_shape=jax.ShapeDtypeStruct((batch, seq, GDN_VAL_DIM), BF16),
        scratch_shapes=[
            pltpu.VMEM((nb, GDN_HEADS, GDN_DK, GDN_DV), F32),
        ],
        compiler_params=pltpu.CompilerParams(dimension_semantics=("arbitrary", "arbitrary"), vmem_limit_bytes=VMEM_LIMIT),
        name="gdn",
    )(per_seq(qkv), per_seq(z), per_seq(rs), lane_vec(a_log), lane_vec(dt_bias), g_out[None, :])
    return y.reshape(n, GDN_VAL_DIM)


def _alibi_slope(h):
    return 2.0 ** (-8.0 * (h + 1) / DSA_HEADS)


def _order_key(x):
    bits = lax.bitcast_convert_type(x, I32)
    return (bits ^ ((bits >> 31) & jnp.int32(0x7FFFFFFF))) >> 1


def _order_key_to_f32(key):
    full = key << 1
    return lax.bitcast_convert_type(full ^ ((full >> 31) & jnp.int32(0x7FFFFFFF)), F32)


def _tiles_for_block(j):
    return (j * Q_BLOCK + Q_BLOCK + KEY_TILE - 1) // KEY_TILE


def _dsa_kernel(qT_ref, qiT_ref, wT_ref, kn_ref, vT_ref, rs_ref, kpos_ref, slope_ref, o_ref,
                idxk_scr, sc_scr, key_scr, coarse_scr, thr_scr, cnt_scr, acc_scr, *, topk, n_pos_bits):
    j = pl.program_id(1)

    @pl.when(j == 0)
    def _new_sequence():
        idxk_scr[...] = rs_ref[:, RS_IDXK:RS_IDXK + IDX_HD].astype(BF16)

    n_tiles = _tiles_for_block(j)
    lane = lax.broadcasted_iota(I32, (1, Q_BLOCK), 1)
    qpos = j * Q_BLOCK + lane
    key_limit = ((qpos >> CHUNK_SHIFT) + 1) << CHUNK_SHIFT
    row = lax.broadcasted_iota(I32, (KEY_TILE, Q_BLOCK), 0)

    w_t = wT_ref[...]
    qi_pairs = [jnp.concatenate([qiT_ref[(2 * p) * IDX_HD:(2 * p + 1) * IDX_HD, :],
                                 qiT_ref[(2 * p + 1) * IDX_HD:(2 * p + 2) * IDX_HD, :]], axis=1)
                for p in range(IDX_HEADS // 2)]

    def score_tiles(t, count, carry):
        s_max, s_min = carry
        r0 = pl.multiple_of(t * KEY_TILE, KEY_TILE)
        kt = idxk_scr[pl.ds(r0, count * KEY_TILE), :]
        accs = [jnp.zeros((KEY_TILE, Q_BLOCK), F32) for _ in range(count)]
        for p in range(IDX_HEADS // 2):
            d = _dot(kt, qi_pairs[p])
            for i in range(count):
                di = d[i * KEY_TILE:(i + 1) * KEY_TILE, :]
                accs[i] = accs[i] + jnp.maximum(di[:, :Q_BLOCK], 0.0) * w_t[2 * p:2 * p + 1, :]
                accs[i] = accs[i] + jnp.maximum(di[:, Q_BLOCK:], 0.0) * w_t[2 * p + 1:2 * p + 2, :]
        for i in range(count):
            admissible = r0 + i * KEY_TILE + row < key_limit
            stored = jnp.where(admissible, accs[i], -jnp.inf)
            sc_scr[t + i] = stored
            key = _order_key(stored)
            key_scr[t + i] = key
            coarse_scr[t + i] = (key >> COARSE_SHIFT).astype(F32)
            s_max = jnp.maximum(s_max, jnp.max(stored, axis=0, keepdims=True))
            s_min = jnp.minimum(s_min, jnp.min(jnp.where(admissible, accs[i], jnp.inf), axis=0, keepdims=True))
        return s_max, s_min

    stats = (jnp.full((1, Q_BLOCK), -jnp.inf, F32), jnp.full((1, Q_BLOCK), jnp.inf, F32))
    stats = lax.fori_loop(0, n_tiles // SCORE_RUN, lambda u, st: score_tiles(u * SCORE_RUN, SCORE_RUN, st), stats)
    rest = n_tiles % SCORE_RUN
    s_max, s_min = lax.switch(rest, [lambda st: st] + [(lambda st, c=c: score_tiles(n_tiles - c, c, st))
                                                       for c in range(1, SCORE_RUN)], stats)

    def count_tiles(per_tile):
        cnt_scr[...] = jnp.zeros(cnt_scr.shape, F32)
        base = jnp.int32(0)
        arm = 1 << (sc_scr.shape[0].bit_length() - 1)
        while arm >= 1:
            has = (n_tiles & arm) != 0

            @pl.when(has)
            def _run(base=base, arm=arm):
                chains = [jnp.zeros((SUBLANES, Q_BLOCK), F32) for _ in range(COUNT_CHAINS)]
                for i in range(arm):
                    flags = per_tile(base + i)
                    for r in range(KEY_TILE // SUBLANES):
                        chains[r % COUNT_CHAINS] = chains[r % COUNT_CHAINS] + flags[r * SUBLANES:(r + 1) * SUBLANES, :]
                while len(chains) > 1:
                    chains = [chains[k] + chains[k + 1] for k in range(0, len(chains), 2)]
                cnt_scr[...] += chains[0]

            base = base + jnp.where(has, arm, 0)
            arm //= 2
        return jnp.sum(cnt_scr[...], axis=0, keepdims=True).astype(I32)

    def count_ge(thr):
        return count_tiles(lambda i: jnp.where(sc_scr[i] >= thr, 1.0, 0.0))

    def count_keys_below(key, tiles):
        one = jnp.int32(0x3F800000)
        chains = [jnp.zeros((SUBLANES, Q_BLOCK), F32) for _ in range(COUNT_CHAINS)]
        for i in range(tiles):
            below = lax.bitcast_convert_type(((key_scr[i] - key) >> 31) & one, F32)
            for r in range(KEY_TILE // SUBLANES):
                chains[r % COUNT_CHAINS] = chains[r % COUNT_CHAINS] + below[r * SUBLANES:(r + 1) * SUBLANES, :]
        while len(chains) > 1:
            chains = [chains[k] + chains[k + 1] for k in range(0, len(chains), 2)]
        return jnp.sum(chains[0], axis=0, keepdims=True)

    def search_with_tiles(tiles):
        most_below = float(tiles * KEY_TILE - topk)

        def bisect_step(_, st):
            lo, hi, mid, mid_key = st
            up, down = mid * 0.5 + hi * 0.5, lo * 0.5 + mid * 0.5
            up_key, down_key = _order_key(up), _order_key(down)
            take = count_keys_below(mid_key, tiles) <= most_below
            return (jnp.where(take, mid, lo), jnp.where(take, hi, mid),
                    jnp.where(take, up, down), jnp.where(take, up_key, down_key))

        def coarse_step(_, st):
            lo, hi, mid, mid_key = st
            bucket = mid_key >> COARSE_SHIFT
            edge = jnp.minimum(hi, _order_key_to_f32((bucket + 1) << COARSE_SHIFT))
            up, down = mid * 0.5 + hi * 0.5, lo * 0.5 + edge * 0.5
            up_key, down_key = _order_key(up), _order_key(down)
            c = bucket.astype(F32) + 0.5
            chains = [jnp.zeros((SUBLANES, Q_BLOCK), F32) for _ in range(COUNT_CHAINS)]
            for i in range(tiles):
                side = lax.clamp(-0.5, c - coarse_scr[i], 0.5)
                for r in range(KEY_TILE // SUBLANES):
                    chains[r % COUNT_CHAINS] = chains[r % COUNT_CHAINS] + side[r * SUBLANES:(r + 1) * SUBLANES, :]
            while len(chains) > 1:
                chains = [chains[k] + chains[k + 1] for k in range(0, len(chains), 2)]
            below = jnp.sum(chains[0], axis=0, keepdims=True) + 0.5 * (tiles * KEY_TILE)
            take = below <= most_below
            return (jnp.where(take, mid, lo), jnp.where(take, hi, edge),
                    jnp.where(take, up, down), jnp.where(take, up_key, down_key))

        def search(st):
            st = lax.fori_loop(0, COARSE_ROUNDS, coarse_step, st)
            return lax.fori_loop(COARSE_ROUNDS, BISECT_ROUNDS, bisect_step, st)

        return search

    above_max = s_max + jnp.maximum(jnp.abs(s_max) * 1e-6, 1e-30)
    first_mid = s_min * 0.5 + above_max * 0.5
    lo = lax.switch(n_tiles - 1, [search_with_tiles(n) for n in range(1, sc_scr.shape[0] + 1)],
                    (s_min, above_max, first_mid, _order_key(first_mid)))[0]
    few = key_limit < topk
    thr = jnp.where(few, jnp.finfo(F32).min, _order_key_to_f32(_order_key(lo)))
    cnt = jnp.where(few, topk, count_ge(thr))
    thr_scr[...] = thr

    @pl.when(jnp.max(jnp.abs(cnt - topk)) > 0)
    def _resolve():
        def max_below(thr):
            def tile(t, m):
                sc = sc_scr[t]
                return jnp.maximum(m, jnp.max(jnp.where(sc < thr, sc, -jnp.inf), axis=0, keepdims=True))
            return lax.fori_loop(0, n_tiles, tile, jnp.full((1, Q_BLOCK), -jnp.inf, F32))

        def lower(st):
            thr, cnt = st
            nxt = max_below(thr)
            short = cnt < topk
            return jnp.where(short, nxt, thr), jnp.where(short, count_ge(nxt), cnt)

        def min_above(thr):
            def tile(t, m):
                sc = sc_scr[t]
                return jnp.minimum(m, jnp.min(jnp.where(sc > thr, sc, jnp.inf), axis=0, keepdims=True))
            return lax.fori_loop(0, n_tiles, tile, jnp.full((1, Q_BLOCK), jnp.inf, F32))

        def count_eq_below(thr, pos_limit):
            def tile(t, c):
                r0 = t * KEY_TILE
                hit = (sc_scr[t] == thr) & (r0 + row < pos_limit)
                return c + jnp.sum(hit.astype(I32), axis=0, keepdims=True)
            return lax.fori_loop(0, n_tiles, tile, jnp.zeros((1, Q_BLOCK), I32))

        def body(st):
            thr, cnt = st
            nxt = min_above(thr)
            cnt_n = count_ge(nxt)
            active = cnt > topk
            advance = active & (cnt_n >= topk)
            tie = active & (cnt_n < topk)
            @pl.when(jnp.max(tie.astype(I32)) > 0)
            def _drop_surplus_ties():
                need = topk - cnt_n
                pos = jnp.zeros((1, Q_BLOCK), I32)
                for b in range(n_pos_bits - 1, -1, -1):
                    cand = pos + (1 << b)
                    pos = jnp.where(count_eq_below(thr, cand) < need, cand, pos)

                def drop_tile(t, carry):
                    r0 = t * KEY_TILE
                    sc = sc_scr[t]
                    sc_scr[t] = jnp.where(tie & (sc == thr) & (r0 + row > pos), -jnp.inf, sc)
                    return carry

                lax.fori_loop(0, n_tiles, drop_tile, 0)

            return jnp.where(advance, nxt, thr), jnp.where(advance, cnt_n, jnp.where(tie, topk, cnt))

        st = lax.while_loop(lambda st: jnp.min(st[1]) < topk, lower, (thr, cnt))
        thr_scr[...] = lax.while_loop(lambda st: jnp.max(st[1]) > topk, body, st)[0]

    thr = thr_scr[...]

    zero = jnp.zeros((DSA_HD, Q_BLOCK), BF16)
    q_pairs = []
    for p in range(DSA_HEADS // 2):
        a = qT_ref[(2 * p) * DSA_HD:(2 * p + 1) * DSA_HD, :]
        b = qT_ref[(2 * p + 1) * DSA_HD:(2 * p + 2) * DSA_HD, :]
        q_pairs.append(jnp.concatenate([jnp.concatenate([a, zero], axis=1),
                                        jnp.concatenate([zero, b], axis=1),
                                        slope_ref[p]], axis=0))
    acc_scr[...] = jnp.zeros(acc_scr.shape, F32)
    qposf = qpos.astype(F32)

    def qk_products(t):
        r0 = pl.multiple_of(t * KEY_TILE, KEY_TILE)
        kt = kn_ref[pl.ds(r0, KEY_TILE), :]
        kp = kpos_ref[pl.ds(r0, KEY_TILE), :]
        s2 = [_dot(jnp.concatenate([kt[:, p * 2 * DSA_HD:(p + 1) * 2 * DSA_HD], kp], axis=1), q_pairs[p])
              for p in range(DSA_HEADS // 2)]
        return t, sc_scr[t] >= thr, s2

    def softmax_pv(tile, ml, last_tile):
        t, sel, s2 = tile
        vt = vT_ref[t]
        if last_tile:
            after = jnp.maximum((t * KEY_TILE + row).astype(F32) - qposf, 0.0)
        m_all, l_all = ml
        m_rows, l_rows = [], []
        for h in range(DSA_HEADS):
            s = s2[h // 2][:, (h % 2) * Q_BLOCK:(h % 2 + 1) * Q_BLOCK]
            if last_tile:
                s = s - (2.0 * LOG2E * _alibi_slope(h)) * after
            s = jnp.where(sel, s, NEG_BIG)
            m_old = m_all[h:h + 1, :]
            m_new = jnp.maximum(m_old, jnp.max(s, axis=0, keepdims=True))
            alpha = jnp.exp2(m_old - m_new)
            pr = jnp.exp2(s - m_new)
            l_rows.append(alpha * l_all[h:h + 1, :] + jnp.sum(pr, axis=0, keepdims=True))
            m_rows.append(m_new)
            hs = slice(h * DSA_HD, (h + 1) * DSA_HD)
            acc_scr[hs, :] = alpha * acc_scr[hs, :] + _dot(vt[hs, :], pr.astype(BF16))
        return jnp.concatenate(m_rows, axis=0), jnp.concatenate(l_rows, axis=0)

    def attend_run(first, count, ml, ends_block):
        tiles = [qk_products(first + i) for i in range(count)]
        for i, tile in enumerate(tiles):
            ml = softmax_pv(tile, ml, ends_block and i == count - 1)
        return ml

    ml = (jnp.full((DSA_HEADS, Q_BLOCK), NEG_BIG, F32), jnp.zeros((DSA_HEADS, Q_BLOCK), F32))
    full_runs = (n_tiles - 1) // ATTEND_RUN
    ml = lax.fori_loop(0, full_runs, lambda u, ml: attend_run(u * ATTEND_RUN, ATTEND_RUN, ml, False), ml)
    rest = n_tiles - full_runs * ATTEND_RUN
    _, l_all = lax.switch(rest - 1, [(lambda ml, c=c: attend_run(n_tiles - c, c, ml, True))
                                     for c in range(1, ATTEND_RUN + 1)], ml)

    outs = [acc_scr[h * DSA_HD:(h + 1) * DSA_HD, :] / l_all[h:h + 1, :] for h in range(DSA_HEADS)]
    o_ref[...] = jnp.concatenate(outs, axis=0).T.astype(o_ref.dtype)


def _dsa(qT, qiT, wT, kn, vT, rs, batch, seq):
    n = kn.shape[0]
    blocks = seq // Q_BLOCK
    tiles = seq // KEY_TILE
    topk = min(TOPK_MAX, seq // 4)
    col_map = lambda b, j: (0, b * blocks + j)
    pos = np.arange(seq)
    digit = 1 << KPOS_DIGIT_BITS
    kpos = np.zeros((seq, LANES), np.float32)
    slope_rows = np.zeros((DSA_HEADS // 2, LANES, 2 * Q_BLOCK), np.float32)
    rest = np.asarray([LOG2E * _alibi_slope(h) for h in range(DSA_HEADS)], np.float32).reshape(DSA_HEADS // 2, 2)
    for i in range(SLOPE_PIECES):
        piece = rest.astype(BF16).astype(np.float32)
        rest = rest - piece
        cols = np.repeat(piece, Q_BLOCK, axis=1)
        kpos[:, 2 * i], kpos[:, 2 * i + 1] = pos // digit, pos % digit
        slope_rows[:, 2 * i, :], slope_rows[:, 2 * i + 1, :] = digit * cols, cols
    kpos, slope_rows = jnp.asarray(kpos.astype(BF16)), jnp.asarray(slope_rows.astype(BF16))
    return pl.pallas_call(
        functools.partial(_dsa_kernel, topk=topk, n_pos_bits=seq.bit_length()),
        grid=(batch, blocks),
        in_specs=[
            pl.BlockSpec((DSA_DIM, Q_BLOCK), col_map),
            pl.BlockSpec((IDX_Q_DIM, Q_BLOCK), col_map),
            pl.BlockSpec((IDX_HEADS, Q_BLOCK), col_map),
            pl.BlockSpec((seq, DSA_DIM), lambda b, j: (b, 0)),
            pl.BlockSpec((tiles, DSA_DIM, KEY_TILE), lambda b, j: (b, 0, 0)),
            pl.BlockSpec((seq, LANES), lambda b, j: (b, 0)),
            _const_spec((seq, LANES)),
            _const_spec((DSA_HEADS // 2, LANES, 2 * Q_BLOCK)),
        ],
        out_specs=pl.BlockSpec((Q_BLOCK, DSA_DIM), lambda b, j: (b * blocks + j, 0)),
        out_shape=jax.ShapeDtypeStruct((n, DSA_DIM), BF16),
        scratch_shapes=[
            pltpu.VMEM((seq, IDX_HD), BF16),
            pltpu.VMEM((tiles, KEY_TILE, Q_BLOCK), F32),
            pltpu.VMEM((tiles, KEY_TILE, Q_BLOCK), I32),
            pltpu.VMEM((tiles, KEY_TILE, Q_BLOCK), F32),
            pltpu.VMEM((1, Q_BLOCK), F32),
            pltpu.VMEM((SUBLANES, Q_BLOCK), F32),
            pltpu.VMEM((DSA_DIM, Q_BLOCK), F32),
        ],
        compiler_params=pltpu.CompilerParams(dimension_semantics=("arbitrary", "arbitrary"), vmem_limit_bytes=VMEM_LIMIT),
        name="dsa",
    )(qT, qiT, wT, kn, vT, rs, kpos, slope_rows)


def _memkv_kernel(mem_ref, g_ref, w_ref, bd_ref, gk_ref, k_ref, v_ref):
    h = _rms_rows(mem_ref[...], g_ref[...]).astype(BF16)
    kv = _dot(h, w_ref[...])
    k = kv[:, :XA_DIM]
    kms = _dot((k * k).astype(BF16), bd_ref[...]) * (1.0 / XA_HD)
    k_ref[...] = (k * lax.rsqrt(kms + EPS) * gk_ref[...]).astype(BF16)
    v_ref[...] = kv[:, XA_DIM:].astype(BF16)


def _memkv(mem2, g_mem, w_xkv, g_xk):
    n, d = mem2.shape
    tm = min(ROW_TILE, n)
    return pl.pallas_call(
        _memkv_kernel,
        grid=(n // tm,),
        in_specs=[
            pl.BlockSpec((tm, d), lambda i: (i, 0)),
            _const_spec((1, d)),
            _const_spec((d, 2 * XA_DIM)),
            _const_spec((XA_DIM, XA_DIM)),
            _const_spec((1, XA_DIM)),
        ],
        out_specs=(pl.BlockSpec((tm, XA_DIM), lambda i: (i, 0)), pl.BlockSpec((tm, XA_DIM), lambda i: (i, 0))),
        out_shape=(jax.ShapeDtypeStruct((n, XA_DIM), BF16), jax.ShapeDtypeStruct((n, XA_DIM), BF16)),
        compiler_params=pltpu.CompilerParams(dimension_semantics=("arbitrary",), vmem_limit_bytes=VMEM_LIMIT),
        name="memkv",
    )(mem2, g_mem[None, :], w_xkv.astype(BF16), _block_diag_ones(XA_DIM, XA_HD), jnp.tile(g_xk, XA_HEADS)[None, :])


def _tail_kernel(x_ref, ya_ref, yb_ref, wout_ref, gx_ref, wxq_ref, bd_ref, gxq_ref, km_ref, vm_ref, wxo_ref,
                 gf_ref, wgu_ref, wd_ref, o_ref, *, ff_bounds):
    tm = x_ref.shape[0]
    part = tm // ROW_PARTS
    groups = [slice(g * part, (g + 1) * part) for g in range(ROW_PARTS)]
    d_ff = wd_ref.shape[0]

    x1 = [x_ref[r, :] + _dot(jnp.concatenate([ya_ref[r, :], yb_ref[r, :]], axis=1), wout_ref[...]) for r in groups]

    qn = []
    for g in range(ROW_PARTS):
        h = _rms_rows(x1[g], gx_ref[...]).astype(BF16)
        q = _dot(h, wxq_ref[...])
        qms = _dot((q * q).astype(BF16), bd_ref[...]) * (1.0 / XA_HD)
        qn.append((q * lax.rsqrt(qms + EPS) * gxq_ref[...]).astype(BF16))

    attn = []
    for g in range(ROW_PARTS):
        heads = []
        for hd in range(XA_HEADS):
            hs = slice(hd * XA_HD, (hd + 1) * XA_HD)
            s = _dot_nt(qn[g][:, hs], km_ref[0, :, hs]) * (XA_HD ** -0.5)
            pr = jnp.exp(s - jnp.max(s, axis=-1, keepdims=True))
            o = _dot(pr.astype(BF16), vm_ref[0, :, hs]) / jnp.sum(pr, axis=-1, keepdims=True)
            heads.append(o.astype(BF16))
        attn.append(jnp.concatenate(heads, axis=1))

    x2 = [x1[g] + _dot(attn[g], wxo_ref[...]) for g in range(ROW_PARTS)]
    h = [_rms_rows(x2[g], gf_ref[...]).astype(BF16) for g in range(ROW_PARTS)]
    acc = x2
    for c0, c1 in zip(ff_bounds[:-1], ff_bounds[1:]):
        for g in range(ROW_PARTS):
            gate = _dot(h[g], wgu_ref[:, c0:c1])
            up = _dot(h[g], wgu_ref[:, d_ff + c0:d_ff + c1])
            a = (_silu(gate) * up).astype(BF16)
            acc[g] = acc[g] + _dot(a, wd_ref[c0:c1, :])
    for g, r in enumerate(groups):
        o_ref[r, :] = acc[g]


def _tail(x2, ya, yb, w_out, g_xattn, w_xq, g_xq, km, vm, w_xo, g_ffn, w_gu, w_down, seq):
    n, d = x2.shape
    tm = ROW_TILE
    d_ff = w_down.shape[0]
    assert d_ff % MXU_WIDTH == 0
    ff_bounds = (0, (d_ff // MXU_WIDTH + 1) // 2 * MXU_WIDTH, d_ff)
    n_mem = km.shape[1]
    per_seq = seq // tm
    return pl.pallas_call(
        functools.partial(_tail_kernel, ff_bounds=ff_bounds),
        grid=(n // tm,),
        in_specs=[
            pl.BlockSpec((tm, d), lambda i: (i, 0)),
            pl.BlockSpec((tm, GDN_VAL_DIM), lambda i: (i, 0)),
            pl.BlockSpec((tm, DSA_DIM), lambda i: (i, 0)),
            _const_spec((GDN_VAL_DIM + DSA_DIM, d)),
            _const_spec((1, d)),
            _const_spec((d, XA_DIM)),
            _const_spec((XA_DIM, XA_DIM)),
            _const_spec((1, XA_DIM)),
            pl.BlockSpec((1, n_mem, XA_DIM), lambda i: (i // per_seq, 0, 0)),
            pl.BlockSpec((1, n_mem, XA_DIM), lambda i: (i // per_seq, 0, 0)),
            _const_spec((XA_DIM, d)),
            _const_spec((1, d)),
            _const_spec((d, 2 * d_ff)),
            _const_spec((d_ff, d)),
        ],
        out_specs=pl.BlockSpec((tm, d), lambda i: (i, 0)),
        out_shape=jax.ShapeDtypeStruct((n, d), F32),
        compiler_params=pltpu.CompilerParams(dimension_semantics=("arbitrary",), vmem_limit_bytes=VMEM_LIMIT),
        name="tail",
    )(x2, ya, yb, w_out.astype(BF16), g_xattn[None, :], w_xq.astype(BF16), _block_diag_ones(XA_DIM, XA_HD),
      jnp.tile(g_xq, XA_HEADS)[None, :], km, vm, w_xo.astype(BF16), g_ffn[None, :], w_gu.astype(BF16),
      w_down.astype(BF16))


def kernel(x, mem, g_mix, w_in, conv_w, a_log, dt_bias, g_gdn_out, g_q_dsa, g_k_dsa, w_out, g_xattn, g_mem, w_xq,
           w_xkv, g_xq, g_xk, w_xo, g_ffn, w_gu, w_down):
    batch, seq, d = x.shape
    n_mem = mem.shape[1]
    assert seq % ROW_TILE == 0 and seq % KEY_TILE == 0 and ROW_TILE % KEY_TILE == 0
    for l in range(g_mix.shape[0]):
        x2 = x.reshape(batch * seq, d)
        qkv, z, kn, rs, qT, vT, qiT, wT = _inproj(x2, g_mix[l], w_in[l], g_q_dsa[l], g_k_dsa[l], conv_w[l], seq)
        ya = _gdn(qkv, z, rs, a_log[l], dt_bias[l], g_gdn_out[l], batch, seq)
        yb = _dsa(qT, qiT, wT, kn, vT, rs, batch, seq)
        km, vm = _memkv(mem.reshape(batch * n_mem, d), g_mem[l], w_xkv[l], g_xk[l])
        km = km.reshape(batch, n_mem, XA_DIM)
        vm = vm.reshape(batch, n_mem, XA_DIM)
        x = _tail(x2, ya, yb, w_out[l], g_xattn[l], w_xq[l], g_xq[l], km, vm, w_xo[l], g_ffn[l], w_gu[l],
                  w_down[l], seq).reshape(batch, seq, d)
    return x
```

```python
import functools

import jax
import jax.numpy as jnp
import numpy as np
from jax import lax
from jax.experimental import pallas as pl
from jax.experimental.pallas import tpu as pltpu

F32 = jnp.float32
BF16 = jnp.bfloat16
I32 = jnp.int32

EPS = 1e-6
CHUNK = 64
CHUNK_SHIFT = CHUNK.bit_length() - 1
assert CHUNK == 1 << CHUNK_SHIFT
GDN_HEADS, GDN_DK, GDN_DV, CONV_K = 4, 128, 128, 4
DSA_HEADS, DSA_HD = 8, 64
IDX_HEADS, IDX_HD = 16, 64
TOPK_MAX = 256
XA_HEADS, XA_HD = 4, 128

GDN_KEY_DIM = GDN_HEADS * GDN_DK
GDN_VAL_DIM = GDN_HEADS * GDN_DV
GDN_CONV_CH = 2 * GDN_KEY_DIM + GDN_VAL_DIM
DSA_DIM = DSA_HEADS * DSA_HD
IDX_Q_DIM = IDX_HEADS * IDX_HD
XA_DIM = XA_HEADS * XA_HD

LANES = 128
SUBLANES = 8
MXU_WIDTH = 256
VMEM_LIMIT = 56 * 1024 * 1024

RS_IDXK = 0
RS_BETA = IDX_HD
RS_A = IDX_HD + GDN_HEADS

ROW_TILE = 512
ROW_PARTS = 2
KEY_TILE = 256
Q_BLOCK = 128
GDN_CHUNKS_PER_STEP = 4
GDN_SEQS_PER_STEP = 4

BISECT_ROUNDS = 20
COARSE_ROUNDS = 14
COARSE_SHIFT = 8
COUNT_CHAINS = 4
SCORE_RUN = 4
ATTEND_RUN = 4
KPOS_DIGIT_BITS = 4
SLOPE_PIECES = 3
LOG2E = 1.4426950408889634
NEG_BIG = -1e30


def _dot(a, b):
    return jnp.dot(a, b, preferred_element_type=F32)


def _dot_nt(a, b):
    return lax.dot_general(a, b, (((1,), (1,)), ((), ())), preferred_element_type=F32)


def _dot_f32(a, b):
    return jnp.dot(a, b, preferred_element_type=F32, precision=lax.Precision.HIGHEST)


def _silu(x):
    half = 0.5 * x
    return half + half * jnp.tanh(half)


def _rms_rows(x, g):
    ms = jnp.mean(x * x, axis=-1, keepdims=True)
    return x * lax.rsqrt(ms + EPS) * g


def _const_spec(shape):
    nd = len(shape)
    return pl.BlockSpec(shape, lambda *_: (0,) * nd, pipeline_mode=pl.Buffered(1))


def _block_diag_ones(n, blk):
    group = np.arange(n) // blk
    return jnp.asarray((group[:, None] == group[None, :]).astype(BF16))


def _inproj_kernel(x_ref, g_ref, wrow_ref, wcol_ref, bd_ref, gk_ref, gq_ref, cw_ref,
                   qkv_ref, z_ref, kn_ref, rs_ref, qT_ref, vT_ref, qiT_ref, wT_ref, xc_ref, *, tiles_per_seq):
    tm = x_ref.shape[0]

    @pl.when(pl.program_id(0) % tiles_per_seq == 0)
    def _start_of_sequence():
        xc_ref[0:SUBLANES, :] = jnp.zeros((SUBLANES, GDN_CONV_CH), F32)

    h = _rms_rows(x_ref[...], g_ref[...]).astype(BF16)
    c0, c1, c2, c3 = GDN_CONV_CH, GDN_CONV_CH + GDN_VAL_DIM, GDN_CONV_CH + GDN_VAL_DIM + DSA_DIM, \
        GDN_CONV_CH + GDN_VAL_DIM + DSA_DIM + LANES

    xc_ref[SUBLANES:SUBLANES + tm, :] = _dot(h, wrow_ref[:, 0:c0])

    zkr = _dot(h, wrow_ref[:, c0:c3])
    z_ref[...] = zkr[:, 0:c1 - c0]
    k = zkr[:, c1 - c0:c2 - c0]
    kms = _dot((k * k).astype(BF16), bd_ref[...]) * (1.0 / DSA_HD)
    kn_ref[...] = (k * lax.rsqrt(kms + EPS) * gk_ref[...]).astype(BF16)
    rs_ref[...] = zkr[:, c2 - c0:c3 - c0]

    r0, r1, r2, r3 = DSA_DIM, 2 * DSA_DIM, 2 * DSA_DIM + IDX_Q_DIM, 2 * DSA_DIM + IDX_Q_DIM + IDX_HEADS
    qvT = _dot_nt(wcol_ref[0:r1, :], h)
    qT = qvT[0:r0, :]
    qms = _dot(bd_ref[...], (qT * qT).astype(BF16)) * (1.0 / DSA_HD)
    qT_ref[...] = (qT * lax.rsqrt(qms + EPS) * gq_ref[...] * (DSA_HD ** -0.5 * LOG2E)).astype(BF16)
    vT = qvT[r0:r1, :].astype(BF16)
    for i in range(vT_ref.shape[0]):
        vT_ref[i] = vT[:, i * KEY_TILE:(i + 1) * KEY_TILE]
    qwT = _dot_nt(wcol_ref[r1:r3, :], h)
    qiT_ref[...] = (qwT[0:IDX_Q_DIM, :] * (IDX_HD ** -0.5)).astype(BF16)
    wT_ref[...] = qwT[IDX_Q_DIM:, :] * (IDX_HEADS ** -0.5)

    cw = cw_ref[...]
    conv = cw[CONV_K - 1:CONV_K, :] * xc_ref[SUBLANES:SUBLANES + tm, :]
    for j in range(CONV_K - 2, -1, -1):
        back = CONV_K - 1 - j
        conv = conv + cw[j:j + 1, :] * xc_ref[SUBLANES - back:SUBLANES - back + tm, :]
    xc_ref[0:SUBLANES, :] = xc_ref[tm:tm + SUBLANES, :]
    act = _silu(conv)
    for hd in range(GDN_HEADS):
        qs = slice(hd * GDN_DK, (hd + 1) * GDN_DK)
        ks = slice(GDN_KEY_DIM + hd * GDN_DK, GDN_KEY_DIM + (hd + 1) * GDN_DK)
        q, k = act[:, qs], act[:, ks]
        qkv_ref[:, qs] = q * lax.rsqrt(jnp.sum(q * q, axis=-1, keepdims=True) + EPS) * (GDN_DK ** -0.5)
        qkv_ref[:, ks] = k * lax.rsqrt(jnp.sum(k * k, axis=-1, keepdims=True) + EPS)
    qkv_ref[:, 2 * GDN_KEY_DIM:] = act[:, 2 * GDN_KEY_DIM:]


def _inproj(x2, g_mix, w_in, g_q, g_k, conv_w, seq):
    n, d = x2.shape
    tm = ROW_TILE
    sizes = (GDN_CONV_CH, GDN_VAL_DIM, GDN_HEADS, GDN_HEADS, DSA_DIM, DSA_DIM, DSA_DIM, IDX_Q_DIM, IDX_HD, IDX_HEADS)
    offs = [0]
    for s in sizes:
        offs.append(offs[-1] + s)
    (w_qkv, w_z, w_b, w_a, w_q, w_k, w_v, w_iq, w_ik, w_iw) = [w_in[:, offs[i]:offs[i + 1]] for i in range(len(sizes))]
    pad = jnp.zeros((d, LANES - IDX_HD - 2 * GDN_HEADS), w_in.dtype)
    w_row = jnp.concatenate([w_qkv, w_z, w_k, w_ik, w_b, w_a, pad], axis=1).astype(BF16)
    w_col = jnp.concatenate([w_q, w_v, w_iq, w_iw], axis=1).T.astype(BF16)
    bd = _block_diag_ones(DSA_DIM, DSA_HD)
    gk_row = jnp.tile(g_k, DSA_HEADS)[None, :]
    gq_col = jnp.tile(g_q, DSA_HEADS)[:, None]
    nrow, ncol = w_row.shape[1], w_col.shape[0]
    out_shape = (
        jax.ShapeDtypeStruct((n, GDN_CONV_CH), F32),
        jax.ShapeDtypeStruct((n, GDN_VAL_DIM), F32),
        jax.ShapeDtypeStruct((n, DSA_DIM), BF16),
        jax.ShapeDtypeStruct((n, LANES), F32),
        jax.ShapeDtypeStruct((DSA_DIM, n), BF16),
        jax.ShapeDtypeStruct((n // KEY_TILE, DSA_DIM, KEY_TILE), BF16),
        jax.ShapeDtypeStruct((IDX_Q_DIM, n), BF16),
        jax.ShapeDtypeStruct((IDX_HEADS, n), F32),
    )
    return pl.pallas_call(
        functools.partial(_inproj_kernel, tiles_per_seq=seq // tm),
        grid=(n // tm,),
        in_specs=[
            pl.BlockSpec((tm, d), lambda i: (i, 0)),
            _const_spec((1, d)),
            _const_spec((d, nrow)),
            _const_spec((ncol, d)),
            _const_spec((DSA_DIM, DSA_DIM)),
            _const_spec((1, DSA_DIM)),
            _const_spec((DSA_DIM, 1)),
            _const_spec((CONV_K, GDN_CONV_CH)),
        ],
        out_specs=(
            pl.BlockSpec((tm, GDN_CONV_CH), lambda i: (i, 0)),
            pl.BlockSpec((tm, GDN_VAL_DIM), lambda i: (i, 0)),
            pl.BlockSpec((tm, DSA_DIM), lambda i: (i, 0)),
            pl.BlockSpec((tm, LANES), lambda i: (i, 0)),
            pl.BlockSpec((DSA_DIM, tm), lambda i: (0, i)),
            pl.BlockSpec((tm // KEY_TILE, DSA_DIM, KEY_TILE), lambda i: (i, 0, 0)),
            pl.BlockSpec((IDX_Q_DIM, tm), lambda i: (0, i)),
            pl.BlockSpec((IDX_HEADS, tm), lambda i: (0, i)),
        ),
        out_shape=out_shape,
        scratch_shapes=[pltpu.VMEM((tm + SUBLANES, GDN_CONV_CH), F32)],
        compiler_params=pltpu.CompilerParams(dimension_semantics=("arbitrary",), vmem_limit_bytes=VMEM_LIMIT),
        name="inproj",
    )(x2, g_mix[None, :], w_row, w_col, bd, gk_row, gq_col, conv_w)


def _gdn_kernel(qkv_ref, z_ref, rs_ref, alog_ref, dtb_ref, gout_ref, y_ref, s_ref):
    nb, rows = qkv_ref.shape[0], qkv_ref.shape[1]
    nc = rows // CHUNK

    @pl.when(pl.program_id(1) == 0)
    def _start_of_sequence():
        s_ref[...] = jnp.zeros(s_ref.shape, F32)

    rs = rs_ref[...]
    beta_all = jax.nn.sigmoid(rs)
    sp_in = rs + dtb_ref[...]
    softplus = jnp.maximum(sp_in, 0.0) + jnp.log(1.0 + jnp.exp(-jnp.abs(sp_in)))
    g_all = -jnp.exp(alog_ref[...]) * softplus

    ri = lax.broadcasted_iota(I32, (CHUNK, CHUNK), 0)
    ci = lax.broadcasted_iota(I32, (CHUNK, CHUNK), 1)
    incl = ri >= ci
    strict = ri > ci
    ltri = incl.astype(F32)
    eye = (ri == ci).astype(F32)
    gout = gout_ref[...]

    chains = [(b, c, h) for c in range(nc) for b in range(nb) for h in range(GDN_HEADS)]
    dcum, dcum_t = {}, {}
    for c in range(nc):
        for b in range(nb):
            d = _dot_f32(ltri, g_all[b, c * CHUNK:(c + 1) * CHUNK, :])
            dcum[b, c] = d
            dcum_t[b, c] = d.T

    q16, k16, kb16, vb16, kw16, qdec16, kdect16, gamma, last = ([] for _ in range(9))
    for b, c, h in chains:
        r = slice(c * CHUNK, (c + 1) * CHUNK)
        q = qkv_ref[b, r, h * GDN_DK:(h + 1) * GDN_DK]
        k = qkv_ref[b, r, GDN_KEY_DIM + h * GDN_DK:GDN_KEY_DIM + (h + 1) * GDN_DK]
        v = qkv_ref[b, r, 2 * GDN_KEY_DIM + h * GDN_DV:2 * GDN_KEY_DIM + (h + 1) * GDN_DV]
        beta = beta_all[b, r, RS_BETA + h:RS_BETA + h + 1]
        d_col = dcum[b, c][:, RS_A + h:RS_A + h + 1]
        d_row = dcum_t[b, c][RS_A + h:RS_A + h + 1, :]
        d_last = dcum[b, c][CHUNK - 1:CHUNK, RS_A + h:RS_A + h + 1]
        e_col = jnp.exp(d_col)
        kb = k * beta
        gamma.append(jnp.exp(jnp.where(incl, d_col - d_row, -jnp.inf)))
        q16.append(q.astype(BF16))
        k16.append(k.astype(BF16))
        kb16.append(kb.astype(BF16))
        vb16.append((v * beta).astype(BF16))
        kw16.append((kb * e_col).astype(BF16))
        qdec16.append((q * e_col).astype(BF16))
        kdect16.append((k * jnp.exp(d_last - d_col)).T.astype(BF16))
        last.append(jnp.exp(d_last))

    n = len(chains)
    p16 = [(-jnp.where(strict, _dot_nt(kb16[i], k16[i]) * gamma[i], 0.0)).astype(BF16) for i in range(n)]
    qk16 = [(_dot_nt(q16[i], k16[i]) * gamma[i]).astype(BF16) for i in range(n)]
    t_mat = [eye + p16[i].astype(F32) for i in range(n)]
    p16 = [_dot(p16[i], p16[i]).astype(BF16) for i in range(n)]
    for _ in range(4):
        t_mat = [t_mat[i] + _dot(t_mat[i].astype(BF16), p16[i]) for i in range(n)]
        p16 = [_dot(p16[i], p16[i]).astype(BF16) for i in range(n)]
    t16 = [(t_mat[i] + _dot(t_mat[i].astype(BF16), p16[i])).astype(BF16) for i in range(n)]
    u = [_dot(t16[i], vb16[i]) for i in range(n)]
    w16 = [_dot(t16[i], kw16[i]).astype(BF16) for i in range(n)]

    per_chunk = nb * GDN_HEADS
    s = [s_ref[b, h] for b in range(nb) for h in range(GDN_HEADS)]
    for c in range(nc):
        ids = range(c * per_chunk, (c + 1) * per_chunk)
        s16 = [s[j].astype(BF16) for j in range(per_chunk)]
        v_new16 = [(u[i] - _dot(w16[i], s16[j])).astype(BF16) for j, i in enumerate(ids)]
        o_state = [_dot(qdec16[i], s16[j]) for j, i in enumerate(ids)]
        o = [o_state[j] + _dot(qk16[i], v_new16[j]) for j, i in enumerate(ids)]
        s = [s[j] * last[i] + _dot(kdect16[i], v_new16[j]) for j, i in enumerate(ids)]
        for j, i in enumerate(ids):
            b, _, h = chains[i]
            zc = z_ref[b, c * CHUNK:(c + 1) * CHUNK, h * GDN_DV:(h + 1) * GDN_DV]
            y = _rms_rows(o[j], gout) * _silu(zc)
            y_ref[b, c * CHUNK:(c + 1) * CHUNK, h * GDN_DV:(h + 1) * GDN_DV] = y.astype(y_ref.dtype)
    for j in range(per_chunk):
        s_ref[j // GDN_HEADS, j % GDN_HEADS] = s[j]


def _gdn(qkv, z, rs, a_log, dt_bias, g_out, batch, seq):
    n = qkv.shape[0]
    rows = GDN_CHUNKS_PER_STEP * CHUNK
    nb = GDN_SEQS_PER_STEP if batch % GDN_SEQS_PER_STEP == 0 else 1
    lane_vec = lambda v: jnp.zeros((1, LANES), F32).at[0, RS_A:RS_A + GDN_HEADS].set(v)
    per_seq = lambda a: a.reshape(batch, seq, a.shape[-1])
    block = lambda width: pl.BlockSpec((nb, rows, width), lambda b, i: (b, i, 0))
    y = pl.pallas_call(
        _gdn_kernel,
        grid=(batch // nb, seq // rows),
        in_specs=[
            block(GDN_CONV_CH),
            block(GDN_VAL_DIM),
            block(LANES),
            _const_spec((1, LANES)),
            _const_spec((1, LANES)),
            _const_spec((1, GDN_DV)),
        ],
        out_specs=block(GDN_VAL_DIM),
        out_shape=jax.ShapeDtypeStruct((batch, seq, GDN_VAL_DIM), BF16),
        scratch_shapes=[
            pltpu.VMEM((nb, GDN_HEADS, GDN_DK, GDN_DV), F32),
        ],
        compiler_params=pltpu.CompilerParams(dimension_semantics=("arbitrary", "arbitrary"), vmem_limit_bytes=VMEM_LIMIT),
        name="gdn",
    )(per_seq(qkv), per_seq(z), per_seq(rs), lane_vec(a_log), lane_vec(dt_bias), g_out[None, :])
    return y.reshape(n, GDN_VAL_DIM)


def _alibi_slope(h):
    return 2.0 ** (-8.0 * (h + 1) / DSA_HEADS)


def _order_key(x):
    bits = lax.bitcast_convert_type(x, I32)
    return (bits ^ ((bits >> 31) & jnp.int32(0x7FFFFFFF))) >> 1


def _order_key_to_f32(key):
    full = key << 1
    return lax.bitcast_convert_type(full ^ ((full >> 31) & jnp.int32(0x7FFFFFFF)), F32)


def _switch_in_two_levels(index, branches, operand, width=4):
    if len(branches) <= width:
        return lax.switch(index, branches, operand)
    groups = [branches[g:g + width] for g in range(0, len(branches), width)]
    return lax.switch(index // width,
                      [(lambda op, grp=grp, base=g * width: lax.switch(index - base, grp, op))
                       for g, grp in enumerate(groups)], operand)


def _tiles_for_block(j):
    return (j * Q_BLOCK + Q_BLOCK + KEY_TILE - 1) // KEY_TILE


def _dsa_kernel(qT_ref, qiT_ref, wT_ref, kn_ref, vT_ref, rs_ref, kpos_ref, slope_ref, o_ref,
                idxk_scr, sc_scr, key_scr, coarse_scr, thr_scr, cnt_scr, acc_scr, *, topk, n_pos_bits):
    j = pl.program_id(1)

    @pl.when(j == 0)
    def _new_sequence():
        idxk_scr[...] = rs_ref[:, RS_IDXK:RS_IDXK + IDX_HD].astype(BF16)

    n_tiles = _tiles_for_block(j)
    lane = lax.broadcasted_iota(I32, (1, Q_BLOCK), 1)
    qpos = j * Q_BLOCK + lane
    key_limit = ((qpos >> CHUNK_SHIFT) + 1) << CHUNK_SHIFT
    row = lax.broadcasted_iota(I32, (KEY_TILE, Q_BLOCK), 0)

    w_t = wT_ref[...]
    qi_pairs = [jnp.concatenate([qiT_ref[(2 * p) * IDX_HD:(2 * p + 1) * IDX_HD, :],
                                 qiT_ref[(2 * p + 1) * IDX_HD:(2 * p + 2) * IDX_HD, :]], axis=1)
                for p in range(IDX_HEADS // 2)]

    def score_tiles(t, count, carry):
        s_max, s_min = carry
        r0 = pl.multiple_of(t * KEY_TILE, KEY_TILE)
        kt = idxk_scr[pl.ds(r0, count * KEY_TILE), :]
        accs = [jnp.zeros((KEY_TILE, Q_BLOCK), F32) for _ in range(count)]
        for p in range(IDX_HEADS // 2):
            d = _dot(kt, qi_pairs[p])
            for i in range(count):
                di = d[i * KEY_TILE:(i + 1) * KEY_TILE, :]
                accs[i] = accs[i] + jnp.maximum(di[:, :Q_BLOCK], 0.0) * w_t[2 * p:2 * p + 1, :]
                accs[i] = accs[i] + jnp.maximum(di[:, Q_BLOCK:], 0.0) * w_t[2 * p + 1:2 * p + 2, :]
        for i in range(count):
            admissible = r0 + i * KEY_TILE + row < key_limit
            stored = jnp.where(admissible, accs[i], -jnp.inf)
            sc_scr[t + i] = stored
            key = _order_key(stored)
            key_scr[t + i] = key
            coarse_scr[t + i] = (key >> COARSE_SHIFT).astype(F32)
            s_max = jnp.maximum(s_max, jnp.max(stored, axis=0, keepdims=True))
            s_min = jnp.minimum(s_min, jnp.min(jnp.where(admissible, accs[i], jnp.inf), axis=0, keepdims=True))
        return s_max, s_min

    stats = (jnp.full((1, Q_BLOCK), -jnp.inf, F32), jnp.full((1, Q_BLOCK), jnp.inf, F32))
    stats = lax.fori_loop(0, n_tiles // SCORE_RUN, lambda u, st: score_tiles(u * SCORE_RUN, SCORE_RUN, st), stats)
    rest = n_tiles % SCORE_RUN
    s_max, s_min = lax.switch(rest, [lambda st: st] + [(lambda st, c=c: score_tiles(n_tiles - c, c, st))
                                                       for c in range(1, SCORE_RUN)], stats)

    def count_tiles(per_tile):
        cnt_scr[...] = jnp.zeros(cnt_scr.shape, F32)
        base = jnp.int32(0)
        arm = 1 << (sc_scr.shape[0].bit_length() - 1)
        while arm >= 1:
            has = (n_tiles & arm) != 0

            @pl.when(has)
            def _run(base=base, arm=arm):
                chains = [jnp.zeros((SUBLANES, Q_BLOCK), F32) for _ in range(COUNT_CHAINS)]
                for i in range(arm):
                    flags = per_tile(base + i)
                    for r in range(KEY_TILE // SUBLANES):
                        chains[r % COUNT_CHAINS] = chains[r % COUNT_CHAINS] + flags[r * SUBLANES:(r + 1) * SUBLANES, :]
                while len(chains) > 1:
                    chains = [chains[k] + chains[k + 1] for k in range(0, len(chains), 2)]
                cnt_scr[...] += chains[0]

            base = base + jnp.where(has, arm, 0)
            arm //= 2
        return jnp.sum(cnt_scr[...], axis=0, keepdims=True).astype(I32)

    def count_ge(thr):
        return count_tiles(lambda i: jnp.where(sc_scr[i] >= thr, 1.0, 0.0))

    groups_per_tile = KEY_TILE // Q_BLOCK

    def key_group(ref, g):
        return ref[g // groups_per_tile, (g % groups_per_tile) * Q_BLOCK:(g % groups_per_tile + 1) * Q_BLOCK, :]

    def sum_groups(per_group, groups):
        chains = [jnp.zeros((SUBLANES, Q_BLOCK), F32) for _ in range(COUNT_CHAINS)]
        for g in range(groups):
            flags = per_group(g)
            for r in range(Q_BLOCK // SUBLANES):
                chains[r % COUNT_CHAINS] = chains[r % COUNT_CHAINS] + flags[r * SUBLANES:(r + 1) * SUBLANES, :]
        while len(chains) > 1:
            chains = [chains[k] + chains[k + 1] for k in range(0, len(chains), 2)]
        return jnp.sum(chains[0], axis=0, keepdims=True)

    def count_keys_below(key, groups):
        one = jnp.int32(0x3F800000)
        return sum_groups(lambda g: lax.bitcast_convert_type(((key_group(key_scr, g) - key) >> 31) & one, F32), groups)

    def search_with_groups(groups):
        most_below = float(groups * Q_BLOCK - topk)

        def bisect_step(_, st):
            lo, hi, mid, mid_key = st
            up, down = mid * 0.5 + hi * 0.5, lo * 0.5 + mid * 0.5
            up_key, down_key = _order_key(up), _order_key(down)
            take = count_keys_below(mid_key, groups) <= most_below
            return (jnp.where(take, mid, lo), jnp.where(take, hi, mid),
                    jnp.where(take, up, down), jnp.where(take, up_key, down_key))

        def coarse_step(_, st):
            lo, hi, mid, mid_key = st
            bucket = mid_key >> COARSE_SHIFT
            edge = jnp.minimum(hi, _order_key_to_f32((bucket + 1) << COARSE_SHIFT))
            up, down = mid * 0.5 + hi * 0.5, lo * 0.5 + edge * 0.5
            up_key, down_key = _order_key(up), _order_key(down)
            c = bucket.astype(F32) + 0.5
            sides = sum_groups(lambda g: lax.clamp(-0.5, c - key_group(coarse_scr, g), 0.5), groups)
            take = sides + 0.5 * (groups * Q_BLOCK) <= most_below
            return (jnp.where(take, mid, lo), jnp.where(take, hi, edge),
                    jnp.where(take, up, down), jnp.where(take, up_key, down_key))

        def search(st):
            st = lax.fori_loop(0, COARSE_ROUNDS, coarse_step, st)
            return lax.fori_loop(COARSE_ROUNDS, BISECT_ROUNDS, bisect_step, st)

        return search

    above_max = s_max + jnp.maximum(jnp.abs(s_max) * 1e-6, 1e-30)
    first_mid = s_min * 0.5 + above_max * 0.5
    searches = [search_with_groups(n) for n in range(1, sc_scr.shape[0] * groups_per_tile + 1)]
    lo = _switch_in_two_levels(j, searches, (s_min, above_max, first_mid, _order_key(first_mid)))[0]
    few = key_limit < topk
    thr = jnp.where(few, jnp.finfo(F32).min, _order_key_to_f32(_order_key(lo)))
    cnt = jnp.where(few, topk, count_ge(thr))
    thr_scr[...] = thr

    @pl.when(jnp.max(jnp.abs(cnt - topk)) > 0)
    def _resolve():
        def max_below(thr):
            def tile(t, m):
                sc = sc_scr[t]
                return jnp.maximum(m, jnp.max(jnp.where(sc < thr, sc, -jnp.inf), axis=0, keepdims=True))
            return lax.fori_loop(0, n_tiles, tile, jnp.full((1, Q_BLOCK), -jnp.inf, F32))

        def lower(st):
            thr, cnt = st
            nxt = max_below(thr)
            short = cnt < topk
            return jnp.where(short, nxt, thr), jnp.where(short, count_ge(nxt), cnt)

        def min_above(thr):
            def tile(t, m):
                sc = sc_scr[t]
                return jnp.minimum(m, jnp.min(jnp.where(sc > thr, sc, jnp.inf), axis=0, keepdims=True))
            return lax.fori_loop(0, n_tiles, tile, jnp.full((1, Q_BLOCK), jnp.inf, F32))

        def count_eq_below(thr, pos_limit):
            def tile(t, c):
                r0 = t * KEY_TILE
                hit = (sc_scr[t] == thr) & (r0 + row < pos_limit)
                return c + jnp.sum(hit.astype(I32), axis=0, keepdims=True)
            return lax.fori_loop(0, n_tiles, tile, jnp.zeros((1, Q_BLOCK), I32))

        def body(st):
            thr, cnt = st
            nxt = min_above(thr)
            cnt_n = count_ge(nxt)
            active = cnt > topk
            advance = active & (cnt_n >= topk)
            tie = active & (cnt_n < topk)
            @pl.when(jnp.max(tie.astype(I32)) > 0)
            def _drop_surplus_ties():
                need = topk - cnt_n
                pos = jnp.zeros((1, Q_BLOCK), I32)
                for b in range(n_pos_bits - 1, -1, -1):
                    cand = pos + (1 << b)
                    pos = jnp.where(count_eq_below(thr, cand) < need, cand, pos)

                def drop_tile(t, carry):
                    r0 = t * KEY_TILE
                    sc = sc_scr[t]
                    sc_scr[t] = jnp.where(tie & (sc == thr) & (r0 + row > pos), -jnp.inf, sc)
                    return carry

                lax.fori_loop(0, n_tiles, drop_tile, 0)

            return jnp.where(advance, nxt, thr), jnp.where(advance, cnt_n, jnp.where(tie, topk, cnt))

        st = lax.while_loop(lambda st: jnp.min(st[1]) < topk, lower, (thr, cnt))
        thr_scr[...] = lax.while_loop(lambda st: jnp.max(st[1]) > topk, body, st)[0]

    thr = thr_scr[...]

    zero = jnp.zeros((DSA_HD, Q_BLOCK), BF16)
    q_pairs = []
    for p in range(DSA_HEADS // 2):
        a = qT_ref[(2 * p) * DSA_HD:(2 * p + 1) * DSA_HD, :]
        b = qT_ref[(2 * p + 1) * DSA_HD:(2 * p + 2) * DSA_HD, :]
        q_pairs.append(jnp.concatenate([jnp.concatenate([a, zero], axis=1),
                                        jnp.concatenate([zero, b], axis=1),
                                        slope_ref[p]], axis=0))
    acc_scr[...] = jnp.zeros(acc_scr.shape, F32)
    qposf = qpos.astype(F32)

    def qk_products(t, rows):
        r0 = pl.multiple_of(t * KEY_TILE, KEY_TILE)
        kt = kn_ref[pl.ds(r0, rows), :]
        kp = kpos_ref[pl.ds(r0, rows), :]
        s2 = [_dot(jnp.concatenate([kt[:, p * 2 * DSA_HD:(p + 1) * 2 * DSA_HD], kp], axis=1), q_pairs[p])
              for p in range(DSA_HEADS // 2)]
        return t, rows, sc_scr[t, 0:rows, :] >= thr, s2

    def softmax_pv(tile, ml, last_tile):
        t, rows, sel, s2 = tile
        vt = vT_ref[t, :, 0:rows]
        if last_tile:
            after = jnp.maximum((t * KEY_TILE + row[0:rows]).astype(F32) - qposf, 0.0)
        m_all, l_all = ml
        m_rows, l_rows = [], []
        for h in range(DSA_HEADS):
            s = s2[h // 2][:, (h % 2) * Q_BLOCK:(h % 2 + 1) * Q_BLOCK]
            if last_tile:
                s = s - (2.0 * LOG2E * _alibi_slope(h)) * after
            s = jnp.where(sel, s, NEG_BIG)
            m_old = m_all[h:h + 1, :]
            m_new = jnp.maximum(m_old, jnp.max(s, axis=0, keepdims=True))
            alpha = jnp.exp2(m_old - m_new)
            pr = jnp.exp2(s - m_new)
            l_rows.append(alpha * l_all[h:h + 1, :] + jnp.sum(pr, axis=0, keepdims=True))
            m_rows.append(m_new)
            hs = slice(h * DSA_HD, (h + 1) * DSA_HD)
            acc_scr[hs, :] = alpha * acc_scr[hs, :] + _dot(vt[hs, :], pr.astype(BF16))
        return jnp.concatenate(m_rows, axis=0), jnp.concatenate(l_rows, axis=0)

    def attend_run(first, count, ml, last_rows=None):
        tiles = [qk_products(first + i, last_rows if (last_rows and i == count - 1) else KEY_TILE)
                 for i in range(count)]
        for i, tile in enumerate(tiles):
            ml = softmax_pv(tile, ml, bool(last_rows) and i == count - 1)
        return ml

    ml = (jnp.full((DSA_HEADS, Q_BLOCK), NEG_BIG, F32), jnp.zeros((DSA_HEADS, Q_BLOCK), F32))
    full_runs = (n_tiles - 1) // ATTEND_RUN
    ml = lax.fori_loop(0, full_runs, lambda u, ml: attend_run(u * ATTEND_RUN, ATTEND_RUN, ml), ml)
    rest = n_tiles - full_runs * ATTEND_RUN
    last_groups = j % groups_per_tile
    endings = [(lambda ml, c=c, g=g: attend_run(n_tiles - c, c, ml, last_rows=(g + 1) * Q_BLOCK))
               for g in range(groups_per_tile) for c in range(1, ATTEND_RUN + 1)]
    _, l_all = _switch_in_two_levels(last_groups * ATTEND_RUN + rest - 1, endings, ml, width=ATTEND_RUN)

    outs = [acc_scr[h * DSA_HD:(h + 1) * DSA_HD, :] / l_all[h:h + 1, :] for h in range(DSA_HEADS)]
    o_ref[...] = jnp.concatenate(outs, axis=0).T.astype(o_ref.dtype)


def _dsa(qT, qiT, wT, kn, vT, rs, batch, seq):
    n = kn.shape[0]
    blocks = seq // Q_BLOCK
    tiles = seq // KEY_TILE
    topk = min(TOPK_MAX, seq // 4)
    col_map = lambda b, j: (0, b * blocks + j)
    pos = np.arange(seq)
    digit = 1 << KPOS_DIGIT_BITS
    kpos = np.zeros((seq, LANES), np.float32)
    slope_rows = np.zeros((DSA_HEADS // 2, LANES, 2 * Q_BLOCK), np.float32)
    rest = np.asarray([LOG2E * _alibi_slope(h) for h in range(DSA_HEADS)], np.float32).reshape(DSA_HEADS // 2, 2)
    for i in range(SLOPE_PIECES):
        piece = rest.astype(BF16).astype(np.float32)
        rest = rest - piece
        cols = np.repeat(piece, Q_BLOCK, axis=1)
        kpos[:, 2 * i], kpos[:, 2 * i + 1] = pos // digit, pos % digit
        slope_rows[:, 2 * i, :], slope_rows[:, 2 * i + 1, :] = digit * cols, cols
    kpos, slope_rows = jnp.asarray(kpos.astype(BF16)), jnp.asarray(slope_rows.astype(BF16))
    return pl.pallas_call(
        functools.partial(_dsa_kernel, topk=topk, n_pos_bits=seq.bit_length()),
        grid=(batch, blocks),
        in_specs=[
            pl.BlockSpec((DSA_DIM, Q_BLOCK), col_map),
            pl.BlockSpec((IDX_Q_DIM, Q_BLOCK), col_map),
            pl.BlockSpec((IDX_HEADS, Q_BLOCK), col_map),
            pl.BlockSpec((seq, DSA_DIM), lambda b, j: (b, 0)),
            pl.BlockSpec((tiles, DSA_DIM, KEY_TILE), lambda b, j: (b, 0, 0)),
            pl.BlockSpec((seq, LANES), lambda b, j: (b, 0)),
            _const_spec((seq, LANES)),
            _const_spec((DSA_HEADS // 2, LANES, 2 * Q_BLOCK)),
        ],
        out_specs=pl.BlockSpec((Q_BLOCK, DSA_DIM), lambda b, j: (b * blocks + j, 0)),
        out_shape=jax.ShapeDtypeStruct((n, DSA_DIM), BF16),
        scratch_shapes=[
            pltpu.VMEM((seq, IDX_HD), BF16),
            pltpu.VMEM((tiles, KEY_TILE, Q_BLOCK), F32),
            pltpu.VMEM((tiles, KEY_TILE, Q_BLOCK), I32),
            pltpu.VMEM((tiles, KEY_TILE, Q_BLOCK), F32),
            pltpu.VMEM((1, Q_BLOCK), F32),
            pltpu.VMEM((SUBLANES, Q_BLOCK), F32),
            pltpu.VMEM((DSA_DIM, Q_BLOCK), F32),
        ],
        compiler_params=pltpu.CompilerParams(dimension_semantics=("arbitrary", "arbitrary"), vmem_limit_bytes=VMEM_LIMIT),
        name="dsa",
    )(qT, qiT, wT, kn, vT, rs, kpos, slope_rows)


def _memkv_kernel(mem_ref, g_ref, w_ref, bd_ref, gk_ref, k_ref, v_ref):
    h = _rms_rows(mem_ref[...], g_ref[...]).astype(BF16)
    kv = _dot(h, w_ref[...])
    k = kv[:, :XA_DIM]
    kms = _dot((k * k).astype(BF16), bd_ref[...]) * (1.0 / XA_HD)
    k_ref[...] = (k * lax.rsqrt(kms + EPS) * gk_ref[...]).astype(BF16)
    v_ref[...] = kv[:, XA_DIM:].astype(BF16)


def _memkv(mem2, g_mem, w_xkv, g_xk):
    n, d = mem2.shape
    tm = min(ROW_TILE, n)
    return pl.pallas_call(
        _memkv_kernel,
        grid=(n // tm,),
        in_specs=[
            pl.BlockSpec((tm, d), lambda i: (i, 0)),
            _const_spec((1, d)),
            _const_spec((d, 2 * XA_DIM)),
            _const_spec((XA_DIM, XA_DIM)),
            _const_spec((1, XA_DIM)),
        ],
        out_specs=(pl.BlockSpec((tm, XA_DIM), lambda i: (i, 0)), pl.BlockSpec((tm, XA_DIM), lambda i: (i, 0))),
        out_shape=(jax.ShapeDtypeStruct((n, XA_DIM), BF16), jax.ShapeDtypeStruct((n, XA_DIM), BF16)),
        compiler_params=pltpu.CompilerParams(dimension_semantics=("arbitrary",), vmem_limit_bytes=VMEM_LIMIT),
        name="memkv",
    )(mem2, g_mem[None, :], w_xkv.astype(BF16), _block_diag_ones(XA_DIM, XA_HD), jnp.tile(g_xk, XA_HEADS)[None, :])


def _tail_kernel(x_ref, ya_ref, yb_ref, wout_ref, gx_ref, wxq_ref, bd_ref, gxq_ref, km_ref, vm_ref, wxo_ref,
                 gf_ref, wgu_ref, wd_ref, o_ref, *, ff_bounds):
    tm = x_ref.shape[0]
    part = tm // ROW_PARTS
    groups = [slice(g * part, (g + 1) * part) for g in range(ROW_PARTS)]
    d_ff = wd_ref.shape[0]

    x1 = [x_ref[r, :] + _dot(jnp.concatenate([ya_ref[r, :], yb_ref[r, :]], axis=1), wout_ref[...]) for r in groups]

    qn = []
    for g in range(ROW_PARTS):
        h = _rms_rows(x1[g], gx_ref[...]).astype(BF16)
        q = _dot(h, wxq_ref[...])
        qms = _dot((q * q).astype(BF16), bd_ref[...]) * (1.0 / XA_HD)
        qn.append((q * lax.rsqrt(qms + EPS) * gxq_ref[...]).astype(BF16))

    attn = []
    for g in range(ROW_PARTS):
        heads = []
        for hd in range(XA_HEADS):
            hs = slice(hd * XA_HD, (hd + 1) * XA_HD)
            s = _dot_nt(qn[g][:, hs], km_ref[0, :, hs]) * (XA_HD ** -0.5)
            pr = jnp.exp(s - jnp.max(s, axis=-1, keepdims=True))
            o = _dot(pr.astype(BF16), vm_ref[0, :, hs]) / jnp.sum(pr, axis=-1, keepdims=True)
            heads.append(o.astype(BF16))
        attn.append(jnp.concatenate(heads, axis=1))

    x2 = [x1[g] + _dot(attn[g], wxo_ref[...]) for g in range(ROW_PARTS)]
    h = [_rms_rows(x2[g], gf_ref[...]).astype(BF16) for g in range(ROW_PARTS)]
    acc = x2
    for c0, c1 in zip(ff_bounds[:-1], ff_bounds[1:]):
        for g in range(ROW_PARTS):
            gate = _dot(h[g], wgu_ref[:, c0:c1])
            up = _dot(h[g], wgu_ref[:, d_ff + c0:d_ff + c1])
            a = (_silu(gate) * up).astype(BF16)
            acc[g] = acc[g] + _dot(a, wd_ref[c0:c1, :])
    for g, r in enumerate(groups):
        o_ref[r, :] = acc[g]


def _tail(x2, ya, yb, w_out, g_xattn, w_xq, g_xq, km, vm, w_xo, g_ffn, w_gu, w_down, seq):
    n, d = x2.shape
    tm = ROW_TILE
    d_ff = w_down.shape[0]
    assert d_ff % MXU_WIDTH == 0
    ff_bounds = (0, (d_ff // MXU_WIDTH + 1) // 2 * MXU_WIDTH, d_ff)
    n_mem = km.shape[1]
    per_seq = seq // tm
    return pl.pallas_call(
        functools.partial(_tail_kernel, ff_bounds=ff_bounds),
        grid=(n // tm,),
        in_specs=[
            pl.BlockSpec((tm, d), lambda i: (i, 0)),
            pl.BlockSpec((tm, GDN_VAL_DIM), lambda i: (i, 0)),
            pl.BlockSpec((tm, DSA_DIM), lambda i: (i, 0)),
            _const_spec((GDN_VAL_DIM + DSA_DIM, d)),
            _const_spec((1, d)),
            _const_spec((d, XA_DIM)),
            _const_spec((XA_DIM, XA_DIM)),
            _const_spec((1, XA_DIM)),
            pl.BlockSpec((1, n_mem, XA_DIM), lambda i: (i // per_seq, 0, 0)),
            pl.BlockSpec((1, n_mem, XA_DIM), lambda i: (i // per_seq, 0, 0)),
            _const_spec((XA_DIM, d)),
            _const_spec((1, d)),
            _const_spec((d, 2 * d_ff)),
            _const_spec((d_ff, d)),
        ],
        out_specs=pl.BlockSpec((tm, d), lambda i: (i, 0)),
        out_shape=jax.ShapeDtypeStruct((n, d), F32),
        compiler_params=pltpu.CompilerParams(dimension_semantics=("arbitrary",), vmem_limit_bytes=VMEM_LIMIT),
        name="tail",
    )(x2, ya, yb, w_out.astype(BF16), g_xattn[None, :], w_xq.astype(BF16), _block_diag_ones(XA_DIM, XA_HD),
      jnp.tile(g_xq, XA_HEADS)[None, :], km, vm, w_xo.astype(BF16), g_ffn[None, :], w_gu.astype(BF16),
      w_down.astype(BF16))


def kernel(x, mem, g_mix, w_in, conv_w, a_log, dt_bias, g_gdn_out, g_q_dsa, g_k_dsa, w_out, g_xattn, g_mem, w_xq,
           w_xkv, g_xq, g_xk, w_xo, g_ffn, w_gu, w_down):
    batch, seq, d = x.shape
    n_mem = mem.shape[1]
    assert seq % ROW_TILE == 0 and seq % KEY_TILE == 0 and ROW_TILE % KEY_TILE == 0
    for l in range(g_mix.shape[0]):
        x2 = x.reshape(batch * seq, d)
        qkv, z, kn, rs, qT, vT, qiT, wT = _inproj(x2, g_mix[l], w_in[l], g_q_dsa[l], g_k_dsa[l], conv_w[l], seq)
        ya = _gdn(qkv, z, rs, a_log[l], dt_bias[l], g_gdn_out[l], batch, seq)
        yb = _dsa(qT, qiT, wT, kn, vT, rs, batch, seq)
        km, vm = _memkv(mem.reshape(batch * n_mem, d), g_mem[l], w_xkv[l], g_xk[l])
        km = km.reshape(batch, n_mem, XA_DIM)
        vm = vm.reshape(batch, n_mem, XA_DIM)
        x = _tail(x2, ya, yb, w_out[l], g_xattn[l], w_xq[l], g_xq[l], km, vm, w_xo[l], g_ffn[l], w_gu[l],
                  w_down[l], seq).reshape(batch, seq, d)
    return x
```

```python
import functools

import jax
import jax.numpy as jnp
import numpy as np
from jax import lax
from jax.experimental import pallas as pl
from jax.experimental.pallas import tpu as pltpu

F32 = jnp.float32
BF16 = jnp.bfloat16
I32 = jnp.int32

EPS = 1e-6
CHUNK = 64
CHUNK_SHIFT = CHUNK.bit_length() - 1
assert CHUNK == 1 << CHUNK_SHIFT
GDN_HEADS, GDN_DK, GDN_DV, CONV_K = 4, 128, 128, 4
DSA_HEADS, DSA_HD = 8, 64
IDX_HEADS, IDX_HD = 16, 64
TOPK_MAX = 256
XA_HEADS, XA_HD = 4, 128

GDN_KEY_DIM = GDN_HEADS * GDN_DK
GDN_VAL_DIM = GDN_HEADS * GDN_DV
GDN_CONV_CH = 2 * GDN_KEY_DIM + GDN_VAL_DIM
DSA_DIM = DSA_HEADS * DSA_HD
IDX_Q_DIM = IDX_HEADS * IDX_HD
XA_DIM = XA_HEADS * XA_HD

LANES = 128
SUBLANES = 8
MXU_WIDTH = 256
VMEM_LIMIT = 56 * 1024 * 1024

RS_IDXK = 0
RS_BETA = IDX_HD
RS_A = IDX_HD + GDN_HEADS

ROW_TILE = 512
ROW_PARTS = 2
KEY_TILE = 256
Q_BLOCK = 128
GDN_CHUNKS_PER_STEP = 4
GDN_SEQS_PER_STEP = 4

BISECT_ROUNDS = 19
COARSE_ROUNDS = 15
COARSE_SHIFT = 8
COUNT_CHAINS = 4
SCORE_RUN = 4
ATTEND_RUN = 4
KPOS_DIGIT_BITS = 4
SLOPE_PIECES = 3
LOG2E = 1.4426950408889634
NEG_BIG = -1e30


def _dot(a, b):
    return jnp.dot(a, b, preferred_element_type=F32)


def _dot_nt(a, b):
    return lax.dot_general(a, b, (((1,), (1,)), ((), ())), preferred_element_type=F32)


def _dot_f32(a, b):
    return jnp.dot(a, b, preferred_element_type=F32, precision=lax.Precision.HIGHEST)


def _silu(x):
    half = 0.5 * x
    return half + half * jnp.tanh(half)


def _rms_rows(x, g):
    ms = jnp.mean(x * x, axis=-1, keepdims=True)
    return x * lax.rsqrt(ms + EPS) * g


def _const_spec(shape):
    nd = len(shape)
    return pl.BlockSpec(shape, lambda *_: (0,) * nd, pipeline_mode=pl.Buffered(1))


def _block_diag_ones(n, blk):
    group = np.arange(n) // blk
    return jnp.asarray((group[:, None] == group[None, :]).astype(BF16))


def _inproj_kernel(x_ref, g_ref, wrow_ref, wcol_ref, bd_ref, gk_ref, gq_ref, cw_ref,
                   qkv_ref, z_ref, kn_ref, rs_ref, qT_ref, vT_ref, qiT_ref, wT_ref, xc_ref, *, tiles_per_seq):
    tm = x_ref.shape[0]

    @pl.when(pl.program_id(0) % tiles_per_seq == 0)
    def _start_of_sequence():
        xc_ref[0:SUBLANES, :] = jnp.zeros((SUBLANES, GDN_CONV_CH), F32)

    h = _rms_rows(x_ref[...], g_ref[...]).astype(BF16)
    c0, c1, c2, c3 = GDN_CONV_CH, GDN_CONV_CH + GDN_VAL_DIM, GDN_CONV_CH + GDN_VAL_DIM + DSA_DIM, \
        GDN_CONV_CH + GDN_VAL_DIM + DSA_DIM + LANES

    xc_ref[SUBLANES:SUBLANES + tm, :] = _dot(h, wrow_ref[:, 0:c0])

    zkr = _dot(h, wrow_ref[:, c0:c3])
    z_ref[...] = zkr[:, 0:c1 - c0]
    k = zkr[:, c1 - c0:c2 - c0]
    kms = _dot((k * k).astype(BF16), bd_ref[...]) * (1.0 / DSA_HD)
    kn_ref[...] = (k * lax.rsqrt(kms + EPS) * gk_ref[...]).astype(BF16)
    rs_ref[...] = zkr[:, c2 - c0:c3 - c0]

    r0, r1, r2, r3 = DSA_DIM, 2 * DSA_DIM, 2 * DSA_DIM + IDX_Q_DIM, 2 * DSA_DIM + IDX_Q_DIM + IDX_HEADS
    qvT = _dot_nt(wcol_ref[0:r1, :], h)
    qT = qvT[0:r0, :]
    qms = _dot(bd_ref[...], (qT * qT).astype(BF16)) * (1.0 / DSA_HD)
    qT_ref[...] = (qT * lax.rsqrt(qms + EPS) * gq_ref[...] * (DSA_HD ** -0.5 * LOG2E)).astype(BF16)
    vT = qvT[r0:r1, :].astype(BF16)
    for i in range(vT_ref.shape[0]):
        vT_ref[i] = vT[:, i * KEY_TILE:(i + 1) * KEY_TILE]
    qwT = _dot_nt(wcol_ref[r1:r3, :], h)
    qiT_ref[...] = (qwT[0:IDX_Q_DIM, :] * (IDX_HD ** -0.5)).astype(BF16)
    wT_ref[...] = qwT[IDX_Q_DIM:, :] * (IDX_HEADS ** -0.5)

    cw = cw_ref[...]
    conv = cw[CONV_K - 1:CONV_K, :] * xc_ref[SUBLANES:SUBLANES + tm, :]
    for j in range(CONV_K - 2, -1, -1):
        back = CONV_K - 1 - j
        conv = conv + cw[j:j + 1, :] * xc_ref[SUBLANES - back:SUBLANES - back + tm, :]
    xc_ref[0:SUBLANES, :] = xc_ref[tm:tm + SUBLANES, :]
    act = _silu(conv)
    for hd in range(GDN_HEADS):
        qs = slice(hd * GDN_DK, (hd + 1) * GDN_DK)
        ks = slice(GDN_KEY_DIM + hd * GDN_DK, GDN_KEY_DIM + (hd + 1) * GDN_DK)
        q, k = act[:, qs], act[:, ks]
        qkv_ref[:, qs] = q * lax.rsqrt(jnp.sum(q * q, axis=-1, keepdims=True) + EPS) * (GDN_DK ** -0.5)
        qkv_ref[:, ks] = k * lax.rsqrt(jnp.sum(k * k, axis=-1, keepdims=True) + EPS)
    qkv_ref[:, 2 * GDN_KEY_DIM:] = act[:, 2 * GDN_KEY_DIM:]


def _inproj(x2, g_mix, w_in, g_q, g_k, conv_w, seq):
    n, d = x2.shape
    tm = ROW_TILE
    sizes = (GDN_CONV_CH, GDN_VAL_DIM, GDN_HEADS, GDN_HEADS, DSA_DIM, DSA_DIM, DSA_DIM, IDX_Q_DIM, IDX_HD, IDX_HEADS)
    offs = [0]
    for s in sizes:
        offs.append(offs[-1] + s)
    (w_qkv, w_z, w_b, w_a, w_q, w_k, w_v, w_iq, w_ik, w_iw) = [w_in[:, offs[i]:offs[i + 1]] for i in range(len(sizes))]
    pad = jnp.zeros((d, LANES - IDX_HD - 2 * GDN_HEADS), w_in.dtype)
    w_row = jnp.concatenate([w_qkv, w_z, w_k, w_ik, w_b, w_a, pad], axis=1).astype(BF16)
    w_col = jnp.concatenate([w_q, w_v, w_iq, w_iw], axis=1).T.astype(BF16)
    bd = _block_diag_ones(DSA_DIM, DSA_HD)
    gk_row = jnp.tile(g_k, DSA_HEADS)[None, :]
    gq_col = jnp.tile(g_q, DSA_HEADS)[:, None]
    nrow, ncol = w_row.shape[1], w_col.shape[0]
    out_shape = (
        jax.ShapeDtypeStruct((n, GDN_CONV_CH), F32),
        jax.ShapeDtypeStruct((n, GDN_VAL_DIM), F32),
        jax.ShapeDtypeStruct((n, DSA_DIM), BF16),
        jax.ShapeDtypeStruct((n, LANES), F32),
        jax.ShapeDtypeStruct((DSA_DIM, n), BF16),
        jax.ShapeDtypeStruct((n // KEY_TILE, DSA_DIM, KEY_TILE), BF16),
        jax.ShapeDtypeStruct((IDX_Q_DIM, n), BF16),
        jax.ShapeDtypeStruct((IDX_HEADS, n), F32),
    )
    return pl.pallas_call(
        functools.partial(_inproj_kernel, tiles_per_seq=seq // tm),
        grid=(n // tm,),
        in_specs=[
            pl.BlockSpec((tm, d), lambda i: (i, 0)),
            _const_spec((1, d)),
            _const_spec((d, nrow)),
            _const_spec((ncol, d)),
            _const_spec((DSA_DIM, DSA_DIM)),
            _const_spec((1, DSA_DIM)),
            _const_spec((DSA_DIM, 1)),
            _const_spec((CONV_K, GDN_CONV_CH)),
        ],
        out_specs=(
            pl.BlockSpec((tm, GDN_CONV_CH), lambda i: (i, 0)),
            pl.BlockSpec((tm, GDN_VAL_DIM), lambda i: (i, 0)),
            pl.BlockSpec((tm, DSA_DIM), lambda i: (i, 0)),
            pl.BlockSpec((tm, LANES), lambda i: (i, 0)),
            pl.BlockSpec((DSA_DIM, tm), lambda i: (0, i)),
            pl.BlockSpec((tm // KEY_TILE, DSA_DIM, KEY_TILE), lambda i: (i, 0, 0)),
            pl.BlockSpec((IDX_Q_DIM, tm), lambda i: (0, i)),
            pl.BlockSpec((IDX_HEADS, tm), lambda i: (0, i)),
        ),
        out_shape=out_shape,
        scratch_shapes=[pltpu.VMEM((tm + SUBLANES, GDN_CONV_CH), F32)],
        compiler_params=pltpu.CompilerParams(dimension_semantics=("arbitrary",), vmem_limit_bytes=VMEM_LIMIT),
        name="inproj",
    )(x2, g_mix[None, :], w_row, w_col, bd, gk_row, gq_col, conv_w)


def _gdn_kernel(qkv_ref, z_ref, rs_ref, alog_ref, dtb_ref, gout_ref, y_ref, s_ref):
    nb, rows = qkv_ref.shape[0], qkv_ref.shape[1]
    nc = rows // CHUNK

    @pl.when(pl.program_id(1) == 0)
    def _start_of_sequence():
        s_ref[...] = jnp.zeros(s_ref.shape, F32)

    rs = rs_ref[...]
    beta_all = jax.nn.sigmoid(rs)
    sp_in = rs + dtb_ref[...]
    softplus = jnp.maximum(sp_in, 0.0) + jnp.log(1.0 + jnp.exp(-jnp.abs(sp_in)))
    g_all = -jnp.exp(alog_ref[...]) * softplus

    ri = lax.broadcasted_iota(I32, (CHUNK, CHUNK), 0)
    ci = lax.broadcasted_iota(I32, (CHUNK, CHUNK), 1)
    incl = ri >= ci
    strict = ri > ci
    ltri = incl.astype(F32)
    eye = (ri == ci).astype(F32)
    gout = gout_ref[...]

    chains = [(b, c, h) for c in range(nc) for b in range(nb) for h in range(GDN_HEADS)]
    dcum, dcum_t = {}, {}
    for c in range(nc):
        for b in range(nb):
            d = _dot_f32(ltri, g_all[b, c * CHUNK:(c + 1) * CHUNK, :])
            dcum[b, c] = d
            dcum_t[b, c] = d.T

    q16, k16, kb16, vb16, kw16, qdec16, kdect16, gamma, last = ([] for _ in range(9))
    for b, c, h in chains:
        r = slice(c * CHUNK, (c + 1) * CHUNK)
        q = qkv_ref[b, r, h * GDN_DK:(h + 1) * GDN_DK]
        k = qkv_ref[b, r, GDN_KEY_DIM + h * GDN_DK:GDN_KEY_DIM + (h + 1) * GDN_DK]
        v = qkv_ref[b, r, 2 * GDN_KEY_DIM + h * GDN_DV:2 * GDN_KEY_DIM + (h + 1) * GDN_DV]
        beta = beta_all[b, r, RS_BETA + h:RS_BETA + h + 1]
        d_col = dcum[b, c][:, RS_A + h:RS_A + h + 1]
        d_row = dcum_t[b, c][RS_A + h:RS_A + h + 1, :]
        d_last = dcum[b, c][CHUNK - 1:CHUNK, RS_A + h:RS_A + h + 1]
        e_col = jnp.exp(d_col)
        kb = k * beta
        gamma.append(jnp.exp(jnp.where(incl, d_col - d_row, -jnp.inf)))
        q16.append(q.astype(BF16))
        k16.append(k.astype(BF16))
        kb16.append(kb.astype(BF16))
        vb16.append((v * beta).astype(BF16))
        kw16.append((kb * e_col).astype(BF16))
        qdec16.append((q * e_col).astype(BF16))
        kdect16.append((k * jnp.exp(d_last - d_col)).T.astype(BF16))
        last.append(jnp.exp(d_last))

    n = len(chains)
    kq_k = [_dot_nt(jnp.concatenate([kb16[i], q16[i]], axis=0), k16[i]) for i in range(n)]
    p16 = [(-jnp.where(strict, kq_k[i][0:CHUNK] * gamma[i], 0.0)).astype(BF16) for i in range(n)]
    qk16 = [(kq_k[i][CHUNK:] * gamma[i]).astype(BF16) for i in range(n)]
    t_mat = [eye + p16[i].astype(F32) for i in range(n)]
    p16 = [_dot(p16[i], p16[i]).astype(BF16) for i in range(n)]
    for _ in range(4):
        t_mat = [t_mat[i] + _dot(t_mat[i].astype(BF16), p16[i]) for i in range(n)]
        p16 = [_dot(p16[i], p16[i]).astype(BF16) for i in range(n)]
    t16 = [(t_mat[i] + _dot(t_mat[i].astype(BF16), p16[i])).astype(BF16) for i in range(n)]
    uw = [_dot(t16[i], jnp.concatenate([vb16[i], kw16[i]], axis=1)) for i in range(n)]
    u = [uw[i][:, 0:GDN_DV] for i in range(n)]
    w16 = [uw[i][:, GDN_DV:].astype(BF16) for i in range(n)]

    per_chunk = nb * GDN_HEADS
    s = [s_ref[b, h] for b in range(nb) for h in range(GDN_HEADS)]
    for c in range(nc):
        ids = range(c * per_chunk, (c + 1) * per_chunk)
        s16 = [s[j].astype(BF16) for j in range(per_chunk)]
        wq_s = [_dot(jnp.concatenate([w16[i], qdec16[i]], axis=0), s16[j]) for j, i in enumerate(ids)]
        v_new16 = [(u[i] - wq_s[j][0:CHUNK]).astype(BF16) for j, i in enumerate(ids)]
        o = [wq_s[j][CHUNK:] + _dot(qk16[i], v_new16[j]) for j, i in enumerate(ids)]
        s = [s[j] * last[i] + _dot(kdect16[i], v_new16[j]) for j, i in enumerate(ids)]
        for j, i in enumerate(ids):
            b, _, h = chains[i]
            zc = z_ref[b, c * CHUNK:(c + 1) * CHUNK, h * GDN_DV:(h + 1) * GDN_DV]
            y = _rms_rows(o[j], gout) * _silu(zc)
            y_ref[b, c * CHUNK:(c + 1) * CHUNK, h * GDN_DV:(h + 1) * GDN_DV] = y.astype(y_ref.dtype)
    for j in range(per_chunk):
        s_ref[j // GDN_HEADS, j % GDN_HEADS] = s[j]


def _gdn(qkv, z, rs, a_log, dt_bias, g_out, batch, seq):
    n = qkv.shape[0]
    rows = GDN_CHUNKS_PER_STEP * CHUNK
    nb = GDN_SEQS_PER_STEP if batch % GDN_SEQS_PER_STEP == 0 else 1
    lane_vec = lambda v: jnp.zeros((1, LANES), F32).at[0, RS_A:RS_A + GDN_HEADS].set(v)
    per_seq = lambda a: a.reshape(batch, seq, a.shape[-1])
    block = lambda width: pl.BlockSpec((nb, rows, width), lambda b, i: (b, i, 0))
    y = pl.pallas_call(
        _gdn_kernel,
        grid=(batch // nb, seq // rows),
        in_specs=[
            block(GDN_CONV_CH),
            block(GDN_VAL_DIM),
            block(LANES),
            _const_spec((1, LANES)),
            _const_spec((1, LANES)),
            _const_spec((1, GDN_DV)),
        ],
        out_specs=block(GDN_VAL_DIM),
        out_shape=jax.ShapeDtypeStruct((batch, seq, GDN_VAL_DIM), BF16),
        scratch_shapes=[
            pltpu.VMEM((nb, GDN_HEADS, GDN_DK, GDN_DV), F32),
        ],
        compiler_params=pltpu.CompilerParams(dimension_semantics=("arbitrary", "arbitrary"), vmem_limit_bytes=VMEM_LIMIT),
        name="gdn",
    )(per_seq(qkv), per_seq(z), per_seq(rs), lane_vec(a_log), lane_vec(dt_bias), g_out[None, :])
    return y.reshape(n, GDN_VAL_DIM)


def _alibi_slope(h):
    return 2.0 ** (-8.0 * (h + 1) / DSA_HEADS)


def _order_key(x):
    bits = lax.bitcast_convert_type(x, I32)
    return (bits ^ ((bits >> 31) & jnp.int32(0x7FFFFFFF))) >> 1


def _order_key_to_f32(key):
    full = key << 1
    return lax.bitcast_convert_type(full ^ ((full >> 31) & jnp.int32(0x7FFFFFFF)), F32)


def _switch_in_two_levels(index, branches, operand, width=4):
    if len(branches) <= width:
        return lax.switch(index, branches, operand)
    groups = [branches[g:g + width] for g in range(0, len(branches), width)]
    return lax.switch(index // width,
                      [(lambda op, grp=grp, base=g * width: lax.switch(index - base, grp, op))
                       for g, grp in enumerate(groups)], operand)


def _tiles_for_block(j):
    return (j * Q_BLOCK + Q_BLOCK + KEY_TILE - 1) // KEY_TILE


def _dsa_kernel(qT_ref, qiT_ref, wT_ref, kn_ref, vT_ref, rs_ref, kpos_ref, slope_ref, o_ref,
                idxk_scr, sc_scr, key_scr, coarse_scr, thr_scr, cnt_scr, acc_scr, *, topk, n_pos_bits):
    j = pl.program_id(1)

    @pl.when(j == 0)
    def _new_sequence():
        idxk_scr[...] = rs_ref[:, RS_IDXK:RS_IDXK + IDX_HD].astype(BF16)

    n_tiles = _tiles_for_block(j)
    lane = lax.broadcasted_iota(I32, (1, Q_BLOCK), 1)
    qpos = j * Q_BLOCK + lane
    key_limit = ((qpos >> CHUNK_SHIFT) + 1) << CHUNK_SHIFT
    row = lax.broadcasted_iota(I32, (KEY_TILE, Q_BLOCK), 0)

    w_t = wT_ref[...]
    qi_pairs = [jnp.concatenate([qiT_ref[(2 * p) * IDX_HD:(2 * p + 1) * IDX_HD, :],
                                 qiT_ref[(2 * p + 1) * IDX_HD:(2 * p + 2) * IDX_HD, :]], axis=1)
                for p in range(IDX_HEADS // 2)]

    def score_tiles(t, count, carry):
        s_max, s_min = carry
        r0 = pl.multiple_of(t * KEY_TILE, KEY_TILE)
        kt = idxk_scr[pl.ds(r0, count * KEY_TILE), :]
        accs = [jnp.zeros((KEY_TILE, Q_BLOCK), F32) for _ in range(count)]
        for p in range(IDX_HEADS // 2):
            d = _dot(kt, qi_pairs[p])
            for i in range(count):
                di = d[i * KEY_TILE:(i + 1) * KEY_TILE, :]
                accs[i] = accs[i] + jnp.maximum(di[:, :Q_BLOCK], 0.0) * w_t[2 * p:2 * p + 1, :]
                accs[i] = accs[i] + jnp.maximum(di[:, Q_BLOCK:], 0.0) * w_t[2 * p + 1:2 * p + 2, :]
        for i in range(count):
            admissible = r0 + i * KEY_TILE + row < key_limit
            stored = jnp.where(admissible, accs[i], -jnp.inf)
            sc_scr[t + i] = stored
            key = _order_key(stored)
            key_scr[t + i] = key
            coarse_scr[t + i] = (key >> COARSE_SHIFT).astype(F32)
            s_max = jnp.maximum(s_max, jnp.max(stored, axis=0, keepdims=True))
            s_min = jnp.minimum(s_min, jnp.min(jnp.where(admissible, accs[i], jnp.inf), axis=0, keepdims=True))
        return s_max, s_min

    stats = (jnp.full((1, Q_BLOCK), -jnp.inf, F32), jnp.full((1, Q_BLOCK), jnp.inf, F32))
    stats = lax.fori_loop(0, n_tiles // SCORE_RUN, lambda u, st: score_tiles(u * SCORE_RUN, SCORE_RUN, st), stats)
    rest = n_tiles % SCORE_RUN
    s_max, s_min = lax.switch(rest, [lambda st: st] + [(lambda st, c=c: score_tiles(n_tiles - c, c, st))
                                                       for c in range(1, SCORE_RUN)], stats)

    def count_tiles(per_tile):
        cnt_scr[...] = jnp.zeros(cnt_scr.shape, F32)
        base = jnp.int32(0)
        arm = 1 << (sc_scr.shape[0].bit_length() - 1)
        while arm >= 1:
            has = (n_tiles & arm) != 0

            @pl.when(has)
            def _run(base=base, arm=arm):
                chains = [jnp.zeros((SUBLANES, Q_BLOCK), F32) for _ in range(COUNT_CHAINS)]
                for i in range(arm):
                    flags = per_tile(base + i)
                    for r in range(KEY_TILE // SUBLANES):
                        chains[r % COUNT_CHAINS] = chains[r % COUNT_CHAINS] + flags[r * SUBLANES:(r + 1) * SUBLANES, :]
                while len(chains) > 1:
                    chains = [chains[k] + chains[k + 1] for k in range(0, len(chains), 2)]
                cnt_scr[...] += chains[0]

            base = base + jnp.where(has, arm, 0)
            arm //= 2
        return jnp.sum(cnt_scr[...], axis=0, keepdims=True).astype(I32)

    def count_ge(thr):
        return count_tiles(lambda i: jnp.where(sc_scr[i] >= thr, 1.0, 0.0))

    groups_per_tile = KEY_TILE // Q_BLOCK

    def key_group(ref, g):
        return ref[g // groups_per_tile, (g % groups_per_tile) * Q_BLOCK:(g % groups_per_tile + 1) * Q_BLOCK, :]

    def sum_groups(per_group, groups):
        chains = [jnp.zeros((SUBLANES, Q_BLOCK), F32) for _ in range(COUNT_CHAINS)]
        for g in range(groups):
            flags = per_group(g)
            for r in range(Q_BLOCK // SUBLANES):
                chains[r % COUNT_CHAINS] = chains[r % COUNT_CHAINS] + flags[r * SUBLANES:(r + 1) * SUBLANES, :]
        while len(chains) > 1:
            chains = [chains[k] + chains[k + 1] for k in range(0, len(chains), 2)]
        return jnp.sum(chains[0], axis=0, keepdims=True)

    def count_keys_below(key, groups):
        one = jnp.int32(0x3F800000)
        return sum_groups(lambda g: lax.bitcast_convert_type(((key_group(key_scr, g) - key) >> 31) & one, F32), groups)

    def search_with_groups(groups):
        most_below = float(groups * Q_BLOCK - topk)

        def bisect_step(_, st):
            lo, hi, mid, mid_key = st
            up, down = mid * 0.5 + hi * 0.5, lo * 0.5 + mid * 0.5
            up_key, down_key = _order_key(up), _order_key(down)
            take = count_keys_below(mid_key, groups) <= most_below
            return (jnp.where(take, mid, lo), jnp.where(take, hi, mid),
                    jnp.where(take, up, down), jnp.where(take, up_key, down_key))

        def coarse_step(_, st):
            lo, hi, mid, mid_key = st
            bucket = mid_key >> COARSE_SHIFT
            edge = jnp.minimum(hi, _order_key_to_f32((bucket + 1) << COARSE_SHIFT))
            up, down = mid * 0.5 + hi * 0.5, lo * 0.5 + edge * 0.5
            up_key, down_key = _order_key(up), _order_key(down)
            c = bucket.astype(F32) + 0.5
            sides = sum_groups(lambda g: lax.clamp(-0.5, c - key_group(coarse_scr, g), 0.5), groups)
            take = sides + 0.5 * (groups * Q_BLOCK) <= most_below
            return (jnp.where(take, mid, lo), jnp.where(take, hi, edge),
                    jnp.where(take, up, down), jnp.where(take, up_key, down_key))

        def search(st):
            st = lax.fori_loop(0, COARSE_ROUNDS, coarse_step, st)
            return lax.fori_loop(COARSE_ROUNDS, BISECT_ROUNDS, bisect_step, st)

        return search

    above_max = s_max + jnp.maximum(jnp.abs(s_max) * 1e-6, 1e-30)
    first_mid = s_min * 0.5 + above_max * 0.5
    searches = [search_with_groups(n) for n in range(1, sc_scr.shape[0] * groups_per_tile + 1)]
    lo = _switch_in_two_levels(j, searches, (s_min, above_max, first_mid, _order_key(first_mid)))[0]
    few = key_limit < topk
    thr = jnp.where(few, jnp.finfo(F32).min, _order_key_to_f32(_order_key(lo)))
    cnt = jnp.where(few, topk, count_ge(thr))
    thr_scr[...] = thr

    @pl.when(jnp.max(jnp.abs(cnt - topk)) > 0)
    def _resolve():
        def max_below(thr):
            def tile(t, m):
                sc = sc_scr[t]
                return jnp.maximum(m, jnp.max(jnp.where(sc < thr, sc, -jnp.inf), axis=0, keepdims=True))
            return lax.fori_loop(0, n_tiles, tile, jnp.full((1, Q_BLOCK), -jnp.inf, F32))

        def lower(st):
            thr, cnt = st
            nxt = max_below(thr)
            short = cnt < topk
            return jnp.where(short, nxt, thr), jnp.where(short, count_ge(nxt), cnt)

        def min_above(thr):
            def tile(t, m):
                sc = sc_scr[t]
                return jnp.minimum(m, jnp.min(jnp.where(sc > thr, sc, jnp.inf), axis=0, keepdims=True))
            return lax.fori_loop(0, n_tiles, tile, jnp.full((1, Q_BLOCK), jnp.inf, F32))

        def count_eq_below(thr, pos_limit):
            def tile(t, c):
                r0 = t * KEY_TILE
                hit = (sc_scr[t] == thr) & (r0 + row < pos_limit)
                return c + jnp.sum(hit.astype(I32), axis=0, keepdims=True)
            return lax.fori_loop(0, n_tiles, tile, jnp.zeros((1, Q_BLOCK), I32))

        def body(st):
            thr, cnt = st
            nxt = min_above(thr)
            cnt_n = count_ge(nxt)
            active = cnt > topk
            advance = active & (cnt_n >= topk)
            tie = active & (cnt_n < topk)
            @pl.when(jnp.max(tie.astype(I32)) > 0)
            def _drop_surplus_ties():
                need = topk - cnt_n
                pos = jnp.zeros((1, Q_BLOCK), I32)
                for b in range(n_pos_bits - 1, -1, -1):
                    cand = pos + (1 << b)
                    pos = jnp.where(count_eq_below(thr, cand) < need, cand, pos)

                def drop_tile(t, carry):
                    r0 = t * KEY_TILE
                    sc = sc_scr[t]
                    sc_scr[t] = jnp.where(tie & (sc == thr) & (r0 + row > pos), -jnp.inf, sc)
                    return carry

                lax.fori_loop(0, n_tiles, drop_tile, 0)

            return jnp.where(advance, nxt, thr), jnp.where(advance, cnt_n, jnp.where(tie, topk, cnt))

        st = lax.while_loop(lambda st: jnp.min(st[1]) < topk, lower, (thr, cnt))
        thr_scr[...] = lax.while_loop(lambda st: jnp.max(st[1]) > topk, body, st)[0]

    thr = thr_scr[...]

    zero = jnp.zeros((DSA_HD, Q_BLOCK), BF16)
    q_pairs = []
    for p in range(DSA_HEADS // 2):
        a = qT_ref[(2 * p) * DSA_HD:(2 * p + 1) * DSA_HD, :]
        b = qT_ref[(2 * p + 1) * DSA_HD:(2 * p + 2) * DSA_HD, :]
        q_pairs.append(jnp.concatenate([jnp.concatenate([a, zero], axis=1),
                                        jnp.concatenate([zero, b], axis=1),
                                        slope_ref[p]], axis=0))
    acc_scr[...] = jnp.zeros(acc_scr.shape, F32)
    qposf = qpos.astype(F32)

    def qk_products(t, rows):
        r0 = pl.multiple_of(t * KEY_TILE, KEY_TILE)
        kt = kn_ref[pl.ds(r0, rows), :]
        kp = kpos_ref[pl.ds(r0, rows), :]
        s2 = [_dot(jnp.concatenate([kt[:, p * 2 * DSA_HD:(p + 1) * 2 * DSA_HD], kp], axis=1), q_pairs[p])
              for p in range(DSA_HEADS // 2)]
        return t, rows, sc_scr[t, 0:rows, :] >= thr, s2

    def softmax_pv(tile, ml, last_tile):
        t, rows, sel, s2 = tile
        vt = vT_ref[t, :, 0:rows]
        if last_tile:
            after = jnp.maximum((t * KEY_TILE + row[0:rows]).astype(F32) - qposf, 0.0)
        m_all, l_all = ml
        m_rows, l_rows = [], []
        for h in range(DSA_HEADS):
            s = s2[h // 2][:, (h % 2) * Q_BLOCK:(h % 2 + 1) * Q_BLOCK]
            if last_tile:
                s = s - (2.0 * LOG2E * _alibi_slope(h)) * after
            s = jnp.where(sel, s, NEG_BIG)
            m_old = m_all[h:h + 1, :]
            m_new = jnp.maximum(m_old, jnp.max(s, axis=0, keepdims=True))
            alpha = jnp.exp2(m_old - m_new)
            pr = jnp.exp2(s - m_new)
            l_rows.append(alpha * l_all[h:h + 1, :] + jnp.sum(pr, axis=0, keepdims=True))
            m_rows.append(m_new)
            hs = slice(h * DSA_HD, (h + 1) * DSA_HD)
            acc_scr[hs, :] = alpha * acc_scr[hs, :] + _dot(vt[hs, :], pr.astype(BF16))
        return jnp.concatenate(m_rows, axis=0), jnp.concatenate(l_rows, axis=0)

    def attend_run(first, count, ml, last_rows=None):
        tiles = [qk_products(first + i, last_rows if (last_rows and i == count - 1) else KEY_TILE)
                 for i in range(count)]
        for i, tile in enumerate(tiles):
            ml = softmax_pv(tile, ml, bool(last_rows) and i == count - 1)
        return ml

    ml = (jnp.full((DSA_HEADS, Q_BLOCK), NEG_BIG, F32), jnp.zeros((DSA_HEADS, Q_BLOCK), F32))
    full_runs = (n_tiles - 1) // ATTEND_RUN
    ml = lax.fori_loop(0, full_runs, lambda u, ml: attend_run(u * ATTEND_RUN, ATTEND_RUN, ml), ml)
    rest = n_tiles - full_runs * ATTEND_RUN
    last_groups = j % groups_per_tile
    endings = [(lambda ml, c=c, g=g: attend_run(n_tiles - c, c, ml, last_rows=(g + 1) * Q_BLOCK))
               for g in range(groups_per_tile) for c in range(1, ATTEND_RUN + 1)]
    _, l_all = _switch_in_two_levels(last_groups * ATTEND_RUN + rest - 1, endings, ml, width=ATTEND_RUN)

    outs = [acc_scr[h * DSA_HD:(h + 1) * DSA_HD, :] / l_all[h:h + 1, :] for h in range(DSA_HEADS)]
    o_ref[...] = jnp.concatenate(outs, axis=0).T.astype(o_ref.dtype)


def _dsa(qT, qiT, wT, kn, vT, rs, batch, seq):
    n = kn.shape[0]
    blocks = seq // Q_BLOCK
    tiles = seq // KEY_TILE
    topk = min(TOPK_MAX, seq // 4)
    col_map = lambda b, j: (0, b * blocks + j)
    pos = np.arange(seq)
    digit = 1 << KPOS_DIGIT_BITS
    kpos = np.zeros((seq, LANES), np.float32)
    slope_rows = np.zeros((DSA_HEADS // 2, LANES, 2 * Q_BLOCK), np.float32)
    rest = np.asarray([LOG2E * _alibi_slope(h) for h in range(DSA_HEADS)], np.float32).reshape(DSA_HEADS // 2, 2)
    for i in range(SLOPE_PIECES):
        piece = rest.astype(BF16).astype(np.float32)
        rest = rest - piece
        cols = np.repeat(piece, Q_BLOCK, axis=1)
        kpos[:, 2 * i], kpos[:, 2 * i + 1] = pos // digit, pos % digit
        slope_rows[:, 2 * i, :], slope_rows[:, 2 * i + 1, :] = digit * cols, cols
    kpos, slope_rows = jnp.asarray(kpos.astype(BF16)), jnp.asarray(slope_rows.astype(BF16))
    return pl.pallas_call(
        functools.partial(_dsa_kernel, topk=topk, n_pos_bits=seq.bit_length()),
        grid=(batch, blocks),
        in_specs=[
            pl.BlockSpec((DSA_DIM, Q_BLOCK), col_map),
            pl.BlockSpec((IDX_Q_DIM, Q_BLOCK), col_map),
            pl.BlockSpec((IDX_HEADS, Q_BLOCK), col_map),
            pl.BlockSpec((seq, DSA_DIM), lambda b, j: (b, 0)),
            pl.BlockSpec((tiles, DSA_DIM, KEY_TILE), lambda b, j: (b, 0, 0)),
            pl.BlockSpec((seq, LANES), lambda b, j: (b, 0)),
            _const_spec((seq, LANES)),
            _const_spec((DSA_HEADS // 2, LANES, 2 * Q_BLOCK)),
        ],
        out_specs=pl.BlockSpec((Q_BLOCK, DSA_DIM), lambda b, j: (b * blocks + j, 0)),
        out_shape=jax.ShapeDtypeStruct((n, DSA_DIM), BF16),
        scratch_shapes=[
            pltpu.VMEM((seq, IDX_HD), BF16),
            pltpu.VMEM((tiles, KEY_TILE, Q_BLOCK), F32),
            pltpu.VMEM((tiles, KEY_TILE, Q_BLOCK), I32),
            pltpu.VMEM((tiles, KEY_TILE, Q_BLOCK), F32),
            pltpu.VMEM((1, Q_BLOCK), F32),
            pltpu.VMEM((SUBLANES, Q_BLOCK), F32),
            pltpu.VMEM((DSA_DIM, Q_BLOCK), F32),
        ],
        compiler_params=pltpu.CompilerParams(dimension_semantics=("arbitrary", "arbitrary"), vmem_limit_bytes=VMEM_LIMIT),
        name="dsa",
    )(qT, qiT, wT, kn, vT, rs, kpos, slope_rows)


def _memkv_kernel(mem_ref, g_ref, w_ref, bd_ref, gk_ref, k_ref, v_ref):
    h = _rms_rows(mem_ref[...], g_ref[...]).astype(BF16)
    kv = _dot(h, w_ref[...])
    k = kv[:, :XA_DIM]
    kms = _dot((k * k).astype(BF16), bd_ref[...]) * (1.0 / XA_HD)
    k_ref[...] = (k * lax.rsqrt(kms + EPS) * gk_ref[...]).astype(BF16)
    v_ref[...] = kv[:, XA_DIM:].astype(BF16)


def _memkv(mem2, g_mem, w_xkv, g_xk):
    n, d = mem2.shape
    tm = min(ROW_TILE, n)
    return pl.pallas_call(
        _memkv_kernel,
        grid=(n // tm,),
        in_specs=[
            pl.BlockSpec((tm, d), lambda i: (i, 0)),
            _const_spec((1, d)),
            _const_spec((d, 2 * XA_DIM)),
            _const_spec((XA_DIM, XA_DIM)),
            _const_spec((1, XA_DIM)),
        ],
        out_specs=(pl.BlockSpec((tm, XA_DIM), lambda i: (i, 0)), pl.BlockSpec((tm, XA_DIM), lambda i: (i, 0))),
        out_shape=(jax.ShapeDtypeStruct((n, XA_DIM), BF16), jax.ShapeDtypeStruct((n, XA_DIM), BF16)),
        compiler_params=pltpu.CompilerParams(dimension_semantics=("arbitrary",), vmem_limit_bytes=VMEM_LIMIT),
        name="memkv",
    )(mem2, g_mem[None, :], w_xkv.astype(BF16), _block_diag_ones(XA_DIM, XA_HD), jnp.tile(g_xk, XA_HEADS)[None, :])


def _tail_kernel(x_ref, ya_ref, yb_ref, wout_ref, gx_ref, wxq_ref, bd_ref, gxq_ref, km_ref, vm_ref, wxo_ref,
                 gf_ref, wgu_ref, wd_ref, o_ref, *, ff_bounds):
    tm = x_ref.shape[0]
    part = tm // ROW_PARTS
    groups = [slice(g * part, (g + 1) * part) for g in range(ROW_PARTS)]
    d_ff = wd_ref.shape[0]

    x1 = [x_ref[r, :] + _dot(jnp.concatenate([ya_ref[r, :], yb_ref[r, :]], axis=1), wout_ref[...]) for r in groups]

    qn = []
    for g in range(ROW_PARTS):
        h = _rms_rows(x1[g], gx_ref[...]).astype(BF16)
        q = _dot(h, wxq_ref[...])
        qms = _dot((q * q).astype(BF16), bd_ref[...]) * (1.0 / XA_HD)
        qn.append((q * lax.rsqrt(qms + EPS) * gxq_ref[...]).astype(BF16))

    attn = []
    for g in range(ROW_PARTS):
        heads = []
        for hd in range(XA_HEADS):
            hs = slice(hd * XA_HD, (hd + 1) * XA_HD)
            s = _dot_nt(qn[g][:, hs], km_ref[0, :, hs]) * (XA_HD ** -0.5)
            pr = jnp.exp(s - jnp.max(s, axis=-1, keepdims=True))
            o = _dot(pr.astype(BF16), vm_ref[0, :, hs]) / jnp.sum(pr, axis=-1, keepdims=True)
            heads.append(o.astype(BF16))
        attn.append(jnp.concatenate(heads, axis=1))

    x2 = [x1[g] + _dot(attn[g], wxo_ref[...]) for g in range(ROW_PARTS)]
    h = [_rms_rows(x2[g], gf_ref[...]).astype(BF16) for g in range(ROW_PARTS)]
    acc = x2
    for c0, c1 in zip(ff_bounds[:-1], ff_bounds[1:]):
        for g in range(ROW_PARTS):
            gate = _dot(h[g], wgu_ref[:, c0:c1])
            up = _dot(h[g], wgu_ref[:, d_ff + c0:d_ff + c1])
            a = (_silu(gate) * up).astype(BF16)
            acc[g] = acc[g] + _dot(a, wd_ref[c0:c1, :])
    for g, r in enumerate(groups):
        o_ref[r, :] = acc[g]


def _tail(x2, ya, yb, w_out, g_xattn, w_xq, g_xq, km, vm, w_xo, g_ffn, w_gu, w_down, seq):
    n, d = x2.shape
    tm = ROW_TILE
    d_ff = w_down.shape[0]
    assert d_ff % MXU_WIDTH == 0
    ff_bounds = (0, (d_ff // MXU_WIDTH + 1) // 2 * MXU_WIDTH, d_ff)
    n_mem = km.shape[1]
    per_seq = seq // tm
    return pl.pallas_call(
        functools.partial(_tail_kernel, ff_bounds=ff_bounds),
        grid=(n // tm,),
        in_specs=[
            pl.BlockSpec((tm, d), lambda i: (i, 0)),
            pl.BlockSpec((tm, GDN_VAL_DIM), lambda i: (i, 0)),
            pl.BlockSpec((tm, DSA_DIM), lambda i: (i, 0)),
            _const_spec((GDN_VAL_DIM + DSA_DIM, d)),
            _const_spec((1, d)),
            _const_spec((d, XA_DIM)),
            _const_spec((XA_DIM, XA_DIM)),
            _const_spec((1, XA_DIM)),
            pl.BlockSpec((1, n_mem, XA_DIM), lambda i: (i // per_seq, 0, 0)),
            pl.BlockSpec((1, n_mem, XA_DIM), lambda i: (i // per_seq, 0, 0)),
            _const_spec((XA_DIM, d)),
            _const_spec((1, d)),
            _const_spec((d, 2 * d_ff)),
            _const_spec((d_ff, d)),
        ],
        out_specs=pl.BlockSpec((tm, d), lambda i: (i, 0)),
        out_shape=jax.ShapeDtypeStruct((n, d), F32),
        compiler_params=pltpu.CompilerParams(dimension_semantics=("arbitrary",), vmem_limit_bytes=VMEM_LIMIT),
        name="tail",
    )(x2, ya, yb, w_out.astype(BF16), g_xattn[None, :], w_xq.astype(BF16), _block_diag_ones(XA_DIM, XA_HD),
      jnp.tile(g_xq, XA_HEADS)[None, :], km, vm, w_xo.astype(BF16), g_ffn[None, :], w_gu.astype(BF16),
      w_down.astype(BF16))


def kernel(x, mem, g_mix, w_in, conv_w, a_log, dt_bias, g_gdn_out, g_q_dsa, g_k_dsa, w_out, g_xattn, g_mem, w_xq,
           w_xkv, g_xq, g_xk, w_xo, g_ffn, w_gu, w_down):
    batch, seq, d = x.shape
    n_mem = mem.shape[1]
    assert seq % ROW_TILE == 0 and seq % KEY_TILE == 0 and ROW_TILE % KEY_TILE == 0
    for l in range(g_mix.shape[0]):
        x2 = x.reshape(batch * seq, d)
        qkv, z, kn, rs, qT, vT, qiT, wT = _inproj(x2, g_mix[l], w_in[l], g_q_dsa[l], g_k_dsa[l], conv_w[l], seq)
        ya = _gdn(qkv, z, rs, a_log[l], dt_bias[l], g_gdn_out[l], batch, seq)
        yb = _dsa(qT, qiT, wT, kn, vT, rs, batch, seq)
        km, vm = _memkv(mem.reshape(batch * n_mem, d), g_mem[l], w_xkv[l], g_xk[l])
        km = km.reshape(batch, n_mem, XA_DIM)
        vm = vm.reshape(batch, n_mem, XA_DIM)
        x = _tail(x2, ya, yb, w_out[l], g_xattn[l], w_xq[l], g_xq[l], km, vm, w_xo[l], g_ffn[l], w_gu[l],
                  w_down[l], seq).reshape(batch, seq, d)
    return x
```

```python
import functools

import jax
import jax.numpy as jnp
import numpy as np
from jax import lax
from jax.experimental import pallas as pl
from jax.experimental.pallas import tpu as pltpu

F32 = jnp.float32
BF16 = jnp.bfloat16
I32 = jnp.int32

EPS = 1e-6
CHUNK = 64
CHUNK_SHIFT = CHUNK.bit_length() - 1
assert CHUNK == 1 << CHUNK_SHIFT
GDN_HEADS, GDN_DK, GDN_DV, CONV_K = 4, 128, 128, 4
DSA_HEADS, DSA_HD = 8, 64
IDX_HEADS, IDX_HD = 16, 64
TOPK_MAX = 256
XA_HEADS, XA_HD = 4, 128

GDN_KEY_DIM = GDN_HEADS * GDN_DK
GDN_VAL_DIM = GDN_HEADS * GDN_DV
GDN_CONV_CH = 2 * GDN_KEY_DIM + GDN_VAL_DIM
DSA_DIM = DSA_HEADS * DSA_HD
IDX_Q_DIM = IDX_HEADS * IDX_HD
XA_DIM = XA_HEADS * XA_HD

LANES = 128
SUBLANES = 8
MXU_WIDTH = 256
VMEM_LIMIT = 56 * 1024 * 1024

RS_IDXK = 0
RS_BETA = IDX_HD
RS_A = IDX_HD + GDN_HEADS

ROW_TILE = 512
ROW_PARTS = 2
KEY_TILE = 256
Q_BLOCK = 128
GDN_CHUNKS_PER_STEP = 4
GDN_SEQS_PER_STEP = 4

BISECT_ROUNDS = 20
COARSE_ROUNDS = 14
COARSE_SHIFT = 8
COUNT_CHAINS = 4
SCORE_RUN = 4
ATTEND_RUN = 4
KPOS_DIGIT_BITS = 4
SLOPE_PIECES = 3
LOG2E = 1.4426950408889634
NEG_BIG = -1e30


def _dot(a, b):
    return jnp.dot(a, b, preferred_element_type=F32)


def _dot_nt(a, b):
    return lax.dot_general(a, b, (((1,), (1,)), ((), ())), preferred_element_type=F32)


def _dot_f32(a, b):
    return jnp.dot(a, b, preferred_element_type=F32, precision=lax.Precision.HIGHEST)


def _silu(x):
    half = 0.5 * x
    return half + half * jnp.tanh(half)


def _rms_rows(x, g):
    ms = jnp.mean(x * x, axis=-1, keepdims=True)
    return x * lax.rsqrt(ms + EPS) * g


def _const_spec(shape):
    nd = len(shape)
    return pl.BlockSpec(shape, lambda *_: (0,) * nd, pipeline_mode=pl.Buffered(1))


def _block_diag_ones(n, blk):
    group = np.arange(n) // blk
    return jnp.asarray((group[:, None] == group[None, :]).astype(BF16))


def _inproj_kernel(x_ref, g_ref, wrow_ref, wcol_ref, bd_ref, gk_ref, gq_ref, cw_ref,
                   qkv_ref, z_ref, kn_ref, rs_ref, qT_ref, vT_ref, qiT_ref, wT_ref, xc_ref, *, tiles_per_seq):
    tm = x_ref.shape[0]

    @pl.when(pl.program_id(0) % tiles_per_seq == 0)
    def _start_of_sequence():
        xc_ref[0:SUBLANES, :] = jnp.zeros((SUBLANES, GDN_CONV_CH), F32)

    h = _rms_rows(x_ref[...], g_ref[...]).astype(BF16)
    c0, c1, c2, c3 = GDN_CONV_CH, GDN_CONV_CH + GDN_VAL_DIM, GDN_CONV_CH + GDN_VAL_DIM + DSA_DIM, \
        GDN_CONV_CH + GDN_VAL_DIM + DSA_DIM + LANES

    xc_ref[SUBLANES:SUBLANES + tm, :] = _dot(h, wrow_ref[:, 0:c0])

    zkr = _dot(h, wrow_ref[:, c0:c3])
    z_ref[...] = zkr[:, 0:c1 - c0]
    k = zkr[:, c1 - c0:c2 - c0]
    kms = _dot((k * k).astype(BF16), bd_ref[...]) * (1.0 / DSA_HD)
    kn_ref[...] = (k * lax.rsqrt(kms + EPS) * gk_ref[...]).astype(BF16)
    rs_ref[...] = zkr[:, c2 - c0:c3 - c0]

    r0, r1, r2, r3 = DSA_DIM, 2 * DSA_DIM, 2 * DSA_DIM + IDX_Q_DIM, 2 * DSA_DIM + IDX_Q_DIM + IDX_HEADS
    qvT = _dot_nt(wcol_ref[0:r1, :], h)
    qT = qvT[0:r0, :]
    qms = _dot(bd_ref[...], (qT * qT).astype(BF16)) * (1.0 / DSA_HD)
    qT_ref[...] = (qT * lax.rsqrt(qms + EPS) * gq_ref[...] * (DSA_HD ** -0.5 * LOG2E)).astype(BF16)
    vT = qvT[r0:r1, :].astype(BF16)
    for i in range(vT_ref.shape[0]):
        vT_ref[i] = vT[:, i * KEY_TILE:(i + 1) * KEY_TILE]
    qwT = _dot_nt(wcol_ref[r1:r3, :], h)
    qiT_ref[...] = (qwT[0:IDX_Q_DIM, :] * (IDX_HD ** -0.5)).astype(BF16)
    wT_ref[...] = qwT[IDX_Q_DIM:, :] * (IDX_HEADS ** -0.5)

    cw = cw_ref[...]
    conv = cw[CONV_K - 1:CONV_K, :] * xc_ref[SUBLANES:SUBLANES + tm, :]
    for j in range(CONV_K - 2, -1, -1):
        back = CONV_K - 1 - j
        conv = conv + cw[j:j + 1, :] * xc_ref[SUBLANES - back:SUBLANES - back + tm, :]
    xc_ref[0:SUBLANES, :] = xc_ref[tm:tm + SUBLANES, :]
    act = _silu(conv)
    for hd in range(GDN_HEADS):
        qs = slice(hd * GDN_DK, (hd + 1) * GDN_DK)
        ks = slice(GDN_KEY_DIM + hd * GDN_DK, GDN_KEY_DIM + (hd + 1) * GDN_DK)
        q, k = act[:, qs], act[:, ks]
        qkv_ref[:, qs] = q * lax.rsqrt(jnp.sum(q * q, axis=-1, keepdims=True) + EPS) * (GDN_DK ** -0.5)
        qkv_ref[:, ks] = k * lax.rsqrt(jnp.sum(k * k, axis=-1, keepdims=True) + EPS)
    qkv_ref[:, 2 * GDN_KEY_DIM:] = act[:, 2 * GDN_KEY_DIM:]


def _inproj(x2, g_mix, w_in, g_q, g_k, conv_w, seq):
    n, d = x2.shape
    tm = ROW_TILE
    sizes = (GDN_CONV_CH, GDN_VAL_DIM, GDN_HEADS, GDN_HEADS, DSA_DIM, DSA_DIM, DSA_DIM, IDX_Q_DIM, IDX_HD, IDX_HEADS)
    offs = [0]
    for s in sizes:
        offs.append(offs[-1] + s)
    (w_qkv, w_z, w_b, w_a, w_q, w_k, w_v, w_iq, w_ik, w_iw) = [w_in[:, offs[i]:offs[i + 1]] for i in range(len(sizes))]
    pad = jnp.zeros((d, LANES - IDX_HD - 2 * GDN_HEADS), w_in.dtype)
    w_row = jnp.concatenate([w_qkv, w_z, w_k, w_ik, w_b, w_a, pad], axis=1).astype(BF16)
    w_col = jnp.concatenate([w_q, w_v, w_iq, w_iw], axis=1).T.astype(BF16)
    bd = _block_diag_ones(DSA_DIM, DSA_HD)
    gk_row = jnp.tile(g_k, DSA_HEADS)[None, :]
    gq_col = jnp.tile(g_q, DSA_HEADS)[:, None]
    nrow, ncol = w_row.shape[1], w_col.shape[0]
    out_shape = (
        jax.ShapeDtypeStruct((n, GDN_CONV_CH), F32),
        jax.ShapeDtypeStruct((n, GDN_VAL_DIM), F32),
        jax.ShapeDtypeStruct((n, DSA_DIM), BF16),
        jax.ShapeDtypeStruct((n, LANES), F32),
        jax.ShapeDtypeStruct((DSA_DIM, n), BF16),
        jax.ShapeDtypeStruct((n // KEY_TILE, DSA_DIM, KEY_TILE), BF16),
        jax.ShapeDtypeStruct((IDX_Q_DIM, n), BF16),
        jax.ShapeDtypeStruct((IDX_HEADS, n), F32),
    )
    return pl.pallas_call(
        functools.partial(_inproj_kernel, tiles_per_seq=seq // tm),
        grid=(n // tm,),
        in_specs=[
            pl.BlockSpec((tm, d), lambda i: (i, 0)),
            _const_spec((1, d)),
            _const_spec((d, nrow)),
            _const_spec((ncol, d)),
            _const_spec((DSA_DIM, DSA_DIM)),
            _const_spec((1, DSA_DIM)),
            _const_spec((DSA_DIM, 1)),
            _const_spec((CONV_K, GDN_CONV_CH)),
        ],
        out_specs=(
            pl.BlockSpec((tm, GDN_CONV_CH), lambda i: (i, 0)),
            pl.BlockSpec((tm, GDN_VAL_DIM), lambda i: (i, 0)),
            pl.BlockSpec((tm, DSA_DIM), lambda i: (i, 0)),
            pl.BlockSpec((tm, LANES), lambda i: (i, 0)),
            pl.BlockSpec((DSA_DIM, tm), lambda i: (0, i)),
            pl.BlockSpec((tm // KEY_TILE, DSA_DIM, KEY_TILE), lambda i: (i, 0, 0)),
            pl.BlockSpec((IDX_Q_DIM, tm), lambda i: (0, i)),
            pl.BlockSpec((IDX_HEADS, tm), lambda i: (0, i)),
        ),
        out_shape=out_shape,
        scratch_shapes=[pltpu.VMEM((tm + SUBLANES, GDN_CONV_CH), F32)],
        compiler_params=pltpu.CompilerParams(dimension_semantics=("arbitrary",), vmem_limit_bytes=VMEM_LIMIT),
        name="inproj",
    )(x2, g_mix[None, :], w_row, w_col, bd, gk_row, gq_col, conv_w)


def _gdn_kernel(qkv_ref, z_ref, rs_ref, alog_ref, dtb_ref, gout_ref, y_ref, s_ref):
    nb, rows = qkv_ref.shape[0], qkv_ref.shape[1]
    nc = rows // CHUNK

    @pl.when(pl.program_id(1) == 0)
    def _start_of_sequence():
        s_ref[...] = jnp.zeros(s_ref.shape, F32)

    rs = rs_ref[...]
    beta_all = jax.nn.sigmoid(rs)
    sp_in = rs + dtb_ref[...]
    softplus = jnp.maximum(sp_in, 0.0) + jnp.log(1.0 + jnp.exp(-jnp.abs(sp_in)))
    g_all = -jnp.exp(alog_ref[...]) * softplus

    ri = lax.broadcasted_iota(I32, (CHUNK, CHUNK), 0)
    ci = lax.broadcasted_iota(I32, (CHUNK, CHUNK), 1)
    incl = ri >= ci
    strict = ri > ci
    ltri = incl.astype(F32)
    eye = (ri == ci).astype(F32)
    gout = gout_ref[...]

    chains = [(b, c, h) for c in range(nc) for b in range(nb) for h in range(GDN_HEADS)]
    dcum, dcum_t = {}, {}
    for c in range(nc):
        for b in range(nb):
            d = _dot_f32(ltri, g_all[b, c * CHUNK:(c + 1) * CHUNK, :])
            dcum[b, c] = d
            dcum_t[b, c] = d.T

    q16, k16, kb16, vb16, kw16, qdec16, kdect16, gamma, last = ([] for _ in range(9))
    for b, c, h in chains:
        r = slice(c * CHUNK, (c + 1) * CHUNK)
        q = qkv_ref[b, r, h * GDN_DK:(h + 1) * GDN_DK]
        k = qkv_ref[b, r, GDN_KEY_DIM + h * GDN_DK:GDN_KEY_DIM + (h + 1) * GDN_DK]
        v = qkv_ref[b, r, 2 * GDN_KEY_DIM + h * GDN_DV:2 * GDN_KEY_DIM + (h + 1) * GDN_DV]
        beta = beta_all[b, r, RS_BETA + h:RS_BETA + h + 1]
        d_col = dcum[b, c][:, RS_A + h:RS_A + h + 1]
        d_row = dcum_t[b, c][RS_A + h:RS_A + h + 1, :]
        d_last = dcum[b, c][CHUNK - 1:CHUNK, RS_A + h:RS_A + h + 1]
        e_col = jnp.exp(d_col)
        kb = k * beta
        gamma.append(jnp.exp(jnp.where(incl, d_col - d_row, -jnp.inf)))
        q16.append(q.astype(BF16))
        k16.append(k.astype(BF16))
        kb16.append(kb.astype(BF16))
        vb16.append((v * beta).astype(BF16))
        kw16.append((kb * e_col).astype(BF16))
        qdec16.append((q * e_col).astype(BF16))
        kdect16.append((k * jnp.exp(d_last - d_col)).T.astype(BF16))
        last.append(jnp.exp(d_last))

    n = len(chains)
    p16 = [(-jnp.where(strict, _dot_nt(kb16[i], k16[i]) * gamma[i], 0.0)).astype(BF16) for i in range(n)]
    qk16 = [(_dot_nt(q16[i], k16[i]) * gamma[i]).astype(BF16) for i in range(n)]
    t_mat = [eye + p16[i].astype(F32) for i in range(n)]
    p16 = [_dot(p16[i], p16[i]).astype(BF16) for i in range(n)]
    for _ in range(4):
        t_mat = [t_mat[i] + _dot(t_mat[i].astype(BF16), p16[i]) for i in range(n)]
        p16 = [_dot(p16[i], p16[i]).astype(BF16) for i in range(n)]
    t16 = [(t_mat[i] + _dot(t_mat[i].astype(BF16), p16[i])).astype(BF16) for i in range(n)]
    u = [_dot(t16[i], vb16[i]) for i in range(n)]
    w16 = [_dot(t16[i], kw16[i]).astype(BF16) for i in range(n)]

    per_chunk = nb * GDN_HEADS
    s = [s_ref[b, h] for b in range(nb) for h in range(GDN_HEADS)]
    for c in range(nc):
        ids = range(c * per_chunk, (c + 1) * per_chunk)
        s16 = [s[j].astype(BF16) for j in range(per_chunk)]
        v_new16 = [(u[i] - _dot(w16[i], s16[j])).astype(BF16) for j, i in enumerate(ids)]
        o_state = [_dot(qdec16[i], s16[j]) for j, i in enumerate(ids)]
        o = [o_state[j] + _dot(qk16[i], v_new16[j]) for j, i in enumerate(ids)]
        s = [s[j] * last[i] + _dot(kdect16[i], v_new16[j]) for j, i in enumerate(ids)]
        for j, i in enumerate(ids):
            b, _, h = chains[i]
            zc = z_ref[b, c * CHUNK:(c + 1) * CHUNK, h * GDN_DV:(h + 1) * GDN_DV]
            y = _rms_rows(o[j], gout) * _silu(zc)
            y_ref[b, c * CHUNK:(c + 1) * CHUNK, h * GDN_DV:(h + 1) * GDN_DV] = y.astype(y_ref.dtype)
    for j in range(per_chunk):
        s_ref[j // GDN_HEADS, j % GDN_HEADS] = s[j]


def _gdn(qkv, z, rs, a_log, dt_bias, g_out, batch, seq):
    n = qkv.shape[0]
    rows = GDN_CHUNKS_PER_STEP * CHUNK
    nb = GDN_SEQS_PER_STEP if batch % GDN_SEQS_PER_STEP == 0 else 1
    lane_vec = lambda v: jnp.zeros((1, LANES), F32).at[0, RS_A:RS_A + GDN_HEADS].set(v)
    per_seq = lambda a: a.reshape(batch, seq, a.shape[-1])
    block = lambda width: pl.BlockSpec((nb, rows, width), lambda b, i: (b, i, 0))
    y = pl.pallas_call(
        _gdn_kernel,
        grid=(batch // nb, seq // rows),
        in_specs=[
            block(GDN_CONV_CH),
            block(GDN_VAL_DIM),
            block(LANES),
            _const_spec((1, LANES)),
            _const_spec((1, LANES)),
            _const_spec((1, GDN_DV)),
        ],
        out_specs=block(GDN_VAL_DIM),
        out_shape=jax.ShapeDtypeStruct((batch, seq, GDN_VAL_DIM), BF16),
        scratch_shapes=[
            pltpu.VMEM((nb, GDN_HEADS, GDN_DK, GDN_DV), F32),
        ],
        compiler_params=pltpu.CompilerParams(dimension_semantics=("arbitrary", "arbitrary"), vmem_limit_bytes=VMEM_LIMIT),
        name="gdn",
    )(per_seq(qkv), per_seq(z), per_seq(rs), lane_vec(a_log), lane_vec(dt_bias), g_out[None, :])
    return y.reshape(n, GDN_VAL_DIM)


def _alibi_slope(h):
    return 2.0 ** (-8.0 * (h + 1) / DSA_HEADS)


def _order_key(x):
    bits = lax.bitcast_convert_type(x, I32)
    return (bits ^ ((bits >> 31) & jnp.int32(0x7FFFFFFF))) >> 1


def _order_key_to_f32(key):
    full = key << 1
    return lax.bitcast_convert_type(full ^ ((full >> 31) & jnp.int32(0x7FFFFFFF)), F32)


def _switch_in_two_levels(index, branches, operand, width=4):
    if len(branches) <= width:
        return lax.switch(index, branches, operand)
    groups = [branches[g:g + width] for g in range(0, len(branches), width)]
    return lax.switch(index // width,
                      [(lambda op, grp=grp, base=g * width: lax.switch(index - base, grp, op))
                       for g, grp in enumerate(groups)], operand)


def _tiles_for_block(j):
    return (j * Q_BLOCK + Q_BLOCK + KEY_TILE - 1) // KEY_TILE


def _dsa_kernel(qT_ref, qiT_ref, wT_ref, kn_ref, vT_ref, rs_ref, kpos_ref, slope_ref, o_ref,
                idxk_scr, sc_scr, key_scr, coarse_scr, thr_scr, cnt_scr, acc_scr, *, topk, n_pos_bits):
    j = pl.program_id(1)

    @pl.when(j == 0)
    def _new_sequence():
        idxk_scr[...] = rs_ref[:, RS_IDXK:RS_IDXK + IDX_HD].astype(BF16)

    n_tiles = _tiles_for_block(j)
    lane = lax.broadcasted_iota(I32, (1, Q_BLOCK), 1)
    qpos = j * Q_BLOCK + lane
    key_limit = ((qpos >> CHUNK_SHIFT) + 1) << CHUNK_SHIFT
    row = lax.broadcasted_iota(I32, (KEY_TILE, Q_BLOCK), 0)

    w_t = wT_ref[...]
    qi_pairs = [jnp.concatenate([qiT_ref[(2 * p) * IDX_HD:(2 * p + 1) * IDX_HD, :],
                                 qiT_ref[(2 * p + 1) * IDX_HD:(2 * p + 2) * IDX_HD, :]], axis=1)
                for p in range(IDX_HEADS // 2)]

    groups_per_tile = KEY_TILE // Q_BLOCK

    def score_tiles(t, count, carry, last_rows=KEY_TILE):
        s_max, s_min = carry
        r0 = pl.multiple_of(t * KEY_TILE, KEY_TILE)
        kt = idxk_scr[pl.ds(r0, (count - 1) * KEY_TILE + last_rows), :]
        rows = [KEY_TILE] * (count - 1) + [last_rows]
        accs = [jnp.zeros((r, Q_BLOCK), F32) for r in rows]
        for p in range(IDX_HEADS // 2):
            d = _dot(kt, qi_pairs[p])
            for i in range(count):
                di = d[i * KEY_TILE:i * KEY_TILE + rows[i], :]
                accs[i] = accs[i] + jnp.maximum(di[:, :Q_BLOCK], 0.0) * w_t[2 * p:2 * p + 1, :]
                accs[i] = accs[i] + jnp.maximum(di[:, Q_BLOCK:], 0.0) * w_t[2 * p + 1:2 * p + 2, :]
        for i in range(count):
            admissible = r0 + i * KEY_TILE + row[0:rows[i]] < key_limit
            stored = jnp.where(admissible, accs[i], -jnp.inf)
            sc_scr[t + i, 0:rows[i], :] = stored
            if rows[i] < KEY_TILE:
                sc_scr[t + i, rows[i]:KEY_TILE, :] = jnp.full((KEY_TILE - rows[i], Q_BLOCK), -jnp.inf, F32)
            key = _order_key(stored)
            key_scr[t + i, 0:rows[i], :] = key
            coarse_scr[t + i, 0:rows[i], :] = (key >> COARSE_SHIFT).astype(F32)
            s_max = jnp.maximum(s_max, jnp.max(stored, axis=0, keepdims=True))
            s_min = jnp.minimum(s_min, jnp.min(jnp.where(admissible, accs[i], jnp.inf), axis=0, keepdims=True))
        return s_max, s_min

    stats = (jnp.full((1, Q_BLOCK), -jnp.inf, F32), jnp.full((1, Q_BLOCK), jnp.inf, F32))
    full_score_runs = (n_tiles - 1) // SCORE_RUN
    stats = lax.fori_loop(0, full_score_runs, lambda u, st: score_tiles(u * SCORE_RUN, SCORE_RUN, st), stats)
    rest = n_tiles - full_score_runs * SCORE_RUN
    score_endings = [(lambda st, c=c, g=g: score_tiles(n_tiles - c, c, st, last_rows=(g + 1) * Q_BLOCK))
                     for g in range(groups_per_tile) for c in range(1, SCORE_RUN + 1)]
    s_max, s_min = _switch_in_two_levels((j % groups_per_tile) * SCORE_RUN + rest - 1, score_endings, stats,
                                         width=SCORE_RUN)

    def count_tiles(per_tile):
        cnt_scr[...] = jnp.zeros(cnt_scr.shape, F32)
        base = jnp.int32(0)
        arm = 1 << (sc_scr.shape[0].bit_length() - 1)
        while arm >= 1:
            has = (n_tiles & arm) != 0

            @pl.when(has)
            def _run(base=base, arm=arm):
                chains = [jnp.zeros((SUBLANES, Q_BLOCK), F32) for _ in range(COUNT_CHAINS)]
                for i in range(arm):
                    flags = per_tile(base + i)
                    for r in range(KEY_TILE // SUBLANES):
                        chains[r % COUNT_CHAINS] = chains[r % COUNT_CHAINS] + flags[r * SUBLANES:(r + 1) * SUBLANES, :]
                while len(chains) > 1:
                    chains = [chains[k] + chains[k + 1] for k in range(0, len(chains), 2)]
                cnt_scr[...] += chains[0]

            base = base + jnp.where(has, arm, 0)
            arm //= 2
        return jnp.sum(cnt_scr[...], axis=0, keepdims=True).astype(I32)

    def count_ge(thr):
        return count_tiles(lambda i: jnp.where(sc_scr[i] >= thr, 1.0, 0.0))

    def key_group(ref, g):
        return ref[g // groups_per_tile, (g % groups_per_tile) * Q_BLOCK:(g % groups_per_tile + 1) * Q_BLOCK, :]

    def sum_groups(per_group, groups):
        chains = [jnp.zeros((SUBLANES, Q_BLOCK), F32) for _ in range(COUNT_CHAINS)]
        for g in range(groups):
            flags = per_group(g)
            for r in range(Q_BLOCK // SUBLANES):
                chains[r % COUNT_CHAINS] = chains[r % COUNT_CHAINS] + flags[r * SUBLANES:(r + 1) * SUBLANES, :]
        while len(chains) > 1:
            chains = [chains[k] + chains[k + 1] for k in range(0, len(chains), 2)]
        return jnp.sum(chains[0], axis=0, keepdims=True)

    def count_keys_below(key, groups):
        one = jnp.int32(0x3F800000)
        return sum_groups(lambda g: lax.bitcast_convert_type(((key_group(key_scr, g) - key) >> 31) & one, F32), groups)

    def search_with_groups(groups):
        most_below = float(groups * Q_BLOCK - topk)

        def bisect_step(_, st):
            lo, hi, mid, mid_key = st
            up, down = mid * 0.5 + hi * 0.5, lo * 0.5 + mid * 0.5
            up_key, down_key = _order_key(up), _order_key(down)
            take = count_keys_below(mid_key, groups) <= most_below
            return (jnp.where(take, mid, lo), jnp.where(take, hi, mid),
                    jnp.where(take, up, down), jnp.where(take, up_key, down_key))

        def coarse_step(_, st):
            lo, hi, mid, mid_key = st
            bucket = mid_key >> COARSE_SHIFT
            edge = jnp.minimum(hi, _order_key_to_f32((bucket + 1) << COARSE_SHIFT))
            up, down = mid * 0.5 + hi * 0.5, lo * 0.5 + edge * 0.5
            up_key, down_key = _order_key(up), _order_key(down)
            c = bucket.astype(F32) + 0.5
            sides = sum_groups(lambda g: lax.clamp(-0.5, c - key_group(coarse_scr, g), 0.5), groups)
            take = sides + 0.5 * (groups * Q_BLOCK) <= most_below
            return (jnp.where(take, mid, lo), jnp.where(take, hi, edge),
                    jnp.where(take, up, down), jnp.where(take, up_key, down_key))

        def search(st):
            st = lax.fori_loop(0, COARSE_ROUNDS, coarse_step, st)
            return lax.fori_loop(COARSE_ROUNDS, BISECT_ROUNDS, bisect_step, st)

        return search

    above_max = s_max + jnp.maximum(jnp.abs(s_max) * 1e-6, 1e-30)
    first_mid = s_min * 0.5 + above_max * 0.5
    searches = [search_with_groups(n) for n in range(1, sc_scr.shape[0] * groups_per_tile + 1)]
    lo = _switch_in_two_levels(j, searches, (s_min, above_max, first_mid, _order_key(first_mid)))[0]
    few = key_limit < topk
    thr = jnp.where(few, jnp.finfo(F32).min, _order_key_to_f32(_order_key(lo)))
    cnt = jnp.where(few, topk, count_ge(thr))
    thr_scr[...] = thr

    @pl.when(jnp.max(jnp.abs(cnt - topk)) > 0)
    def _resolve():
        def max_below(thr):
            def tile(t, m):
                sc = sc_scr[t]
                return jnp.maximum(m, jnp.max(jnp.where(sc < thr, sc, -jnp.inf), axis=0, keepdims=True))
            return lax.fori_loop(0, n_tiles, tile, jnp.full((1, Q_BLOCK), -jnp.inf, F32))

        def lower(st):
            thr, cnt = st
            nxt = max_below(thr)
            short = cnt < topk
            return jnp.where(short, nxt, thr), jnp.where(short, count_ge(nxt), cnt)

        def min_above(thr):
            def tile(t, m):
                sc = sc_scr[t]
                return jnp.minimum(m, jnp.min(jnp.where(sc > thr, sc, jnp.inf), axis=0, keepdims=True))
            return lax.fori_loop(0, n_tiles, tile, jnp.full((1, Q_BLOCK), jnp.inf, F32))

        def count_eq_below(thr, pos_limit):
            def tile(t, c):
                r0 = t * KEY_TILE
                hit = (sc_scr[t] == thr) & (r0 + row < pos_limit)
                return c + jnp.sum(hit.astype(I32), axis=0, keepdims=True)
            return lax.fori_loop(0, n_tiles, tile, jnp.zeros((1, Q_BLOCK), I32))

        def body(st):
            thr, cnt = st
            nxt = min_above(thr)
            cnt_n = count_ge(nxt)
            active = cnt > topk
            advance = active & (cnt_n >= topk)
            tie = active & (cnt_n < topk)
            @pl.when(jnp.max(tie.astype(I32)) > 0)
            def _drop_surplus_ties():
                need = topk - cnt_n
                pos = jnp.zeros((1, Q_BLOCK), I32)
                for b in range(n_pos_bits - 1, -1, -1):
                    cand = pos + (1 << b)
                    pos = jnp.where(count_eq_below(thr, cand) < need, cand, pos)

                def drop_tile(t, carry):
                    r0 = t * KEY_TILE
                    sc = sc_scr[t]
                    sc_scr[t] = jnp.where(tie & (sc == thr) & (r0 + row > pos), -jnp.inf, sc)
                    return carry

                lax.fori_loop(0, n_tiles, drop_tile, 0)

            return jnp.where(advance, nxt, thr), jnp.where(advance, cnt_n, jnp.where(tie, topk, cnt))

        st = lax.while_loop(lambda st: jnp.min(st[1]) < topk, lower, (thr, cnt))
        thr_scr[...] = lax.while_loop(lambda st: jnp.max(st[1]) > topk, body, st)[0]

    thr = thr_scr[...]

    zero = jnp.zeros((DSA_HD, Q_BLOCK), BF16)
    q_pairs = []
    for p in range(DSA_HEADS // 2):
        a = qT_ref[(2 * p) * DSA_HD:(2 * p + 1) * DSA_HD, :]
        b = qT_ref[(2 * p + 1) * DSA_HD:(2 * p + 2) * DSA_HD, :]
        q_pairs.append(jnp.concatenate([jnp.concatenate([a, zero], axis=1),
                                        jnp.concatenate([zero, b], axis=1),
                                        slope_ref[p]], axis=0))
    acc_scr[...] = jnp.zeros(acc_scr.shape, F32)
    qposf = qpos.astype(F32)

    def qk_products(t, rows):
        r0 = pl.multiple_of(t * KEY_TILE, KEY_TILE)
        kt = kn_ref[pl.ds(r0, rows), :]
        kp = kpos_ref[pl.ds(r0, rows), :]
        s2 = [_dot(jnp.concatenate([kt[:, p * 2 * DSA_HD:(p + 1) * 2 * DSA_HD], kp], axis=1), q_pairs[p])
              for p in range(DSA_HEADS // 2)]
        return t, rows, sc_scr[t, 0:rows, :] >= thr, s2

    def softmax_pv(tile, ml, last_tile):
        t, rows, sel, s2 = tile
        vt = vT_ref[t, :, 0:rows]
        if last_tile:
            after = jnp.maximum((t * KEY_TILE + row[0:rows]).astype(F32) - qposf, 0.0)
        m_all, l_all = ml
        m_rows, l_rows = [], []
        for h in range(DSA_HEADS):
            s = s2[h // 2][:, (h % 2) * Q_BLOCK:(h % 2 + 1) * Q_BLOCK]
            if last_tile:
                s = s - (2.0 * LOG2E * _alibi_slope(h)) * after
            s = jnp.where(sel, s, NEG_BIG)
            m_old = m_all[h:h + 1, :]
            m_new = jnp.maximum(m_old, jnp.max(s, axis=0, keepdims=True))
            alpha = jnp.exp2(m_old - m_new)
            pr = jnp.exp2(s - m_new)
            l_rows.append(alpha * l_all[h:h + 1, :] + jnp.sum(pr, axis=0, keepdims=True))
            m_rows.append(m_new)
            hs = slice(h * DSA_HD, (h + 1) * DSA_HD)
            acc_scr[hs, :] = alpha * acc_scr[hs, :] + _dot(vt[hs, :], pr.astype(BF16))
        return jnp.concatenate(m_rows, axis=0), jnp.concatenate(l_rows, axis=0)

    def attend_run(first, count, ml, last_rows=None):
        tiles = [qk_products(first + i, last_rows if (last_rows and i == count - 1) else KEY_TILE)
                 for i in range(count)]
        for i, tile in enumerate(tiles):
            ml = softmax_pv(tile, ml, bool(last_rows) and i == count - 1)
        return ml

    ml = (jnp.full((DSA_HEADS, Q_BLOCK), NEG_BIG, F32), jnp.zeros((DSA_HEADS, Q_BLOCK), F32))
    full_runs = (n_tiles - 1) // ATTEND_RUN
    ml = lax.fori_loop(0, full_runs, lambda u, ml: attend_run(u * ATTEND_RUN, ATTEND_RUN, ml), ml)
    rest = n_tiles - full_runs * ATTEND_RUN
    last_groups = j % groups_per_tile
    endings = [(lambda ml, c=c, g=g: attend_run(n_tiles - c, c, ml, last_rows=(g + 1) * Q_BLOCK))
               for g in range(groups_per_tile) for c in range(1, ATTEND_RUN + 1)]
    _, l_all = _switch_in_two_levels(last_groups * ATTEND_RUN + rest - 1, endings, ml, width=ATTEND_RUN)

    outs = [acc_scr[h * DSA_HD:(h + 1) * DSA_HD, :] / l_all[h:h + 1, :] for h in range(DSA_HEADS)]
    o_ref[...] = jnp.concatenate(outs, axis=0).T.astype(o_ref.dtype)


def _dsa(qT, qiT, wT, kn, vT, rs, batch, seq):
    n = kn.shape[0]
    blocks = seq // Q_BLOCK
    tiles = seq // KEY_TILE
    topk = min(TOPK_MAX, seq // 4)
    col_map = lambda b, j: (0, b * blocks + j)
    pos = np.arange(seq)
    digit = 1 << KPOS_DIGIT_BITS
    kpos = np.zeros((seq, LANES), np.float32)
    slope_rows = np.zeros((DSA_HEADS // 2, LANES, 2 * Q_BLOCK), np.float32)
    rest = np.asarray([LOG2E * _alibi_slope(h) for h in range(DSA_HEADS)], np.float32).reshape(DSA_HEADS // 2, 2)
    for i in range(SLOPE_PIECES):
        piece = rest.astype(BF16).astype(np.float32)
        rest = rest - piece
        cols = np.repeat(piece, Q_BLOCK, axis=1)
        kpos[:, 2 * i], kpos[:, 2 * i + 1] = pos // digit, pos % digit
        slope_rows[:, 2 * i, :], slope_rows[:, 2 * i + 1, :] = digit * cols, cols
    kpos, slope_rows = jnp.asarray(kpos.astype(BF16)), jnp.asarray(slope_rows.astype(BF16))
    return pl.pallas_call(
        functools.partial(_dsa_kernel, topk=topk, n_pos_bits=seq.bit_length()),
        grid=(batch, blocks),
        in_specs=[
            pl.BlockSpec((DSA_DIM, Q_BLOCK), col_map),
            pl.BlockSpec((IDX_Q_DIM, Q_BLOCK), col_map),
            pl.BlockSpec((IDX_HEADS, Q_BLOCK), col_map),
            pl.BlockSpec((seq, DSA_DIM), lambda b, j: (b, 0)),
            pl.BlockSpec((tiles, DSA_DIM, KEY_TILE), lambda b, j: (b, 0, 0)),
            pl.BlockSpec((seq, LANES), lambda b, j: (b, 0)),
            _const_spec((seq, LANES)),
            _const_spec((DSA_HEADS // 2, LANES, 2 * Q_BLOCK)),
        ],
        out_specs=pl.BlockSpec((Q_BLOCK, DSA_DIM), lambda b, j: (b * blocks + j, 0)),
        out_shape=jax.ShapeDtypeStruct((n, DSA_DIM), BF16),
        scratch_shapes=[
            pltpu.VMEM((seq, IDX_HD), BF16),
            pltpu.VMEM((tiles, KEY_TILE, Q_BLOCK), F32),
            pltpu.VMEM((tiles, KEY_TILE, Q_BLOCK), I32),
            pltpu.VMEM((tiles, KEY_TILE, Q_BLOCK), F32),
            pltpu.VMEM((1, Q_BLOCK), F32),
            pltpu.VMEM((SUBLANES, Q_BLOCK), F32),
            pltpu.VMEM((DSA_DIM, Q_BLOCK), F32),
        ],
        compiler_params=pltpu.CompilerParams(dimension_semantics=("arbitrary", "arbitrary"), vmem_limit_bytes=VMEM_LIMIT),
        name="dsa",
    )(qT, qiT, wT, kn, vT, rs, kpos, slope_rows)


def _memkv_kernel(mem_ref, g_ref, w_ref, bd_ref, gk_ref, k_ref, v_ref):
    h = _rms_rows(mem_ref[...], g_ref[...]).astype(BF16)
    kv = _dot(h, w_ref[...])
    k = kv[:, :XA_DIM]
    kms = _dot((k * k).astype(BF16), bd_ref[...]) * (1.0 / XA_HD)
    k_ref[...] = (k * lax.rsqrt(kms + EPS) * gk_ref[...]).astype(BF16)
    v_ref[...] = kv[:, XA_DIM:].astype(BF16)


def _memkv(mem2, g_mem, w_xkv, g_xk):
    n, d = mem2.shape
    tm = min(ROW_TILE, n)
    return pl.pallas_call(
        _memkv_kernel,
        grid=(n // tm,),
        in_specs=[
            pl.BlockSpec((tm, d), lambda i: (i, 0)),
            _const_spec((1, d)),
            _const_spec((d, 2 * XA_DIM)),
            _const_spec((XA_DIM, XA_DIM)),
            _const_spec((1, XA_DIM)),
        ],
        out_specs=(pl.BlockSpec((tm, XA_DIM), lambda i: (i, 0)), pl.BlockSpec((tm, XA_DIM), lambda i: (i, 0))),
        out_shape=(jax.ShapeDtypeStruct((n, XA_DIM), BF16), jax.ShapeDtypeStruct((n, XA_DIM), BF16)),
        compiler_params=pltpu.CompilerParams(dimension_semantics=("arbitrary",), vmem_limit_bytes=VMEM_LIMIT),
        name="memkv",
    )(mem2, g_mem[None, :], w_xkv.astype(BF16), _block_diag_ones(XA_DIM, XA_HD), jnp.tile(g_xk, XA_HEADS)[None, :])


def _tail_kernel(x_ref, ya_ref, yb_ref, wout_ref, gx_ref, wxq_ref, bd_ref, gxq_ref, km_ref, vm_ref, wxo_ref,
                 gf_ref, wgu_ref, wd_ref, o_ref, *, ff_bounds):
    tm = x_ref.shape[0]
    part = tm // ROW_PARTS
    groups = [slice(g * part, (g + 1) * part) for g in range(ROW_PARTS)]
    d_ff = wd_ref.shape[0]

    x1 = [x_ref[r, :] + _dot(jnp.concatenate([ya_ref[r, :], yb_ref[r, :]], axis=1), wout_ref[...]) for r in groups]

    qn = []
    for g in range(ROW_PARTS):
        h = _rms_rows(x1[g], gx_ref[...]).astype(BF16)
        q = _dot(h, wxq_ref[...])
        qms = _dot((q * q).astype(BF16), bd_ref[...]) * (1.0 / XA_HD)
        qn.append((q * lax.rsqrt(qms + EPS) * gxq_ref[...]).astype(BF16))

    attn = []
    for g in range(ROW_PARTS):
        heads = []
        for hd in range(XA_HEADS):
            hs = slice(hd * XA_HD, (hd + 1) * XA_HD)
            s = _dot_nt(qn[g][:, hs], km_ref[0, :, hs]) * (XA_HD ** -0.5)
            pr = jnp.exp(s - jnp.max(s, axis=-1, keepdims=True))
            o = _dot(pr.astype(BF16), vm_ref[0, :, hs]) / jnp.sum(pr, axis=-1, keepdims=True)
            heads.append(o.astype(BF16))
        attn.append(jnp.concatenate(heads, axis=1))

    x2 = [x1[g] + _dot(attn[g], wxo_ref[...]) for g in range(ROW_PARTS)]
    h = [_rms_rows(x2[g], gf_ref[...]).astype(BF16) for g in range(ROW_PARTS)]
    acc = x2
    for c0, c1 in zip(ff_bounds[:-1], ff_bounds[1:]):
        for g in range(ROW_PARTS):
            gate = _dot(h[g], wgu_ref[:, c0:c1])
            up = _dot(h[g], wgu_ref[:, d_ff + c0:d_ff + c1])
            a = (_silu(gate) * up).astype(BF16)
            acc[g] = acc[g] + _dot(a, wd_ref[c0:c1, :])
    for g, r in enumerate(groups):
        o_ref[r, :] = acc[g]


def _tail(x2, ya, yb, w_out, g_xattn, w_xq, g_xq, km, vm, w_xo, g_ffn, w_gu, w_down, seq):
    n, d = x2.shape
    tm = ROW_TILE
    d_ff = w_down.shape[0]
    assert d_ff % MXU_WIDTH == 0
    ff_bounds = (0, (d_ff // MXU_WIDTH + 1) // 2 * MXU_WIDTH, d_ff)
    n_mem = km.shape[1]
    per_seq = seq // tm
    return pl.pallas_call(
        functools.partial(_tail_kernel, ff_bounds=ff_bounds),
        grid=(n // tm,),
        in_specs=[
            pl.BlockSpec((tm, d), lambda i: (i, 0)),
            pl.BlockSpec((tm, GDN_VAL_DIM), lambda i: (i, 0)),
            pl.BlockSpec((tm, DSA_DIM), lambda i: (i, 0)),
            _const_spec((GDN_VAL_DIM + DSA_DIM, d)),
            _const_spec((1, d)),
            _const_spec((d, XA_DIM)),
            _const_spec((XA_DIM, XA_DIM)),
            _const_spec((1, XA_DIM)),
            pl.BlockSpec((1, n_mem, XA_DIM), lambda i: (i // per_seq, 0, 0)),
            pl.BlockSpec((1, n_mem, XA_DIM), lambda i: (i // per_seq, 0, 0)),
            _const_spec((XA_DIM, d)),
            _const_spec((1, d)),
            _const_spec((d, 2 * d_ff)),
            _const_spec((d_ff, d)),
        ],
        out_specs=pl.BlockSpec((tm, d), lambda i: (i, 0)),
        out_shape=jax.ShapeDtypeStruct((n, d), F32),
        compiler_params=pltpu.CompilerParams(dimension_semantics=("arbitrary",), vmem_limit_bytes=VMEM_LIMIT),
        name="tail",
    )(x2, ya, yb, w_out.astype(BF16), g_xattn[None, :], w_xq.astype(BF16), _block_diag_ones(XA_DIM, XA_HD),
      jnp.tile(g_xq, XA_HEADS)[None, :], km, vm, w_xo.astype(BF16), g_ffn[None, :], w_gu.astype(BF16),
      w_down.astype(BF16))


def kernel(x, mem, g_mix, w_in, conv_w, a_log, dt_bias, g_gdn_out, g_q_dsa, g_k_dsa, w_out, g_xattn, g_mem, w_xq,
           w_xkv, g_xq, g_xk, w_xo, g_ffn, w_gu, w_down):
    batch, seq, d = x.shape
    n_mem = mem.shape[1]
    assert seq % ROW_TILE == 0 and seq % KEY_TILE == 0 and ROW_TILE % KEY_TILE == 0
    for l in range(g_mix.shape[0]):
        x2 = x.reshape(batch * seq, d)
        qkv, z, kn, rs, qT, vT, qiT, wT = _inproj(x2, g_mix[l], w_in[l], g_q_dsa[l], g_k_dsa[l], conv_w[l], seq)
        ya = _gdn(qkv, z, rs, a_log[l], dt_bias[l], g_gdn_out[l], batch, seq)
        yb = _dsa(qT, qiT, wT, kn, vT, rs, batch, seq)
        km, vm = _memkv(mem.reshape(batch * n_mem, d), g_mem[l], w_xkv[l], g_xk[l])
        km = km.reshape(batch, n_mem, XA_DIM)
        vm = vm.reshape(batch, n_mem, XA_DIM)
        x = _tail(x2, ya, yb, w_out[l], g_xattn[l], w_xq[l], g_xq[l], km, vm, w_xo[l], g_ffn[l], w_gu[l],
                  w_down[l], seq).reshape(batch, seq, d)
    return x
```
